```python
import jax, jax.numpy as jnp
from jax import lax
import numpy as np

D_MODEL = 2048
BATCH = 4
SEQ = 2048
DEPTH = 2
DEC_BATCH = 128
DEC_SEQ = 1
PAST_LEN = 16384
PAGE_SIZE = 128

GROUP_WIDTH = D_MODEL // 4
SSD_WIDTH = GROUP_WIDTH
SSD_HEAD_DIM = 64
SSD_HEADS = SSD_WIDTH // SSD_HEAD_DIM
SSD_GROUPS = 2
SSD_STATE = 64
SSD_CONV = 4
SSD_CHUNK = 128
XBC_WIDTH = SSD_WIDTH + 2 * SSD_GROUPS * SSD_STATE
S5_WIDTH = GROUP_WIDTH
S5_CH = 16
S5_GROUPS = S5_WIDTH // S5_CH
S5_STATE = 64
SC_WIDTH = GROUP_WIDTH
SC_CONV = 3
POOL_WIDTH = GROUP_WIDTH
POOL_WINDOWS = (2, 4, 8, 16)
POOL_GROUP = POOL_WIDTH // len(POOL_WINDOWS)
POOL_HIST = max(POOL_WINDOWS) - 1
N_MEM = 256
XA_HEADS = 4
XA_HEAD_DIM = D_MODEL // XA_HEADS
D_FF = 4 * D_MODEL
EPS = 1e-6
IN_SPLIT_SIZES = (SSD_WIDTH, XBC_WIDTH, SSD_HEADS, S5_WIDTH, SC_WIDTH, SC_WIDTH, SC_WIDTH, POOL_WIDTH)
IN_WIDTH = sum(IN_SPLIT_SIZES)

kernel_name = 'hymba_ssd_s5_conv_pool_step'


def rmsnorm(x, g):
    xf = x.astype(jnp.float32)
    y = xf * lax.rsqrt(jnp.mean(xf * xf, axis=-1, keepdims=True) + EPS)
    return (y * g.astype(jnp.float32)).astype(x.dtype)


def split_cols(x, sizes):
    idx = np.cumsum(np.array(sizes))[:-1].tolist()
    return jnp.split(x, idx, axis=-1)


def causal_dwconv(full, w):
    ch = full.shape[-1]
    return lax.conv_general_dilated(full, w[:, None, :].astype(full.dtype), window_strides=(1,), padding='VALID',
                                    dimension_numbers=('NWC', 'WIO', 'NWC'), feature_group_count=ch)


def ssd_scan(xh, dt, a, bm, cm, h0):
    b, t, nh, hd = xh.shape
    ln = SSD_CHUNK if t % SSD_CHUNK == 0 else t
    nc = t // ln

    def chunks(v):
        return v.reshape((b, nc, ln) + v.shape[2:])

    xc, dtc, bc, cc = chunks(xh), chunks(dt), chunks(bm), chunks(cm)
    acum = jnp.cumsum(dtc * a, axis=2)
    causal = jnp.tril(jnp.ones((ln, ln), dtype=bool))[None, None, :, :, None]
    seg = acum[:, :, :, None, :] - acum[:, :, None, :, :]
    decay = jnp.exp(jnp.where(causal, seg, -jnp.inf))
    xdt = xc * dtc[..., None]
    scores = jnp.einsum('bclhn,bcshn->bclsh', cc, bc) * decay
    y_diag = jnp.einsum('bclsh,bcshp->bclhp', scores, xdt)
    to_end = jnp.exp(acum[:, :, -1:, :] - acum)
    chunk_states = jnp.einsum('bclhn,bclh,bclhp->bchpn', bc, to_end, xdt)
    chunk_decay = jnp.exp(acum[:, :, -1, :])

    def step(hc, inp):
        s_c, d_c = inp
        return hc * d_c[:, :, None, None] + s_c, hc

    h_last, h_prev = lax.scan(step, h0, (jnp.swapaxes(chunk_states, 0, 1), jnp.swapaxes(chunk_decay, 0, 1)))
    h_prev = jnp.swapaxes(h_prev, 0, 1)
    y_off = jnp.einsum('bclhn,bchpn,bclh->bclhp', cc, h_prev, jnp.exp(acum))
    return (y_diag + y_off).reshape(b, t, nh, hd), h_last


def _lin_combine(left, right):
    a_l, b_l = left
    a_r, b_r = right
    return a_r * a_l, a_r * b_l + b_r


def s5_mixer(u, h0_re, h0_im, p):
    f32 = jnp.float32
    b, t, _ = u.shape
    lam = lax.complex(p['s5_lam_re'].astype(f32), p['s5_lam_im'].astype(f32))
    delta = jnp.exp(p['s5_log_dt'].astype(f32))[:, None]
    abar = jnp.exp(lam * delta)
    bbar = ((abar - 1.0) / lam)[..., None] * lax.complex(p['s5_b_re'].astype(f32), p['s5_b_im'].astype(f32))
    cmat = lax.complex(p['s5_c_re'].astype(f32), p['s5_c_im'].astype(f32))
    ug = u.reshape(b, t, S5_GROUPS, S5_CH).astype(jnp.complex64)
    bu = jnp.einsum('gnk,btgk->btgn', bbar, ug)
    a_pow, hs = lax.associative_scan(_lin_combine, (jnp.broadcast_to(abar, bu.shape), bu), axis=1)
    hs = hs + a_pow * lax.complex(h0_re.astype(f32), h0_im.astype(f32))[:, None]
    y = jnp.real(jnp.einsum('gkn,btgn->btgk', cmat, hs)).reshape(b, t, S5_WIDTH) + p['s5_d'].astype(f32) * u
    y = jax.nn.gelu(y)
    y = y * jax.nn.sigmoid(y @ p['s5_w_glu'].astype(f32) + p['s5_b_glu'].astype(f32))
    h_end = hs[:, -1]
    return y, jnp.real(h_end), jnp.imag(h_end)


def pool_mixer(u, prev, start_pos, p):
    f32 = jnp.float32
    b, t, _ = u.shape
    full = jnp.concatenate([prev.astype(f32), u], axis=1)
    csum = jnp.concatenate([jnp.zeros((b, 1, POOL_WIDTH), f32), jnp.cumsum(full, axis=1)], axis=1)
    upto = csum[:, POOL_HIST + 1:]
    pos = start_pos + jnp.arange(t, dtype=jnp.int32)
    parts = []
    for g, w in enumerate(POOL_WINDOWS):
        cs = slice(g * POOL_GROUP, (g + 1) * POOL_GROUP)
        lo = csum[:, POOL_HIST + 1 - w: POOL_HIST + 1 - w + t, cs]
        cnt = jnp.minimum(w, pos + 1).astype(f32)[None, :, None]
        parts.append((upto[..., cs] - lo) / cnt - u[..., cs])
    pooled = jnp.stack(parts, axis=2)
    y = jnp.einsum('btgc,gcd->btgd', pooled, p['pool_w'].astype(f32)).reshape(b, t, POOL_WIDTH)
    return y * p['pool_scale'].astype(f32), full[:, -POOL_HIST:]


def memory_kv(mem, p):
    b, m, _ = mem.shape
    mn = rmsnorm(mem, p['norm_mem_g'])
    k = (mn @ p['w_k']).reshape(b, m, XA_HEADS, XA_HEAD_DIM)
    v = (mn @ p['w_v']).reshape(b, m, XA_HEADS, XA_HEAD_DIM)
    return k, v


def trunk_layer(h, mem_k, mem_v, st, start_pos, p):
    f32 = jnp.float32
    conv_prev, ssd_h0, s5_re0, s5_im0, sc_prev, pool_prev = st
    b, t, _ = h.shape
    xn = rmsnorm(h, p['norm_mix_g'])
    proj = (xn @ p['w_in']).astype(f32)
    z, xbc, dt_raw, u5, gate_b, gate_c, hv, upool = split_cols(proj, IN_SPLIT_SIZES)

    xbc_full = jnp.concatenate([conv_prev.astype(f32), xbc], axis=1)
    xbc_c = jax.nn.silu(causal_dwconv(xbc_full, p['ssd_conv_w'].astype(f32)) + p['ssd_conv_b'].astype(f32))
    xs, bm, cm = split_cols(xbc_c, (SSD_WIDTH, SSD_GROUPS * SSD_STATE, SSD_GROUPS * SSD_STATE))
    xs = xs.reshape(b, t, SSD_HEADS, SSD_HEAD_DIM)
    rep = SSD_HEADS // SSD_GROUPS
    bm = jnp.repeat(bm.reshape(b, t, SSD_GROUPS, SSD_STATE), rep, axis=2)
    cm = jnp.repeat(cm.reshape(b, t, SSD_GROUPS, SSD_STATE), rep, axis=2)
    dt = jax.nn.softplus(dt_raw + p['ssd_dt_bias'].astype(f32))
    a = -jnp.exp(p['ssd_a_log'].astype(f32))
    y_ssd, ssd_new = ssd_scan(xs, dt, a, bm, cm, ssd_h0.astype(f32))
    y_ssd = (y_ssd + p['ssd_d'].astype(f32)[:, None] * xs).reshape(b, t, SSD_WIDTH) * jax.nn.silu(z)

    y_s5, s5_re, s5_im = s5_mixer(u5, s5_re0, s5_im0, p)

    sc_full = jnp.concatenate([sc_prev.astype(f32), gate_c * hv], axis=1)
    y_sc = gate_b * causal_dwconv(sc_full, p['sc_conv_w'].astype(f32))

    y_pool, pool_new = pool_mixer(upool, pool_prev, start_pos, p)

    mix = jnp.concatenate([y_ssd, y_s5, y_sc, y_pool], axis=-1).reshape(b, t, 4, GROUP_WIDTH)
    mix = mix * lax.rsqrt(jnp.mean(mix * mix, axis=-1, keepdims=True) + EPS)
    mix = (mix * p['mix_out_g'].astype(f32).reshape(4, GROUP_WIDTH)).reshape(b, t, D_MODEL)
    h = h + mix.astype(h.dtype) @ p['w_out']

    q = (rmsnorm(h, p['norm_xa_g']) @ p['w_q']).reshape(b, t, XA_HEADS, XA_HEAD_DIM)
    s = jnp.einsum('bthd,bmhd->bhtm', q.astype(f32), mem_k.astype(f32)) * (XA_HEAD_DIM ** -0.5)
    prob = jax.nn.softmax(s, axis=-1)
    o = jnp.einsum('bhtm,bmhd->bthd', prob, mem_v.astype(f32)).reshape(b, t, D_MODEL)
    h = h + o.astype(h.dtype) @ p['w_o']

    xm = rmsnorm(h, p['norm_mlp_g'])
    h = h + jnp.square(jax.nn.relu(xm @ p['w_up'])) @ p['w_down']

    new_st = (xbc_full[:, -(SSD_CONV - 1):], ssd_new, s5_re, s5_im, sc_full[:, -(SC_CONV - 1):], pool_new)
    return h, new_st


def setup_inputs(seed: int = 0) -> dict:
    key = jax.random.key(seed)
    ks = iter(jax.random.split(key, 64))
    f32 = jnp.float32

    def nrm(shape, scale):
        return scale * jax.random.normal(next(ks), shape, f32)

    def gain(shape):
        return 1.0 + 0.02 * jax.random.normal(next(ks), shape, f32)

    def unif(shape, lo, hi):
        return jax.random.uniform(next(ks), shape, f32, minval=lo, maxval=hi)

    dt0 = jnp.exp(unif((DEPTH, SSD_HEADS), np.log(1e-3), np.log(1e-1)))
    lam_im0 = jnp.pi * jnp.arange(S5_STATE, dtype=f32)
    return {
        'x_prompt': nrm((BATCH, SEQ, D_MODEL), 1.0),
        'x_sample': nrm((DEC_BATCH, DEC_SEQ, D_MODEL), 1.0),
        'mem_prompt': nrm((BATCH, N_MEM, D_MODEL), 1.0),
        'state_ssd_conv': nrm((DEPTH, DEC_BATCH, SSD_CONV - 1, XBC_WIDTH), 1.0),
        'state_ssd': nrm((DEPTH, DEC_BATCH, SSD_HEADS, SSD_HEAD_DIM, SSD_STATE), 0.5),
        'state_s5_re': nrm((DEPTH, DEC_BATCH, S5_GROUPS, S5_STATE), 0.3),
        'state_s5_im': nrm((DEPTH, DEC_BATCH, S5_GROUPS, S5_STATE), 0.3),
        'state_sconv': nrm((DEPTH, DEC_BATCH, SC_CONV - 1, SC_WIDTH), 1.0),
        'state_pool': nrm((DEPTH, DEC_BATCH, POOL_HIST, POOL_WIDTH), 1.0),
        'cache_mem_k': nrm((DEPTH, DEC_BATCH, N_MEM, XA_HEADS, XA_HEAD_DIM), 1.0),
        'cache_mem_v': nrm((DEPTH, DEC_BATCH, N_MEM, XA_HEADS, XA_HEAD_DIM), 1.0),
        'norm_mix_g': gain((DEPTH, D_MODEL)),
        'w_in': nrm((DEPTH, D_MODEL, IN_WIDTH), D_MODEL ** -0.5),
        'ssd_conv_w': nrm((DEPTH, SSD_CONV, XBC_WIDTH), SSD_CONV ** -0.5),
        'ssd_conv_b': nrm((DEPTH, XBC_WIDTH), 0.02),
        'ssd_dt_bias': dt0 + jnp.log(-jnp.expm1(-dt0)),
        'ssd_a_log': jnp.log(unif((DEPTH, SSD_HEADS), 1.0, 16.0)),
        'ssd_d': gain((DEPTH, SSD_HEADS)),
        's5_lam_re': -0.5 + nrm((DEPTH, S5_GROUPS, S5_STATE), 0.01),
        's5_lam_im': lam_im0 + nrm((DEPTH, S5_GROUPS, S5_STATE), 0.01),
        's5_log_dt': unif((DEPTH, S5_GROUPS), np.log(1e-3), np.log(1e-1)),
        's5_b_re': nrm((DEPTH, S5_GROUPS, S5_STATE, S5_CH), (2 * S5_CH) ** -0.5),
        's5_b_im': nrm((DEPTH, S5_GROUPS, S5_STATE, S5_CH), (2 * S5_CH) ** -0.5),
        's5_c_re': nrm((DEPTH, S5_GROUPS, S5_CH, S5_STATE), S5_STATE ** -0.5),
        's5_c_im': nrm((DEPTH, S5_GROUPS, S5_CH, S5_STATE), S5_STATE ** -0.5),
        's5_d': nrm((DEPTH, S5_WIDTH), 1.0),
        's5_w_glu': nrm((DEPTH, S5_WIDTH, S5_WIDTH), S5_WIDTH ** -0.5),
        's5_b_glu': nrm((DEPTH, S5_WIDTH), 0.02),
        'sc_conv_w': nrm((DEPTH, SC_CONV, SC_WIDTH), SC_CONV ** -0.5),
        'pool_w': nrm((DEPTH, len(POOL_WINDOWS), POOL_GROUP, POOL_GROUP), POOL_GROUP ** -0.5),
        'pool_scale': 1.0 + nrm((DEPTH, POOL_WIDTH), 0.1),
        'mix_out_g': gain((DEPTH, D_MODEL)),
        'w_out': nrm((DEPTH, D_MODEL, D_MODEL), D_MODEL ** -0.5),
        'norm_xa_g': gain((DEPTH, D_MODEL)),
        'norm_mem_g': gain((DEPTH, D_MODEL)),
        'w_q': nrm((DEPTH, D_MODEL, D_MODEL), D_MODEL ** -0.5),
        'w_k': nrm((DEPTH, D_MODEL, D_MODEL), D_MODEL ** -0.5),
        'w_v': nrm((DEPTH, D_MODEL, D_MODEL), D_MODEL ** -0.5),
        'w_o': nrm((DEPTH, D_MODEL, D_MODEL), D_MODEL ** -0.5),
        'norm_mlp_g': gain((DEPTH, D_MODEL)),
        'w_up': nrm((DEPTH, D_MODEL, D_FF), D_MODEL ** -0.5),
        'w_down': nrm((DEPTH, D_FF, D_MODEL), D_FF ** -0.5),
        'final_norm_g': gain((D_MODEL,)),
    }


def reference(x_prompt, x_sample, mem_prompt, state_ssd_conv, state_ssd, state_s5_re, state_s5_im,
              state_sconv, state_pool, cache_mem_k, cache_mem_v, norm_mix_g, w_in, ssd_conv_w, ssd_conv_b,
              ssd_dt_bias, ssd_a_log, ssd_d, s5_lam_re, s5_lam_im, s5_log_dt, s5_b_re, s5_b_im, s5_c_re,
              s5_c_im, s5_d, s5_w_glu, s5_b_glu, sc_conv_w, pool_w, pool_scale, mix_out_g, w_out, norm_xa_g,
              norm_mem_g, w_q, w_k, w_v, w_o, norm_mlp_g, w_up, w_down, final_norm_g):
    f32 = jnp.float32
    stacked = dict(norm_mix_g=norm_mix_g, w_in=w_in, ssd_conv_w=ssd_conv_w, ssd_conv_b=ssd_conv_b,
                   ssd_dt_bias=ssd_dt_bias, ssd_a_log=ssd_a_log, ssd_d=ssd_d, s5_lam_re=s5_lam_re,
                   s5_lam_im=s5_lam_im, s5_log_dt=s5_log_dt, s5_b_re=s5_b_re, s5_b_im=s5_b_im,
                   s5_c_re=s5_c_re, s5_c_im=s5_c_im, s5_d=s5_d, s5_w_glu=s5_w_glu, s5_b_glu=s5_b_glu,
                   sc_conv_w=sc_conv_w, pool_w=pool_w, pool_scale=pool_scale, mix_out_g=mix_out_g,
                   w_out=w_out, norm_xa_g=norm_xa_g, norm_mem_g=norm_mem_g, w_q=w_q, w_k=w_k, w_v=w_v,
                   w_o=w_o, norm_mlp_g=norm_mlp_g, w_up=w_up, w_down=w_down)
    bp = x_prompt.shape[0]
    st_prompt = (jnp.zeros((bp, SSD_CONV - 1, XBC_WIDTH), f32),
                 jnp.zeros((bp, SSD_HEADS, SSD_HEAD_DIM, SSD_STATE), f32),
                 jnp.zeros((bp, S5_GROUPS, S5_STATE), f32),
                 jnp.zeros((bp, S5_GROUPS, S5_STATE), f32),
                 jnp.zeros((bp, SC_CONV - 1, SC_WIDTH), f32),
                 jnp.zeros((bp, POOL_HIST, POOL_WIDTH), f32))
    hp, hs = x_prompt, x_sample
    p_new, s_new, p_mk, p_mv = [], [], [], []
    for l in range(DEPTH):
        p = {name: arr[l] for name, arr in stacked.items()}
        mk, mv = memory_kv(mem_prompt, p)
        hp, st_p = trunk_layer(hp, mk, mv, st_prompt, 0, p)
        st_s_in = (state_ssd_conv[l], state_ssd[l], state_s5_re[l], state_s5_im[l], state_sconv[l], state_pool[l])
        hs, st_s = trunk_layer(hs, cache_mem_k[l], cache_mem_v[l], st_s_in, PAST_LEN, p)
        p_new.append(st_p)
        s_new.append(st_s)
        p_mk.append(mk)
        p_mv.append(mv)
    y_prompt = rmsnorm(hp, final_norm_g)
    y_sample = rmsnorm(hs, final_norm_g)
    p_ssd_conv, p_ssd, p_s5_re, p_s5_im, p_sconv, p_pool = [jnp.stack(c) for c in zip(*p_new)]
    s_ssd_conv, s_ssd, s_s5_re, s_s5_im, s_sconv, s_pool = [jnp.stack(c) for c in zip(*s_new)]
    p_mem_k = jnp.stack(p_mk)
    p_mem_v = jnp.stack(p_mv)
    return (y_prompt, y_sample, p_ssd_conv, p_ssd, p_s5_re, p_s5_im, p_sconv, p_pool, p_mem_k, p_mem_v,
            s_ssd_conv, s_ssd, s_s5_re, s_s5_im, s_sconv, s_pool)
```

```python
import functools
import math

import numpy as np
import jax
import jax.numpy as jnp
from jax import lax
from jax.experimental import pallas as pl
from jax.experimental.pallas import tpu as pltpu

F32 = jnp.float32
BF16 = jnp.bfloat16

D_MODEL = 2048
BATCH = 4
SEQ = 2048
DEPTH = 2
DEC_BATCH = 128
PAST_LEN = 16384
GROUP_WIDTH = D_MODEL // 4
SSD_HEAD_DIM = 64
SSD_HEADS = GROUP_WIDTH // SSD_HEAD_DIM
SSD_GROUPS = 2
SSD_STATE = 64
SSD_CONV = 4
SSD_CHUNK = 128
XBC_WIDTH = GROUP_WIDTH + 2 * SSD_GROUPS * SSD_STATE
S5_CH = 16
S5_GROUPS = GROUP_WIDTH // S5_CH
S5_STATE = 64
S5_LANES = S5_GROUPS * S5_STATE
SC_CONV = 3
POOL_WINDOWS = (2, 4, 8, 16)
POOL_GROUP = GROUP_WIDTH // len(POOL_WINDOWS)
POOL_HIST = max(POOL_WINDOWS) - 1
N_MEM = 256
XA_HEADS = 4
XA_HEAD_DIM = D_MODEL // XA_HEADS
D_FF = 4 * D_MODEL
EPS = 1e-6
P_ROWS = BATCH * SEQ

LANE = 128
SUBLANE = 8
VMEM_LIMIT = 56 * 1024 * 1024

COL_Z = 0
COL_U5 = 512
COL_GB = 1024
COL_GC = 1536
COL_HV = 2048
COL_UP = 2560
COL_XBC = 3072
COL_DT = 3840
IN_PAD = 4096

S5_CHUNK = 256
S5_SCAN_LANES = 256


def _cparams(sem):
    return pltpu.CompilerParams(dimension_semantics=sem, vmem_limit_bytes=VMEM_LIMIT)


def _sigmoid(x):
    return 1.0 / (1.0 + jnp.exp(-x))


def _silu(x):
    return x * _sigmoid(x)


def _softplus(x):
    return jnp.maximum(x, 0.0) + jnp.log(1.0 + jnp.exp(-jnp.abs(x)))


def _gelu_tanh(x):
    return 0.5 * x * (1.0 + jnp.tanh(math.sqrt(2.0 / math.pi) * (x + 0.044715 * (x * x * x))))


def _rms(x, g):
    return x * lax.rsqrt(jnp.mean(x * x, axis=-1, keepdims=True) + EPS) * g


def _dot(a, b):
    return jnp.dot(a, b, preferred_element_type=F32)


def _split_bf16(x):
    hi = x.astype(BF16)
    lo = (x - hi.astype(F32)).astype(BF16)
    return hi, lo


def _norm_matmul_kernel(*refs, norm, has_res):
    if has_res:
        x_ref, g_ref, w_ref, r_ref, o_ref, xn_ref = refs
    else:
        x_ref, g_ref, w_ref, o_ref, xn_ref = refs
        r_ref = None

    @pl.when(pl.program_id(1) == 0)
    def _():
        x = x_ref[...].astype(F32)
        if norm:
            x = _rms(x, g_ref[...])
        xn_ref[...] = x.astype(BF16)

    acc = _dot(xn_ref[...], w_ref[...])
    if has_res:
        acc = acc + r_ref[...]
    o_ref[...] = acc.astype(o_ref.dtype)


def norm_matmul(x, g, w, res=None, *, norm=True, out_dtype=F32, tm, tn):
    m, k = x.shape
    n = w.shape[1]
    assert m % tm == 0 and n % tn == 0
    in_specs = [pl.BlockSpec((tm, k), lambda i, j: (i, 0)),
                pl.BlockSpec((1, k), lambda i, j: (0, 0)),
                pl.BlockSpec((k, tn), lambda i, j: (0, j))]
    args = [x, g.reshape(1, k).astype(F32), w]
    if res is not None:
        in_specs.append(pl.BlockSpec((tm, tn), lambda i, j: (i, j)))
        args.append(res)
    return pl.pallas_call(
        functools.partial(_norm_matmul_kernel, norm=norm, has_res=res is not None),
        grid=(m // tm, n // tn),
        in_specs=in_specs,
        out_specs=pl.BlockSpec((tm, tn), lambda i, j: (i, j)),
        out_shape=jax.ShapeDtypeStruct((m, n), out_dtype),
        scratch_shapes=[pltpu.VMEM((tm, k), BF16)],
        compiler_params=_cparams(("parallel", "arbitrary")),
        name="norm_matmul",
    )(*args)


def _out_proj_kernel(m0_ref, m1_ref, m2_ref, m3_ref, w_ref, r_ref, o_ref, xc_ref):
    @pl.when(pl.program_id(1) == 0)
    def _():
        for i, mr in enumerate((m0_ref, m1_ref, m2_ref, m3_ref)):
            xc_ref[:, i * GROUP_WIDTH:(i + 1) * GROUP_WIDTH] = mr[...]

    o_ref[...] = r_ref[...] + _dot(xc_ref[...], w_ref[...])


def out_proj(mixes, w, res, *, tm, tn):
    m = res.shape[0]
    mix_spec = pl.BlockSpec((tm, GROUP_WIDTH), lambda i, j: (i, 0))
    return pl.pallas_call(
        _out_proj_kernel,
        grid=(m // tm, D_MODEL // tn),
        in_specs=[mix_spec] * 4 + [pl.BlockSpec((D_MODEL, tn), lambda i, j: (0, j)),
                                   pl.BlockSpec((tm, tn), lambda i, j: (i, j))],
        out_specs=pl.BlockSpec((tm, tn), lambda i, j: (i, j)),
        out_shape=jax.ShapeDtypeStruct((m, D_MODEL), F32),
        scratch_shapes=[pltpu.VMEM((tm, D_MODEL), BF16)],
        compiler_params=_cparams(("parallel", "arbitrary")),
        name="out_proj",
    )(*mixes, w, res)


def _mlp_kernel(x_ref, g_ref, wu_ref, wd_ref, o_ref, xn_ref):
    j = pl.program_id(1)

    @pl.when(j == 0)
    def _():
        x = x_ref[...]
        xn_ref[...] = _rms(x, g_ref[...]).astype(BF16)
        o_ref[...] = x

    a = jnp.square(jnp.maximum(_dot(xn_ref[...], wu_ref[...]), 0.0))
    o_ref[...] += _dot(a.astype(BF16), wd_ref[...])


def mlp(x, g, w_up, w_down, *, tm, tf):
    m = x.shape[0]
    return pl.pallas_call(
        _mlp_kernel,
        grid=(m // tm, D_FF // tf),
        in_specs=[pl.BlockSpec((tm, D_MODEL), lambda i, j: (i, 0)),
                  pl.BlockSpec((1, D_MODEL), lambda i, j: (0, 0)),
                  pl.BlockSpec((D_MODEL, tf), lambda i, j: (0, j)),
                  pl.BlockSpec((tf, D_MODEL), lambda i, j: (j, 0))],
        out_specs=pl.BlockSpec((tm, D_MODEL), lambda i, j: (i, 0)),
        out_shape=jax.ShapeDtypeStruct((m, D_MODEL), F32),
        scratch_shapes=[pltpu.VMEM((tm, D_MODEL), BF16)],
        compiler_params=_cparams(("parallel", "arbitrary")),
        name="mlp",
    )(x, g.reshape(1, D_MODEL), w_up, w_down)


def _final_norm_kernel(x_ref, g_ref, o_ref):
    o_ref[...] = _rms(x_ref[...], g_ref[...])


def final_norm(x, g, *, tm):
    m = x.shape[0]
    return pl.pallas_call(
        _final_norm_kernel,
        grid=(m // tm,),
        in_specs=[pl.BlockSpec((tm, D_MODEL), lambda i: (i, 0)), pl.BlockSpec((1, D_MODEL), lambda i: (0, 0))],
        out_specs=pl.BlockSpec((tm, D_MODEL), lambda i: (i, 0)),
        out_shape=jax.ShapeDtypeStruct((m, D_MODEL), F32),
        compiler_params=_cparams(("parallel",)),
        name="final_norm",
    )(x, g.reshape(1, D_MODEL))


def _prompt_attn_kernel(q_ref, k_ref, v_ref, o_ref):
    scale = XA_HEAD_DIM ** -0.5
    for h in range(XA_HEADS):
        cs = slice(h * XA_HEAD_DIM, (h + 1) * XA_HEAD_DIM)
        kh = k_ref[0, :, cs].astype(BF16)
        vh = v_ref[0, :, cs].astype(BF16)
        s = lax.dot_general(q_ref[:, cs], kh, (((1,), (1,)), ((), ())), preferred_element_type=F32) * scale
        p = jnp.exp(s - jnp.max(s, axis=-1, keepdims=True))
        p = p / jnp.sum(p, axis=-1, keepdims=True)
        o_ref[:, cs] = _dot(p.astype(BF16), vh).astype(o_ref.dtype)


def prompt_attn(q, k, v, *, tq):
    nq = SEQ // tq
    return pl.pallas_call(
        _prompt_attn_kernel,
        grid=(BATCH, nq),
        in_specs=[pl.BlockSpec((tq, D_MODEL), lambda b, i: (b * nq + i, 0)),
                  pl.BlockSpec((1, N_MEM, D_MODEL), lambda b, i: (b, 0, 0)),
                  pl.BlockSpec((1, N_MEM, D_MODEL), lambda b, i: (b, 0, 0))],
        out_specs=pl.BlockSpec((tq, D_MODEL), lambda b, i: (b * nq + i, 0)),
        out_shape=jax.ShapeDtypeStruct((P_ROWS, D_MODEL), BF16),
        compiler_params=_cparams(("parallel", "arbitrary")),
        name="prompt_attn",
    )(q, k, v)


def _sample_attn_kernel(q_ref, k_ref, v_ref, o_ref):
    scale = XA_HEAD_DIM ** -0.5
    q = q_ref[0]
    for h in range(XA_HEADS):
        cs = slice(h * XA_HEAD_DIM, (h + 1) * XA_HEAD_DIM)
        s = jnp.sum(k_ref[0, :, cs] * q[:, cs], axis=-1, keepdims=True) * scale
        p = jnp.exp(s - jnp.max(s, axis=0, keepdims=True))
        p = p / jnp.sum(p, axis=0, keepdims=True)
        o_ref[0, :, cs] = jnp.sum(p * v_ref[0, :, cs], axis=0, keepdims=True).astype(o_ref.dtype)


def sample_attn(q, k, v):
    row = pl.BlockSpec((1, 1, D_MODEL), lambda b: (b, 0, 0))
    mem = pl.BlockSpec((1, N_MEM, D_MODEL), lambda b: (b, 0, 0))
    out = pl.pallas_call(
        _sample_attn_kernel,
        grid=(DEC_BATCH,),
        in_specs=[row, mem, mem],
        out_specs=row,
        out_shape=jax.ShapeDtypeStruct((DEC_BATCH, 1, D_MODEL), BF16),
        compiler_params=_cparams(("parallel",)),
        name="sample_attn",
    )(q.reshape(DEC_BATCH, 1, D_MODEL), k, v)
    return out.reshape(DEC_BATCH, D_MODEL)


def _s5_prep_kernel(lr_ref, li_ref, ldt_ref, bre_ref, bim_ref, bbr_ref, bbi_ref, pr_ref, pi_ref):
    lr, li = lr_ref[...], li_ref[...]
    delta = jnp.exp(ldt_ref[...])
    mag = jnp.exp(lr * delta)
    ar = mag * jnp.cos(li * delta)
    ai = mag * jnp.sin(li * delta)
    den = lr * lr + li * li
    cr = ((ar - 1.0) * lr + ai * li) / den
    ci = (ai * lr - (ar - 1.0) * li) / den
    br, bi = bre_ref[...], bim_ref[...]
    bbr_ref[...] = cr * br - ci * bi
    bbi_ref[...] = cr * bi + ci * br
    qr, qi = ar, ai
    pr_ref[0] = qr
    pi_ref[0] = qi
    for e in range(1, SUBLANE):
        qr, qi = qr * ar - qi * ai, qr * ai + qi * ar
        pr_ref[e] = qr
        pi_ref[e] = qi


def s5_prepare(lam_re, lam_im, log_dt, b_re, b_im, c_re, c_im):
    gk = S5_GROUPS * S5_CH
    rep = lambda a: jnp.repeat(a, S5_CH, axis=0)
    to_rows = lambda b: jnp.transpose(b, (0, 2, 1)).reshape(gk, S5_STATE)
    shp = jax.ShapeDtypeStruct((gk, S5_STATE), F32)
    pshp = jax.ShapeDtypeStruct((SUBLANE, gk, S5_STATE), F32)
    bbr, bbi, pr, pi = pl.pallas_call(
        _s5_prep_kernel, out_shape=(shp, shp, pshp, pshp), name="s5_prep",
    )(rep(lam_re), rep(lam_im), rep(jnp.broadcast_to(log_dt[:, None], (S5_GROUPS, S5_STATE))),
      to_rows(b_re), to_rows(b_im))

    gpb = LANE // S5_CH
    nblk = S5_GROUPS // gpb
    eye = jnp.eye(gpb, dtype=F32)

    def b_blocks(bb):
        bb = bb.reshape(nblk, gpb, S5_CH, S5_STATE)
        return jnp.einsum("igkn,gh->igkhn", bb, eye).reshape(nblk, LANE, gpb * S5_STATE).astype(BF16)

    def c_blocks(cc):
        cc = cc.reshape(nblk, gpb, S5_CH, S5_STATE)
        return jnp.einsum("igkn,gh->ignhk", cc, eye).reshape(nblk, gpb * S5_STATE, LANE).astype(BF16)

    pw_r = pr[:, ::S5_CH, :].reshape(SUBLANE, S5_LANES)
    pw_i = pi[:, ::S5_CH, :].reshape(SUBLANE, S5_LANES)
    row = jnp.arange(SUBLANE)[:, None]
    consts = []
    for d in (1, 2, 4):
        consts.append(jnp.where(row >= d, pw_r[d - 1][None, :], 0.0))
        consts.append(jnp.where(row >= d, pw_i[d - 1][None, :], 0.0))
    consts += [pw_r, pw_i]
    return dict(b_re=b_blocks(bbr), b_im=b_blocks(bbi), c_re=c_blocks(c_re), c_imn=c_blocks(-c_im),
                scan=jnp.stack(consts), a_re=pw_r[0:1], a_im=pw_i[0:1])


def _s5_bu(ub, bre_ref, bim_ref):
    res_r, res_i = [], []
    for i in range(GROUP_WIDTH // LANE):
        ui = ub[:, i * LANE:(i + 1) * LANE]
        res_r.append(_dot(ui, bre_ref[i]))
        res_i.append(_dot(ui, bim_ref[i]))
    return res_r, res_i


def _s5_tail(u, hs_re, hs_im, cre_ref, cimn_ref, d_ref, wglu_ref, bglu_ref, g_ref):
    ys = []
    for i in range(GROUP_WIDTH // LANE):
        ys.append(_dot(hs_re(i).astype(BF16), cre_ref[i]) + _dot(hs_im(i).astype(BF16), cimn_ref[i]))
    y = jnp.concatenate(ys, axis=-1) + d_ref[...] * u
    y = _gelu_tanh(y)
    y = y * _sigmoid(_dot(y.astype(BF16), wglu_ref[...]) + bglu_ref[...])
    return _rms(y, g_ref[...])


def _s5_param_specs(zero_map3, zero_map2):
    nblk = GROUP_WIDTH // LANE
    sblk = S5_LANES // nblk
    return [pl.BlockSpec((nblk, LANE, sblk), zero_map3),
            pl.BlockSpec((nblk, LANE, sblk), zero_map3),
            pl.BlockSpec((nblk, sblk, LANE), zero_map3),
            pl.BlockSpec((nblk, sblk, LANE), zero_map3),
            pl.BlockSpec((1, GROUP_WIDTH), zero_map2),
            pl.BlockSpec((GROUP_WIDTH, GROUP_WIDTH), zero_map2),
            pl.BlockSpec((1, GROUP_WIDTH), zero_map2),
            pl.BlockSpec((1, GROUP_WIDTH), zero_map2)]


def _prompt_s5_kernel(u_ref, scan_ref, bre_ref, bim_ref, cre_ref, cimn_ref, d_ref, wglu_ref, bglu_ref, g_ref,
                      mix_ref, sre_ref, sim_ref, hre_ref, him_ref, cr_ref, ci_ref):
    rows = u_ref.shape[0]
    sblk = S5_LANES // (GROUP_WIDTH // LANE)

    @pl.when(pl.program_id(1) == 0)
    def _():
        cr_ref[...] = jnp.zeros_like(cr_ref)
        ci_ref[...] = jnp.zeros_like(ci_ref)

    u = u_ref[...]
    bu_r, bu_i = _s5_bu(u.astype(BF16), bre_ref, bim_ref)
    for i in range(len(bu_r)):
        hre_ref[:, i * sblk:(i + 1) * sblk] = bu_r[i]
        him_ref[:, i * sblk:(i + 1) * sblk] = bu_i[i]

    for lb in range(0, S5_LANES, S5_SCAN_LANES):
        ls = slice(lb, lb + S5_SCAN_LANES)
        k = [scan_ref[i, :, ls] for i in range(8)]

        def tile_step(ti, carry, ls=ls, k=k):
            c_r, c_i = carry
            rs = pl.ds(pl.multiple_of(ti * SUBLANE, SUBLANE), SUBLANE)
            x_r, x_i = hre_ref[rs, ls], him_ref[rs, ls]
            for s, d in enumerate((1, 2, 4)):
                a_r, a_i = k[2 * s], k[2 * s + 1]
                s_r = pltpu.roll(x_r, d, 0)
                s_i = pltpu.roll(x_i, d, 0)
                x_r, x_i = x_r + a_r * s_r - a_i * s_i, x_i + a_r * s_i + a_i * s_r
            h_r = x_r + k[6] * c_r - k[7] * c_i
            h_i = x_i + k[6] * c_i + k[7] * c_r
            hre_ref[rs, ls] = h_r
            him_ref[rs, ls] = h_i
            bshape = (SUBLANE, S5_SCAN_LANES)
            return (jnp.broadcast_to(h_r[SUBLANE - 1:SUBLANE, :], bshape),
                    jnp.broadcast_to(h_i[SUBLANE - 1:SUBLANE, :], bshape))

        c_r, c_i = lax.fori_loop(0, rows // SUBLANE, tile_step, (cr_ref[:, ls], ci_ref[:, ls]))
        cr_ref[:, ls] = c_r
        ci_ref[:, ls] = c_i

    sre_ref[0] = cr_ref[0:1, :]
    sim_ref[0] = ci_ref[0:1, :]
    y = _s5_tail(u, lambda i: hre_ref[:, i * sblk:(i + 1) * sblk], lambda i: him_ref[:, i * sblk:(i + 1) * sblk],
                 cre_ref, cimn_ref, d_ref, wglu_ref, bglu_ref, g_ref)
    mix_ref[...] = y.astype(mix_ref.dtype)


def prompt_s5(proj, s5p, d, w_glu, b_glu, g):
    nt = SEQ // S5_CHUNK
    z3 = lambda b, t: (0, 0, 0)
    z2 = lambda b, t: (0, 0)
    st = jax.ShapeDtypeStruct((BATCH, 1, S5_LANES), F32)
    st_spec = pl.BlockSpec((1, 1, S5_LANES), lambda b, t: (b, 0, 0))
    return pl.pallas_call(
        _prompt_s5_kernel,
        grid=(BATCH, nt),
        in_specs=[pl.BlockSpec((S5_CHUNK, GROUP_WIDTH), lambda b, t: (b * nt + t, COL_U5 // GROUP_WIDTH)),
                  pl.BlockSpec((8, SUBLANE, S5_LANES), z3)] + _s5_param_specs(z3, z2),
        out_specs=(pl.BlockSpec((S5_CHUNK, GROUP_WIDTH), lambda b, t: (b * nt + t, 0)), st_spec, st_spec),
        out_shape=(jax.ShapeDtypeStruct((P_ROWS, GROUP_WIDTH), BF16), st, st),
        scratch_shapes=[pltpu.VMEM((S5_CHUNK, S5_LANES), F32), pltpu.VMEM((S5_CHUNK, S5_LANES), F32),
                        pltpu.VMEM((SUBLANE, S5_LANES), F32), pltpu.VMEM((SUBLANE, S5_LANES), F32)],
        compiler_params=_cparams(("parallel", "arbitrary")),
        name="prompt_s5",
    )(proj, s5p["scan"], s5p["b_re"], s5p["b_im"], s5p["c_re"], s5p["c_imn"],
      d.reshape(1, -1), w_glu, b_glu.reshape(1, -1), g.reshape(1, -1))


def _prompt_ssd_kernel(xbc_ref, z_ref, dt_ref, cw_ref, cb_ref, dtb_ref, alog_ref, dexp_ref, g_ref,
                       mix_ref, cst_ref, hst_ref, xbuf_ref, h_ref):
    L = SSD_CHUNK
    hist = SSD_CONV - 1
    base = SUBLANE

    @pl.when(pl.program_id(1) == 0)
    def _():
        xbuf_ref[0:base, :] = jnp.zeros((base, XBC_WIDTH), F32)
        h_ref[...] = jnp.zeros_like(h_ref)

    @pl.when(pl.program_id(1) > 0)
    def _():
        xbuf_ref[base - hist:base, :] = xbuf_ref[base + L - hist:base + L, :]

    xbuf_ref[base:base + L, :] = xbc_ref[...]
    cst_ref[0] = xbuf_ref[base + L - hist:base + L, :]
    conv = cb_ref[...] + cw_ref[hist:hist + 1, :] * xbc_ref[...]
    for k in range(hist):
        conv = conv + cw_ref[k:k + 1, :] * xbuf_ref[base - hist + k:base - hist + k + L, :]
    xc = _silu(conv)
    xs = xc[:, :GROUP_WIDTH]
    ng = SSD_GROUPS * SSD_STATE
    bm = xc[:, GROUP_WIDTH:GROUP_WIDTH + ng].astype(BF16)
    cm = xc[:, GROUP_WIDTH + ng:].astype(BF16)

    dt = _softplus(dt_ref[...] + dtb_ref[...])
    a = -jnp.exp(alog_ref[...])
    ri = lax.broadcasted_iota(jnp.int32, (L, L), 0)
    ci = lax.broadcasted_iota(jnp.int32, (L, L), 1)
    causal = ri >= ci
    acum = jnp.dot(causal.astype(F32), dt * a, preferred_element_type=F32, precision=lax.Precision.HIGHEST)
    acum_t = acum.T
    last = acum[L - 1:L, :]
    to_end = jnp.exp(last - acum)
    e_acum = jnp.exp(acum)
    chunk_decay = jnp.exp(last)

    ys = []
    rep = SSD_HEADS // SSD_GROUPS
    cb = [lax.dot_general(cm[:, g * SSD_STATE:(g + 1) * SSD_STATE], bm[:, g * SSD_STATE:(g + 1) * SSD_STATE],
                          (((1,), (1,)), ((), ())), preferred_element_type=F32) for g in range(SSD_GROUPS)]
    for h in range(SSD_HEADS):
        g = h // rep
        hs = slice(h * SSD_HEAD_DIM, (h + 1) * SSD_HEAD_DIM)
        gs = slice(g * SSD_STATE, (g + 1) * SSD_STATE)
        seg = acum[:, h:h + 1] - acum_t[h:h + 1, :]
        decay = jnp.exp(jnp.where(causal, seg, -jnp.inf))
        xs_h = xs[:, hs]
        xdt = xs_h * dt[:, h:h + 1]
        y = _dot((cb[g] * decay).astype(BF16), xdt.astype(BF16))
        h_prev = h_ref[h]
        y_off = lax.dot_general(cm[:, gs], h_prev.astype(BF16), (((1,), (1,)), ((), ())),
                                preferred_element_type=F32)
        y = y + y_off * e_acum[:, h:h + 1]
        st = lax.dot_general((xdt * to_end[:, h:h + 1]).astype(BF16), bm[:, gs], (((0,), (0,)), ((), ())),
                             preferred_element_type=F32)
        h_ref[h] = h_prev * chunk_decay[:, h:h + 1] + st
        ys.append(y)
    y = (jnp.concatenate(ys, axis=-1) + dexp_ref[...] * xs) * _silu(z_ref[...])
    mix_ref[...] = _rms(y, g_ref[...]).astype(mix_ref.dtype)
    hst_ref[0] = h_ref[...]


def prompt_ssd(proj, conv_w, conv_b, dt_bias, a_log, d, g):
    nc = SEQ // SSD_CHUNK
    z2 = lambda b, c: (0, 0)
    pad8 = lambda v: jnp.pad(v, (0, LANE - SSD_HEADS)).reshape(1, LANE)
    return pl.pallas_call(
        _prompt_ssd_kernel,
        grid=(BATCH, nc),
        in_specs=[pl.BlockSpec((SSD_CHUNK, XBC_WIDTH), lambda b, c: (b * nc + c, COL_XBC // XBC_WIDTH)),
                  pl.BlockSpec((SSD_CHUNK, GROUP_WIDTH), lambda b, c: (b * nc + c, COL_Z // GROUP_WIDTH)),
                  pl.BlockSpec((SSD_CHUNK, LANE), lambda b, c: (b * nc + c, COL_DT // LANE)),
                  pl.BlockSpec((SSD_CONV, XBC_WIDTH), z2),
                  pl.BlockSpec((1, XBC_WIDTH), z2),
                  pl.BlockSpec((1, LANE), z2),
                  pl.BlockSpec((1, LANE), z2),
                  pl.BlockSpec((1, GROUP_WIDTH), z2),
                  pl.BlockSpec((1, GROUP_WIDTH), z2)],
        out_specs=(pl.BlockSpec((SSD_CHUNK, GROUP_WIDTH), lambda b, c: (b * nc + c, 0)),
                   pl.BlockSpec((1, SSD_CONV - 1, XBC_WIDTH), lambda b, c: (b, 0, 0)),
                   pl.BlockSpec((1, SSD_HEADS, SSD_HEAD_DIM, SSD_STATE), lambda b, c: (b, 0, 0, 0))),
        out_shape=(jax.ShapeDtypeStruct((P_ROWS, GROUP_WIDTH), BF16),
                   jax.ShapeDtypeStruct((BATCH, SSD_CONV - 1, XBC_WIDTH), F32),
                   jax.ShapeDtypeStruct((BATCH, SSD_HEADS, SSD_HEAD_DIM, SSD_STATE), F32)),
        scratch_shapes=[pltpu.VMEM((SUBLANE + SSD_CHUNK, XBC_WIDTH), F32),
                        pltpu.VMEM((SSD_HEADS, SSD_HEAD_DIM, SSD_STATE), F32)],
        compiler_params=_cparams(("parallel", "arbitrary")),
        name="prompt_ssd",
    )(proj, proj, proj, conv_w, conv_b.reshape(1, -1), pad8(dt_bias), pad8(a_log),
      jnp.repeat(d, SSD_HEAD_DIM).reshape(1, -1), g.reshape(1, -1))


def _pool_counts(pos, w):
    return jnp.minimum(w, pos + 1).astype(F32)


def _prompt_convpool_kernel(gb_ref, gc_ref, hv_ref, up_ref, scw_ref, pw_ref, ps_ref, gsc_ref, gpl_ref,
                            msc_ref, mpl_ref, scst_ref, plst_ref, vbuf_ref, pbuf_ref):
    rows = gb_ref.shape[0]
    vb = SUBLANE
    pb = 2 * SUBLANE
    vh = SC_CONV - 1
    t = pl.program_id(1)

    @pl.when(t == 0)
    def _():
        vbuf_ref[0:vb, :] = jnp.zeros((vb, GROUP_WIDTH), F32)
        pbuf_ref[0:pb, :] = jnp.zeros((pb, GROUP_WIDTH), F32)

    @pl.when(t > 0)
    def _():
        vbuf_ref[vb - vh:vb, :] = vbuf_ref[vb + rows - vh:vb + rows, :]
        pbuf_ref[0:pb, :] = pbuf_ref[rows:rows + pb, :]

    v = gc_ref[...] * hv_ref[...]
    vbuf_ref[vb:vb + rows, :] = v
    acc = scw_ref[vh:vh + 1, :] * v
    for k in range(vh):
        acc = acc + scw_ref[k:k + 1, :] * vbuf_ref[vb - vh + k:vb - vh + k + rows, :]
    msc_ref[...] = _rms(gb_ref[...] * acc, gsc_ref[...]).astype(msc_ref.dtype)
    scst_ref[0] = vbuf_ref[vb + rows - vh:vb + rows, :]

    u = up_ref[...]
    pbuf_ref[pb:pb + rows, :] = u
    pos = t * rows + lax.broadcasted_iota(jnp.int32, (rows, 1), 0)
    ys = []
    for gi, w in enumerate(POOL_WINDOWS):
        cs = slice(gi * POOL_GROUP, (gi + 1) * POOL_GROUP)
        s = u[:, cs]
        for j in range(1, w):
            s = s + pbuf_ref[pb - j:pb - j + rows, cs]
        pooled = s / _pool_counts(pos, w) - u[:, cs]
        ys.append(_dot(pooled.astype(BF16), pw_ref[gi]))
    y = jnp.concatenate(ys, axis=-1) * ps_ref[...]
    mpl_ref[...] = _rms(y, gpl_ref[...]).astype(mpl_ref.dtype)
    plst_ref[0] = pbuf_ref[pb + rows - POOL_HIST:pb + rows, :]


def prompt_convpool(proj, sc_w, pool_w, pool_scale, g_sc, g_pool):
    nt = SEQ // S5_CHUNK
    z2 = lambda b, t: (0, 0)
    col = lambda c: pl.BlockSpec((S5_CHUNK, GROUP_WIDTH), lambda b, t: (b * nt + t, c // GROUP_WIDTH))
    mix_spec = pl.BlockSpec((S5_CHUNK, GROUP_WIDTH), lambda b, t: (b * nt + t, 0))
    mix_shape = jax.ShapeDtypeStruct((P_ROWS, GROUP_WIDTH), BF16)
    return pl.pallas_call(
        _prompt_convpool_kernel,
        grid=(BATCH, nt),
        in_specs=[col(COL_GB), col(COL_GC), col(COL_HV), col(COL_UP),
                  pl.BlockSpec((SC_CONV, GROUP_WIDTH), z2),
                  pl.BlockSpec((len(POOL_WINDOWS), POOL_GROUP, POOL_GROUP), lambda b, t: (0, 0, 0)),
                  pl.BlockSpec((1, GROUP_WIDTH), z2), pl.BlockSpec((1, GROUP_WIDTH), z2),
                  pl.BlockSpec((1, GROUP_WIDTH), z2)],
        out_specs=(mix_spec, mix_spec,
                   pl.BlockSpec((1, SC_CONV - 1, GROUP_WIDTH), lambda b, t: (b, 0, 0)),
                   pl.BlockSpec((1, POOL_HIST, GROUP_WIDTH), lambda b, t: (b, 0, 0))),
        out_shape=(mix_shape, mix_shape,
                   jax.ShapeDtypeStruct((BATCH, SC_CONV - 1, GROUP_WIDTH), F32),
                   jax.ShapeDtypeStruct((BATCH, POOL_HIST, GROUP_WIDTH), F32)),
        scratch_shapes=[pltpu.VMEM((SUBLANE + S5_CHUNK, GROUP_WIDTH), F32),
                        pltpu.VMEM((2 * SUBLANE + S5_CHUNK, GROUP_WIDTH), F32)],
        compiler_params=_cparams(("parallel", "arbitrary")),
        name="prompt_convpool",
    )(proj, proj, proj, proj, sc_w, pool_w, pool_scale.reshape(1, -1), g_sc.reshape(1, -1), g_pool.reshape(1, -1))


def _sample_mix_kernel(proj_ref, cprev_ref, s5r_ref, s5i_ref, scprev_ref, plprev_ref,
                       cw_ref, cb_ref, dtb_ref, alog_ref,
                       ar_ref, ai_ref, bre_ref, bim_ref, cre_ref, cimn_ref, d5_ref, wglu_ref, bglu_ref, g5_ref,
                       scw_ref, pw_ref, ps_ref, gsc_ref, gpl_ref,
                       m5_ref, msc_ref, mpl_ref, cst_ref, s5ro_ref, s5io_ref, scst_ref, plst_ref,
                       xs_ref, xdt_ref, da_ref, b2_ref, c2_ref):
    nb = proj_ref.shape[0]
    W = GROUP_WIDTH
    xbc = proj_ref[:, COL_XBC:COL_XBC + XBC_WIDTH]
    hist = SSD_CONV - 1
    conv = cb_ref[...] + cw_ref[hist:hist + 1, :] * xbc
    for k in range(hist):
        conv = conv + cw_ref[k:k + 1, :] * cprev_ref[:, k * XBC_WIDTH:(k + 1) * XBC_WIDTH]
    cst_ref[:, 0:(hist - 1) * XBC_WIDTH] = cprev_ref[:, XBC_WIDTH:hist * XBC_WIDTH]
    cst_ref[:, (hist - 1) * XBC_WIDTH:hist * XBC_WIDTH] = xbc
    xc = _silu(conv)
    xs = xc[:, :W]
    bm = xc[:, W:W + LANE]
    cm = xc[:, W + LANE:W + 2 * LANE]
    dt = _softplus(proj_ref[:, COL_DT:COL_DT + LANE] + dtb_ref[...])
    da = jnp.exp(dt * (-jnp.exp(alog_ref[...])))
    lane_w = lax.broadcasted_iota(jnp.int32, (nb, W), 1)
    dt_exp = jnp.zeros((nb, W), F32)
    for h in range(SSD_HEADS):
        dt_exp = jnp.where(lane_w // SSD_HEAD_DIM == h, dt[:, h:h + 1], dt_exp)
        da_ref[:, h * LANE:(h + 1) * LANE] = jnp.broadcast_to(da[:, h:h + 1], (nb, LANE))
    xs_ref[...] = xs
    xdt_ref[...] = xs * dt_exp
    lane = lax.broadcasted_iota(jnp.int32, (nb, LANE), 1)
    low = lane < SSD_STATE
    for src, dst in ((bm, b2_ref), (cm, c2_ref)):
        swapped = pltpu.roll(src, SSD_STATE, 1)
        dst[:, 0:LANE] = jnp.where(low, src, swapped)
        dst[:, LANE:2 * LANE] = jnp.where(low, swapped, src)

    u5 = proj_ref[:, COL_U5:COL_U5 + W]
    bu_r, bu_i = _s5_bu(u5.astype(BF16), bre_ref, bim_ref)
    sblk = S5_LANES // len(bu_r)
    for i in range(len(bu_r)):
        ls = slice(i * sblk, (i + 1) * sblk)
        h_r, h_i, a_r, a_i = s5r_ref[:, ls], s5i_ref[:, ls], ar_ref[:, ls], ai_ref[:, ls]
        s5ro_ref[:, ls] = bu_r[i] + a_r * h_r - a_i * h_i
        s5io_ref[:, ls] = bu_i[i] + a_r * h_i + a_i * h_r
    y5 = _s5_tail(u5, lambda i: s5ro_ref[:, i * sblk:(i + 1) * sblk], lambda i: s5io_ref[:, i * sblk:(i + 1) * sblk],
                  cre_ref, cimn_ref, d5_ref, wglu_ref, bglu_ref, g5_ref)
    m5_ref[...] = y5.astype(m5_ref.dtype)

    v = proj_ref[:, COL_GC:COL_GC + W] * proj_ref[:, COL_HV:COL_HV + W]
    vh = SC_CONV - 1
    acc = scw_ref[vh:vh + 1, :] * v
    for k in range(vh):
        acc = acc + scw_ref[k:k + 1, :] * scprev_ref[:, k * W:(k + 1) * W]
    msc_ref[...] = _rms(proj_ref[:, COL_GB:COL_GB + W] * acc, gsc_ref[...]).astype(msc_ref.dtype)
    scst_ref[:, 0:(vh - 1) * W] = scprev_ref[:, W:vh * W]
    scst_ref[:, (vh - 1) * W:vh * W] = v

    up = proj_ref[:, COL_UP:COL_UP + W]
    ys = []
    for gi, w in enumerate(POOL_WINDOWS):
        cs = slice(gi * POOL_GROUP, (gi + 1) * POOL_GROUP)
        s = up[:, cs]
        for j in range(1, w):
            k = POOL_HIST - j
            s = s + plprev_ref[:, k * W + gi * POOL_GROUP:k * W + (gi + 1) * POOL_GROUP]
        pooled = s / float(min(w, PAST_LEN + 1)) - up[:, cs]
        ys.append(_dot(pooled.astype(BF16), pw_ref[gi]))
    y = jnp.concatenate(ys, axis=-1) * ps_ref[...]
    mpl_ref[...] = _rms(y, gpl_ref[...]).astype(mpl_ref.dtype)
    plst_ref[:, 0:(POOL_HIST - 1) * W] = plprev_ref[:, W:POOL_HIST * W]
    plst_ref[:, (POOL_HIST - 1) * W:POOL_HIST * W] = up


def sample_mix(proj, st, lp, s5p):
    nb = DEC_BATCH
    W = GROUP_WIDTH
    conv_prev, s5_re, s5_im, sc_prev, pool_prev = st
    pad8 = lambda v: jnp.pad(v, (0, LANE - SSD_HEADS)).reshape(1, LANE)
    row = lambda v: v.reshape(1, -1)
    f = lambda *s: jax.ShapeDtypeStruct(s, F32)
    b = lambda *s: jax.ShapeDtypeStruct(s, BF16)
    out_shape = (b(nb, W), b(nb, W), b(nb, W),
                 f(nb, (SSD_CONV - 1) * XBC_WIDTH), f(nb, S5_LANES), f(nb, S5_LANES),
                 f(nb, (SC_CONV - 1) * W), f(nb, POOL_HIST * W),
                 f(nb, W), f(nb, W), f(nb, SSD_HEADS * LANE), f(nb, SSD_GROUPS * LANE), f(nb, SSD_GROUPS * LANE))
    return pl.pallas_call(
        _sample_mix_kernel, out_shape=out_shape, compiler_params=pltpu.CompilerParams(vmem_limit_bytes=VMEM_LIMIT),
        name="sample_mix",
    )(proj, conv_prev.reshape(nb, -1), s5_re.reshape(nb, -1), s5_im.reshape(nb, -1),
      sc_prev.reshape(nb, -1), pool_prev.reshape(nb, -1),
      lp["ssd_conv_w"], row(lp["ssd_conv_b"]), pad8(lp["ssd_dt_bias"]), pad8(lp["ssd_a_log"]),
      s5p["a_re"], s5p["a_im"], s5p["b_re"], s5p["b_im"], s5p["c_re"], s5p["c_imn"],
      row(lp["s5_d"]), lp["s5_w_glu"], row(lp["s5_b_glu"]), row(lp["g_s5"]),
      lp["sc_conv_w"], lp["pool_w"], row(lp["pool_scale"]), row(lp["g_sc"]), row(lp["g_pool"]))


def _sample_ssd_kernel(h0_ref, xdt_ref, da_ref, b2_ref, c2_ref, e_ref, r_ref, xs_ref, z_ref, dexp_ref, g_ref,
                       hn_ref, mix_ref, xrep_ref, prod_ref, y_ref):
    hp = pl.program_id(0)
    hpl = 2 * SSD_HEAD_DIM * SSD_STATE
    per_head = SSD_HEAD_DIM * SSD_STATE
    x_hi, x_lo = _split_bf16(xdt_ref[...])
    xrep_ref[...] = _dot(x_hi, e_ref[...]) + _dot(x_lo, e_ref[...])
    b2, c2 = b2_ref[...], c2_ref[...]
    for j in range(hpl // LANE):
        ls = slice(j * LANE, (j + 1) * LANE)
        hl = (j * LANE) // per_head
        hn = da_ref[:, hl * LANE:(hl + 1) * LANE] * h0_ref[:, ls] + xrep_ref[:, ls] * b2
        hn_ref[:, ls] = hn
        prod_ref[:, ls] = hn * c2
    p_hi, p_lo = _split_bf16(prod_ref[...])
    y_ref[hp] = _dot(p_hi, r_ref[...]) + _dot(p_lo, r_ref[...])

    @pl.when(hp == pl.num_programs(0) - 1)
    def _():
        y = jnp.concatenate([y_ref[i] for i in range(SSD_HEADS // 2)], axis=-1)
        y = (y + dexp_ref[...] * xs_ref[...]) * _silu(z_ref[...])
        mix_ref[...] = _rms(y, g_ref[...]).astype(mix_ref.dtype)


def sample_ssd(h0, xdt, da, b2, c2, xs, proj, d, g):
    nb = DEC_BATCH
    npairs = SSD_HEADS // 2
    hpl = 2 * SSD_HEAD_DIM * SSD_STATE
    expand = jnp.repeat(jnp.eye(LANE, dtype=BF16), SSD_STATE, axis=1)
    z1 = lambda i: (0, 0)
    hn, mix = pl.pallas_call(
        _sample_ssd_kernel,
        grid=(npairs,),
        in_specs=[pl.BlockSpec((nb, hpl), lambda i: (0, i)),
                  pl.BlockSpec((nb, LANE), lambda i: (0, i)),
                  pl.BlockSpec((nb, 2 * LANE), lambda i: (0, i)),
                  pl.BlockSpec((nb, LANE), lambda i: (0, i // (npairs // SSD_GROUPS))),
                  pl.BlockSpec((nb, LANE), lambda i: (0, i // (npairs // SSD_GROUPS))),
                  pl.BlockSpec((LANE, hpl), z1),
                  pl.BlockSpec((hpl, LANE), z1),
                  pl.BlockSpec((nb, GROUP_WIDTH), z1),
                  pl.BlockSpec((nb, GROUP_WIDTH), lambda i: (0, COL_Z // GROUP_WIDTH)),
                  pl.BlockSpec((1, GROUP_WIDTH), z1),
                  pl.BlockSpec((1, GROUP_WIDTH), z1)],
        out_specs=(pl.BlockSpec((nb, hpl), lambda i: (0, i)),
                   pl.BlockSpec((nb, GROUP_WIDTH), z1)),
        out_shape=(jax.ShapeDtypeStruct((nb, SSD_HEADS * SSD_HEAD_DIM * SSD_STATE), F32),
                   jax.ShapeDtypeStruct((nb, GROUP_WIDTH), BF16)),
        scratch_shapes=[pltpu.VMEM((nb, hpl), F32), pltpu.VMEM((nb, hpl), F32),
                        pltpu.VMEM((npairs, nb, LANE), F32)],
        compiler_params=_cparams(("arbitrary",)),
        name="sample_ssd",
    )(h0.reshape(nb, -1), xdt, da, b2, c2, expand, expand.T, xs, proj,
      jnp.repeat(d, SSD_HEAD_DIM).reshape(1, -1), g.reshape(1, -1))
    return hn.reshape(nb, SSD_HEADS, SSD_HEAD_DIM, SSD_STATE), mix


def _regroup_w_in(w_in):
    sizes = (GROUP_WIDTH, XBC_WIDTH, SSD_HEADS, GROUP_WIDTH, GROUP_WIDTH, GROUP_WIDTH, GROUP_WIDTH, GROUP_WIDTH)
    offs = np.concatenate([[0], np.cumsum(sizes)])
    z, xbc, dt, u5, gb, gc, hv, up = [w_in[:, int(offs[i]):int(offs[i + 1])] for i in range(len(sizes))]
    pad = jnp.zeros((D_MODEL, IN_PAD - COL_DT - SSD_HEADS), w_in.dtype)
    return jnp.concatenate([z, u5, gb, gc, hv, up, xbc, dt, pad], axis=1).astype(BF16)


def _layer_params(l, w):
    g_mix = w["mix_out_g"][l].reshape(4, GROUP_WIDTH)
    lp = {k: v[l] for k, v in w.items()}
    lp.update(g_ssd=g_mix[0], g_s5=g_mix[1], g_sc=g_mix[2], g_pool=g_mix[3])
    lp["w_in"] = _regroup_w_in(lp["w_in"])
    for k in ("w_out", "w_q", "w_k", "w_v", "w_o", "w_up", "w_down", "s5_w_glu", "pool_w"):
        lp[k] = lp[k].astype(BF16)
    return lp


P_TM = 1024
P_TN = 1024


def _dense_tail(h, mixes, lp, attn_fn, *, tm, tn, tf, q_dtype):
    h = out_proj(mixes, lp["w_out"], h, tm=tm, tn=tn)
    q = norm_matmul(h, lp["norm_xa_g"], lp["w_q"], out_dtype=q_dtype, tm=tm, tn=tn)
    o = attn_fn(q)
    h = norm_matmul(o, lp["norm_xa_g"], lp["w_o"], res=h, norm=False, tm=tm, tn=tn)
    return mlp(h, lp["norm_mlp_g"], lp["w_up"], lp["w_down"], tm=tm, tf=tf)


def _forward(x_prompt, x_sample, mem_prompt, state_ssd_conv, state_ssd, state_s5_re, state_s5_im,
             state_sconv, state_pool, cache_mem_k, cache_mem_v, final_norm_g, w):
    hp = x_prompt.reshape(P_ROWS, D_MODEL)
    hs = x_sample.reshape(DEC_BATCH, D_MODEL)
    mem = mem_prompt.reshape(BATCH * N_MEM, D_MODEL)
    p_out = [[] for _ in range(8)]
    s_out = [[] for _ in range(6)]
    for l in range(DEPTH):
        lp = _layer_params(l, w)
        s5p = s5_prepare(lp["s5_lam_re"], lp["s5_lam_im"], lp["s5_log_dt"], lp["s5_b_re"], lp["s5_b_im"],
                         lp["s5_c_re"], lp["s5_c_im"])
        mk = norm_matmul(mem, lp["norm_mem_g"], lp["w_k"], tm=BATCH * N_MEM, tn=P_TN)
        mv = norm_matmul(mem, lp["norm_mem_g"], lp["w_v"], tm=BATCH * N_MEM, tn=P_TN)
        mk3, mv3 = mk.reshape(BATCH, N_MEM, D_MODEL), mv.reshape(BATCH, N_MEM, D_MODEL)

        proj = norm_matmul(hp, lp["norm_mix_g"], lp["w_in"], tm=P_TM, tn=P_TN)
        m_ssd, p_conv, p_ssd = prompt_ssd(proj, lp["ssd_conv_w"], lp["ssd_conv_b"], lp["ssd_dt_bias"],
                                          lp["ssd_a_log"], lp["ssd_d"], lp["g_ssd"])
        m_s5, p_s5r, p_s5i = prompt_s5(proj, s5p, lp["s5_d"], lp["s5_w_glu"], lp["s5_b_glu"], lp["g_s5"])
        m_sc, m_pl, p_sc, p_pl = prompt_convpool(proj, lp["sc_conv_w"], lp["pool_w"], lp["pool_scale"],
                                                 lp["g_sc"], lp["g_pool"])
        hp = _dense_tail(hp, (m_ssd, m_s5, m_sc, m_pl), lp,
                         lambda q: prompt_attn(q, mk3, mv3, tq=512),
                         tm=P_TM, tn=P_TN, tf=512, q_dtype=BF16)
        for lst, val in zip(p_out, (p_conv, p_ssd, p_s5r.reshape(BATCH, S5_GROUPS, S5_STATE),
                                    p_s5i.reshape(BATCH, S5_GROUPS, S5_STATE), p_sc, p_pl,
                                    mk.reshape(BATCH, N_MEM, XA_HEADS, XA_HEAD_DIM),
                                    mv.reshape(BATCH, N_MEM, XA_HEADS, XA_HEAD_DIM))):
            lst.append(val)

        proj_s = norm_matmul(hs, lp["norm_mix_g"], lp["w_in"], tm=DEC_BATCH, tn=P_TN)
        (m5, msc, mpl, s_conv, s_s5r, s_s5i, s_sc, s_pl, xs, xdt, da, b2, c2) = sample_mix(
            proj_s, (state_ssd_conv[l], state_s5_re[l], state_s5_im[l], state_sconv[l], state_pool[l]), lp, s5p)
        s_ssd, mssd = sample_ssd(state_ssd[l], xdt, da, b2, c2, xs, proj_s, lp["ssd_d"], lp["g_ssd"])
        ck = cache_mem_k[l].reshape(DEC_BATCH, N_MEM, D_MODEL)
        cv = cache_mem_v[l].reshape(DEC_BATCH, N_MEM, D_MODEL)
        hs = _dense_tail(hs, (mssd, m5, msc, mpl), lp, lambda q: sample_attn(q, ck, cv),
                         tm=DEC_BATCH, tn=P_TN, tf=1024, q_dtype=F32)
        for lst, val in zip(s_out, (s_conv.reshape(DEC_BATCH, SSD_CONV - 1, XBC_WIDTH), s_ssd,
                                    s_s5r.reshape(DEC_BATCH, S5_GROUPS, S5_STATE),
                                    s_s5i.reshape(DEC_BATCH, S5_GROUPS, S5_STATE),
                                    s_sc.reshape(DEC_BATCH, SC_CONV - 1, GROUP_WIDTH),
                                    s_pl.reshape(DEC_BATCH, POOL_HIST, GROUP_WIDTH))):
            lst.append(val)

    y_prompt = final_norm(hp, final_norm_g, tm=P_TM).reshape(BATCH, SEQ, D_MODEL)
    y_sample = final_norm(hs, final_norm_g, tm=DEC_BATCH).reshape(DEC_BATCH, 1, D_MODEL)
    return (y_prompt, y_sample) + tuple(jnp.stack(c) for c in p_out) + tuple(jnp.stack(c) for c in s_out)


_forward_jit = jax.jit(_forward)


def kernel(x_prompt, x_sample, mem_prompt, state_ssd_conv, state_ssd, state_s5_re, state_s5_im, state_sconv, state_pool, cache_mem_k, cache_mem_v, norm_mix_g, w_in, ssd_conv_w, ssd_conv_b, ssd_dt_bias, ssd_a_log, ssd_d, s5_lam_re, s5_lam_im, s5_log_dt, s5_b_re, s5_b_im, s5_c_re, s5_c_im, s5_d, s5_w_glu, s5_b_glu, sc_conv_w, pool_w, pool_scale, mix_out_g, w_out, norm_xa_g, norm_mem_g, w_q, w_k, w_v, w_o, norm_mlp_g, w_up, w_down, final_norm_g):
    w = dict(norm_mix_g=norm_mix_g, w_in=w_in, ssd_conv_w=ssd_conv_w, ssd_conv_b=ssd_conv_b,
             ssd_dt_bias=ssd_dt_bias, ssd_a_log=ssd_a_log, ssd_d=ssd_d, s5_lam_re=s5_lam_re,
             s5_lam_im=s5_lam_im, s5_log_dt=s5_log_dt, s5_b_re=s5_b_re, s5_b_im=s5_b_im,
             s5_c_re=s5_c_re, s5_c_im=s5_c_im, s5_d=s5_d, s5_w_glu=s5_w_glu, s5_b_glu=s5_b_glu,
             sc_conv_w=sc_conv_w, pool_w=pool_w, pool_scale=pool_scale, mix_out_g=mix_out_g,
             w_out=w_out, norm_xa_g=norm_xa_g, norm_mem_g=norm_mem_g, w_q=w_q, w_k=w_k, w_v=w_v,
             w_o=w_o, norm_mlp_g=norm_mlp_g, w_up=w_up, w_down=w_down)
    return _forward_jit(x_prompt, x_sample, mem_prompt, state_ssd_conv, state_ssd, state_s5_re, state_s5_im,
                        state_sconv, state_pool, cache_mem_k, cache_mem_v, final_norm_g, w)
```

```python
import functools
import math

import numpy as np
import jax
import jax.numpy as jnp
from jax import lax
from jax.experimental import pallas as pl
from jax.experimental.pallas import tpu as pltpu

F32 = jnp.float32
BF16 = jnp.bfloat16

D_MODEL = 2048
BATCH = 4
SEQ = 2048
DEPTH = 2
DEC_BATCH = 128
PAST_LEN = 16384
GROUP_WIDTH = D_MODEL // 4
SSD_HEAD_DIM = 64
SSD_HEADS = GROUP_WIDTH // SSD_HEAD_DIM
SSD_GROUPS = 2
SSD_STATE = 64
SSD_CONV = 4
SSD_CHUNK = 128
XBC_WIDTH = GROUP_WIDTH + 2 * SSD_GROUPS * SSD_STATE
S5_CH = 16
S5_GROUPS = GROUP_WIDTH // S5_CH
S5_STATE = 64
S5_LANES = S5_GROUPS * S5_STATE
SC_CONV = 3
POOL_WINDOWS = (2, 4, 8, 16)
POOL_GROUP = GROUP_WIDTH // len(POOL_WINDOWS)
POOL_HIST = max(POOL_WINDOWS) - 1
N_MEM = 256
XA_HEADS = 4
XA_HEAD_DIM = D_MODEL // XA_HEADS
D_FF = 4 * D_MODEL
EPS = 1e-6
P_ROWS = BATCH * SEQ

LANE = 128
SUBLANE = 8
VMEM_LIMIT = 56 * 1024 * 1024

COL_Z = 0
COL_U5 = 512
COL_GB = 1024
COL_GC = 1536
COL_HV = 2048
COL_UP = 2560
COL_XBC = 3072
COL_DT = 3840
IN_PAD = 4096

S5_CHUNK = 256
S5_SCAN_LANES = 256


def _cparams(sem):
    return pltpu.CompilerParams(dimension_semantics=sem, vmem_limit_bytes=VMEM_LIMIT)


def _sigmoid(x):
    return 1.0 / (1.0 + jnp.exp(-x))


def _silu(x):
    return x * _sigmoid(x)


def _softplus(x):
    return jnp.maximum(x, 0.0) + jnp.log(1.0 + jnp.exp(-jnp.abs(x)))


def _gelu_tanh(x):
    return 0.5 * x * (1.0 + jnp.tanh(math.sqrt(2.0 / math.pi) * (x + 0.044715 * (x * x * x))))


def _rms(x, g):
    return x * lax.rsqrt(jnp.mean(x * x, axis=-1, keepdims=True) + EPS) * g


def _dot(a, b):
    return jnp.dot(a, b, preferred_element_type=F32)


def _split_bf16(x):
    hi = x.astype(BF16)
    lo = (x - hi.astype(F32)).astype(BF16)
    return hi, lo


def _norm_matmul_kernel(*refs, norm, has_res):
    if has_res:
        x_ref, g_ref, w_ref, r_ref, o_ref, xn_ref = refs
    else:
        x_ref, g_ref, w_ref, o_ref, xn_ref = refs
        r_ref = None

    @pl.when(pl.program_id(1) == 0)
    def _():
        x = x_ref[...].astype(F32)
        if norm:
            x = _rms(x, g_ref[...])
        xn_ref[...] = x.astype(BF16)

    acc = _dot(xn_ref[...], w_ref[...])
    if has_res:
        acc = acc + r_ref[...]
    o_ref[...] = acc.astype(o_ref.dtype)


def norm_matmul(x, g, w, res=None, *, norm=True, out_dtype=F32, tm, tn):
    m, k = x.shape
    n = w.shape[1]
    assert m % tm == 0 and n % tn == 0
    in_specs = [pl.BlockSpec((tm, k), lambda i, j: (i, 0)),
                pl.BlockSpec((1, k), lambda i, j: (0, 0)),
                pl.BlockSpec((k, tn), lambda i, j: (0, j))]
    args = [x, g.reshape(1, k).astype(F32), w]
    if res is not None:
        in_specs.append(pl.BlockSpec((tm, tn), lambda i, j: (i, j)))
        args.append(res)
    return pl.pallas_call(
        functools.partial(_norm_matmul_kernel, norm=norm, has_res=res is not None),
        grid=(m // tm, n // tn),
        in_specs=in_specs,
        out_specs=pl.BlockSpec((tm, tn), lambda i, j: (i, j)),
        out_shape=jax.ShapeDtypeStruct((m, n), out_dtype),
        scratch_shapes=[pltpu.VMEM((tm, k), BF16)],
        compiler_params=_cparams(("parallel", "arbitrary")),
        name="norm_matmul",
    )(*args)


def _out_proj_kernel(m0_ref, m1_ref, m2_ref, m3_ref, w_ref, r_ref, o_ref, xc_ref):
    @pl.when(pl.program_id(1) == 0)
    def _():
        for i, mr in enumerate((m0_ref, m1_ref, m2_ref, m3_ref)):
            xc_ref[:, i * GROUP_WIDTH:(i + 1) * GROUP_WIDTH] = mr[...]

    o_ref[...] = r_ref[...] + _dot(xc_ref[...], w_ref[...])


def out_proj(mixes, w, res, *, tm, tn):
    m = res.shape[0]
    mix_spec = pl.BlockSpec((tm, GROUP_WIDTH), lambda i, j: (i, 0))
    return pl.pallas_call(
        _out_proj_kernel,
        grid=(m // tm, D_MODEL // tn),
        in_specs=[mix_spec] * 4 + [pl.BlockSpec((D_MODEL, tn), lambda i, j: (0, j)),
                                   pl.BlockSpec((tm, tn), lambda i, j: (i, j))],
        out_specs=pl.BlockSpec((tm, tn), lambda i, j: (i, j)),
        out_shape=jax.ShapeDtypeStruct((m, D_MODEL), F32),
        scratch_shapes=[pltpu.VMEM((tm, D_MODEL), BF16)],
        compiler_params=_cparams(("parallel", "arbitrary")),
        name="out_proj",
    )(*mixes, w, res)


def _mlp_kernel(x_ref, g_ref, wu_ref, wd_ref, o_ref, xn_ref):
    j = pl.program_id(1)

    @pl.when(j == 0)
    def _():
        x = x_ref[...]
        xn_ref[...] = _rms(x, g_ref[...]).astype(BF16)
        o_ref[...] = x

    a = jnp.square(jnp.maximum(_dot(xn_ref[...], wu_ref[...]), 0.0))
    o_ref[...] += _dot(a.astype(BF16), wd_ref[...])


def mlp(x, g, w_up, w_down, *, tm, tf):
    m = x.shape[0]
    return pl.pallas_call(
        _mlp_kernel,
        grid=(m // tm, D_FF // tf),
        in_specs=[pl.BlockSpec((tm, D_MODEL), lambda i, j: (i, 0)),
                  pl.BlockSpec((1, D_MODEL), lambda i, j: (0, 0)),
                  pl.BlockSpec((D_MODEL, tf), lambda i, j: (0, j)),
                  pl.BlockSpec((tf, D_MODEL), lambda i, j: (j, 0))],
        out_specs=pl.BlockSpec((tm, D_MODEL), lambda i, j: (i, 0)),
        out_shape=jax.ShapeDtypeStruct((m, D_MODEL), F32),
        scratch_shapes=[pltpu.VMEM((tm, D_MODEL), BF16)],
        compiler_params=_cparams(("parallel", "arbitrary")),
        name="mlp",
    )(x, g.reshape(1, D_MODEL), w_up, w_down)


def _final_norm_kernel(x_ref, g_ref, o_ref):
    o_ref[...] = _rms(x_ref[...], g_ref[...])


def final_norm(x, g, *, tm):
    m = x.shape[0]
    return pl.pallas_call(
        _final_norm_kernel,
        grid=(m // tm,),
        in_specs=[pl.BlockSpec((tm, D_MODEL), lambda i: (i, 0)), pl.BlockSpec((1, D_MODEL), lambda i: (0, 0))],
        out_specs=pl.BlockSpec((tm, D_MODEL), lambda i: (i, 0)),
        out_shape=jax.ShapeDtypeStruct((m, D_MODEL), F32),
        compiler_params=_cparams(("parallel",)),
        name="final_norm",
    )(x, g.reshape(1, D_MODEL))


def _prompt_attn_kernel(q_ref, k_ref, v_ref, o_ref):
    scale = XA_HEAD_DIM ** -0.5
    for h in range(XA_HEADS):
        cs = slice(h * XA_HEAD_DIM, (h + 1) * XA_HEAD_DIM)
        kh = k_ref[0, :, cs].astype(BF16)
        vh = v_ref[0, :, cs].astype(BF16)
        s = lax.dot_general(q_ref[:, cs], kh, (((1,), (1,)), ((), ())), preferred_element_type=F32) * scale
        p = jnp.exp(s - jnp.max(s, axis=-1, keepdims=True))
        p = p / jnp.sum(p, axis=-1, keepdims=True)
        o_ref[:, cs] = _dot(p.astype(BF16), vh).astype(o_ref.dtype)


def prompt_attn(q, k, v, *, tq):
    nq = SEQ // tq
    return pl.pallas_call(
        _prompt_attn_kernel,
        grid=(BATCH, nq),
        in_specs=[pl.BlockSpec((tq, D_MODEL), lambda b, i: (b * nq + i, 0)),
                  pl.BlockSpec((1, N_MEM, D_MODEL), lambda b, i: (b, 0, 0)),
                  pl.BlockSpec((1, N_MEM, D_MODEL), lambda b, i: (b, 0, 0))],
        out_specs=pl.BlockSpec((tq, D_MODEL), lambda b, i: (b * nq + i, 0)),
        out_shape=jax.ShapeDtypeStruct((P_ROWS, D_MODEL), BF16),
        compiler_params=_cparams(("parallel", "arbitrary")),
        name="prompt_attn",
    )(q, k, v)


def _sample_attn_kernel(q_ref, k_ref, v_ref, o_ref):
    scale = XA_HEAD_DIM ** -0.5
    s = jnp.sum(k_ref[...] * q_ref[...], axis=-1, keepdims=True) * scale
    p = jnp.exp(s - jnp.max(s, axis=0, keepdims=True))
    p = p / jnp.sum(p, axis=0, keepdims=True)
    o_ref[...] = jnp.sum(p * v_ref[...], axis=0, keepdims=True).astype(o_ref.dtype)


def sample_attn(q, cache_k, cache_v, layer):
    row = pl.BlockSpec((1, XA_HEADS, XA_HEAD_DIM), lambda b: (b, 0, 0))
    mem = pl.BlockSpec((None, None, N_MEM, XA_HEADS, XA_HEAD_DIM), lambda b: (layer, b, 0, 0, 0))
    out = pl.pallas_call(
        _sample_attn_kernel,
        grid=(DEC_BATCH,),
        in_specs=[row, mem, mem],
        out_specs=row,
        out_shape=jax.ShapeDtypeStruct((DEC_BATCH, XA_HEADS, XA_HEAD_DIM), BF16),
        compiler_params=_cparams(("parallel",)),
        name="sample_attn",
    )(q.reshape(DEC_BATCH, XA_HEADS, XA_HEAD_DIM), cache_k, cache_v)
    return out.reshape(DEC_BATCH, D_MODEL)


def _s5_prep_kernel(lr_ref, li_ref, ldt_ref, bre_ref, bim_ref, bbr_ref, bbi_ref, pr_ref, pi_ref):
    lr, li = lr_ref[...], li_ref[...]
    delta = jnp.exp(ldt_ref[...])
    mag = jnp.exp(lr * delta)
    ar = mag * jnp.cos(li * delta)
    ai = mag * jnp.sin(li * delta)
    den = lr * lr + li * li
    cr = ((ar - 1.0) * lr + ai * li) / den
    ci = (ai * lr - (ar - 1.0) * li) / den
    br, bi = bre_ref[...], bim_ref[...]
    bbr_ref[...] = cr * br - ci * bi
    bbi_ref[...] = cr * bi + ci * br
    qr, qi = ar, ai
    pr_ref[0] = qr
    pi_ref[0] = qi
    for e in range(1, SUBLANE):
        qr, qi = qr * ar - qi * ai, qr * ai + qi * ar
        pr_ref[e] = qr
        pi_ref[e] = qi


def s5_prepare(lam_re, lam_im, log_dt, b_re, b_im, c_re, c_im):
    gk = S5_GROUPS * S5_CH
    rep = lambda a: jnp.repeat(a, S5_CH, axis=0)
    to_rows = lambda b: jnp.transpose(b, (0, 2, 1)).reshape(gk, S5_STATE)
    shp = jax.ShapeDtypeStruct((gk, S5_STATE), F32)
    pshp = jax.ShapeDtypeStruct((SUBLANE, gk, S5_STATE), F32)
    bbr, bbi, pr, pi = pl.pallas_call(
        _s5_prep_kernel, out_shape=(shp, shp, pshp, pshp), name="s5_prep",
    )(rep(lam_re), rep(lam_im), rep(jnp.broadcast_to(log_dt[:, None], (S5_GROUPS, S5_STATE))),
      to_rows(b_re), to_rows(b_im))

    gpb = LANE // S5_CH
    nblk = S5_GROUPS // gpb
    eye = jnp.eye(gpb, dtype=F32)

    def b_blocks(bb):
        bb = bb.reshape(nblk, gpb, S5_CH, S5_STATE)
        return jnp.einsum("igkn,gh->igkhn", bb, eye).reshape(nblk, LANE, gpb * S5_STATE).astype(BF16)

    def c_blocks(cc):
        cc = cc.reshape(nblk, gpb, S5_CH, S5_STATE)
        return jnp.einsum("igkn,gh->ignhk", cc, eye).reshape(nblk, gpb * S5_STATE, LANE).astype(BF16)

    pw_r = pr[:, ::S5_CH, :].reshape(SUBLANE, S5_LANES)
    pw_i = pi[:, ::S5_CH, :].reshape(SUBLANE, S5_LANES)
    row = jnp.arange(SUBLANE)[:, None]
    consts = []
    for d in (1, 2, 4):
        consts.append(jnp.where(row >= d, pw_r[d - 1][None, :], 0.0))
        consts.append(jnp.where(row >= d, pw_i[d - 1][None, :], 0.0))
    consts += [pw_r, pw_i]
    return dict(b_re=b_blocks(bbr), b_im=b_blocks(bbi), c_re=c_blocks(c_re), c_imn=c_blocks(-c_im),
                scan=jnp.stack(consts), a_re=pw_r[0:1], a_im=pw_i[0:1])


def _s5_bu(ub, bre_ref, bim_ref):
    res_r, res_i = [], []
    for i in range(GROUP_WIDTH // LANE):
        ui = ub[:, i * LANE:(i + 1) * LANE]
        res_r.append(_dot(ui, bre_ref[i]))
        res_i.append(_dot(ui, bim_ref[i]))
    return res_r, res_i


def _s5_tail(u, hs_re, hs_im, cre_ref, cimn_ref, d_ref, wglu_ref, bglu_ref, g_ref):
    ys = []
    for i in range(GROUP_WIDTH // LANE):
        ys.append(_dot(hs_re(i).astype(BF16), cre_ref[i]) + _dot(hs_im(i).astype(BF16), cimn_ref[i]))
    y = jnp.concatenate(ys, axis=-1) + d_ref[...] * u
    y = _gelu_tanh(y)
    y = y * _sigmoid(_dot(y.astype(BF16), wglu_ref[...]) + bglu_ref[...])
    return _rms(y, g_ref[...])


def _s5_param_specs(zero_map3, zero_map2):
    nblk = GROUP_WIDTH // LANE
    sblk = S5_LANES // nblk
    return [pl.BlockSpec((nblk, LANE, sblk), zero_map3),
            pl.BlockSpec((nblk, LANE, sblk), zero_map3),
            pl.BlockSpec((nblk, sblk, LANE), zero_map3),
            pl.BlockSpec((nblk, sblk, LANE), zero_map3),
            pl.BlockSpec((1, GROUP_WIDTH), zero_map2),
            pl.BlockSpec((GROUP_WIDTH, GROUP_WIDTH), zero_map2),
            pl.BlockSpec((1, GROUP_WIDTH), zero_map2),
            pl.BlockSpec((1, GROUP_WIDTH), zero_map2)]


def _prompt_s5_kernel(u_ref, scan_ref, bre_ref, bim_ref, cre_ref, cimn_ref, d_ref, wglu_ref, bglu_ref, g_ref,
                      mix_ref, sre_ref, sim_ref, hre_ref, him_ref, cr_ref, ci_ref):
    rows = u_ref.shape[0]
    sblk = S5_LANES // (GROUP_WIDTH // LANE)

    @pl.when(pl.program_id(1) == 0)
    def _():
        cr_ref[...] = jnp.zeros_like(cr_ref)
        ci_ref[...] = jnp.zeros_like(ci_ref)

    u = u_ref[...]
    bu_r, bu_i = _s5_bu(u.astype(BF16), bre_ref, bim_ref)
    for i in range(len(bu_r)):
        hre_ref[:, i * sblk:(i + 1) * sblk] = bu_r[i]
        him_ref[:, i * sblk:(i + 1) * sblk] = bu_i[i]

    for lb in range(0, S5_LANES, S5_SCAN_LANES):
        ls = slice(lb, lb + S5_SCAN_LANES)
        k = [scan_ref[i, :, ls] for i in range(8)]

        def tile_step(ti, carry, ls=ls, k=k):
            c_r, c_i = carry
            rs = pl.ds(pl.multiple_of(ti * SUBLANE, SUBLANE), SUBLANE)
            x_r, x_i = hre_ref[rs, ls], him_ref[rs, ls]
            for s, d in enumerate((1, 2, 4)):
                a_r, a_i = k[2 * s], k[2 * s + 1]
                s_r = pltpu.roll(x_r, d, 0)
                s_i = pltpu.roll(x_i, d, 0)
                x_r, x_i = x_r + a_r * s_r - a_i * s_i, x_i + a_r * s_i + a_i * s_r
            h_r = x_r + k[6] * c_r - k[7] * c_i
            h_i = x_i + k[6] * c_i + k[7] * c_r
            hre_ref[rs, ls] = h_r
            him_ref[rs, ls] = h_i
            bshape = (SUBLANE, S5_SCAN_LANES)
            return (jnp.broadcast_to(h_r[SUBLANE - 1:SUBLANE, :], bshape),
                    jnp.broadcast_to(h_i[SUBLANE - 1:SUBLANE, :], bshape))

        c_r, c_i = lax.fori_loop(0, rows // SUBLANE, tile_step, (cr_ref[:, ls], ci_ref[:, ls]))
        cr_ref[:, ls] = c_r
        ci_ref[:, ls] = c_i

    sre_ref[0] = cr_ref[0:1, :]
    sim_ref[0] = ci_ref[0:1, :]
    y = _s5_tail(u, lambda i: hre_ref[:, i * sblk:(i + 1) * sblk], lambda i: him_ref[:, i * sblk:(i + 1) * sblk],
                 cre_ref, cimn_ref, d_ref, wglu_ref, bglu_ref, g_ref)
    mix_ref[...] = y.astype(mix_ref.dtype)


def prompt_s5(proj, s5p, d, w_glu, b_glu, g):
    nt = SEQ // S5_CHUNK
    z3 = lambda b, t: (0, 0, 0)
    z2 = lambda b, t: (0, 0)
    st = jax.ShapeDtypeStruct((BATCH, 1, S5_LANES), F32)
    st_spec = pl.BlockSpec((1, 1, S5_LANES), lambda b, t: (b, 0, 0))
    return pl.pallas_call(
        _prompt_s5_kernel,
        grid=(BATCH, nt),
        in_specs=[pl.BlockSpec((S5_CHUNK, GROUP_WIDTH), lambda b, t: (b * nt + t, COL_U5 // GROUP_WIDTH)),
                  pl.BlockSpec((8, SUBLANE, S5_LANES), z3)] + _s5_param_specs(z3, z2),
        out_specs=(pl.BlockSpec((S5_CHUNK, GROUP_WIDTH), lambda b, t: (b * nt + t, 0)), st_spec, st_spec),
        out_shape=(jax.ShapeDtypeStruct((P_ROWS, GROUP_WIDTH), BF16), st, st),
        scratch_shapes=[pltpu.VMEM((S5_CHUNK, S5_LANES), F32), pltpu.VMEM((S5_CHUNK, S5_LANES), F32),
                        pltpu.VMEM((SUBLANE, S5_LANES), F32), pltpu.VMEM((SUBLANE, S5_LANES), F32)],
        compiler_params=_cparams(("parallel", "arbitrary")),
        name="prompt_s5",
    )(proj, s5p["scan"], s5p["b_re"], s5p["b_im"], s5p["c_re"], s5p["c_imn"],
      d.reshape(1, -1), w_glu, b_glu.reshape(1, -1), g.reshape(1, -1))


def _prompt_ssd_kernel(xbc_ref, z_ref, dt_ref, cw_ref, cb_ref, dtb_ref, alog_ref, dexp_ref, g_ref,
                       mix_ref, cst_ref, hst_ref, xbuf_ref, h_ref):
    L = SSD_CHUNK
    hist = SSD_CONV - 1
    base = SUBLANE

    @pl.when(pl.program_id(1) == 0)
    def _():
        xbuf_ref[0:base, :] = jnp.zeros((base, XBC_WIDTH), F32)
        h_ref[...] = jnp.zeros_like(h_ref)

    @pl.when(pl.program_id(1) > 0)
    def _():
        xbuf_ref[base - hist:base, :] = xbuf_ref[base + L - hist:base + L, :]

    xbuf_ref[base:base + L, :] = xbc_ref[...]
    cst_ref[0] = xbuf_ref[base + L - hist:base + L, :]
    conv = cb_ref[...] + cw_ref[hist:hist + 1, :] * xbc_ref[...]
    for k in range(hist):
        conv = conv + cw_ref[k:k + 1, :] * xbuf_ref[base - hist + k:base - hist + k + L, :]
    xc = _silu(conv)
    xs = xc[:, :GROUP_WIDTH]
    ng = SSD_GROUPS * SSD_STATE
    bm = xc[:, GROUP_WIDTH:GROUP_WIDTH + ng].astype(BF16)
    cm = xc[:, GROUP_WIDTH + ng:].astype(BF16)

    dt = _softplus(dt_ref[...] + dtb_ref[...])
    a = -jnp.exp(alog_ref[...])
    ri = lax.broadcasted_iota(jnp.int32, (L, L), 0)
    ci = lax.broadcasted_iota(jnp.int32, (L, L), 1)
    causal = ri >= ci
    acum = jnp.dot(causal.astype(F32), dt * a, preferred_element_type=F32, precision=lax.Precision.HIGHEST)
    acum_t = acum.T
    last = acum[L - 1:L, :]
    to_end = jnp.exp(last - acum)
    e_acum = jnp.exp(acum)
    chunk_decay = jnp.exp(last)

    ys = []
    rep = SSD_HEADS // SSD_GROUPS
    cb = [lax.dot_general(cm[:, g * SSD_STATE:(g + 1) * SSD_STATE], bm[:, g * SSD_STATE:(g + 1) * SSD_STATE],
                          (((1,), (1,)), ((), ())), preferred_element_type=F32) for g in range(SSD_GROUPS)]
    for h in range(SSD_HEADS):
        g = h // rep
        hs = slice(h * SSD_HEAD_DIM, (h + 1) * SSD_HEAD_DIM)
        gs = slice(g * SSD_STATE, (g + 1) * SSD_STATE)
        seg = acum[:, h:h + 1] - acum_t[h:h + 1, :]
        decay = jnp.exp(jnp.where(causal, seg, -jnp.inf))
        xs_h = xs[:, hs]
        xdt = xs_h * dt[:, h:h + 1]
        y = _dot((cb[g] * decay).astype(BF16), xdt.astype(BF16))
        h_prev = h_ref[h]
        y_off = lax.dot_general(cm[:, gs], h_prev.astype(BF16), (((1,), (1,)), ((), ())),
                                preferred_element_type=F32)
        y = y + y_off * e_acum[:, h:h + 1]
        st = lax.dot_general((xdt * to_end[:, h:h + 1]).astype(BF16), bm[:, gs], (((0,), (0,)), ((), ())),
                             preferred_element_type=F32)
        h_ref[h] = h_prev * chunk_decay[:, h:h + 1] + st
        ys.append(y)
    y = (jnp.concatenate(ys, axis=-1) + dexp_ref[...] * xs) * _silu(z_ref[...])
    mix_ref[...] = _rms(y, g_ref[...]).astype(mix_ref.dtype)
    hst_ref[0] = h_ref[...]


def prompt_ssd(proj, conv_w, conv_b, dt_bias, a_log, d, g):
    nc = SEQ // SSD_CHUNK
    z2 = lambda b, c: (0, 0)
    pad8 = lambda v: jnp.pad(v, (0, LANE - SSD_HEADS)).reshape(1, LANE)
    return pl.pallas_call(
        _prompt_ssd_kernel,
        grid=(BATCH, nc),
        in_specs=[pl.BlockSpec((SSD_CHUNK, XBC_WIDTH), lambda b, c: (b * nc + c, COL_XBC // XBC_WIDTH)),
                  pl.BlockSpec((SSD_CHUNK, GROUP_WIDTH), lambda b, c: (b * nc + c, COL_Z // GROUP_WIDTH)),
                  pl.BlockSpec((SSD_CHUNK, LANE), lambda b, c: (b * nc + c, COL_DT // LANE)),
                  pl.BlockSpec((SSD_CONV, XBC_WIDTH), z2),
                  pl.BlockSpec((1, XBC_WIDTH), z2),
                  pl.BlockSpec((1, LANE), z2),
                  pl.BlockSpec((1, LANE), z2),
                  pl.BlockSpec((1, GROUP_WIDTH), z2),
                  pl.BlockSpec((1, GROUP_WIDTH), z2)],
        out_specs=(pl.BlockSpec((SSD_CHUNK, GROUP_WIDTH), lambda b, c: (b * nc + c, 0)),
                   pl.BlockSpec((1, SSD_CONV - 1, XBC_WIDTH), lambda b, c: (b, 0, 0)),
                   pl.BlockSpec((1, SSD_HEADS, SSD_HEAD_DIM, SSD_STATE), lambda b, c: (b, 0, 0, 0))),
        out_shape=(jax.ShapeDtypeStruct((P_ROWS, GROUP_WIDTH), BF16),
                   jax.ShapeDtypeStruct((BATCH, SSD_CONV - 1, XBC_WIDTH), F32),
                   jax.ShapeDtypeStruct((BATCH, SSD_HEADS, SSD_HEAD_DIM, SSD_STATE), F32)),
        scratch_shapes=[pltpu.VMEM((SUBLANE + SSD_CHUNK, XBC_WIDTH), F32),
                        pltpu.VMEM((SSD_HEADS, SSD_HEAD_DIM, SSD_STATE), F32)],
        compiler_params=_cparams(("parallel", "arbitrary")),
        name="prompt_ssd",
    )(proj, proj, proj, conv_w, conv_b.reshape(1, -1), pad8(dt_bias), pad8(a_log),
      jnp.repeat(d, SSD_HEAD_DIM).reshape(1, -1), g.reshape(1, -1))


def _pool_counts(pos, w):
    return jnp.minimum(w, pos + 1).astype(F32)


def _prompt_convpool_kernel(gb_ref, gc_ref, hv_ref, up_ref, scw_ref, pw_ref, ps_ref, gsc_ref, gpl_ref,
                            msc_ref, mpl_ref, scst_ref, plst_ref, vbuf_ref, pbuf_ref):
    rows = gb_ref.shape[0]
    vb = SUBLANE
    pb = 2 * SUBLANE
    vh = SC_CONV - 1
    t = pl.program_id(1)

    @pl.when(t == 0)
    def _():
        vbuf_ref[0:vb, :] = jnp.zeros((vb, GROUP_WIDTH), F32)
        pbuf_ref[0:pb, :] = jnp.zeros((pb, GROUP_WIDTH), F32)

    @pl.when(t > 0)
    def _():
        vbuf_ref[vb - vh:vb, :] = vbuf_ref[vb + rows - vh:vb + rows, :]
        pbuf_ref[0:pb, :] = pbuf_ref[rows:rows + pb, :]

    v = gc_ref[...] * hv_ref[...]
    vbuf_ref[vb:vb + rows, :] = v
    acc = scw_ref[vh:vh + 1, :] * v
    for k in range(vh):
        acc = acc + scw_ref[k:k + 1, :] * vbuf_ref[vb - vh + k:vb - vh + k + rows, :]
    msc_ref[...] = _rms(gb_ref[...] * acc, gsc_ref[...]).astype(msc_ref.dtype)
    scst_ref[0] = vbuf_ref[vb + rows - vh:vb + rows, :]

    u = up_ref[...]
    pbuf_ref[pb:pb + rows, :] = u
    pos = t * rows + lax.broadcasted_iota(jnp.int32, (rows, 1), 0)
    ys = []
    for gi, w in enumerate(POOL_WINDOWS):
        cs = slice(gi * POOL_GROUP, (gi + 1) * POOL_GROUP)
        s = u[:, cs]
        for j in range(1, w):
            s = s + pbuf_ref[pb - j:pb - j + rows, cs]
        pooled = s / _pool_counts(pos, w) - u[:, cs]
        ys.append(_dot(pooled.astype(BF16), pw_ref[gi]))
    y = jnp.concatenate(ys, axis=-1) * ps_ref[...]
    mpl_ref[...] = _rms(y, gpl_ref[...]).astype(mpl_ref.dtype)
    plst_ref[0] = pbuf_ref[pb + rows - POOL_HIST:pb + rows, :]


def prompt_convpool(proj, sc_w, pool_w, pool_scale, g_sc, g_pool):
    nt = SEQ // S5_CHUNK
    z2 = lambda b, t: (0, 0)
    col = lambda c: pl.BlockSpec((S5_CHUNK, GROUP_WIDTH), lambda b, t: (b * nt + t, c // GROUP_WIDTH))
    mix_spec = pl.BlockSpec((S5_CHUNK, GROUP_WIDTH), lambda b, t: (b * nt + t, 0))
    mix_shape = jax.ShapeDtypeStruct((P_ROWS, GROUP_WIDTH), BF16)
    return pl.pallas_call(
        _prompt_convpool_kernel,
        grid=(BATCH, nt),
        in_specs=[col(COL_GB), col(COL_GC), col(COL_HV), col(COL_UP),
                  pl.BlockSpec((SC_CONV, GROUP_WIDTH), z2),
                  pl.BlockSpec((len(POOL_WINDOWS), POOL_GROUP, POOL_GROUP), lambda b, t: (0, 0, 0)),
                  pl.BlockSpec((1, GROUP_WIDTH), z2), pl.BlockSpec((1, GROUP_WIDTH), z2),
                  pl.BlockSpec((1, GROUP_WIDTH), z2)],
        out_specs=(mix_spec, mix_spec,
                   pl.BlockSpec((1, SC_CONV - 1, GROUP_WIDTH), lambda b, t: (b, 0, 0)),
                   pl.BlockSpec((1, POOL_HIST, GROUP_WIDTH), lambda b, t: (b, 0, 0))),
        out_shape=(mix_shape, mix_shape,
                   jax.ShapeDtypeStruct((BATCH, SC_CONV - 1, GROUP_WIDTH), F32),
                   jax.ShapeDtypeStruct((BATCH, POOL_HIST, GROUP_WIDTH), F32)),
        scratch_shapes=[pltpu.VMEM((SUBLANE + S5_CHUNK, GROUP_WIDTH), F32),
                        pltpu.VMEM((2 * SUBLANE + S5_CHUNK, GROUP_WIDTH), F32)],
        compiler_params=_cparams(("parallel", "arbitrary")),
        name="prompt_convpool",
    )(proj, proj, proj, proj, sc_w, pool_w, pool_scale.reshape(1, -1), g_sc.reshape(1, -1), g_pool.reshape(1, -1))


def _sample_mix_kernel(proj_ref, cprev_ref, s5r_ref, s5i_ref, scprev_ref, plprev_ref,
                       cw_ref, cb_ref, dtb_ref, alog_ref,
                       ar_ref, ai_ref, bre_ref, bim_ref, cre_ref, cimn_ref, d5_ref, wglu_ref, bglu_ref, g5_ref,
                       scw_ref, pw_ref, ps_ref, gsc_ref, gpl_ref,
                       m5_ref, msc_ref, mpl_ref, cst_ref, s5ro_ref, s5io_ref, scst_ref, plst_ref,
                       xs_ref, xdt_ref, da_ref, b2_ref, c2_ref):
    nb = proj_ref.shape[0]
    W = GROUP_WIDTH
    xbc = proj_ref[:, COL_XBC:COL_XBC + XBC_WIDTH]
    hist = SSD_CONV - 1
    conv = cb_ref[...] + cw_ref[hist:hist + 1, :] * xbc
    for k in range(hist):
        conv = conv + cw_ref[k:k + 1, :] * cprev_ref[:, k * XBC_WIDTH:(k + 1) * XBC_WIDTH]
    cst_ref[:, 0:(hist - 1) * XBC_WIDTH] = cprev_ref[:, XBC_WIDTH:hist * XBC_WIDTH]
    cst_ref[:, (hist - 1) * XBC_WIDTH:hist * XBC_WIDTH] = xbc
    xc = _silu(conv)
    xs = xc[:, :W]
    bm = xc[:, W:W + LANE]
    cm = xc[:, W + LANE:W + 2 * LANE]
    dt = _softplus(proj_ref[:, COL_DT:COL_DT + LANE] + dtb_ref[...])
    da = jnp.exp(dt * (-jnp.exp(alog_ref[...])))
    lane_w = lax.broadcasted_iota(jnp.int32, (nb, W), 1)
    dt_exp = jnp.zeros((nb, W), F32)
    for h in range(SSD_HEADS):
        dt_exp = jnp.where(lane_w // SSD_HEAD_DIM == h, dt[:, h:h + 1], dt_exp)
        da_ref[:, h * LANE:(h + 1) * LANE] = jnp.broadcast_to(da[:, h:h + 1], (nb, LANE))
    xs_ref[...] = xs
    xdt_ref[...] = xs * dt_exp
    lane = lax.broadcasted_iota(jnp.int32, (nb, LANE), 1)
    low = lane < SSD_STATE
    for src, dst in ((bm, b2_ref), (cm, c2_ref)):
        swapped = pltpu.roll(src, SSD_STATE, 1)
        dst[:, 0:LANE] = jnp.where(low, src, swapped)
        dst[:, LANE:2 * LANE] = jnp.where(low, swapped, src)

    u5 = proj_ref[:, COL_U5:COL_U5 + W]
    bu_r, bu_i = _s5_bu(u5.astype(BF16), bre_ref, bim_ref)
    sblk = S5_LANES // len(bu_r)
    for i in range(len(bu_r)):
        ls = slice(i * sblk, (i + 1) * sblk)
        h_r, h_i, a_r, a_i = s5r_ref[:, ls], s5i_ref[:, ls], ar_ref[:, ls], ai_ref[:, ls]
        s5ro_ref[:, ls] = bu_r[i] + a_r * h_r - a_i * h_i
        s5io_ref[:, ls] = bu_i[i] + a_r * h_i + a_i * h_r
    y5 = _s5_tail(u5, lambda i: s5ro_ref[:, i * sblk:(i + 1) * sblk], lambda i: s5io_ref[:, i * sblk:(i + 1) * sblk],
                  cre_ref, cimn_ref, d5_ref, wglu_ref, bglu_ref, g5_ref)
    m5_ref[...] = y5.astype(m5_ref.dtype)

    v = proj_ref[:, COL_GC:COL_GC + W] * proj_ref[:, COL_HV:COL_HV + W]
    vh = SC_CONV - 1
    acc = scw_ref[vh:vh + 1, :] * v
    for k in range(vh):
        acc = acc + scw_ref[k:k + 1, :] * scprev_ref[:, k * W:(k + 1) * W]
    msc_ref[...] = _rms(proj_ref[:, COL_GB:COL_GB + W] * acc, gsc_ref[...]).astype(msc_ref.dtype)
    scst_ref[:, 0:(vh - 1) * W] = scprev_ref[:, W:vh * W]
    scst_ref[:, (vh - 1) * W:vh * W] = v

    up = proj_ref[:, COL_UP:COL_UP + W]
    ys = []
    for gi, w in enumerate(POOL_WINDOWS):
        cs = slice(gi * POOL_GROUP, (gi + 1) * POOL_GROUP)
        s = up[:, cs]
        for j in range(1, w):
            k = POOL_HIST - j
            s = s + plprev_ref[:, k * W + gi * POOL_GROUP:k * W + (gi + 1) * POOL_GROUP]
        pooled = s / float(min(w, PAST_LEN + 1)) - up[:, cs]
        ys.append(_dot(pooled.astype(BF16), pw_ref[gi]))
    y = jnp.concatenate(ys, axis=-1) * ps_ref[...]
    mpl_ref[...] = _rms(y, gpl_ref[...]).astype(mpl_ref.dtype)
    plst_ref[:, 0:(POOL_HIST - 1) * W] = plprev_ref[:, W:POOL_HIST * W]
    plst_ref[:, (POOL_HIST - 1) * W:POOL_HIST * W] = up


def sample_mix(proj, st, lp, s5p):
    nb = DEC_BATCH
    W = GROUP_WIDTH
    conv_prev, s5_re, s5_im, sc_prev, pool_prev = st
    pad8 = lambda v: jnp.pad(v, (0, LANE - SSD_HEADS)).reshape(1, LANE)
    row = lambda v: v.reshape(1, -1)
    f = lambda *s: jax.ShapeDtypeStruct(s, F32)
    b = lambda *s: jax.ShapeDtypeStruct(s, BF16)
    out_shape = (b(nb, W), b(nb, W), b(nb, W),
                 f(nb, (SSD_CONV - 1) * XBC_WIDTH), f(nb, S5_LANES), f(nb, S5_LANES),
                 f(nb, (SC_CONV - 1) * W), f(nb, POOL_HIST * W),
                 f(nb, W), f(nb, W), f(nb, SSD_HEADS * LANE), f(nb, SSD_GROUPS * LANE), f(nb, SSD_GROUPS * LANE))
    return pl.pallas_call(
        _sample_mix_kernel, out_shape=out_shape, compiler_params=pltpu.CompilerParams(vmem_limit_bytes=VMEM_LIMIT),
        name="sample_mix",
    )(proj, conv_prev.reshape(nb, -1), s5_re.reshape(nb, -1), s5_im.reshape(nb, -1),
      sc_prev.reshape(nb, -1), pool_prev.reshape(nb, -1),
      lp["ssd_conv_w"], row(lp["ssd_conv_b"]), pad8(lp["ssd_dt_bias"]), pad8(lp["ssd_a_log"]),
      s5p["a_re"], s5p["a_im"], s5p["b_re"], s5p["b_im"], s5p["c_re"], s5p["c_imn"],
      row(lp["s5_d"]), lp["s5_w_glu"], row(lp["s5_b_glu"]), row(lp["g_s5"]),
      lp["sc_conv_w"], lp["pool_w"], row(lp["pool_scale"]), row(lp["g_sc"]), row(lp["g_pool"]))


def _sample_ssd_kernel(h0_ref, xdt_ref, da_ref, b2_ref, c2_ref, e_ref, r_ref, xs_ref, z_ref, dexp_ref, g_ref,
                       hn_ref, mix_ref, xrep_ref, prod_ref, y_ref):
    hp = pl.program_id(0)
    hpl = 2 * SSD_HEAD_DIM * SSD_STATE
    per_head = SSD_HEAD_DIM * SSD_STATE
    x_hi, x_lo = _split_bf16(xdt_ref[...])
    xrep_ref[...] = _dot(x_hi, e_ref[...]) + _dot(x_lo, e_ref[...])
    b2, c2 = b2_ref[...], c2_ref[...]
    for j in range(hpl // LANE):
        ls = slice(j * LANE, (j + 1) * LANE)
        hl = (j * LANE) // per_head
        hn = da_ref[:, hl * LANE:(hl + 1) * LANE] * h0_ref[:, ls] + xrep_ref[:, ls] * b2
        hn_ref[:, ls] = hn
        prod_ref[:, ls] = hn * c2
    p_hi, p_lo = _split_bf16(prod_ref[...])
    y_ref[hp] = _dot(p_hi, r_ref[...]) + _dot(p_lo, r_ref[...])

    @pl.when(hp == pl.num_programs(0) - 1)
    def _():
        y = jnp.concatenate([y_ref[i] for i in range(SSD_HEADS // 2)], axis=-1)
        y = (y + dexp_ref[...] * xs_ref[...]) * _silu(z_ref[...])
        mix_ref[...] = _rms(y, g_ref[...]).astype(mix_ref.dtype)


def sample_ssd(h0, xdt, da, b2, c2, xs, proj, d, g):
    nb = DEC_BATCH
    npairs = SSD_HEADS // 2
    hpl = 2 * SSD_HEAD_DIM * SSD_STATE
    expand = jnp.repeat(jnp.eye(LANE, dtype=BF16), SSD_STATE, axis=1)
    z1 = lambda i: (0, 0)
    hn, mix = pl.pallas_call(
        _sample_ssd_kernel,
        grid=(npairs,),
        in_specs=[pl.BlockSpec((nb, hpl), lambda i: (0, i)),
                  pl.BlockSpec((nb, LANE), lambda i: (0, i)),
                  pl.BlockSpec((nb, 2 * LANE), lambda i: (0, i)),
                  pl.BlockSpec((nb, LANE), lambda i: (0, i // (npairs // SSD_GROUPS))),
                  pl.BlockSpec((nb, LANE), lambda i: (0, i // (npairs // SSD_GROUPS))),
                  pl.BlockSpec((LANE, hpl), z1),
                  pl.BlockSpec((hpl, LANE), z1),
                  pl.BlockSpec((nb, GROUP_WIDTH), z1),
                  pl.BlockSpec((nb, GROUP_WIDTH), lambda i: (0, COL_Z // GROUP_WIDTH)),
                  pl.BlockSpec((1, GROUP_WIDTH), z1),
                  pl.BlockSpec((1, GROUP_WIDTH), z1)],
        out_specs=(pl.BlockSpec((nb, hpl), lambda i: (0, i)),
                   pl.BlockSpec((nb, GROUP_WIDTH), z1)),
        out_shape=(jax.ShapeDtypeStruct((nb, SSD_HEADS * SSD_HEAD_DIM * SSD_STATE), F32),
                   jax.ShapeDtypeStruct((nb, GROUP_WIDTH), BF16)),
        scratch_shapes=[pltpu.VMEM((nb, hpl), F32), pltpu.VMEM((nb, hpl), F32),
                        pltpu.VMEM((npairs, nb, LANE), F32)],
        compiler_params=_cparams(("arbitrary",)),
        name="sample_ssd",
    )(h0.reshape(nb, -1), xdt, da, b2, c2, expand, expand.T, xs, proj,
      jnp.repeat(d, SSD_HEAD_DIM).reshape(1, -1), g.reshape(1, -1))
    return hn.reshape(nb, SSD_HEADS, SSD_HEAD_DIM, SSD_STATE), mix


def _regroup_w_in(w_in):
    sizes = (GROUP_WIDTH, XBC_WIDTH, SSD_HEADS, GROUP_WIDTH, GROUP_WIDTH, GROUP_WIDTH, GROUP_WIDTH, GROUP_WIDTH)
    offs = np.concatenate([[0], np.cumsum(sizes)])
    z, xbc, dt, u5, gb, gc, hv, up = [w_in[:, int(offs[i]):int(offs[i + 1])] for i in range(len(sizes))]
    pad = jnp.zeros((D_MODEL, IN_PAD - COL_DT - SSD_HEADS), w_in.dtype)
    return jnp.concatenate([z, u5, gb, gc, hv, up, xbc, dt, pad], axis=1).astype(BF16)


def _layer_params(l, w):
    g_mix = w["mix_out_g"][l].reshape(4, GROUP_WIDTH)
    lp = {k: v[l] for k, v in w.items()}
    lp.update(g_ssd=g_mix[0], g_s5=g_mix[1], g_sc=g_mix[2], g_pool=g_mix[3])
    lp["w_in"] = _regroup_w_in(lp["w_in"])
    for k in ("w_out", "w_q", "w_k", "w_v", "w_o", "w_up", "w_down", "s5_w_glu", "pool_w"):
        lp[k] = lp[k].astype(BF16)
    return lp


P_TM = 1024
P_TN = 1024


def _dense_tail(h, mixes, lp, attn_fn, *, tm, tn, tf, q_dtype):
    h = out_proj(mixes, lp["w_out"], h, tm=tm, tn=tn)
    q = norm_matmul(h, lp["norm_xa_g"], lp["w_q"], out_dtype=q_dtype, tm=tm, tn=tn)
    o = attn_fn(q)
    h = norm_matmul(o, lp["norm_xa_g"], lp["w_o"], res=h, norm=False, tm=tm, tn=tn)
    return mlp(h, lp["norm_mlp_g"], lp["w_up"], lp["w_down"], tm=tm, tf=tf)


def _forward(x_prompt, x_sample, mem_prompt, state_ssd_conv, state_ssd, state_s5_re, state_s5_im,
             state_sconv, state_pool, cache_mem_k, cache_mem_v, final_norm_g, w):
    hp = x_prompt.reshape(P_ROWS, D_MODEL)
    hs = x_sample.reshape(DEC_BATCH, D_MODEL)
    mem = mem_prompt.reshape(BATCH * N_MEM, D_MODEL)
    p_out = [[] for _ in range(8)]
    s_out = [[] for _ in range(6)]
    for l in range(DEPTH):
        lp = _layer_params(l, w)
        s5p = s5_prepare(lp["s5_lam_re"], lp["s5_lam_im"], lp["s5_log_dt"], lp["s5_b_re"], lp["s5_b_im"],
                         lp["s5_c_re"], lp["s5_c_im"])
        mk = norm_matmul(mem, lp["norm_mem_g"], lp["w_k"], tm=BATCH * N_MEM, tn=P_TN)
        mv = norm_matmul(mem, lp["norm_mem_g"], lp["w_v"], tm=BATCH * N_MEM, tn=P_TN)
        mk3, mv3 = mk.reshape(BATCH, N_MEM, D_MODEL), mv.reshape(BATCH, N_MEM, D_MODEL)

        proj = norm_matmul(hp, lp["norm_mix_g"], lp["w_in"], tm=P_TM, tn=P_TN)
        m_ssd, p_conv, p_ssd = prompt_ssd(proj, lp["ssd_conv_w"], lp["ssd_conv_b"], lp["ssd_dt_bias"],
                                          lp["ssd_a_log"], lp["ssd_d"], lp["g_ssd"])
        m_s5, p_s5r, p_s5i = prompt_s5(proj, s5p, lp["s5_d"], lp["s5_w_glu"], lp["s5_b_glu"], lp["g_s5"])
        m_sc, m_pl, p_sc, p_pl = prompt_convpool(proj, lp["sc_conv_w"], lp["pool_w"], lp["pool_scale"],
                                                 lp["g_sc"], lp["g_pool"])
        hp = _dense_tail(hp, (m_ssd, m_s5, m_sc, m_pl), lp,
                         lambda q: prompt_attn(q, mk3, mv3, tq=512),
                         tm=P_TM, tn=P_TN, tf=512, q_dtype=BF16)
        for lst, val in zip(p_out, (p_conv, p_ssd, p_s5r.reshape(BATCH, S5_GROUPS, S5_STATE),
                                    p_s5i.reshape(BATCH, S5_GROUPS, S5_STATE), p_sc, p_pl,
                                    mk.reshape(BATCH, N_MEM, XA_HEADS, XA_HEAD_DIM),
                                    mv.reshape(BATCH, N_MEM, XA_HEADS, XA_HEAD_DIM))):
            lst.append(val)

        proj_s = norm_matmul(hs, lp["norm_mix_g"], lp["w_in"], tm=DEC_BATCH, tn=P_TN)
        (m5, msc, mpl, s_conv, s_s5r, s_s5i, s_sc, s_pl, xs, xdt, da, b2, c2) = sample_mix(
            proj_s, (state_ssd_conv[l], state_s5_re[l], state_s5_im[l], state_sconv[l], state_pool[l]), lp, s5p)
        s_ssd, mssd = sample_ssd(state_ssd[l], xdt, da, b2, c2, xs, proj_s, lp["ssd_d"], lp["g_ssd"])
        hs = _dense_tail(hs, (mssd, m5, msc, mpl), lp, lambda q, l=l: sample_attn(q, cache_mem_k, cache_mem_v, l),
                         tm=DEC_BATCH, tn=P_TN, tf=1024, q_dtype=F32)
        for lst, val in zip(s_out, (s_conv.reshape(DEC_BATCH, SSD_CONV - 1, XBC_WIDTH), s_ssd,
                                    s_s5r.reshape(DEC_BATCH, S5_GROUPS, S5_STATE),
                                    s_s5i.reshape(DEC_BATCH, S5_GROUPS, S5_STATE),
                                    s_sc.reshape(DEC_BATCH, SC_CONV - 1, GROUP_WIDTH),
                                    s_pl.reshape(DEC_BATCH, POOL_HIST, GROUP_WIDTH))):
            lst.append(val)

    y_prompt = final_norm(hp, final_norm_g, tm=P_TM).reshape(BATCH, SEQ, D_MODEL)
    y_sample = final_norm(hs, final_norm_g, tm=DEC_BATCH).reshape(DEC_BATCH, 1, D_MODEL)
    return (y_prompt, y_sample) + tuple(jnp.stack(c) for c in p_out) + tuple(jnp.stack(c) for c in s_out)


_forward_jit = jax.jit(_forward)


def kernel(x_prompt, x_sample, mem_prompt, state_ssd_conv, state_ssd, state_s5_re, state_s5_im, state_sconv, state_pool, cache_mem_k, cache_mem_v, norm_mix_g, w_in, ssd_conv_w, ssd_conv_b, ssd_dt_bias, ssd_a_log, ssd_d, s5_lam_re, s5_lam_im, s5_log_dt, s5_b_re, s5_b_im, s5_c_re, s5_c_im, s5_d, s5_w_glu, s5_b_glu, sc_conv_w, pool_w, pool_scale, mix_out_g, w_out, norm_xa_g, norm_mem_g, w_q, w_k, w_v, w_o, norm_mlp_g, w_up, w_down, final_norm_g):
    w = dict(norm_mix_g=norm_mix_g, w_in=w_in, ssd_conv_w=ssd_conv_w, ssd_conv_b=ssd_conv_b,
             ssd_dt_bias=ssd_dt_bias, ssd_a_log=ssd_a_log, ssd_d=ssd_d, s5_lam_re=s5_lam_re,
             s5_lam_im=s5_lam_im, s5_log_dt=s5_log_dt, s5_b_re=s5_b_re, s5_b_im=s5_b_im,
             s5_c_re=s5_c_re, s5_c_im=s5_c_im, s5_d=s5_d, s5_w_glu=s5_w_glu, s5_b_glu=s5_b_glu,
             sc_conv_w=sc_conv_w, pool_w=pool_w, pool_scale=pool_scale, mix_out_g=mix_out_g,
             w_out=w_out, norm_xa_g=norm_xa_g, norm_mem_g=norm_mem_g, w_q=w_q, w_k=w_k, w_v=w_v,
             w_o=w_o, norm_mlp_g=norm_mlp_g, w_up=w_up, w_down=w_down)
    return _forward_jit(x_prompt, x_sample, mem_prompt, state_ssd_conv, state_ssd, state_s5_re, state_s5_im,
                        state_sconv, state_pool, cache_mem_k, cache_mem_v, final_norm_g, w)
```

```python
import functools
import math

import numpy as np
import jax
import jax.numpy as jnp
from jax import lax
from jax.experimental import pallas as pl
from jax.experimental.pallas import tpu as pltpu

F32 = jnp.float32
BF16 = jnp.bfloat16

D_MODEL = 2048
BATCH = 4
SEQ = 2048
DEPTH = 2
DEC_BATCH = 128
PAST_LEN = 16384
GROUP_WIDTH = D_MODEL // 4
SSD_HEAD_DIM = 64
SSD_HEADS = GROUP_WIDTH // SSD_HEAD_DIM
SSD_GROUPS = 2
SSD_STATE = 64
SSD_CONV = 4
SSD_CHUNK = 128
XBC_WIDTH = GROUP_WIDTH + 2 * SSD_GROUPS * SSD_STATE
S5_CH = 16
S5_GROUPS = GROUP_WIDTH // S5_CH
S5_STATE = 64
S5_LANES = S5_GROUPS * S5_STATE
SC_CONV = 3
POOL_WINDOWS = (2, 4, 8, 16)
POOL_GROUP = GROUP_WIDTH // len(POOL_WINDOWS)
POOL_HIST = max(POOL_WINDOWS) - 1
N_MEM = 256
XA_HEADS = 4
XA_HEAD_DIM = D_MODEL // XA_HEADS
D_FF = 4 * D_MODEL
EPS = 1e-6
P_ROWS = BATCH * SEQ
ALL_ROWS = P_ROWS + DEC_BATCH

LANE = 128
SUBLANE = 8
VMEM_LIMIT = 56 * 1024 * 1024

COL_Z = 0
COL_U5 = 512
COL_GB = 1024
COL_GC = 1536
COL_HV = 2048
COL_UP = 2560
COL_XBC = 3072
COL_DT = 3840
IN_PAD = 4096

ROW_TILE = ALL_ROWS // 8
COL_TILE = 1024
FF_TILE = 512
SAMPLE_BLOCK = P_ROWS // DEC_BATCH

S5_CHUNK = 256
S5_SEG = S5_CHUNK // SUBLANE
S5_SCAN_LANES = 512


def _cparams(sem):
    return pltpu.CompilerParams(dimension_semantics=sem, vmem_limit_bytes=VMEM_LIMIT)


def _layer_spec(shape, layer):
    zeros = (0,) * len(shape)
    return pl.BlockSpec((None,) + tuple(shape), lambda *_: (layer,) + zeros)


def _whole_spec(shape):
    zeros = (0,) * len(shape)
    return pl.BlockSpec(tuple(shape), lambda *_: zeros)


def _sigmoid(x):
    return 1.0 / (1.0 + jnp.exp(-x))


def _silu(x):
    return x * _sigmoid(x)


def _softplus(x):
    return jnp.maximum(x, 0.0) + jnp.log(1.0 + jnp.exp(-jnp.abs(x)))


def _gelu_tanh(x):
    return 0.5 * x * (1.0 + jnp.tanh(math.sqrt(2.0 / math.pi) * (x + 0.044715 * (x * x * x))))


def _rms(x, g):
    return x * lax.rsqrt(jnp.mean(x * x, axis=-1, keepdims=True) + EPS) * g


def _dot(a, b):
    return jnp.dot(a, b, preferred_element_type=F32)


def _split_bf16(x):
    hi = x.astype(BF16)
    lo = (x - hi.astype(F32)).astype(BF16)
    return hi, lo


def _norm_matmul_kernel(*refs, has_res):
    if has_res:
        x_ref, g_ref, w_ref, r_ref, o_ref, xn_ref = refs
    else:
        x_ref, g_ref, w_ref, o_ref, xn_ref = refs
        r_ref = None

    @pl.when(pl.program_id(1) == 0)
    def _():
        xn_ref[...] = _rms(x_ref[...], g_ref[...]).astype(BF16)

    acc = _dot(xn_ref[...], w_ref[...])
    if has_res:
        acc = acc + r_ref[...]
    o_ref[...] = acc.astype(o_ref.dtype)


def norm_matmul(x, g, w, layer, *, out_dtype=F32, tm, tn):
    m, k = x.shape
    n = w.shape[2]
    return pl.pallas_call(
        functools.partial(_norm_matmul_kernel, has_res=False),
        grid=(m // tm, n // tn),
        in_specs=[pl.BlockSpec((tm, k), lambda i, j: (i, 0)),
                  _layer_spec((1, k), layer),
                  pl.BlockSpec((None, k, tn), lambda i, j: (layer, 0, j))],
        out_specs=pl.BlockSpec((tm, tn), lambda i, j: (i, j)),
        out_shape=jax.ShapeDtypeStruct((m, n), out_dtype),
        scratch_shapes=[pltpu.VMEM((tm, k), BF16)],
        compiler_params=_cparams(("parallel", "arbitrary")),
        name="norm_matmul",
    )(x, g, w)


def _res_matmul_kernel(x_ref, w_ref, r_ref, o_ref):
    o_ref[...] = r_ref[...] + _dot(x_ref[...], w_ref[...])


def res_matmul(x, w, res, layer, *, tm, tn):
    m, k = x.shape
    n = w.shape[2]
    return pl.pallas_call(
        _res_matmul_kernel,
        grid=(m // tm, n // tn),
        in_specs=[pl.BlockSpec((tm, k), lambda i, j: (i, 0)),
                  pl.BlockSpec((None, k, tn), lambda i, j: (layer, 0, j)),
                  pl.BlockSpec((tm, tn), lambda i, j: (i, j))],
        out_specs=pl.BlockSpec((tm, tn), lambda i, j: (i, j)),
        out_shape=jax.ShapeDtypeStruct((m, n), F32),
        compiler_params=_cparams(("parallel", "arbitrary")),
        name="res_matmul",
    )(x, w, res)


def _out_proj_kernel(m0_ref, m1_ref, m2_ref, m3_ref, w_ref, r_ref, o_ref, xc_ref):
    @pl.when(pl.program_id(1) == 0)
    def _():
        for i, mr in enumerate((m0_ref, m1_ref, m2_ref, m3_ref)):
            xc_ref[:, i * GROUP_WIDTH:(i + 1) * GROUP_WIDTH] = mr[...]

    o_ref[...] = r_ref[...] + _dot(xc_ref[...], w_ref[...])


def out_proj(mixes, w, res, layer, *, tm, tn):
    m = res.shape[0]
    mix_spec = pl.BlockSpec((tm, GROUP_WIDTH), lambda i, j: (i, 0))
    return pl.pallas_call(
        _out_proj_kernel,
        grid=(m // tm, D_MODEL // tn),
        in_specs=[mix_spec] * 4 + [pl.BlockSpec((None, D_MODEL, tn), lambda i, j: (layer, 0, j)),
                                   pl.BlockSpec((tm, tn), lambda i, j: (i, j))],
        out_specs=pl.BlockSpec((tm, tn), lambda i, j: (i, j)),
        out_shape=jax.ShapeDtypeStruct((m, D_MODEL), F32),
        scratch_shapes=[pltpu.VMEM((tm, D_MODEL), BF16)],
        compiler_params=_cparams(("parallel", "arbitrary")),
        name="out_proj",
    )(*mixes, w, res)


def _mlp_kernel(x_ref, g_ref, wu_ref, wd_ref, o_ref, xn_ref):
    j = pl.program_id(1)

    @pl.when(j == 0)
    def _():
        x = x_ref[...]
        xn_ref[...] = _rms(x, g_ref[...]).astype(BF16)
        o_ref[...] = x

    a = jnp.square(jnp.maximum(_dot(xn_ref[...], wu_ref[...]), 0.0))
    o_ref[...] += _dot(a.astype(BF16), wd_ref[...])


def mlp(x, g, w_up, w_down, layer, *, tm, tf):
    m = x.shape[0]
    return pl.pallas_call(
        _mlp_kernel,
        grid=(m // tm, D_FF // tf),
        in_specs=[pl.BlockSpec((tm, D_MODEL), lambda i, j: (i, 0)),
                  _layer_spec((1, D_MODEL), layer),
                  pl.BlockSpec((None, D_MODEL, tf), lambda i, j: (layer, 0, j)),
                  pl.BlockSpec((None, tf, D_MODEL), lambda i, j: (layer, j, 0))],
        out_specs=pl.BlockSpec((tm, D_MODEL), lambda i, j: (i, 0)),
        out_shape=jax.ShapeDtypeStruct((m, D_MODEL), F32),
        scratch_shapes=[pltpu.VMEM((tm, D_MODEL), BF16)],
        compiler_params=_cparams(("parallel", "arbitrary")),
        name="mlp",
    )(x, g, w_up, w_down)


def _final_norm_kernel(x_ref, g_ref, o_ref):
    o_ref[...] = _rms(x_ref[...], g_ref[...])


def final_norm(x, g, *, rows, first_block, tm):
    return pl.pallas_call(
        _final_norm_kernel,
        grid=(rows // tm,),
        in_specs=[pl.BlockSpec((tm, D_MODEL), lambda i: (first_block + i, 0)), _whole_spec((1, D_MODEL))],
        out_specs=pl.BlockSpec((tm, D_MODEL), lambda i: (i, 0)),
        out_shape=jax.ShapeDtypeStruct((rows, D_MODEL), F32),
        compiler_params=_cparams(("parallel",)),
        name="final_norm",
    )(x, g.reshape(1, D_MODEL))


def _prompt_attn_kernel(q_ref, k_ref, v_ref, o_ref):
    scale = XA_HEAD_DIM ** -0.5
    for h in range(XA_HEADS):
        cs = slice(h * XA_HEAD_DIM, (h + 1) * XA_HEAD_DIM)
        kh = k_ref[0, :, cs].astype(BF16)
        vh = v_ref[0, :, cs].astype(BF16)
        s = lax.dot_general(q_ref[:, cs], kh, (((1,), (1,)), ((), ())), preferred_element_type=F32) * scale
        p = jnp.exp(s - jnp.max(s, axis=-1, keepdims=True))
        p = p / jnp.sum(p, axis=-1, keepdims=True)
        o_ref[:, cs] = _dot(p.astype(BF16), vh).astype(o_ref.dtype)


def prompt_attn(q, k, v, *, tq):
    nq = SEQ // tq
    return pl.pallas_call(
        _prompt_attn_kernel,
        grid=(BATCH, nq),
        in_specs=[pl.BlockSpec((tq, D_MODEL), lambda b, i: (b * nq + i, 0)),
                  pl.BlockSpec((1, N_MEM, D_MODEL), lambda b, i: (b, 0, 0)),
                  pl.BlockSpec((1, N_MEM, D_MODEL), lambda b, i: (b, 0, 0))],
        out_specs=pl.BlockSpec((tq, D_MODEL), lambda b, i: (b * nq + i, 0)),
        out_shape=jax.ShapeDtypeStruct((ALL_ROWS, D_MODEL), BF16),
        compiler_params=_cparams(("parallel", "arbitrary")),
        name="prompt_attn",
    )(q, k, v)


def _sample_attn_kernel(q_ref, k_ref, v_ref, o_ref):
    scale = XA_HEAD_DIM ** -0.5
    s = jnp.sum(k_ref[...] * q_ref[...], axis=-1, keepdims=True) * scale
    p = jnp.exp(s - jnp.max(s, axis=0, keepdims=True))
    p = p / jnp.sum(p, axis=0, keepdims=True)
    o_ref[...] = jnp.sum(p * v_ref[...], axis=0, keepdims=True).astype(o_ref.dtype)


def sample_attn(q, cache_k, cache_v, layer):
    row = pl.BlockSpec((1, XA_HEADS, XA_HEAD_DIM), lambda b: (b, 0, 0))
    mem = pl.BlockSpec((None, None, N_MEM, XA_HEADS, XA_HEAD_DIM), lambda b: (layer, b, 0, 0, 0))
    return pl.pallas_call(
        _sample_attn_kernel,
        grid=(DEC_BATCH,),
        in_specs=[row, mem, mem],
        out_specs=row,
        out_shape=jax.ShapeDtypeStruct((DEC_BATCH, XA_HEADS, XA_HEAD_DIM), BF16),
        compiler_params=_cparams(("parallel",)),
        name="sample_attn",
    )(q, cache_k, cache_v)


def _s5_abar(lr, li, ldt):
    delta = jnp.exp(ldt)
    mag = jnp.exp(lr * delta)
    return mag * jnp.cos(li * delta), mag * jnp.sin(li * delta)


def _s5_bbar_kernel(lr_ref, li_ref, ldt_ref, bre_ref, bim_ref, bbr_ref, bbi_ref):
    lr, li = lr_ref[...], li_ref[...]
    ar, ai = _s5_abar(lr, li, ldt_ref[...])
    den = lr * lr + li * li
    cr = ((ar - 1.0) * lr + ai * li) / den
    ci = (ai * lr - (ar - 1.0) * li) / den
    br, bi = bre_ref[...], bim_ref[...]
    bbr_ref[...] = cr * br - ci * bi
    bbi_ref[...] = cr * bi + ci * br


def _s5_pow_kernel(lr_ref, li_ref, ldt_ref, pr_ref, pi_ref):
    ar, ai = _s5_abar(lr_ref[...], li_ref[...], ldt_ref[...])
    qr, qi = ar, ai
    pr_ref[0] = qr
    pi_ref[0] = qi
    for e in range(1, S5_SEG):
        qr, qi = qr * ar - qi * ai, qr * ai + qi * ar
        pr_ref[e] = qr
        pi_ref[e] = qi


def s5_prepare(lam_re, lam_im, log_dt, b_re, b_im, c_re, c_im):
    dg = DEPTH * S5_GROUPS
    ldt = jnp.broadcast_to(log_dt[..., None], (DEPTH, S5_GROUPS, S5_STATE))
    rep = lambda a: jnp.repeat(a.reshape(dg, S5_STATE), S5_CH, axis=0)
    to_rows = lambda b: jnp.transpose(b, (0, 1, 3, 2)).reshape(dg * S5_CH, S5_STATE)
    shp = jax.ShapeDtypeStruct((dg * S5_CH, S5_STATE), F32)
    bbr, bbi = pl.pallas_call(_s5_bbar_kernel, out_shape=(shp, shp), name="s5_bbar")(
        rep(lam_re), rep(lam_im), rep(ldt), to_rows(b_re), to_rows(b_im))
    pshp = jax.ShapeDtypeStruct((S5_SEG, dg, S5_STATE), F32)
    pr, pi = pl.pallas_call(_s5_pow_kernel, out_shape=(pshp, pshp), name="s5_pow")(
        lam_re.reshape(dg, S5_STATE), lam_im.reshape(dg, S5_STATE), ldt.reshape(dg, S5_STATE))

    gpb = LANE // S5_CH
    nblk = S5_GROUPS // gpb
    eye = jnp.eye(gpb, dtype=F32)

    def b_blocks(bb):
        bb = bb.reshape(DEPTH, nblk, gpb, S5_CH, S5_STATE)
        return jnp.einsum("digkn,gh->digkhn", bb, eye).reshape(DEPTH, nblk, LANE, gpb * S5_STATE).astype(BF16)

    def c_blocks(cc):
        cc = cc.reshape(DEPTH, nblk, gpb, S5_CH, S5_STATE)
        return jnp.einsum("digkn,gh->dignhk", cc, eye).reshape(DEPTH, nblk, gpb * S5_STATE, LANE).astype(BF16)

    def pow_rows(p):
        return jnp.transpose(p.reshape(S5_SEG, DEPTH, S5_LANES), (1, 0, 2))

    pw_r, pw_i = pow_rows(pr), pow_rows(pi)
    tile = lambda p: jnp.broadcast_to(p[:, :, None, :], (DEPTH, S5_SEG, SUBLANE, S5_LANES))
    return dict(b_re=b_blocks(bbr), b_im=b_blocks(bbi), c_re=c_blocks(c_re), c_imn=c_blocks(-c_im),
                pw_re=pw_r, pw_im=pw_i, pwt_re=tile(pw_r), pwt_im=tile(pw_i))


def _s5_bu(ub, bre_ref, bim_ref):
    res_r, res_i = [], []
    for i in range(GROUP_WIDTH // LANE):
        ui = ub[:, i * LANE:(i + 1) * LANE]
        res_r.append(_dot(ui, bre_ref[i]))
        res_i.append(_dot(ui, bim_ref[i]))
    return res_r, res_i


def _s5_tail(u, hs_re, hs_im, cre_ref, cimn_ref, d_ref, wglu_ref, bglu_ref, g_ref):
    ys = []
    for i in range(GROUP_WIDTH // LANE):
        ys.append(_dot(hs_re(i).astype(BF16), cre_ref[i]) + _dot(hs_im(i).astype(BF16), cimn_ref[i]))
    y = jnp.concatenate(ys, axis=-1) + d_ref[...] * u
    y = _gelu_tanh(y)
    y = y * _sigmoid(_dot(y.astype(BF16), wglu_ref[...]) + bglu_ref[...])
    return _rms(y, g_ref[...])


def _s5_param_specs(layer):
    nblk = GROUP_WIDTH // LANE
    sblk = S5_LANES // nblk
    return [_layer_spec((nblk, LANE, sblk), layer),
            _layer_spec((nblk, LANE, sblk), layer),
            _layer_spec((nblk, sblk, LANE), layer),
            _layer_spec((nblk, sblk, LANE), layer),
            _layer_spec((1, GROUP_WIDTH), layer),
            _layer_spec((GROUP_WIDTH, GROUP_WIDTH), layer),
            _layer_spec((1, GROUP_WIDTH), layer),
            _layer_spec((1, GROUP_WIDTH), layer)]


def _s5_param_args(sp):
    return (sp["s5_b_re"], sp["s5_b_im"], sp["s5_c_re"], sp["s5_c_imn"], sp["s5_d"], sp["s5_w_glu"],
            sp["s5_b_glu"], sp["g_s5"])


def _cmul_add(x_r, x_i, a_r, a_i, h_r, h_i):
    return x_r + a_r * h_r - a_i * h_i, x_i + a_r * h_i + a_i * h_r


def _prompt_s5_kernel(u_ref, perm_ref, unperm_ref, pw_re_ref, pw_im_ref, pwt_re_ref, pwt_im_ref,
                      bre_ref, bim_ref, cre_ref, cimn_ref, d_ref, wglu_ref, bglu_ref, g_ref,
                      mix_ref, sre_ref, sim_ref, hre_ref, him_ref, cr_ref, ci_ref):
    sblk = S5_LANES // (GROUP_WIDTH // LANE)
    W = S5_SCAN_LANES

    @pl.when(pl.program_id(1) == 0)
    def _():
        cr_ref[...] = jnp.zeros_like(cr_ref)
        ci_ref[...] = jnp.zeros_like(ci_ref)

    u_hi, u_lo = _split_bf16(u_ref[...])
    ub = _dot(perm_ref[...], u_hi)
    u = ub + _dot(perm_ref[...], u_lo)
    bu_r, bu_i = _s5_bu(ub.astype(BF16), bre_ref, bim_ref)
    for i in range(len(bu_r)):
        hre_ref[:, i * sblk:(i + 1) * sblk] = bu_r[i]
        him_ref[:, i * sblk:(i + 1) * sblk] = bu_i[i]

    row = lax.broadcasted_iota(jnp.int32, (SUBLANE, W), 0)
    for lb in range(0, S5_LANES, W):
        ls = slice(lb, lb + W)
        a_r = jnp.broadcast_to(pw_re_ref[0:1, ls], (SUBLANE, W))
        a_i = jnp.broadcast_to(pw_im_ref[0:1, ls], (SUBLANE, W))

        def local_step(j, h, ls=ls, a_r=a_r, a_i=a_i):
            rs = pl.ds(pl.multiple_of(j * SUBLANE, SUBLANE), SUBLANE)
            n_r, n_i = _cmul_add(hre_ref[rs, ls], him_ref[rs, ls], a_r, a_i, h[0], h[1])
            hre_ref[rs, ls] = n_r
            him_ref[rs, ls] = n_i
            return n_r, n_i

        zero = jnp.zeros((SUBLANE, W), F32)
        e_r, e_i = lax.fori_loop(0, S5_SEG, local_step, (zero, zero))

        s_r = pw_re_ref[S5_SEG - 1:S5_SEG, ls]
        s_i = pw_im_ref[S5_SEG - 1:S5_SEG, ls]
        c_r, c_i = cr_ref[:, ls], ci_ref[:, ls]
        in_r, in_i = zero, zero
        for s in range(SUBLANE):
            in_r = jnp.where(row == s, c_r, in_r)
            in_i = jnp.where(row == s, c_i, in_i)
            c_r, c_i = _cmul_add(e_r[s:s + 1, :], e_i[s:s + 1, :], s_r, s_i, c_r, c_i)
        cr_ref[:, ls] = c_r
        ci_ref[:, ls] = c_i

        def fix_step(j, carry, ls=ls, in_r=in_r, in_i=in_i):
            rs = pl.ds(pl.multiple_of(j * SUBLANE, SUBLANE), SUBLANE)
            n_r, n_i = _cmul_add(hre_ref[rs, ls], him_ref[rs, ls], pwt_re_ref[j, :, ls], pwt_im_ref[j, :, ls],
                                 in_r, in_i)
            hre_ref[rs, ls] = n_r
            him_ref[rs, ls] = n_i
            return carry

        lax.fori_loop(0, S5_SEG, fix_step, 0)

    sre_ref[0] = cr_ref[...]
    sim_ref[0] = ci_ref[...]
    y = _s5_tail(u, lambda i: hre_ref[:, i * sblk:(i + 1) * sblk], lambda i: him_ref[:, i * sblk:(i + 1) * sblk],
                 cre_ref, cimn_ref, d_ref, wglu_ref, bglu_ref, g_ref)
    mix_ref[...] = _dot(unperm_ref[...], y.astype(mix_ref.dtype)).astype(mix_ref.dtype)


def prompt_s5(proj, sp, layer):
    nt = SEQ // S5_CHUNK
    t_of_row = (np.arange(S5_CHUNK) % SUBLANE) * S5_SEG + np.arange(S5_CHUNK) // SUBLANE
    perm = jnp.asarray(np.eye(S5_CHUNK, dtype=np.float32)[t_of_row], BF16)
    st = jax.ShapeDtypeStruct((BATCH, 1, S5_LANES), F32)
    st_spec = pl.BlockSpec((1, 1, S5_LANES), lambda b, t: (b, 0, 0))
    return pl.pallas_call(
        _prompt_s5_kernel,
        grid=(BATCH, nt),
        in_specs=[pl.BlockSpec((S5_CHUNK, GROUP_WIDTH), lambda b, t: (b * nt + t, COL_U5 // GROUP_WIDTH)),
                  _whole_spec((S5_CHUNK, S5_CHUNK)), _whole_spec((S5_CHUNK, S5_CHUNK)),
                  _layer_spec((S5_SEG, S5_LANES), layer), _layer_spec((S5_SEG, S5_LANES), layer),
                  _layer_spec((S5_SEG, SUBLANE, S5_LANES), layer), _layer_spec((S5_SEG, SUBLANE, S5_LANES), layer)]
                 + _s5_param_specs(layer),
        out_specs=(pl.BlockSpec((S5_CHUNK, GROUP_WIDTH), lambda b, t: (b * nt + t, 0)), st_spec, st_spec),
        out_shape=(jax.ShapeDtypeStruct((ALL_ROWS, GROUP_WIDTH), BF16), st, st),
        scratch_shapes=[pltpu.VMEM((S5_CHUNK, S5_LANES), F32), pltpu.VMEM((S5_CHUNK, S5_LANES), F32),
                        pltpu.VMEM((1, S5_LANES), F32), pltpu.VMEM((1, S5_LANES), F32)],
        compiler_params=_cparams(("parallel", "arbitrary")),
        name="prompt_s5",
    )(proj, perm, perm.T, sp["s5_pw_re"], sp["s5_pw_im"], sp["s5_pwt_re"], sp["s5_pwt_im"], *_s5_param_args(sp))


def _prompt_ssd_kernel(xbc_ref, z_ref, dt_ref, cw_ref, cb_ref, dtb_ref, alog_ref, dexp_ref, g_ref,
                       mix_ref, cst_ref, hst_ref, xbuf_ref, h_ref):
    L = SSD_CHUNK
    hist = SSD_CONV - 1
    base = SUBLANE

    @pl.when(pl.program_id(1) == 0)
    def _():
        xbuf_ref[0:base, :] = jnp.zeros((base, XBC_WIDTH), F32)
        h_ref[...] = jnp.zeros_like(h_ref)

    @pl.when(pl.program_id(1) > 0)
    def _():
        xbuf_ref[base - hist:base, :] = xbuf_ref[base + L - hist:base + L, :]

    xbuf_ref[base:base + L, :] = xbc_ref[...]
    cst_ref[0] = xbuf_ref[base + L - hist:base + L, :]
    conv = cb_ref[...] + cw_ref[hist:hist + 1, :] * xbc_ref[...]
    for k in range(hist):
        conv = conv + cw_ref[k:k + 1, :] * xbuf_ref[base - hist + k:base - hist + k + L, :]
    xc = _silu(conv)
    xs = xc[:, :GROUP_WIDTH]
    ng = SSD_GROUPS * SSD_STATE
    bm = xc[:, GROUP_WIDTH:GROUP_WIDTH + ng].astype(BF16)
    cm = xc[:, GROUP_WIDTH + ng:].astype(BF16)

    dt = _softplus(dt_ref[...] + dtb_ref[...])
    a = -jnp.exp(alog_ref[...])
    ri = lax.broadcasted_iota(jnp.int32, (L, L), 0)
    ci = lax.broadcasted_iota(jnp.int32, (L, L), 1)
    causal = ri >= ci
    acum = jnp.dot(causal.astype(F32), dt * a, preferred_element_type=F32, precision=lax.Precision.HIGHEST)
    acum_t = acum.T
    last = acum[L - 1:L, :]
    to_end = jnp.exp(last - acum)
    e_acum = jnp.exp(acum)
    chunk_decay = jnp.exp(last)

    ys = []
    rep = SSD_HEADS // SSD_GROUPS
    cb = [lax.dot_general(cm[:, g * SSD_STATE:(g + 1) * SSD_STATE], bm[:, g * SSD_STATE:(g + 1) * SSD_STATE],
                          (((1,), (1,)), ((), ())), preferred_element_type=F32) for g in range(SSD_GROUPS)]
    for h in range(SSD_HEADS):
        g = h // rep
        hs = slice(h * SSD_HEAD_DIM, (h + 1) * SSD_HEAD_DIM)
        gs = slice(g * SSD_STATE, (g + 1) * SSD_STATE)
        seg = acum[:, h:h + 1] - acum_t[h:h + 1, :]
        decay = jnp.exp(jnp.where(causal, seg, -jnp.inf))
        xs_h = xs[:, hs]
        xdt = xs_h * dt[:, h:h + 1]
        y = _dot((cb[g] * decay).astype(BF16), xdt.astype(BF16))
        h_prev = h_ref[h]
        y_off = lax.dot_general(cm[:, gs], h_prev.astype(BF16), (((1,), (1,)), ((), ())),
                                preferred_element_type=F32)
        y = y + y_off * e_acum[:, h:h + 1]
        st = lax.dot_general((xdt * to_end[:, h:h + 1]).astype(BF16), bm[:, gs], (((0,), (0,)), ((), ())),
                             preferred_element_type=F32)
        h_ref[h] = h_prev * chunk_decay[:, h:h + 1] + st
        ys.append(y)
    y = (jnp.concatenate(ys, axis=-1) + dexp_ref[...] * xs) * _silu(z_ref[...])
    mix_ref[...] = _rms(y, g_ref[...]).astype(mix_ref.dtype)
    hst_ref[0] = h_ref[...]


def prompt_ssd(proj, sp, layer):
    nc = SEQ // SSD_CHUNK
    return pl.pallas_call(
        _prompt_ssd_kernel,
        grid=(BATCH, nc),
        in_specs=[pl.BlockSpec((SSD_CHUNK, XBC_WIDTH), lambda b, c: (b * nc + c, COL_XBC // XBC_WIDTH)),
                  pl.BlockSpec((SSD_CHUNK, GROUP_WIDTH), lambda b, c: (b * nc + c, COL_Z // GROUP_WIDTH)),
                  pl.BlockSpec((SSD_CHUNK, LANE), lambda b, c: (b * nc + c, COL_DT // LANE)),
                  _layer_spec((SSD_CONV, XBC_WIDTH), layer),
                  _layer_spec((1, XBC_WIDTH), layer),
                  _layer_spec((1, LANE), layer),
                  _layer_spec((1, LANE), layer),
                  _layer_spec((1, GROUP_WIDTH), layer),
                  _layer_spec((1, GROUP_WIDTH), layer)],
        out_specs=(pl.BlockSpec((SSD_CHUNK, GROUP_WIDTH), lambda b, c: (b * nc + c, 0)),
                   pl.BlockSpec((1, SSD_CONV - 1, XBC_WIDTH), lambda b, c: (b, 0, 0)),
                   pl.BlockSpec((1, SSD_HEADS, SSD_HEAD_DIM, SSD_STATE), lambda b, c: (b, 0, 0, 0))),
        out_shape=(jax.ShapeDtypeStruct((ALL_ROWS, GROUP_WIDTH), BF16),
                   jax.ShapeDtypeStruct((BATCH, SSD_CONV - 1, XBC_WIDTH), F32),
                   jax.ShapeDtypeStruct((BATCH, SSD_HEADS, SSD_HEAD_DIM, SSD_STATE), F32)),
        scratch_shapes=[pltpu.VMEM((SUBLANE + SSD_CHUNK, XBC_WIDTH), F32),
                        pltpu.VMEM((SSD_HEADS, SSD_HEAD_DIM, SSD_STATE), F32)],
        compiler_params=_cparams(("parallel", "arbitrary")),
        name="prompt_ssd",
    )(proj, proj, proj, sp["ssd_conv_w"], sp["ssd_conv_b"], sp["ssd_dt_bias"], sp["ssd_a_log"],
      sp["ssd_d_exp"], sp["g_ssd"])


def _pool_counts(pos, w):
    return jnp.minimum(w, pos + 1).astype(F32)


def _prompt_convpool_kernel(gb_ref, gc_ref, hv_ref, up_ref, scw_ref, pw_ref, ps_ref, gsc_ref, gpl_ref,
                            msc_ref, mpl_ref, scst_ref, plst_ref, vbuf_ref, pbuf_ref):
    rows = gb_ref.shape[0]
    vb = SUBLANE
    pb = 2 * SUBLANE
    vh = SC_CONV - 1
    t = pl.program_id(1)

    @pl.when(t == 0)
    def _():
        vbuf_ref[0:vb, :] = jnp.zeros((vb, GROUP_WIDTH), F32)
        pbuf_ref[0:pb, :] = jnp.zeros((pb, GROUP_WIDTH), F32)

    @pl.when(t > 0)
    def _():
        vbuf_ref[vb - vh:vb, :] = vbuf_ref[vb + rows - vh:vb + rows, :]
        pbuf_ref[0:pb, :] = pbuf_ref[rows:rows + pb, :]

    v = gc_ref[...] * hv_ref[...]
    vbuf_ref[vb:vb + rows, :] = v
    acc = scw_ref[vh:vh + 1, :] * v
    for k in range(vh):
        acc = acc + scw_ref[k:k + 1, :] * vbuf_ref[vb - vh + k:vb - vh + k + rows, :]
    msc_ref[...] = _rms(gb_ref[...] * acc, gsc_ref[...]).astype(msc_ref.dtype)
    scst_ref[0] = vbuf_ref[vb + rows - vh:vb + rows, :]

    u = up_ref[...]
    pbuf_ref[pb:pb + rows, :] = u
    pos = t * rows + lax.broadcasted_iota(jnp.int32, (rows, 1), 0)
    ys = []
    for gi, w in enumerate(POOL_WINDOWS):
        cs = slice(gi * POOL_GROUP, (gi + 1) * POOL_GROUP)
        s = u[:, cs]
        for j in range(1, w):
            s = s + pbuf_ref[pb - j:pb - j + rows, cs]
        pooled = s / _pool_counts(pos, w) - u[:, cs]
        ys.append(_dot(pooled.astype(BF16), pw_ref[gi]))
    y = jnp.concatenate(ys, axis=-1) * ps_ref[...]
    mpl_ref[...] = _rms(y, gpl_ref[...]).astype(mpl_ref.dtype)
    plst_ref[0] = pbuf_ref[pb + rows - POOL_HIST:pb + rows, :]


def prompt_convpool(proj, sp, layer):
    nt = SEQ // S5_CHUNK
    col = lambda c: pl.BlockSpec((S5_CHUNK, GROUP_WIDTH), lambda b, t: (b * nt + t, c // GROUP_WIDTH))
    mix_spec = pl.BlockSpec((S5_CHUNK, GROUP_WIDTH), lambda b, t: (b * nt + t, 0))
    mix_shape = jax.ShapeDtypeStruct((ALL_ROWS, GROUP_WIDTH), BF16)
    return pl.pallas_call(
        _prompt_convpool_kernel,
        grid=(BATCH, nt),
        in_specs=[col(COL_GB), col(COL_GC), col(COL_HV), col(COL_UP),
                  _layer_spec((SC_CONV, GROUP_WIDTH), layer),
                  _layer_spec((len(POOL_WINDOWS), POOL_GROUP, POOL_GROUP), layer),
                  _layer_spec((1, GROUP_WIDTH), layer), _layer_spec((1, GROUP_WIDTH), layer),
                  _layer_spec((1, GROUP_WIDTH), layer)],
        out_specs=(mix_spec, mix_spec,
                   pl.BlockSpec((1, SC_CONV - 1, GROUP_WIDTH), lambda b, t: (b, 0, 0)),
                   pl.BlockSpec((1, POOL_HIST, GROUP_WIDTH), lambda b, t: (b, 0, 0))),
        out_shape=(mix_shape, mix_shape,
                   jax.ShapeDtypeStruct((BATCH, SC_CONV - 1, GROUP_WIDTH), F32),
                   jax.ShapeDtypeStruct((BATCH, POOL_HIST, GROUP_WIDTH), F32)),
        scratch_shapes=[pltpu.VMEM((SUBLANE + S5_CHUNK, GROUP_WIDTH), F32),
                        pltpu.VMEM((2 * SUBLANE + S5_CHUNK, GROUP_WIDTH), F32)],
        compiler_params=_cparams(("parallel", "arbitrary")),
        name="prompt_convpool",
    )(proj, proj, proj, proj, sp["sc_conv_w"], sp["pool_w"], sp["pool_scale"], sp["g_sc"], sp["g_pool"])


def _sample_mix_kernel(proj_ref, cprev_ref, s5r_ref, s5i_ref, scprev_ref, plprev_ref,
                       cw_ref, cb_ref, dtb_ref, alog_ref,
                       pwr_ref, pwi_ref, bre_ref, bim_ref, cre_ref, cimn_ref, d5_ref, wglu_ref, bglu_ref, g5_ref,
                       scw_ref, pw_ref, ps_ref, gsc_ref, gpl_ref,
                       m5_ref, msc_ref, mpl_ref, cst_ref, s5ro_ref, s5io_ref, scst_ref, plst_ref,
                       xs_ref, xdt_ref, da_ref, b2_ref, c2_ref):
    nb = proj_ref.shape[0]
    W = GROUP_WIDTH
    xbc = proj_ref[:, COL_XBC:COL_XBC + XBC_WIDTH]
    hist = SSD_CONV - 1
    conv = cb_ref[...] + cw_ref[hist:hist + 1, :] * xbc
    for k in range(hist):
        conv = conv + cw_ref[k:k + 1, :] * cprev_ref[:, k * XBC_WIDTH:(k + 1) * XBC_WIDTH]
    cst_ref[:, 0:(hist - 1) * XBC_WIDTH] = cprev_ref[:, XBC_WIDTH:hist * XBC_WIDTH]
    cst_ref[:, (hist - 1) * XBC_WIDTH:hist * XBC_WIDTH] = xbc
    xc = _silu(conv)
    xs = xc[:, :W]
    bm = xc[:, W:W + LANE]
    cm = xc[:, W + LANE:W + 2 * LANE]
    dt = _softplus(proj_ref[:, COL_DT:COL_DT + LANE] + dtb_ref[...])
    da = jnp.exp(dt * (-jnp.exp(alog_ref[...])))
    lane_w = lax.broadcasted_iota(jnp.int32, (nb, W), 1)
    dt_exp = jnp.zeros((nb, W), F32)
    for h in range(SSD_HEADS):
        dt_exp = jnp.where(lane_w // SSD_HEAD_DIM == h, dt[:, h:h + 1], dt_exp)
        da_ref[:, h * LANE:(h + 1) * LANE] = jnp.broadcast_to(da[:, h:h + 1], (nb, LANE))
    xs_ref[...] = xs
    xdt_ref[...] = xs * dt_exp
    lane = lax.broadcasted_iota(jnp.int32, (nb, LANE), 1)
    low = lane < SSD_STATE
    for src, dst in ((bm, b2_ref), (cm, c2_ref)):
        swapped = pltpu.roll(src, SSD_STATE, 1)
        dst[:, 0:LANE] = jnp.where(low, src, swapped)
        dst[:, LANE:2 * LANE] = jnp.where(low, swapped, src)

    u5 = proj_ref[:, COL_U5:COL_U5 + W]
    bu_r, bu_i = _s5_bu(u5.astype(BF16), bre_ref, bim_ref)
    sblk = S5_LANES // len(bu_r)
    for i in range(len(bu_r)):
        ls = slice(i * sblk, (i + 1) * sblk)
        n_r, n_i = _cmul_add(bu_r[i], bu_i[i], pwr_ref[0:1, ls], pwi_ref[0:1, ls], s5r_ref[:, ls], s5i_ref[:, ls])
        s5ro_ref[:, ls] = n_r
        s5io_ref[:, ls] = n_i
    y5 = _s5_tail(u5, lambda i: s5ro_ref[:, i * sblk:(i + 1) * sblk], lambda i: s5io_ref[:, i * sblk:(i + 1) * sblk],
                  cre_ref, cimn_ref, d5_ref, wglu_ref, bglu_ref, g5_ref)
    m5_ref[...] = y5.astype(m5_ref.dtype)

    v = proj_ref[:, COL_GC:COL_GC + W] * proj_ref[:, COL_HV:COL_HV + W]
    vh = SC_CONV - 1
    acc = scw_ref[vh:vh + 1, :] * v
    for k in range(vh):
        acc = acc + scw_ref[k:k + 1, :] * scprev_ref[:, k * W:(k + 1) * W]
    msc_ref[...] = _rms(proj_ref[:, COL_GB:COL_GB + W] * acc, gsc_ref[...]).astype(msc_ref.dtype)
    scst_ref[:, 0:(vh - 1) * W] = scprev_ref[:, W:vh * W]
    scst_ref[:, (vh - 1) * W:vh * W] = v

    up = proj_ref[:, COL_UP:COL_UP + W]
    ys = []
    for gi, w in enumerate(POOL_WINDOWS):
        cs = slice(gi * POOL_GROUP, (gi + 1) * POOL_GROUP)
        s = up[:, cs]
        for j in range(1, w):
            k = POOL_HIST - j
            s = s + plprev_ref[:, k * W + gi * POOL_GROUP:k * W + (gi + 1) * POOL_GROUP]
        pooled = s / float(min(w, PAST_LEN + 1)) - up[:, cs]
        ys.append(_dot(pooled.astype(BF16), pw_ref[gi]))
    y = jnp.concatenate(ys, axis=-1) * ps_ref[...]
    mpl_ref[...] = _rms(y, gpl_ref[...]).astype(mpl_ref.dtype)
    plst_ref[:, 0:(POOL_HIST - 1) * W] = plprev_ref[:, W:POOL_HIST * W]
    plst_ref[:, (POOL_HIST - 1) * W:POOL_HIST * W] = up


def sample_mix(proj, states, sp, layer):
    nb = DEC_BATCH
    W = GROUP_WIDTH
    f = lambda n: jax.ShapeDtypeStruct((nb, n), F32)
    b = lambda n: jax.ShapeDtypeStruct((nb, n), BF16)
    out_widths = ((SSD_CONV - 1) * XBC_WIDTH, S5_LANES, S5_LANES, (SC_CONV - 1) * W, POOL_HIST * W,
                  W, W, SSD_HEADS * LANE, SSD_GROUPS * LANE, SSD_GROUPS * LANE)
    out_shape = (b(W), b(W), b(W)) + tuple(f(n) for n in out_widths)
    nblk = GROUP_WIDTH // LANE
    in_specs = ([pl.BlockSpec((nb, IN_PAD), lambda i: (SAMPLE_BLOCK, 0))]
                + [_layer_spec((nb, s.shape[2]), layer) for s in states]
                + [_layer_spec((SSD_CONV, XBC_WIDTH), layer), _layer_spec((1, XBC_WIDTH), layer),
                   _layer_spec((1, LANE), layer), _layer_spec((1, LANE), layer),
                   _layer_spec((S5_SEG, S5_LANES), layer), _layer_spec((S5_SEG, S5_LANES), layer)]
                + _s5_param_specs(layer)
                + [_layer_spec((SC_CONV, W), layer), _layer_spec((len(POOL_WINDOWS), POOL_GROUP, POOL_GROUP), layer),
                   _layer_spec((1, W), layer), _layer_spec((1, W), layer), _layer_spec((1, W), layer)])
    return pl.pallas_call(
        _sample_mix_kernel,
        grid=(1,),
        in_specs=in_specs,
        out_specs=tuple(_whole_spec(s.shape) for s in out_shape),
        out_shape=out_shape,
        compiler_params=_cparams(("arbitrary",)),
        name="sample_mix",
    )(proj, *states, sp["ssd_conv_w"], sp["ssd_conv_b"], sp["ssd_dt_bias"], sp["ssd_a_log"],
      sp["s5_pw_re"], sp["s5_pw_im"], *_s5_param_args(sp),
      sp["sc_conv_w"], sp["pool_w"], sp["pool_scale"], sp["g_sc"], sp["g_pool"])


def _sample_ssd_kernel(h0_ref, xdt_ref, da_ref, b2_ref, c2_ref, e_ref, r_ref, xs_ref, z_ref, dexp_ref, g_ref,
                       hn_ref, mix_ref, xrep_ref, prod_ref, y_ref):
    hp = pl.program_id(0)
    hpl = 2 * SSD_HEAD_DIM * SSD_STATE
    per_head = SSD_HEAD_DIM * SSD_STATE
    x_hi, x_lo = _split_bf16(xdt_ref[...])
    xrep_ref[...] = _dot(x_hi, e_ref[...]) + _dot(x_lo, e_ref[...])
    b2, c2 = b2_ref[...], c2_ref[...]
    for j in range(hpl // LANE):
        ls = slice(j * LANE, (j + 1) * LANE)
        hl = (j * LANE) // per_head
        hn = da_ref[:, hl * LANE:(hl + 1) * LANE] * h0_ref[:, ls] + xrep_ref[:, ls] * b2
        hn_ref[:, ls] = hn
        prod_ref[:, ls] = hn * c2
    p_hi, p_lo = _split_bf16(prod_ref[...])
    y_ref[hp] = _dot(p_hi, r_ref[...]) + _dot(p_lo, r_ref[...])

    @pl.when(hp == pl.num_programs(0) - 1)
    def _():
        y = jnp.concatenate([y_ref[i] for i in range(SSD_HEADS // 2)], axis=-1)
        y = (y + dexp_ref[...] * xs_ref[...]) * _silu(z_ref[...])
        mix_ref[...] = _rms(y, g_ref[...]).astype(mix_ref.dtype)


def sample_ssd(h0_all, xdt, da, b2, c2, xs, proj, sp, layer):
    nb = DEC_BATCH
    npairs = SSD_HEADS // 2
    hpl = 2 * SSD_HEAD_DIM * SSD_STATE
    expand = jnp.repeat(jnp.eye(LANE, dtype=BF16), SSD_STATE, axis=1)
    return pl.pallas_call(
        _sample_ssd_kernel,
        grid=(npairs,),
        in_specs=[pl.BlockSpec((None, nb, hpl), lambda i: (layer, 0, i)),
                  pl.BlockSpec((nb, LANE), lambda i: (0, i)),
                  pl.BlockSpec((nb, 2 * LANE), lambda i: (0, i)),
                  pl.BlockSpec((nb, LANE), lambda i: (0, i // (npairs // SSD_GROUPS))),
                  pl.BlockSpec((nb, LANE), lambda i: (0, i // (npairs // SSD_GROUPS))),
                  _whole_spec((LANE, hpl)),
                  _whole_spec((hpl, LANE)),
                  _whole_spec((nb, GROUP_WIDTH)),
                  pl.BlockSpec((nb, GROUP_WIDTH), lambda i: (SAMPLE_BLOCK, COL_Z // GROUP_WIDTH)),
                  _layer_spec((1, GROUP_WIDTH), layer),
                  _layer_spec((1, GROUP_WIDTH), layer)],
        out_specs=(pl.BlockSpec((nb, hpl), lambda i: (0, i)),
                   _whole_spec((nb, GROUP_WIDTH))),
        out_shape=(jax.ShapeDtypeStruct((nb, SSD_HEADS * SSD_HEAD_DIM * SSD_STATE), F32),
                   jax.ShapeDtypeStruct((nb, GROUP_WIDTH), BF16)),
        scratch_shapes=[pltpu.VMEM((nb, hpl), F32), pltpu.VMEM((nb, hpl), F32),
                        pltpu.VMEM((npairs, nb, LANE), F32)],
        compiler_params=_cparams(("arbitrary",)),
        name="sample_ssd",
    )(h0_all, xdt, da, b2, c2, expand, expand.T, xs, proj, sp["ssd_d_exp"], sp["g_ssd"])


def _regroup_w_in(w_in):
    sizes = (GROUP_WIDTH, XBC_WIDTH, SSD_HEADS, GROUP_WIDTH, GROUP_WIDTH, GROUP_WIDTH, GROUP_WIDTH, GROUP_WIDTH)
    offs = np.concatenate([[0], np.cumsum(sizes)])
    z, xbc, dt, u5, gb, gc, hv, up = [w_in[..., int(offs[i]):int(offs[i + 1])] for i in range(len(sizes))]
    pad = jnp.zeros((DEPTH, D_MODEL, IN_PAD - COL_DT - SSD_HEADS), w_in.dtype)
    return jnp.concatenate([z, u5, gb, gc, hv, up, xbc, dt, pad], axis=-1).astype(BF16)


def _stacked_params(w):
    row = lambda v: v.reshape(DEPTH, 1, -1)
    pad_heads = lambda v: row(jnp.pad(v, ((0, 0), (0, LANE - SSD_HEADS))))
    g_mix = w["mix_out_g"].reshape(DEPTH, 4, 1, GROUP_WIDTH)
    sp = dict(w_in=_regroup_w_in(w["w_in"]))
    for k in ("w_out", "w_q", "w_k", "w_v", "w_o", "w_up", "w_down", "s5_w_glu", "pool_w"):
        sp[k] = w[k].astype(BF16)
    for k in ("norm_mix_g", "norm_xa_g", "norm_mem_g", "norm_mlp_g", "ssd_conv_b", "s5_d", "s5_b_glu", "pool_scale"):
        sp[k] = row(w[k])
    sp.update(ssd_conv_w=w["ssd_conv_w"], sc_conv_w=w["sc_conv_w"],
              ssd_dt_bias=pad_heads(w["ssd_dt_bias"]), ssd_a_log=pad_heads(w["ssd_a_log"]),
              ssd_d_exp=row(jnp.repeat(w["ssd_d"], SSD_HEAD_DIM, axis=1)),
              g_ssd=g_mix[:, 0], g_s5=g_mix[:, 1], g_sc=g_mix[:, 2], g_pool=g_mix[:, 3])
    s5 = s5_prepare(w["s5_lam_re"], w["s5_lam_im"], w["s5_log_dt"], w["s5_b_re"], w["s5_b_im"],
                    w["s5_c_re"], w["s5_c_im"])
    sp.update({"s5_" + k: v for k, v in s5.items()})
    return sp


def _forward(x_prompt, x_sample, mem_prompt, state_ssd_conv, state_ssd, state_s5_re, state_s5_im,
             state_sconv, state_pool, cache_mem_k, cache_mem_v, final_norm_g, w):
    sp = _stacked_params(w)
    flat = lambda s: s.reshape(DEPTH, DEC_BATCH, -1)
    s_states = tuple(flat(s) for s in (state_ssd_conv, state_s5_re, state_s5_im, state_sconv, state_pool))
    s_ssd0 = flat(state_ssd)
    h = jnp.concatenate([x_prompt.reshape(P_ROWS, D_MODEL), x_sample.reshape(DEC_BATCH, D_MODEL)], axis=0)
    mem = mem_prompt.reshape(BATCH * N_MEM, D_MODEL)
    p_out = [[] for _ in range(8)]
    s_out = [[] for _ in range(6)]
    put_sample = lambda full, rows: lax.dynamic_update_slice(full, rows, (P_ROWS, 0))
    for l in range(DEPTH):
        mk = norm_matmul(mem, sp["norm_mem_g"], sp["w_k"], l, tm=BATCH * N_MEM, tn=COL_TILE)
        mv = norm_matmul(mem, sp["norm_mem_g"], sp["w_v"], l, tm=BATCH * N_MEM, tn=COL_TILE)

        proj = norm_matmul(h, sp["norm_mix_g"], sp["w_in"], l, tm=ROW_TILE, tn=COL_TILE)
        m_ssd, p_conv, p_ssd = prompt_ssd(proj, sp, l)
        m_s5, p_s5r, p_s5i = prompt_s5(proj, sp, l)
        m_sc, m_pl, p_sc, p_pl = prompt_convpool(proj, sp, l)
        (s_m5, s_msc, s_mpl, s_conv, s_s5r, s_s5i, s_sc, s_pl, xs, xdt, da, b2, c2) = sample_mix(
            proj, s_states, sp, l)
        s_ssd, s_mssd = sample_ssd(s_ssd0, xdt, da, b2, c2, xs, proj, sp, l)
        mixes = (put_sample(m_ssd, s_mssd), put_sample(m_s5, s_m5),
                 put_sample(m_sc, s_msc), put_sample(m_pl, s_mpl))

        h = out_proj(mixes, sp["w_out"], h, l, tm=ROW_TILE, tn=COL_TILE)
        q = norm_matmul(h, sp["norm_xa_g"], sp["w_q"], l, out_dtype=BF16, tm=ROW_TILE, tn=COL_TILE)
        o = prompt_attn(q, mk.reshape(BATCH, N_MEM, D_MODEL), mv.reshape(BATCH, N_MEM, D_MODEL), tq=512)
        q_s = q[P_ROWS:].astype(F32).reshape(DEC_BATCH, XA_HEADS, XA_HEAD_DIM)
        o_s = sample_attn(q_s, cache_mem_k, cache_mem_v, l)
        o = put_sample(o, o_s.reshape(DEC_BATCH, D_MODEL))
        h = res_matmul(o, sp["w_o"], h, l, tm=ROW_TILE, tn=COL_TILE)
        h = mlp(h, sp["norm_mlp_g"], sp["w_up"], sp["w_down"], l, tm=ROW_TILE, tf=FF_TILE)

        for lst, val in zip(p_out, (p_conv, p_ssd, p_s5r, p_s5i, p_sc, p_pl, mk, mv)):
            lst.append(val)
        for lst, val in zip(s_out, (s_conv, s_ssd, s_s5r, s_s5i, s_sc, s_pl)):
            lst.append(val)

    y_prompt = final_norm(h, final_norm_g, rows=P_ROWS, first_block=0, tm=COL_TILE)
    y_sample = final_norm(h, final_norm_g, rows=DEC_BATCH, first_block=SAMPLE_BLOCK, tm=DEC_BATCH)
    p_shapes = ((BATCH, SSD_CONV - 1, XBC_WIDTH), (BATCH, SSD_HEADS, SSD_HEAD_DIM, SSD_STATE),
                (BATCH, S5_GROUPS, S5_STATE), (BATCH, S5_GROUPS, S5_STATE),
                (BATCH, SC_CONV - 1, GROUP_WIDTH), (BATCH, POOL_HIST, GROUP_WIDTH),
                (BATCH, N_MEM, XA_HEADS, XA_HEAD_DIM), (BATCH, N_MEM, XA_HEADS, XA_HEAD_DIM))
    s_shapes = tuple((DEC_BATCH,) + s[1:] for s in p_shapes[:6])
    stack = lambda vals, shape: jnp.stack(vals).reshape((DEPTH,) + shape)
    return ((y_prompt.reshape(BATCH, SEQ, D_MODEL), y_sample.reshape(DEC_BATCH, 1, D_MODEL))
            + tuple(stack(v, s) for v, s in zip(p_out, p_shapes))
            + tuple(stack(v, s) for v, s in zip(s_out, s_shapes)))


_forward_jit = jax.jit(_forward)


def kernel(x_prompt, x_sample, mem_prompt, state_ssd_conv, state_ssd, state_s5_re, state_s5_im, state_sconv, state_pool, cache_mem_k, cache_mem_v, norm_mix_g, w_in, ssd_conv_w, ssd_conv_b, ssd_dt_bias, ssd_a_log, ssd_d, s5_lam_re, s5_lam_im, s5_log_dt, s5_b_re, s5_b_im, s5_c_re, s5_c_im, s5_d, s5_w_glu, s5_b_glu, sc_conv_w, pool_w, pool_scale, mix_out_g, w_out, norm_xa_g, norm_mem_g, w_q, w_k, w_v, w_o, norm_mlp_g, w_up, w_down, final_norm_g):
    w = dict(norm_mix_g=norm_mix_g, w_in=w_in, ssd_conv_w=ssd_conv_w, ssd_conv_b=ssd_conv_b,
             ssd_dt_bias=ssd_dt_bias, ssd_a_log=ssd_a_log, ssd_d=ssd_d, s5_lam_re=s5_lam_re,
             s5_lam_im=s5_lam_im, s5_log_dt=s5_log_dt, s5_b_re=s5_b_re, s5_b_im=s5_b_im,
             s5_c_re=s5_c_re, s5_c_im=s5_c_im, s5_d=s5_d, s5_w_glu=s5_w_glu, s5_b_glu=s5_b_glu,
             sc_conv_w=sc_conv_w, pool_w=pool_w, pool_scale=pool_scale, mix_out_g=mix_out_g,
             w_out=w_out, norm_xa_g=norm_xa_g, norm_mem_g=norm_mem_g, w_q=w_q, w_k=w_k, w_v=w_v,
             w_o=w_o, norm_mlp_g=norm_mlp_g, w_up=w_up, w_down=w_down)
    return _forward_jit(x_prompt, x_sample, mem_prompt, state_ssd_conv, state_ssd, state_s5_re, state_s5_im,
                        state_sconv, state_pool, cache_mem_k, cache_mem_v, final_norm_g, w)
```

```python
import functools
import math

import numpy as np
import jax
import jax.numpy as jnp
from jax import lax
from jax.experimental import pallas as pl
from jax.experimental.pallas import tpu as pltpu

F32 = jnp.float32
BF16 = jnp.bfloat16

D_MODEL = 2048
BATCH = 4
SEQ = 2048
DEPTH = 2
DEC_BATCH = 128
PAST_LEN = 16384
GROUP_WIDTH = D_MODEL // 4
SSD_HEAD_DIM = 64
SSD_HEADS = GROUP_WIDTH // SSD_HEAD_DIM
SSD_GROUPS = 2
SSD_STATE = 64
SSD_CONV = 4
SSD_CHUNK = 128
XBC_WIDTH = GROUP_WIDTH + 2 * SSD_GROUPS * SSD_STATE
S5_CH = 16
S5_GROUPS = GROUP_WIDTH // S5_CH
S5_STATE = 64
S5_LANES = S5_GROUPS * S5_STATE
SC_CONV = 3
POOL_WINDOWS = (2, 4, 8, 16)
POOL_GROUP = GROUP_WIDTH // len(POOL_WINDOWS)
POOL_HIST = max(POOL_WINDOWS) - 1
N_MEM = 256
XA_HEADS = 4
XA_HEAD_DIM = D_MODEL // XA_HEADS
D_FF = 4 * D_MODEL
EPS = 1e-6
P_ROWS = BATCH * SEQ
ALL_ROWS = P_ROWS + DEC_BATCH

LANE = 128
SUBLANE = 8
VMEM_LIMIT = 56 * 1024 * 1024

COL_Z = 0
COL_XBC = 512
COL_DT = 1280
COL_U5 = 1536
COL_GB = 2048
COL_GC = 2560
COL_HV = 3072
COL_UP = 3584
IN_PAD = 4096
ZX_WIDTH = COL_DT

ROW_TILE = ALL_ROWS // 8
COL_TILE = 1024
FF_TILE = 512
SAMPLE_BLOCK = P_ROWS // DEC_BATCH

SAMPLE_ATTN_ROWS = 2
S5_CHUNK = 256
S5_SEG = S5_CHUNK // SUBLANE
S5_SCAN_LANES = 512


def _cparams(sem):
    return pltpu.CompilerParams(dimension_semantics=sem, vmem_limit_bytes=VMEM_LIMIT)


def _layer_spec(shape, layer):
    zeros = (0,) * len(shape)
    return pl.BlockSpec((None,) + tuple(shape), lambda *_: (layer,) + zeros)


def _whole_spec(shape):
    zeros = (0,) * len(shape)
    return pl.BlockSpec(tuple(shape), lambda *_: zeros)


def _sigmoid(x):
    return 1.0 / (1.0 + jnp.exp(-x))


def _silu(x):
    return x * _sigmoid(x)


def _softplus(x):
    return jnp.maximum(x, 0.0) + jnp.log(1.0 + jnp.exp(-jnp.abs(x)))


def _gelu_tanh(x):
    return 0.5 * x * (1.0 + jnp.tanh(math.sqrt(2.0 / math.pi) * (x + 0.044715 * (x * x * x))))


def _rms(x, g):
    return x * lax.rsqrt(jnp.mean(x * x, axis=-1, keepdims=True) + EPS) * g


def _dot(a, b):
    return jnp.dot(a, b, preferred_element_type=F32)


def _split_bf16(x):
    hi = x.astype(BF16)
    lo = (x - hi.astype(F32)).astype(BF16)
    return hi, lo


def _norm_matmul_kernel(*refs, has_res):
    if has_res:
        x_ref, g_ref, w_ref, r_ref, o_ref, xn_ref = refs
    else:
        x_ref, g_ref, w_ref, o_ref, xn_ref = refs
        r_ref = None

    @pl.when(pl.program_id(1) == 0)
    def _():
        xn_ref[...] = _rms(x_ref[...], g_ref[...]).astype(BF16)

    acc = _dot(xn_ref[...], w_ref[...])
    if has_res:
        acc = acc + r_ref[...]
    o_ref[...] = acc.astype(o_ref.dtype)


def norm_matmul(x, g, w, layer, *, out_dtype=F32, tm, tn):
    m, k = x.shape
    n = w.shape[2]
    return pl.pallas_call(
        functools.partial(_norm_matmul_kernel, has_res=False),
        grid=(m // tm, n // tn),
        in_specs=[pl.BlockSpec((tm, k), lambda i, j: (i, 0)),
                  _layer_spec((1, k), layer),
                  pl.BlockSpec((None, k, tn), lambda i, j: (layer, 0, j))],
        out_specs=pl.BlockSpec((tm, tn), lambda i, j: (i, j)),
        out_shape=jax.ShapeDtypeStruct((m, n), out_dtype),
        scratch_shapes=[pltpu.VMEM((tm, k), BF16)],
        compiler_params=_cparams(("parallel", "arbitrary")),
        name="norm_matmul",
    )(x, g, w)


def _res_matmul_kernel(x_ref, w_ref, r_ref, o_ref):
    o_ref[...] = r_ref[...] + _dot(x_ref[...], w_ref[...])


def res_matmul(x, w, res, layer, *, tm, tn):
    m, k = x.shape
    n = w.shape[2]
    return pl.pallas_call(
        _res_matmul_kernel,
        grid=(m // tm, n // tn),
        in_specs=[pl.BlockSpec((tm, k), lambda i, j: (i, 0)),
                  pl.BlockSpec((None, k, tn), lambda i, j: (layer, 0, j)),
                  pl.BlockSpec((tm, tn), lambda i, j: (i, j))],
        out_specs=pl.BlockSpec((tm, tn), lambda i, j: (i, j)),
        out_shape=jax.ShapeDtypeStruct((m, n), F32),
        compiler_params=_cparams(("parallel", "arbitrary")),
        name="res_matmul",
    )(x, w, res)


def _out_proj_kernel(m0_ref, m1_ref, m2_ref, m3_ref, w_ref, r_ref, o_ref, xc_ref):
    @pl.when(pl.program_id(1) == 0)
    def _():
        for i, mr in enumerate((m0_ref, m1_ref, m2_ref, m3_ref)):
            xc_ref[:, i * GROUP_WIDTH:(i + 1) * GROUP_WIDTH] = mr[...]

    o_ref[...] = r_ref[...] + _dot(xc_ref[...], w_ref[...])


def out_proj(mixes, w, res, layer, *, tm, tn):
    m = res.shape[0]
    mix_spec = pl.BlockSpec((tm, GROUP_WIDTH), lambda i, j: (i, 0))
    return pl.pallas_call(
        _out_proj_kernel,
        grid=(m // tm, D_MODEL // tn),
        in_specs=[mix_spec] * 4 + [pl.BlockSpec((None, D_MODEL, tn), lambda i, j: (layer, 0, j)),
                                   pl.BlockSpec((tm, tn), lambda i, j: (i, j))],
        out_specs=pl.BlockSpec((tm, tn), lambda i, j: (i, j)),
        out_shape=jax.ShapeDtypeStruct((m, D_MODEL), F32),
        scratch_shapes=[pltpu.VMEM((tm, D_MODEL), BF16)],
        compiler_params=_cparams(("parallel", "arbitrary")),
        name="out_proj",
    )(*mixes, w, res)


def _mlp_kernel(x_ref, g_ref, wu_ref, wd_ref, o_ref, xn_ref):
    j = pl.program_id(1)

    @pl.when(j == 0)
    def _():
        x = x_ref[...]
        xn_ref[...] = _rms(x, g_ref[...]).astype(BF16)
        o_ref[...] = x

    a = jnp.square(jnp.maximum(_dot(xn_ref[...], wu_ref[...]), 0.0))
    o_ref[...] += _dot(a.astype(BF16), wd_ref[...])


def mlp(x, g, w_up, w_down, layer, *, tm, tf):
    m = x.shape[0]
    return pl.pallas_call(
        _mlp_kernel,
        grid=(m // tm, D_FF // tf),
        in_specs=[pl.BlockSpec((tm, D_MODEL), lambda i, j: (i, 0)),
                  _layer_spec((1, D_MODEL), layer),
                  pl.BlockSpec((None, D_MODEL, tf), lambda i, j: (layer, 0, j)),
                  pl.BlockSpec((None, tf, D_MODEL), lambda i, j: (layer, j, 0))],
        out_specs=pl.BlockSpec((tm, D_MODEL), lambda i, j: (i, 0)),
        out_shape=jax.ShapeDtypeStruct((m, D_MODEL), F32),
        scratch_shapes=[pltpu.VMEM((tm, D_MODEL), BF16)],
        compiler_params=_cparams(("parallel", "arbitrary")),
        name="mlp",
    )(x, g, w_up, w_down)


def _final_norm_kernel(x_ref, g_ref, o_ref):
    o_ref[...] = _rms(x_ref[...], g_ref[...])


def final_norm(x, g, *, rows, first_block, tm):
    return pl.pallas_call(
        _final_norm_kernel,
        grid=(rows // tm,),
        in_specs=[pl.BlockSpec((tm, D_MODEL), lambda i: (first_block + i, 0)), _whole_spec((1, D_MODEL))],
        out_specs=pl.BlockSpec((tm, D_MODEL), lambda i: (i, 0)),
        out_shape=jax.ShapeDtypeStruct((rows, D_MODEL), F32),
        compiler_params=_cparams(("parallel",)),
        name="final_norm",
    )(x, g.reshape(1, D_MODEL))


def _prompt_attn_kernel(q_ref, k_ref, v_ref, o_ref):
    scale = XA_HEAD_DIM ** -0.5
    for h in range(XA_HEADS):
        cs = slice(h * XA_HEAD_DIM, (h + 1) * XA_HEAD_DIM)
        kh = k_ref[0, :, cs].astype(BF16)
        vh = v_ref[0, :, cs].astype(BF16)
        s = lax.dot_general(q_ref[:, cs], kh, (((1,), (1,)), ((), ())), preferred_element_type=F32) * scale
        p = jnp.exp(s - jnp.max(s, axis=-1, keepdims=True))
        p = p / jnp.sum(p, axis=-1, keepdims=True)
        o_ref[:, cs] = _dot(p.astype(BF16), vh).astype(o_ref.dtype)


def prompt_attn(q, k, v, *, tq):
    nq = SEQ // tq
    return pl.pallas_call(
        _prompt_attn_kernel,
        grid=(BATCH, nq),
        in_specs=[pl.BlockSpec((tq, D_MODEL), lambda b, i: (b * nq + i, 0)),
                  pl.BlockSpec((1, N_MEM, D_MODEL), lambda b, i: (b, 0, 0)),
                  pl.BlockSpec((1, N_MEM, D_MODEL), lambda b, i: (b, 0, 0))],
        out_specs=pl.BlockSpec((tq, D_MODEL), lambda b, i: (b * nq + i, 0)),
        out_shape=jax.ShapeDtypeStruct((ALL_ROWS, D_MODEL), BF16),
        compiler_params=_cparams(("parallel", "arbitrary")),
        name="prompt_attn",
    )(q, k, v)


def _sample_attn_kernel(q_ref, k_ref, v_ref, o_ref):
    scale = XA_HEAD_DIM ** -0.5
    half, quarter = XA_HEAD_DIM // 2, XA_HEAD_DIM // 4
    for r in range(SAMPLE_ATTN_ROWS):
        prod = k_ref[r] * (q_ref[r] * scale)
        fold = prod[:, :, :half] + prod[:, :, half:]
        fold = fold[:, :, :quarter] + fold[:, :, quarter:]
        s = jnp.sum(fold, axis=-1, keepdims=True)
        e = jnp.exp(s - jnp.max(s, axis=0, keepdims=True))
        o = jnp.sum(e * v_ref[r], axis=0) / jnp.sum(e, axis=0)
        o_ref[r] = o.astype(o_ref.dtype)


def sample_attn(q, cache_k, cache_v, layer):
    nr = SAMPLE_ATTN_ROWS
    row = pl.BlockSpec((nr, XA_HEADS, XA_HEAD_DIM), lambda b: (b, 0, 0))
    mem = pl.BlockSpec((None, nr, N_MEM, XA_HEADS, XA_HEAD_DIM), lambda b: (layer, b, 0, 0, 0))
    return pl.pallas_call(
        _sample_attn_kernel,
        grid=(DEC_BATCH // nr,),
        in_specs=[row, mem, mem],
        out_specs=row,
        out_shape=jax.ShapeDtypeStruct((DEC_BATCH, XA_HEADS, XA_HEAD_DIM), BF16),
        compiler_params=_cparams(("parallel",)),
        name="sample_attn",
    )(q, cache_k, cache_v)


def _s5_abar(lr, li, ldt):
    delta = jnp.exp(ldt)
    mag = jnp.exp(lr * delta)
    return mag * jnp.cos(li * delta), mag * jnp.sin(li * delta)


def _s5_bbar_kernel(lr_ref, li_ref, ldt_ref, bre_ref, bim_ref, bbr_ref, bbi_ref):
    lr, li = lr_ref[...], li_ref[...]
    ar, ai = _s5_abar(lr, li, ldt_ref[...])
    den = lr * lr + li * li
    cr = ((ar - 1.0) * lr + ai * li) / den
    ci = (ai * lr - (ar - 1.0) * li) / den
    br, bi = bre_ref[...], bim_ref[...]
    bbr_ref[...] = cr * br - ci * bi
    bbi_ref[...] = cr * bi + ci * br


def _s5_pow_kernel(lr_ref, li_ref, ldt_ref, pr_ref, pi_ref):
    ar, ai = _s5_abar(lr_ref[...], li_ref[...], ldt_ref[...])
    qr, qi = ar, ai
    pr_ref[0] = qr
    pi_ref[0] = qi
    for e in range(1, S5_SEG):
        qr, qi = qr * ar - qi * ai, qr * ai + qi * ar
        pr_ref[e] = qr
        pi_ref[e] = qi


def s5_prepare(lam_re, lam_im, log_dt, b_re, b_im, c_re, c_im):
    dg = DEPTH * S5_GROUPS
    ldt = jnp.broadcast_to(log_dt[..., None], (DEPTH, S5_GROUPS, S5_STATE))
    rep = lambda a: jnp.repeat(a.reshape(dg, S5_STATE), S5_CH, axis=0)
    to_rows = lambda b: jnp.transpose(b, (0, 1, 3, 2)).reshape(dg * S5_CH, S5_STATE)
    shp = jax.ShapeDtypeStruct((dg * S5_CH, S5_STATE), F32)
    bbr, bbi = pl.pallas_call(_s5_bbar_kernel, out_shape=(shp, shp), name="s5_bbar")(
        rep(lam_re), rep(lam_im), rep(ldt), to_rows(b_re), to_rows(b_im))
    pshp = jax.ShapeDtypeStruct((S5_SEG, dg, S5_STATE), F32)
    pr, pi = pl.pallas_call(_s5_pow_kernel, out_shape=(pshp, pshp), name="s5_pow")(
        lam_re.reshape(dg, S5_STATE), lam_im.reshape(dg, S5_STATE), ldt.reshape(dg, S5_STATE))

    gpb = LANE // S5_CH
    nblk = S5_GROUPS // gpb
    eye = jnp.eye(gpb, dtype=F32)

    def b_blocks(bb):
        bb = bb.reshape(DEPTH, nblk, gpb, S5_CH, S5_STATE)
        return jnp.einsum("digkn,gh->digkhn", bb, eye).reshape(DEPTH, nblk, LANE, gpb * S5_STATE).astype(BF16)

    def c_blocks(cc):
        cc = cc.reshape(DEPTH, nblk, gpb, S5_CH, S5_STATE)
        return jnp.einsum("digkn,gh->dignhk", cc, eye).reshape(DEPTH, nblk, gpb * S5_STATE, LANE).astype(BF16)

    def pow_rows(p):
        return jnp.transpose(p.reshape(S5_SEG, DEPTH, S5_LANES), (1, 0, 2))

    pw_r, pw_i = pow_rows(pr), pow_rows(pi)
    tile = lambda p: jnp.broadcast_to(p[:, :, None, :], (DEPTH, S5_SEG, SUBLANE, S5_LANES))
    return dict(b_re=b_blocks(bbr), b_im=b_blocks(bbi), c_re=c_blocks(c_re), c_imn=c_blocks(-c_im),
                pw_re=pw_r, pw_im=pw_i, pwt_re=tile(pw_r), pwt_im=tile(pw_i))


def _s5_bu(ub, bre_ref, bim_ref):
    res_r, res_i = [], []
    for i in range(GROUP_WIDTH // LANE):
        ui = ub[:, i * LANE:(i + 1) * LANE]
        res_r.append(_dot(ui, bre_ref[i]))
        res_i.append(_dot(ui, bim_ref[i]))
    return res_r, res_i


def _s5_tail(u, hs_re, hs_im, cre_ref, cimn_ref, d_ref, wglu_ref, bglu_ref, g_ref):
    ys = []
    for i in range(GROUP_WIDTH // LANE):
        ys.append(_dot(hs_re(i).astype(BF16), cre_ref[i]) + _dot(hs_im(i).astype(BF16), cimn_ref[i]))
    y = jnp.concatenate(ys, axis=-1) + d_ref[...] * u
    y = _gelu_tanh(y)
    y = y * _sigmoid(_dot(y.astype(BF16), wglu_ref[...]) + bglu_ref[...])
    return _rms(y, g_ref[...])


def _s5_param_specs(layer):
    nblk = GROUP_WIDTH // LANE
    sblk = S5_LANES // nblk
    return [_layer_spec((nblk, LANE, sblk), layer),
            _layer_spec((nblk, LANE, sblk), layer),
            _layer_spec((nblk, sblk, LANE), layer),
            _layer_spec((nblk, sblk, LANE), layer),
            _layer_spec((1, GROUP_WIDTH), layer),
            _layer_spec((GROUP_WIDTH, GROUP_WIDTH), layer),
            _layer_spec((1, GROUP_WIDTH), layer),
            _layer_spec((1, GROUP_WIDTH), layer)]


def _s5_param_args(sp):
    return (sp["s5_b_re"], sp["s5_b_im"], sp["s5_c_re"], sp["s5_c_imn"], sp["s5_d"], sp["s5_w_glu"],
            sp["s5_b_glu"], sp["g_s5"])


def _cmul_add(x_r, x_i, a_r, a_i, h_r, h_i):
    return x_r + a_r * h_r - a_i * h_i, x_i + a_r * h_i + a_i * h_r


def _prompt_s5_kernel(u_ref, perm_ref, unperm_ref, pw_re_ref, pw_im_ref, pwt_re_ref, pwt_im_ref,
                      bre_ref, bim_ref, cre_ref, cimn_ref, d_ref, wglu_ref, bglu_ref, g_ref,
                      mix_ref, sre_ref, sim_ref, hre_ref, him_ref, cr_ref, ci_ref):
    sblk = S5_LANES // (GROUP_WIDTH // LANE)
    W = S5_SCAN_LANES

    @pl.when(pl.program_id(1) == 0)
    def _():
        cr_ref[...] = jnp.zeros_like(cr_ref)
        ci_ref[...] = jnp.zeros_like(ci_ref)

    u_hi, u_lo = _split_bf16(u_ref[...])
    ub = _dot(perm_ref[...], u_hi)
    u = ub + _dot(perm_ref[...], u_lo)
    bu_r, bu_i = _s5_bu(ub.astype(BF16), bre_ref, bim_ref)
    for i in range(len(bu_r)):
        hre_ref[:, i * sblk:(i + 1) * sblk] = bu_r[i]
        him_ref[:, i * sblk:(i + 1) * sblk] = bu_i[i]

    row = lax.broadcasted_iota(jnp.int32, (SUBLANE, W), 0)
    for lb in range(0, S5_LANES, W):
        ls = slice(lb, lb + W)
        a_r = jnp.broadcast_to(pw_re_ref[0:1, ls], (SUBLANE, W))
        a_i = jnp.broadcast_to(pw_im_ref[0:1, ls], (SUBLANE, W))

        def local_step(j, h, ls=ls, a_r=a_r, a_i=a_i):
            rs = pl.ds(pl.multiple_of(j * SUBLANE, SUBLANE), SUBLANE)
            n_r, n_i = _cmul_add(hre_ref[rs, ls], him_ref[rs, ls], a_r, a_i, h[0], h[1])
            hre_ref[rs, ls] = n_r
            him_ref[rs, ls] = n_i
            return n_r, n_i

        zero = jnp.zeros((SUBLANE, W), F32)
        e_r, e_i = lax.fori_loop(0, S5_SEG, local_step, (zero, zero))

        s_r = pw_re_ref[S5_SEG - 1:S5_SEG, ls]
        s_i = pw_im_ref[S5_SEG - 1:S5_SEG, ls]
        c_r, c_i = cr_ref[:, ls], ci_ref[:, ls]
        in_r, in_i = zero, zero
        for s in range(SUBLANE):
            in_r = jnp.where(row == s, c_r, in_r)
            in_i = jnp.where(row == s, c_i, in_i)
            c_r, c_i = _cmul_add(e_r[s:s + 1, :], e_i[s:s + 1, :], s_r, s_i, c_r, c_i)
        cr_ref[:, ls] = c_r
        ci_ref[:, ls] = c_i

        def fix_step(j, carry, ls=ls, in_r=in_r, in_i=in_i):
            rs = pl.ds(pl.multiple_of(j * SUBLANE, SUBLANE), SUBLANE)
            n_r, n_i = _cmul_add(hre_ref[rs, ls], him_ref[rs, ls], pwt_re_ref[j, :, ls], pwt_im_ref[j, :, ls],
                                 in_r, in_i)
            hre_ref[rs, ls] = n_r
            him_ref[rs, ls] = n_i
            return carry

        lax.fori_loop(0, S5_SEG, fix_step, 0)

    sre_ref[0] = cr_ref[...]
    sim_ref[0] = ci_ref[...]
    y = _s5_tail(u, lambda i: hre_ref[:, i * sblk:(i + 1) * sblk], lambda i: him_ref[:, i * sblk:(i + 1) * sblk],
                 cre_ref, cimn_ref, d_ref, wglu_ref, bglu_ref, g_ref)
    mix_ref[...] = _dot(unperm_ref[...], y.astype(mix_ref.dtype)).astype(mix_ref.dtype)


def prompt_s5(proj, sp, layer):
    nt = SEQ // S5_CHUNK
    t_of_row = (np.arange(S5_CHUNK) % SUBLANE) * S5_SEG + np.arange(S5_CHUNK) // SUBLANE
    perm = jnp.asarray(np.eye(S5_CHUNK, dtype=np.float32)[t_of_row], BF16)
    st = jax.ShapeDtypeStruct((BATCH, 1, S5_LANES), F32)
    st_spec = pl.BlockSpec((1, 1, S5_LANES), lambda b, t: (b, 0, 0))
    return pl.pallas_call(
        _prompt_s5_kernel,
        grid=(BATCH, nt),
        in_specs=[pl.BlockSpec((S5_CHUNK, GROUP_WIDTH), lambda b, t: (b * nt + t, COL_U5 // GROUP_WIDTH)),
                  _whole_spec((S5_CHUNK, S5_CHUNK)), _whole_spec((S5_CHUNK, S5_CHUNK)),
                  _layer_spec((S5_SEG, S5_LANES), layer), _layer_spec((S5_SEG, S5_LANES), layer),
                  _layer_spec((S5_SEG, SUBLANE, S5_LANES), layer), _layer_spec((S5_SEG, SUBLANE, S5_LANES), layer)]
                 + _s5_param_specs(layer),
        out_specs=(pl.BlockSpec((S5_CHUNK, GROUP_WIDTH), lambda b, t: (b * nt + t, 0)), st_spec, st_spec),
        out_shape=(jax.ShapeDtypeStruct((ALL_ROWS, GROUP_WIDTH), BF16), st, st),
        scratch_shapes=[pltpu.VMEM((S5_CHUNK, S5_LANES), F32), pltpu.VMEM((S5_CHUNK, S5_LANES), F32),
                        pltpu.VMEM((1, S5_LANES), F32), pltpu.VMEM((1, S5_LANES), F32)],
        compiler_params=_cparams(("parallel", "arbitrary")),
        name="prompt_s5",
    )(proj, perm, perm.T, sp["s5_pw_re"], sp["s5_pw_im"], sp["s5_pwt_re"], sp["s5_pwt_im"], *_s5_param_args(sp))


def _prompt_ssd_kernel(zx_ref, dt_ref, cw_ref, cb_ref, dtb_ref, alog_ref, dexp_ref, g_ref,
                       mix_ref, cst_ref, hst_ref, xbuf_ref, h_ref):
    L = SSD_CHUNK
    hist = SSD_CONV - 1
    base = SUBLANE

    @pl.when(pl.program_id(1) == 0)
    def _():
        xbuf_ref[0:base, :] = jnp.zeros((base, XBC_WIDTH), F32)
        h_ref[...] = jnp.zeros_like(h_ref)

    @pl.when(pl.program_id(1) > 0)
    def _():
        xbuf_ref[base - hist:base, :] = xbuf_ref[base + L - hist:base + L, :]

    xbc = zx_ref[:, COL_XBC:COL_XBC + XBC_WIDTH]
    xbuf_ref[base:base + L, :] = xbc
    cst_ref[0] = xbuf_ref[base + L - hist:base + L, :]
    conv = cb_ref[...] + cw_ref[hist:hist + 1, :] * xbc
    for k in range(hist):
        conv = conv + cw_ref[k:k + 1, :] * xbuf_ref[base - hist + k:base - hist + k + L, :]
    xc = _silu(conv)
    xs = xc[:, :GROUP_WIDTH]
    ng = SSD_GROUPS * SSD_STATE
    bm = xc[:, GROUP_WIDTH:GROUP_WIDTH + ng].astype(BF16)
    cm = xc[:, GROUP_WIDTH + ng:].astype(BF16)

    dt = _softplus(dt_ref[...] + dtb_ref[...])
    a = -jnp.exp(alog_ref[...])
    ri = lax.broadcasted_iota(jnp.int32, (L, L), 0)
    ci = lax.broadcasted_iota(jnp.int32, (L, L), 1)
    causal = ri >= ci
    acum = jnp.dot(causal.astype(F32), dt * a, preferred_element_type=F32, precision=lax.Precision.HIGHEST)
    acum_t = acum.T
    last = acum[L - 1:L, :]
    to_end = jnp.exp(last - acum)
    e_acum = jnp.exp(acum)
    chunk_decay = jnp.exp(last)

    ys = []
    rep = SSD_HEADS // SSD_GROUPS
    cb = [lax.dot_general(cm[:, g * SSD_STATE:(g + 1) * SSD_STATE], bm[:, g * SSD_STATE:(g + 1) * SSD_STATE],
                          (((1,), (1,)), ((), ())), preferred_element_type=F32) for g in range(SSD_GROUPS)]
    for h in range(SSD_HEADS):
        g = h // rep
        hs = slice(h * SSD_HEAD_DIM, (h + 1) * SSD_HEAD_DIM)
        gs = slice(g * SSD_STATE, (g + 1) * SSD_STATE)
        seg = acum[:, h:h + 1] - acum_t[h:h + 1, :]
        decay = jnp.exp(jnp.where(causal, seg, -jnp.inf))
        xs_h = xs[:, hs]
        xdt = xs_h * dt[:, h:h + 1]
        y = _dot((cb[g] * decay).astype(BF16), xdt.astype(BF16))
        h_prev = h_ref[h]
        y_off = lax.dot_general(cm[:, gs], h_prev.astype(BF16), (((1,), (1,)), ((), ())),
                                preferred_element_type=F32)
        y = y + y_off * e_acum[:, h:h + 1]
        st = lax.dot_general((xdt * to_end[:, h:h + 1]).astype(BF16), bm[:, gs], (((0,), (0,)), ((), ())),
                             preferred_element_type=F32)
        h_ref[h] = h_prev * chunk_decay[:, h:h + 1] + st
        ys.append(y)
    y = (jnp.concatenate(ys, axis=-1) + dexp_ref[...] * xs) * _silu(zx_ref[:, COL_Z:COL_Z + GROUP_WIDTH])
    mix_ref[...] = _rms(y, g_ref[...]).astype(mix_ref.dtype)
    hst_ref[0] = h_ref[...]


def prompt_ssd(proj, sp, layer):
    nc = SEQ // SSD_CHUNK
    return pl.pallas_call(
        _prompt_ssd_kernel,
        grid=(BATCH, nc),
        in_specs=[pl.BlockSpec((SSD_CHUNK, ZX_WIDTH), lambda b, c: (b * nc + c, 0)),
                  pl.BlockSpec((SSD_CHUNK, LANE), lambda b, c: (b * nc + c, COL_DT // LANE)),
                  _layer_spec((SSD_CONV, XBC_WIDTH), layer),
                  _layer_spec((1, XBC_WIDTH), layer),
                  _layer_spec((1, LANE), layer),
                  _layer_spec((1, LANE), layer),
                  _layer_spec((1, GROUP_WIDTH), layer),
                  _layer_spec((1, GROUP_WIDTH), layer)],
        out_specs=(pl.BlockSpec((SSD_CHUNK, GROUP_WIDTH), lambda b, c: (b * nc + c, 0)),
                   pl.BlockSpec((1, SSD_CONV - 1, XBC_WIDTH), lambda b, c: (b, 0, 0)),
                   pl.BlockSpec((1, SSD_HEADS, SSD_HEAD_DIM, SSD_STATE), lambda b, c: (b, 0, 0, 0))),
        out_shape=(jax.ShapeDtypeStruct((ALL_ROWS, GROUP_WIDTH), BF16),
                   jax.ShapeDtypeStruct((BATCH, SSD_CONV - 1, XBC_WIDTH), F32),
                   jax.ShapeDtypeStruct((BATCH, SSD_HEADS, SSD_HEAD_DIM, SSD_STATE), F32)),
        scratch_shapes=[pltpu.VMEM((SUBLANE + SSD_CHUNK, XBC_WIDTH), F32),
                        pltpu.VMEM((SSD_HEADS, SSD_HEAD_DIM, SSD_STATE), F32)],
        compiler_params=_cparams(("parallel", "arbitrary")),
        name="prompt_ssd",
    )(proj, proj, sp["ssd_conv_w"], sp["ssd_conv_b"], sp["ssd_dt_bias"], sp["ssd_a_log"],
      sp["ssd_d_exp"], sp["g_ssd"])


def _pool_counts(pos, w):
    return jnp.minimum(w, pos + 1).astype(F32)


def _prompt_convpool_kernel(gb_ref, gc_ref, hv_ref, up_ref, scw_ref, pw_ref, ps_ref, gsc_ref, gpl_ref,
                            msc_ref, mpl_ref, scst_ref, plst_ref, vbuf_ref, pbuf_ref):
    rows = gb_ref.shape[0]
    vb = SUBLANE
    pb = 2 * SUBLANE
    vh = SC_CONV - 1
    t = pl.program_id(1)

    @pl.when(t == 0)
    def _():
        vbuf_ref[0:vb, :] = jnp.zeros((vb, GROUP_WIDTH), F32)
        pbuf_ref[0:pb, :] = jnp.zeros((pb, GROUP_WIDTH), F32)

    @pl.when(t > 0)
    def _():
        vbuf_ref[vb - vh:vb, :] = vbuf_ref[vb + rows - vh:vb + rows, :]
        pbuf_ref[0:pb, :] = pbuf_ref[rows:rows + pb, :]

    v = gc_ref[...] * hv_ref[...]
    vbuf_ref[vb:vb + rows, :] = v
    acc = scw_ref[vh:vh + 1, :] * v
    for k in range(vh):
        acc = acc + scw_ref[k:k + 1, :] * vbuf_ref[vb - vh + k:vb - vh + k + rows, :]
    msc_ref[...] = _rms(gb_ref[...] * acc, gsc_ref[...]).astype(msc_ref.dtype)
    scst_ref[0] = vbuf_ref[vb + rows - vh:vb + rows, :]

    u = up_ref[...]
    pbuf_ref[pb:pb + rows, :] = u
    pos = t * rows + lax.broadcasted_iota(jnp.int32, (rows, 1), 0)
    ys = []
    for gi, w in enumerate(POOL_WINDOWS):
        cs = slice(gi * POOL_GROUP, (gi + 1) * POOL_GROUP)
        s = u[:, cs]
        for j in range(1, w):
            s = s + pbuf_ref[pb - j:pb - j + rows, cs]
        pooled = s / _pool_counts(pos, w) - u[:, cs]
        ys.append(_dot(pooled.astype(BF16), pw_ref[gi]))
    y = jnp.concatenate(ys, axis=-1) * ps_ref[...]
    mpl_ref[...] = _rms(y, gpl_ref[...]).astype(mpl_ref.dtype)
    plst_ref[0] = pbuf_ref[pb + rows - POOL_HIST:pb + rows, :]


def prompt_convpool(proj, sp, layer):
    nt = SEQ // S5_CHUNK
    col = lambda c: pl.BlockSpec((S5_CHUNK, GROUP_WIDTH), lambda b, t: (b * nt + t, c // GROUP_WIDTH))
    mix_spec = pl.BlockSpec((S5_CHUNK, GROUP_WIDTH), lambda b, t: (b * nt + t, 0))
    mix_shape = jax.ShapeDtypeStruct((ALL_ROWS, GROUP_WIDTH), BF16)
    return pl.pallas_call(
        _prompt_convpool_kernel,
        grid=(BATCH, nt),
        in_specs=[col(COL_GB), col(COL_GC), col(COL_HV), col(COL_UP),
                  _layer_spec((SC_CONV, GROUP_WIDTH), layer),
                  _layer_spec((len(POOL_WINDOWS), POOL_GROUP, POOL_GROUP), layer),
                  _layer_spec((1, GROUP_WIDTH), layer), _layer_spec((1, GROUP_WIDTH), layer),
                  _layer_spec((1, GROUP_WIDTH), layer)],
        out_specs=(mix_spec, mix_spec,
                   pl.BlockSpec((1, SC_CONV - 1, GROUP_WIDTH), lambda b, t: (b, 0, 0)),
                   pl.BlockSpec((1, POOL_HIST, GROUP_WIDTH), lambda b, t: (b, 0, 0))),
        out_shape=(mix_shape, mix_shape,
                   jax.ShapeDtypeStruct((BATCH, SC_CONV - 1, GROUP_WIDTH), F32),
                   jax.ShapeDtypeStruct((BATCH, POOL_HIST, GROUP_WIDTH), F32)),
        scratch_shapes=[pltpu.VMEM((SUBLANE + S5_CHUNK, GROUP_WIDTH), F32),
                        pltpu.VMEM((2 * SUBLANE + S5_CHUNK, GROUP_WIDTH), F32)],
        compiler_params=_cparams(("parallel", "arbitrary")),
        name="prompt_convpool",
    )(proj, proj, proj, proj, sp["sc_conv_w"], sp["pool_w"], sp["pool_scale"], sp["g_sc"], sp["g_pool"])


def _sample_mix_kernel(proj_ref, cprev_ref, s5r_ref, s5i_ref, scprev_ref, plprev_ref,
                       cw_ref, cb_ref, dtb_ref, alog_ref,
                       pwr_ref, pwi_ref, bre_ref, bim_ref, cre_ref, cimn_ref, d5_ref, wglu_ref, bglu_ref, g5_ref,
                       scw_ref, pw_ref, ps_ref, gsc_ref, gpl_ref,
                       m5_ref, msc_ref, mpl_ref, cst_ref, s5ro_ref, s5io_ref, scst_ref, plst_ref,
                       xs_ref, xdt_ref, da_ref, b2_ref, c2_ref):
    nb = proj_ref.shape[0]
    W = GROUP_WIDTH
    xbc = proj_ref[:, COL_XBC:COL_XBC + XBC_WIDTH]
    hist = SSD_CONV - 1
    conv = cb_ref[...] + cw_ref[hist:hist + 1, :] * xbc
    for k in range(hist):
        conv = conv + cw_ref[k:k + 1, :] * cprev_ref[:, k * XBC_WIDTH:(k + 1) * XBC_WIDTH]
    cst_ref[:, 0:(hist - 1) * XBC_WIDTH] = cprev_ref[:, XBC_WIDTH:hist * XBC_WIDTH]
    cst_ref[:, (hist - 1) * XBC_WIDTH:hist * XBC_WIDTH] = xbc
    xc = _silu(conv)
    xs = xc[:, :W]
    bm = xc[:, W:W + LANE]
    cm = xc[:, W + LANE:W + 2 * LANE]
    dt = _softplus(proj_ref[:, COL_DT:COL_DT + LANE] + dtb_ref[...])
    da = jnp.exp(dt * (-jnp.exp(alog_ref[...])))
    lane_w = lax.broadcasted_iota(jnp.int32, (nb, W), 1)
    dt_exp = jnp.zeros((nb, W), F32)
    for h in range(SSD_HEADS):
        dt_exp = jnp.where(lane_w // SSD_HEAD_DIM == h, dt[:, h:h + 1], dt_exp)
        da_ref[:, h * LANE:(h + 1) * LANE] = jnp.broadcast_to(da[:, h:h + 1], (nb, LANE))
    xs_ref[...] = xs
    xdt_ref[...] = xs * dt_exp
    lane = lax.broadcasted_iota(jnp.int32, (nb, LANE), 1)
    low = lane < SSD_STATE
    for src, dst in ((bm, b2_ref), (cm, c2_ref)):
        swapped = pltpu.roll(src, SSD_STATE, 1)
        dst[:, 0:LANE] = jnp.where(low, src, swapped)
        dst[:, LANE:2 * LANE] = jnp.where(low, swapped, src)

    u5 = proj_ref[:, COL_U5:COL_U5 + W]
    bu_r, bu_i = _s5_bu(u5.astype(BF16), bre_ref, bim_ref)
    sblk = S5_LANES // len(bu_r)
    for i in range(len(bu_r)):
        ls = slice(i * sblk, (i + 1) * sblk)
        n_r, n_i = _cmul_add(bu_r[i], bu_i[i], pwr_ref[0:1, ls], pwi_ref[0:1, ls], s5r_ref[:, ls], s5i_ref[:, ls])
        s5ro_ref[:, ls] = n_r
        s5io_ref[:, ls] = n_i
    y5 = _s5_tail(u5, lambda i: s5ro_ref[:, i * sblk:(i + 1) * sblk], lambda i: s5io_ref[:, i * sblk:(i + 1) * sblk],
                  cre_ref, cimn_ref, d5_ref, wglu_ref, bglu_ref, g5_ref)
    m5_ref[...] = y5.astype(m5_ref.dtype)

    v = proj_ref[:, COL_GC:COL_GC + W] * proj_ref[:, COL_HV:COL_HV + W]
    vh = SC_CONV - 1
    acc = scw_ref[vh:vh + 1, :] * v
    for k in range(vh):
        acc = acc + scw_ref[k:k + 1, :] * scprev_ref[:, k * W:(k + 1) * W]
    msc_ref[...] = _rms(proj_ref[:, COL_GB:COL_GB + W] * acc, gsc_ref[...]).astype(msc_ref.dtype)
    scst_ref[:, 0:(vh - 1) * W] = scprev_ref[:, W:vh * W]
    scst_ref[:, (vh - 1) * W:vh * W] = v

    up = proj_ref[:, COL_UP:COL_UP + W]
    ys = []
    for gi, w in enumerate(POOL_WINDOWS):
        cs = slice(gi * POOL_GROUP, (gi + 1) * POOL_GROUP)
        s = up[:, cs]
        for j in range(1, w):
            k = POOL_HIST - j
            s = s + plprev_ref[:, k * W + gi * POOL_GROUP:k * W + (gi + 1) * POOL_GROUP]
        pooled = s / float(min(w, PAST_LEN + 1)) - up[:, cs]
        ys.append(_dot(pooled.astype(BF16), pw_ref[gi]))
    y = jnp.concatenate(ys, axis=-1) * ps_ref[...]
    mpl_ref[...] = _rms(y, gpl_ref[...]).astype(mpl_ref.dtype)
    plst_ref[:, 0:(POOL_HIST - 1) * W] = plprev_ref[:, W:POOL_HIST * W]
    plst_ref[:, (POOL_HIST - 1) * W:POOL_HIST * W] = up


def sample_mix(proj, states, sp, layer):
    nb = DEC_BATCH
    W = GROUP_WIDTH
    f = lambda n: jax.ShapeDtypeStruct((nb, n), F32)
    b = lambda n: jax.ShapeDtypeStruct((nb, n), BF16)
    out_widths = ((SSD_CONV - 1) * XBC_WIDTH, S5_LANES, S5_LANES, (SC_CONV - 1) * W, POOL_HIST * W,
                  W, W, SSD_HEADS * LANE, SSD_GROUPS * LANE, SSD_GROUPS * LANE)
    out_shape = (b(W), b(W), b(W)) + tuple(f(n) for n in out_widths)
    nblk = GROUP_WIDTH // LANE
    in_specs = ([pl.BlockSpec((nb, IN_PAD), lambda i: (SAMPLE_BLOCK, 0))]
                + [_layer_spec((nb, s.shape[2]), layer) for s in states]
                + [_layer_spec((SSD_CONV, XBC_WIDTH), layer), _layer_spec((1, XBC_WIDTH), layer),
                   _layer_spec((1, LANE), layer), _layer_spec((1, LANE), layer),
                   _layer_spec((S5_SEG, S5_LANES), layer), _layer_spec((S5_SEG, S5_LANES), layer)]
                + _s5_param_specs(layer)
                + [_layer_spec((SC_CONV, W), layer), _layer_spec((len(POOL_WINDOWS), POOL_GROUP, POOL_GROUP), layer),
                   _layer_spec((1, W), layer), _layer_spec((1, W), layer), _layer_spec((1, W), layer)])
    return pl.pallas_call(
        _sample_mix_kernel,
        grid=(1,),
        in_specs=in_specs,
        out_specs=tuple(_whole_spec(s.shape) for s in out_shape),
        out_shape=out_shape,
        compiler_params=_cparams(("arbitrary",)),
        name="sample_mix",
    )(proj, *states, sp["ssd_conv_w"], sp["ssd_conv_b"], sp["ssd_dt_bias"], sp["ssd_a_log"],
      sp["s5_pw_re"], sp["s5_pw_im"], *_s5_param_args(sp),
      sp["sc_conv_w"], sp["pool_w"], sp["pool_scale"], sp["g_sc"], sp["g_pool"])


def _sample_ssd_kernel(h0_ref, xdt_ref, da_ref, b2_ref, c2_ref, e_ref, r_ref, xs_ref, z_ref, dexp_ref, g_ref,
                       hn_ref, mix_ref, xrep_ref, prod_ref, y_ref):
    hp = pl.program_id(0)
    hpl = 2 * SSD_HEAD_DIM * SSD_STATE
    per_head = SSD_HEAD_DIM * SSD_STATE
    x_hi, x_lo = _split_bf16(xdt_ref[...])
    xrep_ref[...] = _dot(x_hi, e_ref[...]) + _dot(x_lo, e_ref[...])
    b2, c2 = b2_ref[...], c2_ref[...]
    for j in range(hpl // LANE):
        ls = slice(j * LANE, (j + 1) * LANE)
        hl = (j * LANE) // per_head
        hn = da_ref[:, hl * LANE:(hl + 1) * LANE] * h0_ref[:, ls] + xrep_ref[:, ls] * b2
        hn_ref[:, ls] = hn
        prod_ref[:, ls] = hn * c2
    p_hi, p_lo = _split_bf16(prod_ref[...])
    y_ref[hp] = _dot(p_hi, r_ref[...]) + _dot(p_lo, r_ref[...])

    @pl.when(hp == pl.num_programs(0) - 1)
    def _():
        y = jnp.concatenate([y_ref[i] for i in range(SSD_HEADS // 2)], axis=-1)
        y = (y + dexp_ref[...] * xs_ref[...]) * _silu(z_ref[...])
        mix_ref[...] = _rms(y, g_ref[...]).astype(mix_ref.dtype)


def sample_ssd(h0_all, xdt, da, b2, c2, xs, proj, sp, layer):
    nb = DEC_BATCH
    npairs = SSD_HEADS // 2
    hpl = 2 * SSD_HEAD_DIM * SSD_STATE
    expand = jnp.repeat(jnp.eye(LANE, dtype=BF16), SSD_STATE, axis=1)
    return pl.pallas_call(
        _sample_ssd_kernel,
        grid=(npairs,),
        in_specs=[pl.BlockSpec((None, nb, hpl), lambda i: (layer, 0, i)),
                  pl.BlockSpec((nb, LANE), lambda i: (0, i)),
                  pl.BlockSpec((nb, 2 * LANE), lambda i: (0, i)),
                  pl.BlockSpec((nb, LANE), lambda i: (0, i // (npairs // SSD_GROUPS))),
                  pl.BlockSpec((nb, LANE), lambda i: (0, i // (npairs // SSD_GROUPS))),
                  _whole_spec((LANE, hpl)),
                  _whole_spec((hpl, LANE)),
                  _whole_spec((nb, GROUP_WIDTH)),
                  pl.BlockSpec((nb, GROUP_WIDTH), lambda i: (SAMPLE_BLOCK, COL_Z // GROUP_WIDTH)),
                  _layer_spec((1, GROUP_WIDTH), layer),
                  _layer_spec((1, GROUP_WIDTH), layer)],
        out_specs=(pl.BlockSpec((nb, hpl), lambda i: (0, i)),
                   _whole_spec((nb, GROUP_WIDTH))),
        out_shape=(jax.ShapeDtypeStruct((nb, SSD_HEADS * SSD_HEAD_DIM * SSD_STATE), F32),
                   jax.ShapeDtypeStruct((nb, GROUP_WIDTH), BF16)),
        scratch_shapes=[pltpu.VMEM((nb, hpl), F32), pltpu.VMEM((nb, hpl), F32),
                        pltpu.VMEM((npairs, nb, LANE), F32)],
        compiler_params=_cparams(("arbitrary",)),
        name="sample_ssd",
    )(h0_all, xdt, da, b2, c2, expand, expand.T, xs, proj, sp["ssd_d_exp"], sp["g_ssd"])


def _regroup_w_in(w_in):
    head = COL_DT + SSD_HEADS
    pad = jnp.zeros((DEPTH, D_MODEL, COL_U5 - head), w_in.dtype)
    return jnp.concatenate([w_in[..., :head], pad, w_in[..., head:]], axis=-1).astype(BF16)


def _stacked_params(w):
    row = lambda v: v.reshape(DEPTH, 1, -1)
    pad_heads = lambda v: row(jnp.pad(v, ((0, 0), (0, LANE - SSD_HEADS))))
    g_mix = w["mix_out_g"].reshape(DEPTH, 4, 1, GROUP_WIDTH)
    sp = dict(w_in=_regroup_w_in(w["w_in"]))
    for k in ("w_out", "w_q", "w_k", "w_v", "w_o", "w_up", "w_down", "s5_w_glu", "pool_w"):
        sp[k] = w[k].astype(BF16)
    for k in ("norm_mix_g", "norm_xa_g", "norm_mem_g", "norm_mlp_g", "ssd_conv_b", "s5_d", "s5_b_glu", "pool_scale"):
        sp[k] = row(w[k])
    sp.update(ssd_conv_w=w["ssd_conv_w"], sc_conv_w=w["sc_conv_w"],
              ssd_dt_bias=pad_heads(w["ssd_dt_bias"]), ssd_a_log=pad_heads(w["ssd_a_log"]),
              ssd_d_exp=row(jnp.repeat(w["ssd_d"], SSD_HEAD_DIM, axis=1)),
              g_ssd=g_mix[:, 0], g_s5=g_mix[:, 1], g_sc=g_mix[:, 2], g_pool=g_mix[:, 3])
    s5 = s5_prepare(w["s5_lam_re"], w["s5_lam_im"], w["s5_log_dt"], w["s5_b_re"], w["s5_b_im"],
                    w["s5_c_re"], w["s5_c_im"])
    sp.update({"s5_" + k: v for k, v in s5.items()})
    return sp


def _forward(x_prompt, x_sample, mem_prompt, state_ssd_conv, state_ssd, state_s5_re, state_s5_im,
             state_sconv, state_pool, cache_mem_k, cache_mem_v, final_norm_g, w):
    sp = _stacked_params(w)
    flat = lambda s: s.reshape(DEPTH, DEC_BATCH, -1)
    s_states = tuple(flat(s) for s in (state_ssd_conv, state_s5_re, state_s5_im, state_sconv, state_pool))
    s_ssd0 = flat(state_ssd)
    h = jnp.concatenate([x_prompt.reshape(P_ROWS, D_MODEL), x_sample.reshape(DEC_BATCH, D_MODEL)], axis=0)
    mem = mem_prompt.reshape(BATCH * N_MEM, D_MODEL)
    p_out = [[] for _ in range(8)]
    s_out = [[] for _ in range(6)]
    put_sample = lambda full, rows: lax.dynamic_update_slice(full, rows, (P_ROWS, 0))
    for l in range(DEPTH):
        mk = norm_matmul(mem, sp["norm_mem_g"], sp["w_k"], l, tm=BATCH * N_MEM, tn=COL_TILE)
        mv = norm_matmul(mem, sp["norm_mem_g"], sp["w_v"], l, tm=BATCH * N_MEM, tn=COL_TILE)

        proj = norm_matmul(h, sp["norm_mix_g"], sp["w_in"], l, tm=ROW_TILE, tn=COL_TILE)
        m_ssd, p_conv, p_ssd = prompt_ssd(proj, sp, l)
        m_s5, p_s5r, p_s5i = prompt_s5(proj, sp, l)
        m_sc, m_pl, p_sc, p_pl = prompt_convpool(proj, sp, l)
        (s_m5, s_msc, s_mpl, s_conv, s_s5r, s_s5i, s_sc, s_pl, xs, xdt, da, b2, c2) = sample_mix(
            proj, s_states, sp, l)
        s_ssd, s_mssd = sample_ssd(s_ssd0, xdt, da, b2, c2, xs, proj, sp, l)
        mixes = (put_sample(m_ssd, s_mssd), put_sample(m_s5, s_m5),
                 put_sample(m_sc, s_msc), put_sample(m_pl, s_mpl))

        h = out_proj(mixes, sp["w_out"], h, l, tm=ROW_TILE, tn=COL_TILE)
        q = norm_matmul(h, sp["norm_xa_g"], sp["w_q"], l, out_dtype=BF16, tm=ROW_TILE, tn=COL_TILE)
        o = prompt_attn(q, mk.reshape(BATCH, N_MEM, D_MODEL), mv.reshape(BATCH, N_MEM, D_MODEL), tq=512)
        q_s = q[P_ROWS:].astype(F32).reshape(DEC_BATCH, XA_HEADS, XA_HEAD_DIM)
        o_s = sample_attn(q_s, cache_mem_k, cache_mem_v, l)
        o = put_sample(o, o_s.reshape(DEC_BATCH, D_MODEL))
        h = res_matmul(o, sp["w_o"], h, l, tm=ROW_TILE, tn=COL_TILE)
        h = mlp(h, sp["norm_mlp_g"], sp["w_up"], sp["w_down"], l, tm=ROW_TILE, tf=FF_TILE)

        for lst, val in zip(p_out, (p_conv, p_ssd, p_s5r, p_s5i, p_sc, p_pl, mk, mv)):
            lst.append(val)
        for lst, val in zip(s_out, (s_conv, s_ssd, s_s5r, s_s5i, s_sc, s_pl)):
            lst.append(val)

    y_prompt = final_norm(h, final_norm_g, rows=P_ROWS, first_block=0, tm=COL_TILE)
    y_sample = final_norm(h, final_norm_g, rows=DEC_BATCH, first_block=SAMPLE_BLOCK, tm=DEC_BATCH)
    p_shapes = ((BATCH, SSD_CONV - 1, XBC_WIDTH), (BATCH, SSD_HEADS, SSD_HEAD_DIM, SSD_STATE),
                (BATCH, S5_GROUPS, S5_STATE), (BATCH, S5_GROUPS, S5_STATE),
                (BATCH, SC_CONV - 1, GROUP_WIDTH), (BATCH, POOL_HIST, GROUP_WIDTH),
                (BATCH, N_MEM, XA_HEADS, XA_HEAD_DIM), (BATCH, N_MEM, XA_HEADS, XA_HEAD_DIM))
    s_shapes = tuple((DEC_BATCH,) + s[1:] for s in p_shapes[:6])
    stack = lambda vals, shape: jnp.stack(vals).reshape((DEPTH,) + shape)
    return ((y_prompt.reshape(BATCH, SEQ, D_MODEL), y_sample.reshape(DEC_BATCH, 1, D_MODEL))
            + tuple(stack(v, s) for v, s in zip(p_out, p_shapes))
            + tuple(stack(v, s) for v, s in zip(s_out, s_shapes)))


_forward_jit = jax.jit(_forward)


def kernel(x_prompt, x_sample, mem_prompt, state_ssd_conv, state_ssd, state_s5_re, state_s5_im, state_sconv, state_pool, cache_mem_k, cache_mem_v, norm_mix_g, w_in, ssd_conv_w, ssd_conv_b, ssd_dt_bias, ssd_a_log, ssd_d, s5_lam_re, s5_lam_im, s5_log_dt, s5_b_re, s5_b_im, s5_c_re, s5_c_im, s5_d, s5_w_glu, s5_b_glu, sc_conv_w, pool_w, pool_scale, mix_out_g, w_out, norm_xa_g, norm_mem_g, w_q, w_k, w_v, w_o, norm_mlp_g, w_up, w_down, final_norm_g):
    w = dict(norm_mix_g=norm_mix_g, w_in=w_in, ssd_conv_w=ssd_conv_w, ssd_conv_b=ssd_conv_b,
             ssd_dt_bias=ssd_dt_bias, ssd_a_log=ssd_a_log, ssd_d=ssd_d, s5_lam_re=s5_lam_re,
             s5_lam_im=s5_lam_im, s5_log_dt=s5_log_dt, s5_b_re=s5_b_re, s5_b_im=s5_b_im,
             s5_c_re=s5_c_re, s5_c_im=s5_c_im, s5_d=s5_d, s5_w_glu=s5_w_glu, s5_b_glu=s5_b_glu,
             sc_conv_w=sc_conv_w, pool_w=pool_w, pool_scale=pool_scale, mix_out_g=mix_out_g,
             w_out=w_out, norm_xa_g=norm_xa_g, norm_mem_g=norm_mem_g, w_q=w_q, w_k=w_k, w_v=w_v,
             w_o=w_o, norm_mlp_g=norm_mlp_g, w_up=w_up, w_down=w_down)
    return _forward_jit(x_prompt, x_sample, mem_prompt, state_ssd_conv, state_ssd, state_s5_re, state_s5_im,
                        state_sconv, state_pool, cache_mem_k, cache_mem_v, final_norm_g, w)
```

```python
import functools
import math

import numpy as np
import jax
import jax.numpy as jnp
from jax import lax
from jax.experimental import pallas as pl
from jax.experimental.pallas import tpu as pltpu

F32 = jnp.float32
BF16 = jnp.bfloat16

D_MODEL = 2048
BATCH = 4
SEQ = 2048
DEPTH = 2
DEC_BATCH = 128
PAST_LEN = 16384
GROUP_WIDTH = D_MODEL // 4
SSD_HEAD_DIM = 64
SSD_HEADS = GROUP_WIDTH // SSD_HEAD_DIM
SSD_GROUPS = 2
SSD_STATE = 64
SSD_CONV = 4
SSD_CHUNK = 128
XBC_WIDTH = GROUP_WIDTH + 2 * SSD_GROUPS * SSD_STATE
S5_CH = 16
S5_GROUPS = GROUP_WIDTH // S5_CH
S5_STATE = 64
S5_LANES = S5_GROUPS * S5_STATE
SC_CONV = 3
POOL_WINDOWS = (2, 4, 8, 16)
POOL_GROUP = GROUP_WIDTH // len(POOL_WINDOWS)
POOL_HIST = max(POOL_WINDOWS) - 1
N_MEM = 256
XA_HEADS = 4
XA_HEAD_DIM = D_MODEL // XA_HEADS
D_FF = 4 * D_MODEL
EPS = 1e-6
P_ROWS = BATCH * SEQ
ALL_ROWS = P_ROWS + DEC_BATCH

LANE = 128
SUBLANE = 8
VMEM_LIMIT = 56 * 1024 * 1024

COL_Z = 0
COL_XBC = 512
COL_DT = 1280
COL_U5 = 1536
COL_GB = 2048
COL_GC = 2560
COL_HV = 3072
COL_UP = 3584
IN_PAD = 4096
ZX_WIDTH = COL_DT

ROW_TILE = ALL_ROWS // 8
COL_TILE = 1024
MLP_ROW_TILE = ALL_ROWS // 10
FF_TILE = 512
SAMPLE_BLOCK = P_ROWS // DEC_BATCH

SAMPLE_ATTN_ROWS = 2
S5_CHUNK = 256
S5_SEG = S5_CHUNK // SUBLANE
S5_SCAN_LANES = 512


def _cparams(sem):
    return pltpu.CompilerParams(dimension_semantics=sem, vmem_limit_bytes=VMEM_LIMIT)


def _layer_spec(shape, layer):
    zeros = (0,) * len(shape)
    return pl.BlockSpec((None,) + tuple(shape), lambda *_: (layer,) + zeros)


def _whole_spec(shape):
    zeros = (0,) * len(shape)
    return pl.BlockSpec(tuple(shape), lambda *_: zeros)


def _prompt_row_block(b, t, steps, rows):
    return jnp.minimum(b * steps + t, P_ROWS // rows)


def _prompt_batch(b):
    return jnp.minimum(b, BATCH - 1)


def _prompt_step(body, row_outputs):
    b = pl.program_id(0)
    pl.when(b < BATCH)(body)

    @pl.when(b == BATCH)
    def _():
        for ref in row_outputs:
            ref[...] = jnp.zeros_like(ref)


def _sigmoid(x):
    return 1.0 / (1.0 + jnp.exp(-x))


def _silu(x):
    return x * _sigmoid(x)


def _softplus(x):
    return jnp.maximum(x, 0.0) + jnp.log(1.0 + jnp.exp(-jnp.abs(x)))


def _gelu_tanh(x):
    return 0.5 * x * (1.0 + jnp.tanh(math.sqrt(2.0 / math.pi) * (x + 0.044715 * (x * x * x))))


def _rms(x, g):
    return x * lax.rsqrt(jnp.mean(x * x, axis=-1, keepdims=True) + EPS) * g


def _dot(a, b):
    return jnp.dot(a, b, preferred_element_type=F32)


def _split_bf16(x):
    hi = x.astype(BF16)
    lo = (x - hi.astype(F32)).astype(BF16)
    return hi, lo


def _norm_matmul_kernel(*refs, has_res):
    if has_res:
        x_ref, g_ref, w_ref, r_ref, o_ref, xn_ref = refs
    else:
        x_ref, g_ref, w_ref, o_ref, xn_ref = refs
        r_ref = None

    @pl.when(pl.program_id(1) == 0)
    def _():
        xn_ref[...] = _rms(x_ref[...], g_ref[...]).astype(BF16)

    acc = _dot(xn_ref[...], w_ref[...])
    if has_res:
        acc = acc + r_ref[...]
    o_ref[...] = acc.astype(o_ref.dtype)


def norm_matmul(x, g, w, layer, *, out_dtype=F32, tm, tn):
    m, k = x.shape
    n = w.shape[2]
    return pl.pallas_call(
        functools.partial(_norm_matmul_kernel, has_res=False),
        grid=(m // tm, n // tn),
        in_specs=[pl.BlockSpec((tm, k), lambda i, j: (i, 0)),
                  _layer_spec((1, k), layer),
                  pl.BlockSpec((None, k, tn), lambda i, j: (layer, 0, j))],
        out_specs=pl.BlockSpec((tm, tn), lambda i, j: (i, j)),
        out_shape=jax.ShapeDtypeStruct((m, n), out_dtype),
        scratch_shapes=[pltpu.VMEM((tm, k), BF16)],
        compiler_params=_cparams(("parallel", "arbitrary")),
        name="norm_matmul",
    )(x, g, w)


def _res_matmul_kernel(x_ref, w_ref, r_ref, o_ref):
    o_ref[...] = r_ref[...] + _dot(x_ref[...], w_ref[...])


def res_matmul(x, w, res, layer, *, tm, tn):
    m, k = x.shape
    n = w.shape[2]
    return pl.pallas_call(
        _res_matmul_kernel,
        grid=(m // tm, n // tn),
        in_specs=[pl.BlockSpec((tm, k), lambda i, j: (i, 0)),
                  pl.BlockSpec((None, k, tn), lambda i, j: (layer, 0, j)),
                  pl.BlockSpec((tm, tn), lambda i, j: (i, j))],
        out_specs=pl.BlockSpec((tm, tn), lambda i, j: (i, j)),
        out_shape=jax.ShapeDtypeStruct((m, n), F32),
        compiler_params=_cparams(("parallel", "arbitrary")),
        name="res_matmul",
    )(x, w, res)


def _out_proj_kernel(m0_ref, m1_ref, m2_ref, m3_ref, w_ref, r_ref, o_ref, xc_ref):
    @pl.when(pl.program_id(1) == 0)
    def _():
        for i, mr in enumerate((m0_ref, m1_ref, m2_ref, m3_ref)):
            xc_ref[:, i * GROUP_WIDTH:(i + 1) * GROUP_WIDTH] = mr[...]

    o_ref[...] = r_ref[...] + _dot(xc_ref[...], w_ref[...])


def out_proj(mixes, w, res, layer, *, tm, tn):
    m = res.shape[0]
    mix_spec = pl.BlockSpec((tm, GROUP_WIDTH), lambda i, j: (i, 0))
    return pl.pallas_call(
        _out_proj_kernel,
        grid=(m // tm, D_MODEL // tn),
        in_specs=[mix_spec] * 4 + [pl.BlockSpec((None, D_MODEL, tn), lambda i, j: (layer, 0, j)),
                                   pl.BlockSpec((tm, tn), lambda i, j: (i, j))],
        out_specs=pl.BlockSpec((tm, tn), lambda i, j: (i, j)),
        out_shape=jax.ShapeDtypeStruct((m, D_MODEL), F32),
        scratch_shapes=[pltpu.VMEM((tm, D_MODEL), BF16)],
        compiler_params=_cparams(("parallel", "arbitrary")),
        name="out_proj",
    )(*mixes, w, res)


def _mlp_kernel(x_ref, g_ref, wu_ref, wd_ref, o_ref, xn_ref):
    j = pl.program_id(1)

    @pl.when(j == 0)
    def _():
        x = x_ref[...]
        xn_ref[...] = _rms(x, g_ref[...]).astype(BF16)
        o_ref[...] = x

    a = jnp.square(jnp.maximum(_dot(xn_ref[...], wu_ref[...].astype(BF16)), 0.0))
    o_ref[...] += _dot(a.astype(BF16), wd_ref[...].astype(BF16))


def mlp(x, g, w_up, w_down, layer, *, tm, tf):
    m = x.shape[0]
    return pl.pallas_call(
        _mlp_kernel,
        grid=(m // tm, D_FF // tf),
        in_specs=[pl.BlockSpec((tm, D_MODEL), lambda i, j: (i, 0)),
                  _layer_spec((1, D_MODEL), layer),
                  pl.BlockSpec((None, D_MODEL, tf), lambda i, j: (layer, 0, j)),
                  pl.BlockSpec((None, tf, D_MODEL), lambda i, j: (layer, j, 0))],
        out_specs=pl.BlockSpec((tm, D_MODEL), lambda i, j: (i, 0)),
        out_shape=jax.ShapeDtypeStruct((m, D_MODEL), F32),
        scratch_shapes=[pltpu.VMEM((tm, D_MODEL), BF16)],
        compiler_params=_cparams(("parallel", "arbitrary")),
        name="mlp",
    )(x, g, w_up, w_down)


def _final_norm_kernel(x_ref, g_ref, o_ref):
    o_ref[...] = _rms(x_ref[...], g_ref[...])


def final_norm(x, g, *, rows, first_block, tm):
    return pl.pallas_call(
        _final_norm_kernel,
        grid=(rows // tm,),
        in_specs=[pl.BlockSpec((tm, D_MODEL), lambda i: (first_block + i, 0)), _whole_spec((1, D_MODEL))],
        out_specs=pl.BlockSpec((tm, D_MODEL), lambda i: (i, 0)),
        out_shape=jax.ShapeDtypeStruct((rows, D_MODEL), F32),
        compiler_params=_cparams(("parallel",)),
        name="final_norm",
    )(x, g.reshape(1, D_MODEL))


def _prompt_attn_kernel(q_ref, k_ref, v_ref, o_ref):
    _prompt_step(lambda: _prompt_attn_body(q_ref, k_ref, v_ref, o_ref), (o_ref,))


def _prompt_attn_body(q_ref, k_ref, v_ref, o_ref):
    scale = XA_HEAD_DIM ** -0.5
    for h in range(XA_HEADS):
        cs = slice(h * XA_HEAD_DIM, (h + 1) * XA_HEAD_DIM)
        kh = k_ref[0, :, cs].astype(BF16)
        vh = v_ref[0, :, cs].astype(BF16)
        s = lax.dot_general(q_ref[:, cs], kh, (((1,), (1,)), ((), ())), preferred_element_type=F32) * scale
        p = jnp.exp(s - jnp.max(s, axis=-1, keepdims=True))
        p = p / jnp.sum(p, axis=-1, keepdims=True)
        o_ref[:, cs] = _dot(p.astype(BF16), vh).astype(o_ref.dtype)


def prompt_attn(q, k, v, *, tq):
    nq = SEQ // tq
    return pl.pallas_call(
        _prompt_attn_kernel,
        grid=(BATCH + 1, nq),
        in_specs=[pl.BlockSpec((tq, D_MODEL), lambda b, i: (_prompt_row_block(b, i, nq, tq), 0)),
                  pl.BlockSpec((1, N_MEM, D_MODEL), lambda b, i: (_prompt_batch(b), 0, 0)),
                  pl.BlockSpec((1, N_MEM, D_MODEL), lambda b, i: (_prompt_batch(b), 0, 0))],
        out_specs=pl.BlockSpec((tq, D_MODEL), lambda b, i: (_prompt_row_block(b, i, nq, tq), 0)),
        out_shape=jax.ShapeDtypeStruct((ALL_ROWS, D_MODEL), BF16),
        compiler_params=_cparams(("arbitrary", "arbitrary")),
        name="prompt_attn",
    )(q, k, v)


def _sample_attn_kernel(q_ref, k_ref, v_ref, o_ref):
    scale = XA_HEAD_DIM ** -0.5
    half, quarter = XA_HEAD_DIM // 2, XA_HEAD_DIM // 4
    for r in range(SAMPLE_ATTN_ROWS):
        prod = k_ref[r] * (q_ref[r] * scale)
        fold = prod[:, :, :half] + prod[:, :, half:]
        fold = fold[:, :, :quarter] + fold[:, :, quarter:]
        s = jnp.sum(fold, axis=-1, keepdims=True)
        e = jnp.exp(s - jnp.max(s, axis=0, keepdims=True))
        o = jnp.sum(e * v_ref[r], axis=0) / jnp.sum(e, axis=0)
        o_ref[r] = o.astype(o_ref.dtype)


def sample_attn(q, cache_k, cache_v, layer):
    nr = SAMPLE_ATTN_ROWS
    row = pl.BlockSpec((nr, XA_HEADS, XA_HEAD_DIM), lambda b: (b, 0, 0))
    mem = pl.BlockSpec((None, nr, N_MEM, XA_HEADS, XA_HEAD_DIM), lambda b: (layer, b, 0, 0, 0))
    return pl.pallas_call(
        _sample_attn_kernel,
        grid=(DEC_BATCH // nr,),
        in_specs=[row, mem, mem],
        out_specs=row,
        out_shape=jax.ShapeDtypeStruct((DEC_BATCH, XA_HEADS, XA_HEAD_DIM), BF16),
        compiler_params=_cparams(("parallel",)),
        name="sample_attn",
    )(q, cache_k, cache_v)


def _s5_abar(lr, li, ldt):
    delta = jnp.exp(ldt)
    mag = jnp.exp(lr * delta)
    return mag * jnp.cos(li * delta), mag * jnp.sin(li * delta)


def _s5_bbar_kernel(lr_ref, li_ref, ldt_ref, bre_ref, bim_ref, bbr_ref, bbi_ref):
    lr, li = lr_ref[...], li_ref[...]
    ar, ai = _s5_abar(lr, li, ldt_ref[...])
    den = lr * lr + li * li
    cr = ((ar - 1.0) * lr + ai * li) / den
    ci = (ai * lr - (ar - 1.0) * li) / den
    br, bi = bre_ref[...], bim_ref[...]
    bbr_ref[...] = cr * br - ci * bi
    bbi_ref[...] = cr * bi + ci * br


def _s5_pow_kernel(lr_ref, li_ref, ldt_ref, pr_ref, pi_ref):
    ar, ai = _s5_abar(lr_ref[...], li_ref[...], ldt_ref[...])
    qr, qi = ar, ai
    pr_ref[0] = qr
    pi_ref[0] = qi
    for e in range(1, S5_SEG):
        qr, qi = qr * ar - qi * ai, qr * ai + qi * ar
        pr_ref[e] = qr
        pi_ref[e] = qi


def s5_prepare(lam_re, lam_im, log_dt, b_re, b_im, c_re, c_im):
    dg = DEPTH * S5_GROUPS
    ldt = jnp.broadcast_to(log_dt[..., None], (DEPTH, S5_GROUPS, S5_STATE))
    rep = lambda a: jnp.repeat(a.reshape(dg, S5_STATE), S5_CH, axis=0)
    to_rows = lambda b: jnp.transpose(b, (0, 1, 3, 2)).reshape(dg * S5_CH, S5_STATE)
    shp = jax.ShapeDtypeStruct((dg * S5_CH, S5_STATE), F32)
    bbr, bbi = pl.pallas_call(_s5_bbar_kernel, out_shape=(shp, shp), name="s5_bbar")(
        rep(lam_re), rep(lam_im), rep(ldt), to_rows(b_re), to_rows(b_im))
    pshp = jax.ShapeDtypeStruct((S5_SEG, dg, S5_STATE), F32)
    pr, pi = pl.pallas_call(_s5_pow_kernel, out_shape=(pshp, pshp), name="s5_pow")(
        lam_re.reshape(dg, S5_STATE), lam_im.reshape(dg, S5_STATE), ldt.reshape(dg, S5_STATE))

    gpb = LANE // S5_CH
    nblk = S5_GROUPS // gpb
    eye = jnp.eye(gpb, dtype=F32)

    def b_blocks(bb):
        bb = bb.reshape(DEPTH, nblk, gpb, S5_CH, S5_STATE)
        return jnp.einsum("digkn,gh->digkhn", bb, eye).reshape(DEPTH, nblk, LANE, gpb * S5_STATE).astype(BF16)

    def c_blocks(cc):
        cc = cc.reshape(DEPTH, nblk, gpb, S5_CH, S5_STATE)
        return jnp.einsum("digkn,gh->dignhk", cc, eye).reshape(DEPTH, nblk, gpb * S5_STATE, LANE).astype(BF16)

    def pow_rows(p):
        return jnp.transpose(p.reshape(S5_SEG, DEPTH, S5_LANES), (1, 0, 2))

    pw_r, pw_i = pow_rows(pr), pow_rows(pi)
    tile = lambda p: jnp.broadcast_to(p[:, :, None, :], (DEPTH, S5_SEG, SUBLANE, S5_LANES))
    return dict(b_re=b_blocks(bbr), b_im=b_blocks(bbi), c_re=c_blocks(c_re), c_imn=c_blocks(-c_im),
                pw_re=pw_r, pw_im=pw_i, pwt_re=tile(pw_r), pwt_im=tile(pw_i))


def _s5_bu(ub, bre_ref, bim_ref):
    res_r, res_i = [], []
    for i in range(GROUP_WIDTH // LANE):
        ui = ub[:, i * LANE:(i + 1) * LANE]
        res_r.append(_dot(ui, bre_ref[i]))
        res_i.append(_dot(ui, bim_ref[i]))
    return res_r, res_i


def _s5_tail(u, hs_re, hs_im, cre_ref, cimn_ref, d_ref, wglu_ref, bglu_ref, g_ref):
    ys = []
    for i in range(GROUP_WIDTH // LANE):
        ys.append(_dot(hs_re(i).astype(BF16), cre_ref[i]) + _dot(hs_im(i).astype(BF16), cimn_ref[i]))
    y = jnp.concatenate(ys, axis=-1) + d_ref[...] * u
    y = _gelu_tanh(y)
    y = y * _sigmoid(_dot(y.astype(BF16), wglu_ref[...]) + bglu_ref[...])
    return _rms(y, g_ref[...])


def _s5_param_specs(layer):
    nblk = GROUP_WIDTH // LANE
    sblk = S5_LANES // nblk
    return [_layer_spec((nblk, LANE, sblk), layer),
            _layer_spec((nblk, LANE, sblk), layer),
            _layer_spec((nblk, sblk, LANE), layer),
            _layer_spec((nblk, sblk, LANE), layer),
            _layer_spec((1, GROUP_WIDTH), layer),
            _layer_spec((GROUP_WIDTH, GROUP_WIDTH), layer),
            _layer_spec((1, GROUP_WIDTH), layer),
            _layer_spec((1, GROUP_WIDTH), layer)]


def _s5_param_args(sp):
    return (sp["s5_b_re"], sp["s5_b_im"], sp["s5_c_re"], sp["s5_c_imn"], sp["s5_d"], sp["s5_w_glu"],
            sp["s5_b_glu"], sp["g_s5"])


def _cmul_add(x_r, x_i, a_r, a_i, h_r, h_i):
    return x_r + a_r * h_r - a_i * h_i, x_i + a_r * h_i + a_i * h_r


def _prompt_s5_kernel(*refs):
    _prompt_step(lambda: _prompt_s5_body(*refs), (refs[15],))


def _prompt_s5_body(u_ref, perm_ref, unperm_ref, pw_re_ref, pw_im_ref, pwt_re_ref, pwt_im_ref,
                    bre_ref, bim_ref, cre_ref, cimn_ref, d_ref, wglu_ref, bglu_ref, g_ref,
                    mix_ref, sre_ref, sim_ref, hre_ref, him_ref, cr_ref, ci_ref):
    sblk = S5_LANES // (GROUP_WIDTH // LANE)
    W = S5_SCAN_LANES

    @pl.when(pl.program_id(1) == 0)
    def _():
        cr_ref[...] = jnp.zeros_like(cr_ref)
        ci_ref[...] = jnp.zeros_like(ci_ref)

    u_hi, u_lo = _split_bf16(u_ref[...])
    ub = _dot(perm_ref[...], u_hi)
    u = ub + _dot(perm_ref[...], u_lo)
    bu_r, bu_i = _s5_bu(ub.astype(BF16), bre_ref, bim_ref)
    for i in range(len(bu_r)):
        hre_ref[:, i * sblk:(i + 1) * sblk] = bu_r[i]
        him_ref[:, i * sblk:(i + 1) * sblk] = bu_i[i]

    row = lax.broadcasted_iota(jnp.int32, (SUBLANE, W), 0)
    for lb in range(0, S5_LANES, W):
        ls = slice(lb, lb + W)
        a_r = jnp.broadcast_to(pw_re_ref[0:1, ls], (SUBLANE, W))
        a_i = jnp.broadcast_to(pw_im_ref[0:1, ls], (SUBLANE, W))

        def local_step(j, h, ls=ls, a_r=a_r, a_i=a_i):
            rs = pl.ds(pl.multiple_of(j * SUBLANE, SUBLANE), SUBLANE)
            n_r, n_i = _cmul_add(hre_ref[rs, ls], him_ref[rs, ls], a_r, a_i, h[0], h[1])
            hre_ref[rs, ls] = n_r
            him_ref[rs, ls] = n_i
            return n_r, n_i

        zero = jnp.zeros((SUBLANE, W), F32)
        e_r, e_i = lax.fori_loop(0, S5_SEG, local_step, (zero, zero))

        s_r = pw_re_ref[S5_SEG - 1:S5_SEG, ls]
        s_i = pw_im_ref[S5_SEG - 1:S5_SEG, ls]
        c_r, c_i = cr_ref[:, ls], ci_ref[:, ls]
        in_r, in_i = zero, zero
        for s in range(SUBLANE):
            in_r = jnp.where(row == s, c_r, in_r)
            in_i = jnp.where(row == s, c_i, in_i)
            c_r, c_i = _cmul_add(e_r[s:s + 1, :], e_i[s:s + 1, :], s_r, s_i, c_r, c_i)
        cr_ref[:, ls] = c_r
        ci_ref[:, ls] = c_i

        def fix_step(j, carry, ls=ls, in_r=in_r, in_i=in_i):
            rs = pl.ds(pl.multiple_of(j * SUBLANE, SUBLANE), SUBLANE)
            n_r, n_i = _cmul_add(hre_ref[rs, ls], him_ref[rs, ls], pwt_re_ref[j, :, ls], pwt_im_ref[j, :, ls],
                                 in_r, in_i)
            hre_ref[rs, ls] = n_r
            him_ref[rs, ls] = n_i
            return carry

        lax.fori_loop(0, S5_SEG, fix_step, 0)

    sre_ref[0] = cr_ref[...]
    sim_ref[0] = ci_ref[...]
    y = _s5_tail(u, lambda i: hre_ref[:, i * sblk:(i + 1) * sblk], lambda i: him_ref[:, i * sblk:(i + 1) * sblk],
                 cre_ref, cimn_ref, d_ref, wglu_ref, bglu_ref, g_ref)
    mix_ref[...] = _dot(unperm_ref[...], y.astype(mix_ref.dtype)).astype(mix_ref.dtype)


def prompt_s5(proj, sp, layer):
    nt = SEQ // S5_CHUNK
    t_of_row = (np.arange(S5_CHUNK) % SUBLANE) * S5_SEG + np.arange(S5_CHUNK) // SUBLANE
    perm = jnp.asarray(np.eye(S5_CHUNK, dtype=np.float32)[t_of_row], BF16)
    st = jax.ShapeDtypeStruct((BATCH, 1, S5_LANES), F32)
    st_spec = pl.BlockSpec((1, 1, S5_LANES), lambda b, t: (_prompt_batch(b), 0, 0))
    rb = lambda b, t: _prompt_row_block(b, t, nt, S5_CHUNK)
    return pl.pallas_call(
        _prompt_s5_kernel,
        grid=(BATCH + 1, nt),
        in_specs=[pl.BlockSpec((S5_CHUNK, GROUP_WIDTH), lambda b, t: (rb(b, t), COL_U5 // GROUP_WIDTH)),
                  _whole_spec((S5_CHUNK, S5_CHUNK)), _whole_spec((S5_CHUNK, S5_CHUNK)),
                  _layer_spec((S5_SEG, S5_LANES), layer), _layer_spec((S5_SEG, S5_LANES), layer),
                  _layer_spec((S5_SEG, SUBLANE, S5_LANES), layer), _layer_spec((S5_SEG, SUBLANE, S5_LANES), layer)]
                 + _s5_param_specs(layer),
        out_specs=(pl.BlockSpec((S5_CHUNK, GROUP_WIDTH), lambda b, t: (rb(b, t), 0)), st_spec, st_spec),
        out_shape=(jax.ShapeDtypeStruct((ALL_ROWS, GROUP_WIDTH), BF16), st, st),
        scratch_shapes=[pltpu.VMEM((S5_CHUNK, S5_LANES), F32), pltpu.VMEM((S5_CHUNK, S5_LANES), F32),
                        pltpu.VMEM((1, S5_LANES), F32), pltpu.VMEM((1, S5_LANES), F32)],
        compiler_params=_cparams(("arbitrary", "arbitrary")),
        name="prompt_s5",
    )(proj, perm, perm.T, sp["s5_pw_re"], sp["s5_pw_im"], sp["s5_pwt_re"], sp["s5_pwt_im"], *_s5_param_args(sp))


def _prompt_ssd_kernel(*refs):
    _prompt_step(lambda: _prompt_ssd_body(*refs), (refs[8],))


def _prompt_ssd_body(zx_ref, dt_ref, cw_ref, cb_ref, dtb_ref, alog_ref, dexp_ref, g_ref,
                     mix_ref, cst_ref, hst_ref, xbuf_ref, h_ref):
    L = SSD_CHUNK
    hist = SSD_CONV - 1
    base = SUBLANE

    @pl.when(pl.program_id(1) == 0)
    def _():
        xbuf_ref[0:base, :] = jnp.zeros((base, XBC_WIDTH), F32)
        h_ref[...] = jnp.zeros_like(h_ref)

    @pl.when(pl.program_id(1) > 0)
    def _():
        xbuf_ref[base - hist:base, :] = xbuf_ref[base + L - hist:base + L, :]

    xbc = zx_ref[:, COL_XBC:COL_XBC + XBC_WIDTH]
    xbuf_ref[base:base + L, :] = xbc
    cst_ref[0] = xbuf_ref[base + L - hist:base + L, :]
    conv = cb_ref[...] + cw_ref[hist:hist + 1, :] * xbc
    for k in range(hist):
        conv = conv + cw_ref[k:k + 1, :] * xbuf_ref[base - hist + k:base - hist + k + L, :]
    xc = _silu(conv)
    xs = xc[:, :GROUP_WIDTH]
    ng = SSD_GROUPS * SSD_STATE
    bm = xc[:, GROUP_WIDTH:GROUP_WIDTH + ng].astype(BF16)
    cm = xc[:, GROUP_WIDTH + ng:].astype(BF16)

    dt = _softplus(dt_ref[...] + dtb_ref[...])
    a = -jnp.exp(alog_ref[...])
    ri = lax.broadcasted_iota(jnp.int32, (L, L), 0)
    ci = lax.broadcasted_iota(jnp.int32, (L, L), 1)
    causal = ri >= ci
    acum = jnp.dot(causal.astype(F32), dt * a, preferred_element_type=F32, precision=lax.Precision.HIGHEST)
    acum_t = acum.T
    last = acum[L - 1:L, :]
    to_end = jnp.exp(last - acum)
    e_acum = jnp.exp(acum)
    chunk_decay = jnp.exp(last)

    ys = []
    rep = SSD_HEADS // SSD_GROUPS
    cb = [lax.dot_general(cm[:, g * SSD_STATE:(g + 1) * SSD_STATE], bm[:, g * SSD_STATE:(g + 1) * SSD_STATE],
                          (((1,), (1,)), ((), ())), preferred_element_type=F32) for g in range(SSD_GROUPS)]
    for h in range(SSD_HEADS):
        g = h // rep
        hs = slice(h * SSD_HEAD_DIM, (h + 1) * SSD_HEAD_DIM)
        gs = slice(g * SSD_STATE, (g + 1) * SSD_STATE)
        seg = acum[:, h:h + 1] - acum_t[h:h + 1, :]
        decay = jnp.exp(jnp.where(causal, seg, -jnp.inf))
        xs_h = xs[:, hs]
        xdt = xs_h * dt[:, h:h + 1]
        y = _dot((cb[g] * decay).astype(BF16), xdt.astype(BF16))
        h_prev = h_ref[h]
        y_off = lax.dot_general(cm[:, gs], h_prev.astype(BF16), (((1,), (1,)), ((), ())),
                                preferred_element_type=F32)
        y = y + y_off * e_acum[:, h:h + 1]
        st = lax.dot_general((xdt * to_end[:, h:h + 1]).astype(BF16), bm[:, gs], (((0,), (0,)), ((), ())),
                             preferred_element_type=F32)
        h_ref[h] = h_prev * chunk_decay[:, h:h + 1] + st
        ys.append(y)
    y = (jnp.concatenate(ys, axis=-1) + dexp_ref[...] * xs) * _silu(zx_ref[:, COL_Z:COL_Z + GROUP_WIDTH])
    mix_ref[...] = _rms(y, g_ref[...]).astype(mix_ref.dtype)
    hst_ref[0] = h_ref[...]


def prompt_ssd(proj, sp, layer):
    nc = SEQ // SSD_CHUNK
    rb = lambda b, c: _prompt_row_block(b, c, nc, SSD_CHUNK)
    return pl.pallas_call(
        _prompt_ssd_kernel,
        grid=(BATCH + 1, nc),
        in_specs=[pl.BlockSpec((SSD_CHUNK, ZX_WIDTH), lambda b, c: (rb(b, c), 0)),
                  pl.BlockSpec((SSD_CHUNK, LANE), lambda b, c: (rb(b, c), COL_DT // LANE)),
                  _layer_spec((SSD_CONV, XBC_WIDTH), layer),
                  _layer_spec((1, XBC_WIDTH), layer),
                  _layer_spec((1, LANE), layer),
                  _layer_spec((1, LANE), layer),
                  _layer_spec((1, GROUP_WIDTH), layer),
                  _layer_spec((1, GROUP_WIDTH), layer)],
        out_specs=(pl.BlockSpec((SSD_CHUNK, GROUP_WIDTH), lambda b, c: (rb(b, c), 0)),
                   pl.BlockSpec((1, SSD_CONV - 1, XBC_WIDTH), lambda b, c: (_prompt_batch(b), 0, 0)),
                   pl.BlockSpec((1, SSD_HEADS, SSD_HEAD_DIM, SSD_STATE), lambda b, c: (_prompt_batch(b), 0, 0, 0))),
        out_shape=(jax.ShapeDtypeStruct((ALL_ROWS, GROUP_WIDTH), BF16),
                   jax.ShapeDtypeStruct((BATCH, SSD_CONV - 1, XBC_WIDTH), F32),
                   jax.ShapeDtypeStruct((BATCH, SSD_HEADS, SSD_HEAD_DIM, SSD_STATE), F32)),
        scratch_shapes=[pltpu.VMEM((SUBLANE + SSD_CHUNK, XBC_WIDTH), F32),
                        pltpu.VMEM((SSD_HEADS, SSD_HEAD_DIM, SSD_STATE), F32)],
        compiler_params=_cparams(("arbitrary", "arbitrary")),
        name="prompt_ssd",
    )(proj, proj, sp["ssd_conv_w"], sp["ssd_conv_b"], sp["ssd_dt_bias"], sp["ssd_a_log"],
      sp["ssd_d_exp"], sp["g_ssd"])


def _pool_counts(pos, w):
    return jnp.minimum(w, pos + 1).astype(F32)


def _prompt_convpool_kernel(*refs):
    _prompt_step(lambda: _prompt_convpool_body(*refs), (refs[9], refs[10]))


def _prompt_convpool_body(gb_ref, gc_ref, hv_ref, up_ref, scw_ref, pw_ref, ps_ref, gsc_ref, gpl_ref,
                          msc_ref, mpl_ref, scst_ref, plst_ref, vbuf_ref, pbuf_ref):
    rows = gb_ref.shape[0]
    vb = SUBLANE
    pb = 2 * SUBLANE
    vh = SC_CONV - 1
    t = pl.program_id(1)

    @pl.when(t == 0)
    def _():
        vbuf_ref[0:vb, :] = jnp.zeros((vb, GROUP_WIDTH), F32)
        pbuf_ref[0:pb, :] = jnp.zeros((pb, GROUP_WIDTH), F32)

    @pl.when(t > 0)
    def _():
        vbuf_ref[vb - vh:vb, :] = vbuf_ref[vb + rows - vh:vb + rows, :]
        pbuf_ref[0:pb, :] = pbuf_ref[rows:rows + pb, :]

    v = gc_ref[...] * hv_ref[...]
    vbuf_ref[vb:vb + rows, :] = v
    acc = scw_ref[vh:vh + 1, :] * v
    for k in range(vh):
        acc = acc + scw_ref[k:k + 1, :] * vbuf_ref[vb - vh + k:vb - vh + k + rows, :]
    msc_ref[...] = _rms(gb_ref[...] * acc, gsc_ref[...]).astype(msc_ref.dtype)
    scst_ref[0] = vbuf_ref[vb + rows - vh:vb + rows, :]

    u = up_ref[...]
    pbuf_ref[pb:pb + rows, :] = u
    pos = t * rows + lax.broadcasted_iota(jnp.int32, (rows, 1), 0)
    ys = []
    for gi, w in enumerate(POOL_WINDOWS):
        cs = slice(gi * POOL_GROUP, (gi + 1) * POOL_GROUP)
        s = u[:, cs]
        for j in range(1, w):
            s = s + pbuf_ref[pb - j:pb - j + rows, cs]
        pooled = s / _pool_counts(pos, w) - u[:, cs]
        ys.append(_dot(pooled.astype(BF16), pw_ref[gi]))
    y = jnp.concatenate(ys, axis=-1) * ps_ref[...]
    mpl_ref[...] = _rms(y, gpl_ref[...]).astype(mpl_ref.dtype)
    plst_ref[0] = pbuf_ref[pb + rows - POOL_HIST:pb + rows, :]


def prompt_convpool(proj, sp, layer):
    nt = SEQ // S5_CHUNK
    rb = lambda b, t: _prompt_row_block(b, t, nt, S5_CHUNK)
    col = lambda c: pl.BlockSpec((S5_CHUNK, GROUP_WIDTH), lambda b, t: (rb(b, t), c // GROUP_WIDTH))
    mix_spec = pl.BlockSpec((S5_CHUNK, GROUP_WIDTH), lambda b, t: (rb(b, t), 0))
    mix_shape = jax.ShapeDtypeStruct((ALL_ROWS, GROUP_WIDTH), BF16)
    return pl.pallas_call(
        _prompt_convpool_kernel,
        grid=(BATCH + 1, nt),
        in_specs=[col(COL_GB), col(COL_GC), col(COL_HV), col(COL_UP),
                  _layer_spec((SC_CONV, GROUP_WIDTH), layer),
                  _layer_spec((len(POOL_WINDOWS), POOL_GROUP, POOL_GROUP), layer),
                  _layer_spec((1, GROUP_WIDTH), layer), _layer_spec((1, GROUP_WIDTH), layer),
                  _layer_spec((1, GROUP_WIDTH), layer)],
        out_specs=(mix_spec, mix_spec,
                   pl.BlockSpec((1, SC_CONV - 1, GROUP_WIDTH), lambda b, t: (_prompt_batch(b), 0, 0)),
                   pl.BlockSpec((1, POOL_HIST, GROUP_WIDTH), lambda b, t: (_prompt_batch(b), 0, 0))),
        out_shape=(mix_shape, mix_shape,
                   jax.ShapeDtypeStruct((BATCH, SC_CONV - 1, GROUP_WIDTH), F32),
                   jax.ShapeDtypeStruct((BATCH, POOL_HIST, GROUP_WIDTH), F32)),
        scratch_shapes=[pltpu.VMEM((SUBLANE + S5_CHUNK, GROUP_WIDTH), F32),
                        pltpu.VMEM((2 * SUBLANE + S5_CHUNK, GROUP_WIDTH), F32)],
        compiler_params=_cparams(("arbitrary", "arbitrary")),
        name="prompt_convpool",
    )(proj, proj, proj, proj, sp["sc_conv_w"], sp["pool_w"], sp["pool_scale"], sp["g_sc"], sp["g_pool"])


def _sample_mix_kernel(proj_ref, cprev_ref, s5r_ref, s5i_ref, scprev_ref, plprev_ref,
                       cw_ref, cb_ref, dtb_ref, alog_ref,
                       pwr_ref, pwi_ref, bre_ref, bim_ref, cre_ref, cimn_ref, d5_ref, wglu_ref, bglu_ref, g5_ref,
                       scw_ref, pw_ref, ps_ref, gsc_ref, gpl_ref,
                       m5_ref, msc_ref, mpl_ref, cst_ref, s5ro_ref, s5io_ref, scst_ref, plst_ref,
                       xs_ref, xdt_ref, da_ref, b2_ref, c2_ref):
    nb = proj_ref.shape[0]
    W = GROUP_WIDTH
    xbc = proj_ref[:, COL_XBC:COL_XBC + XBC_WIDTH]
    hist = SSD_CONV - 1
    conv = cb_ref[...] + cw_ref[hist:hist + 1, :] * xbc
    for k in range(hist):
        conv = conv + cw_ref[k:k + 1, :] * cprev_ref[:, k * XBC_WIDTH:(k + 1) * XBC_WIDTH]
    cst_ref[:, 0:(hist - 1) * XBC_WIDTH] = cprev_ref[:, XBC_WIDTH:hist * XBC_WIDTH]
    cst_ref[:, (hist - 1) * XBC_WIDTH:hist * XBC_WIDTH] = xbc
    xc = _silu(conv)
    xs = xc[:, :W]
    bm = xc[:, W:W + LANE]
    cm = xc[:, W + LANE:W + 2 * LANE]
    dt = _softplus(proj_ref[:, COL_DT:COL_DT + LANE] + dtb_ref[...])
    da = jnp.exp(dt * (-jnp.exp(alog_ref[...])))
    lane_w = lax.broadcasted_iota(jnp.int32, (nb, W), 1)
    dt_exp = jnp.zeros((nb, W), F32)
    for h in range(SSD_HEADS):
        dt_exp = jnp.where(lane_w // SSD_HEAD_DIM == h, dt[:, h:h + 1], dt_exp)
        da_ref[:, h * LANE:(h + 1) * LANE] = jnp.broadcast_to(da[:, h:h + 1], (nb, LANE))
    xs_ref[...] = xs
    xdt_ref[...] = xs * dt_exp
    lane = lax.broadcasted_iota(jnp.int32, (nb, LANE), 1)
    low = lane < SSD_STATE
    for src, dst in ((bm, b2_ref), (cm, c2_ref)):
        swapped = pltpu.roll(src, SSD_STATE, 1)
        dst[:, 0:LANE] = jnp.where(low, src, swapped)
        dst[:, LANE:2 * LANE] = jnp.where(low, swapped, src)

    u5 = proj_ref[:, COL_U5:COL_U5 + W]
    bu_r, bu_i = _s5_bu(u5.astype(BF16), bre_ref, bim_ref)
    sblk = S5_LANES // len(bu_r)
    for i in range(len(bu_r)):
        ls = slice(i * sblk, (i + 1) * sblk)
        n_r, n_i = _cmul_add(bu_r[i], bu_i[i], pwr_ref[0:1, ls], pwi_ref[0:1, ls], s5r_ref[:, ls], s5i_ref[:, ls])
        s5ro_ref[:, ls] = n_r
        s5io_ref[:, ls] = n_i
    y5 = _s5_tail(u5, lambda i: s5ro_ref[:, i * sblk:(i + 1) * sblk], lambda i: s5io_ref[:, i * sblk:(i + 1) * sblk],
                  cre_ref, cimn_ref, d5_ref, wglu_ref, bglu_ref, g5_ref)
    m5_ref[...] = y5.astype(m5_ref.dtype)

    v = proj_ref[:, COL_GC:COL_GC + W] * proj_ref[:, COL_HV:COL_HV + W]
    vh = SC_CONV - 1
    acc = scw_ref[vh:vh + 1, :] * v
    for k in range(vh):
        acc = acc + scw_ref[k:k + 1, :] * scprev_ref[:, k * W:(k + 1) * W]
    msc_ref[...] = _rms(proj_ref[:, COL_GB:COL_GB + W] * acc, gsc_ref[...]).astype(msc_ref.dtype)
    scst_ref[:, 0:(vh - 1) * W] = scprev_ref[:, W:vh * W]
    scst_ref[:, (vh - 1) * W:vh * W] = v

    up = proj_ref[:, COL_UP:COL_UP + W]
    ys = []
    for gi, w in enumerate(POOL_WINDOWS):
        cs = slice(gi * POOL_GROUP, (gi + 1) * POOL_GROUP)
        s = up[:, cs]
        for j in range(1, w):
            k = POOL_HIST - j
            s = s + plprev_ref[:, k * W + gi * POOL_GROUP:k * W + (gi + 1) * POOL_GROUP]
        pooled = s / float(min(w, PAST_LEN + 1)) - up[:, cs]
        ys.append(_dot(pooled.astype(BF16), pw_ref[gi]))
    y = jnp.concatenate(ys, axis=-1) * ps_ref[...]
    mpl_ref[...] = _rms(y, gpl_ref[...]).astype(mpl_ref.dtype)
    plst_ref[:, 0:(POOL_HIST - 1) * W] = plprev_ref[:, W:POOL_HIST * W]
    plst_ref[:, (POOL_HIST - 1) * W:POOL_HIST * W] = up


def sample_mix(proj, states, sp, layer):
    nb = DEC_BATCH
    W = GROUP_WIDTH
    f = lambda n: jax.ShapeDtypeStruct((nb, n), F32)
    b = lambda n: jax.ShapeDtypeStruct((nb, n), BF16)
    out_widths = ((SSD_CONV - 1) * XBC_WIDTH, S5_LANES, S5_LANES, (SC_CONV - 1) * W, POOL_HIST * W,
                  W, W, SSD_HEADS * LANE, SSD_GROUPS * LANE, SSD_GROUPS * LANE)
    out_shape = (b(W), b(W), b(W)) + tuple(f(n) for n in out_widths)
    nblk = GROUP_WIDTH // LANE
    in_specs = ([pl.BlockSpec((nb, IN_PAD), lambda i: (SAMPLE_BLOCK, 0))]
                + [_layer_spec((nb, s.shape[2]), layer) for s in states]
                + [_layer_spec((SSD_CONV, XBC_WIDTH), layer), _layer_spec((1, XBC_WIDTH), layer),
                   _layer_spec((1, LANE), layer), _layer_spec((1, LANE), layer),
                   _layer_spec((S5_SEG, S5_LANES), layer), _layer_spec((S5_SEG, S5_LANES), layer)]
                + _s5_param_specs(layer)
                + [_layer_spec((SC_CONV, W), layer), _layer_spec((len(POOL_WINDOWS), POOL_GROUP, POOL_GROUP), layer),
                   _layer_spec((1, W), layer), _layer_spec((1, W), layer), _layer_spec((1, W), layer)])
    return pl.pallas_call(
        _sample_mix_kernel,
        grid=(1,),
        in_specs=in_specs,
        out_specs=tuple(_whole_spec(s.shape) for s in out_shape),
        out_shape=out_shape,
        compiler_params=_cparams(("arbitrary",)),
        name="sample_mix",
    )(proj, *states, sp["ssd_conv_w"], sp["ssd_conv_b"], sp["ssd_dt_bias"], sp["ssd_a_log"],
      sp["s5_pw_re"], sp["s5_pw_im"], *_s5_param_args(sp),
      sp["sc_conv_w"], sp["pool_w"], sp["pool_scale"], sp["g_sc"], sp["g_pool"])


def _sample_ssd_kernel(h0_ref, xdt_ref, da_ref, b2_ref, c2_ref, e_ref, r_ref, xs_ref, z_ref, dexp_ref, g_ref,
                       hn_ref, mix_ref, xrep_ref, prod_ref, y_ref):
    hp = pl.program_id(0)
    hpl = 2 * SSD_HEAD_DIM * SSD_STATE
    per_head = SSD_HEAD_DIM * SSD_STATE
    x_hi, x_lo = _split_bf16(xdt_ref[...])
    xrep_ref[...] = _dot(x_hi, e_ref[...]) + _dot(x_lo, e_ref[...])
    b2, c2 = b2_ref[...], c2_ref[...]
    for j in range(hpl // LANE):
        ls = slice(j * LANE, (j + 1) * LANE)
        hl = (j * LANE) // per_head
        hn = da_ref[:, hl * LANE:(hl + 1) * LANE] * h0_ref[:, ls] + xrep_ref[:, ls] * b2
        hn_ref[:, ls] = hn
        prod_ref[:, ls] = hn * c2
    p_hi, p_lo = _split_bf16(prod_ref[...])
    y_ref[hp] = _dot(p_hi, r_ref[...]) + _dot(p_lo, r_ref[...])

    @pl.when(hp == pl.num_programs(0) - 1)
    def _():
        y = jnp.concatenate([y_ref[i] for i in range(SSD_HEADS // 2)], axis=-1)
        y = (y + dexp_ref[...] * xs_ref[...]) * _silu(z_ref[...])
        mix_ref[...] = _rms(y, g_ref[...]).astype(mix_ref.dtype)


def sample_ssd(h0_all, xdt, da, b2, c2, xs, proj, sp, layer):
    nb = DEC_BATCH
    npairs = SSD_HEADS // 2
    hpl = 2 * SSD_HEAD_DIM * SSD_STATE
    expand = jnp.repeat(jnp.eye(LANE, dtype=BF16), SSD_STATE, axis=1)
    return pl.pallas_call(
        _sample_ssd_kernel,
        grid=(npairs,),
        in_specs=[pl.BlockSpec((None, nb, hpl), lambda i: (layer, 0, i)),
                  pl.BlockSpec((nb, LANE), lambda i: (0, i)),
                  pl.BlockSpec((nb, 2 * LANE), lambda i: (0, i)),
                  pl.BlockSpec((nb, LANE), lambda i: (0, i // (npairs // SSD_GROUPS))),
                  pl.BlockSpec((nb, LANE), lambda i: (0, i // (npairs // SSD_GROUPS))),
                  _whole_spec((LANE, hpl)),
                  _whole_spec((hpl, LANE)),
                  _whole_spec((nb, GROUP_WIDTH)),
                  pl.BlockSpec((nb, GROUP_WIDTH), lambda i: (SAMPLE_BLOCK, COL_Z // GROUP_WIDTH)),
                  _layer_spec((1, GROUP_WIDTH), layer),
                  _layer_spec((1, GROUP_WIDTH), layer)],
        out_specs=(pl.BlockSpec((nb, hpl), lambda i: (0, i)),
                   _whole_spec((nb, GROUP_WIDTH))),
        out_shape=(jax.ShapeDtypeStruct((nb, SSD_HEADS * SSD_HEAD_DIM * SSD_STATE), F32),
                   jax.ShapeDtypeStruct((nb, GROUP_WIDTH), BF16)),
        scratch_shapes=[pltpu.VMEM((nb, hpl), F32), pltpu.VMEM((nb, hpl), F32),
                        pltpu.VMEM((npairs, nb, LANE), F32)],
        compiler_params=_cparams(("arbitrary",)),
        name="sample_ssd",
    )(h0_all, xdt, da, b2, c2, expand, expand.T, xs, proj, sp["ssd_d_exp"], sp["g_ssd"])


def _regroup_w_in(w_in):
    head = COL_DT + SSD_HEADS
    pad = jnp.zeros((DEPTH, D_MODEL, COL_U5 - head), w_in.dtype)
    return jnp.concatenate([w_in[..., :head], pad, w_in[..., head:]], axis=-1).astype(BF16)


def _stacked_params(w):
    row = lambda v: v.reshape(DEPTH, 1, -1)
    pad_heads = lambda v: row(jnp.pad(v, ((0, 0), (0, LANE - SSD_HEADS))))
    g_mix = w["mix_out_g"].reshape(DEPTH, 4, 1, GROUP_WIDTH)
    sp = dict(w_in=_regroup_w_in(w["w_in"]))
    for k in ("w_out", "w_q", "w_k", "w_v", "w_o", "s5_w_glu", "pool_w"):
        sp[k] = w[k].astype(BF16)
    sp.update(w_up=w["w_up"], w_down=w["w_down"])
    for k in ("norm_mix_g", "norm_xa_g", "norm_mem_g", "norm_mlp_g", "ssd_conv_b", "s5_d", "s5_b_glu", "pool_scale"):
        sp[k] = row(w[k])
    sp.update(ssd_conv_w=w["ssd_conv_w"], sc_conv_w=w["sc_conv_w"],
              ssd_dt_bias=pad_heads(w["ssd_dt_bias"]), ssd_a_log=pad_heads(w["ssd_a_log"]),
              ssd_d_exp=row(jnp.repeat(w["ssd_d"], SSD_HEAD_DIM, axis=1)),
              g_ssd=g_mix[:, 0], g_s5=g_mix[:, 1], g_sc=g_mix[:, 2], g_pool=g_mix[:, 3])
    s5 = s5_prepare(w["s5_lam_re"], w["s5_lam_im"], w["s5_log_dt"], w["s5_b_re"], w["s5_b_im"],
                    w["s5_c_re"], w["s5_c_im"])
    sp.update({"s5_" + k: v for k, v in s5.items()})
    return sp


def _forward(x_prompt, x_sample, mem_prompt, state_ssd_conv, state_ssd, state_s5_re, state_s5_im,
             state_sconv, state_pool, cache_mem_k, cache_mem_v, final_norm_g, w):
    sp = _stacked_params(w)
    flat = lambda s: s.reshape(DEPTH, DEC_BATCH, -1)
    s_states = tuple(flat(s) for s in (state_ssd_conv, state_s5_re, state_s5_im, state_sconv, state_pool))
    s_ssd0 = flat(state_ssd)
    h = jnp.concatenate([x_prompt.reshape(P_ROWS, D_MODEL), x_sample.reshape(DEC_BATCH, D_MODEL)], axis=0)
    mem = mem_prompt.reshape(BATCH * N_MEM, D_MODEL)
    p_out = [[] for _ in range(8)]
    s_out = [[] for _ in range(6)]
    put_sample = lambda full, rows: lax.dynamic_update_slice(full, rows, (P_ROWS, 0))
    for l in range(DEPTH):
        mk = norm_matmul(mem, sp["norm_mem_g"], sp["w_k"], l, tm=BATCH * N_MEM, tn=COL_TILE)
        mv = norm_matmul(mem, sp["norm_mem_g"], sp["w_v"], l, tm=BATCH * N_MEM, tn=COL_TILE)

        proj = norm_matmul(h, sp["norm_mix_g"], sp["w_in"], l, tm=ROW_TILE, tn=COL_TILE)
        m_ssd, p_conv, p_ssd = prompt_ssd(proj, sp, l)
        m_s5, p_s5r, p_s5i = prompt_s5(proj, sp, l)
        m_sc, m_pl, p_sc, p_pl = prompt_convpool(proj, sp, l)
        (s_m5, s_msc, s_mpl, s_conv, s_s5r, s_s5i, s_sc, s_pl, xs, xdt, da, b2, c2) = sample_mix(
            proj, s_states, sp, l)
        s_ssd, s_mssd = sample_ssd(s_ssd0, xdt, da, b2, c2, xs, proj, sp, l)
        mixes = (put_sample(m_ssd, s_mssd), put_sample(m_s5, s_m5),
                 put_sample(m_sc, s_msc), put_sample(m_pl, s_mpl))

        h = out_proj(mixes, sp["w_out"], h, l, tm=ROW_TILE, tn=COL_TILE)
        q = norm_matmul(h, sp["norm_xa_g"], sp["w_q"], l, out_dtype=BF16, tm=ROW_TILE, tn=COL_TILE)
        o = prompt_attn(q, mk.reshape(BATCH, N_MEM, D_MODEL), mv.reshape(BATCH, N_MEM, D_MODEL), tq=512)
        q_s = q[P_ROWS:].astype(F32).reshape(DEC_BATCH, XA_HEADS, XA_HEAD_DIM)
        o_s = sample_attn(q_s, cache_mem_k, cache_mem_v, l)
        o = put_sample(o, o_s.reshape(DEC_BATCH, D_MODEL))
        h = res_matmul(o, sp["w_o"], h, l, tm=ROW_TILE, tn=COL_TILE)
        h = mlp(h, sp["norm_mlp_g"], sp["w_up"], sp["w_down"], l, tm=MLP_ROW_TILE, tf=FF_TILE)

        for lst, val in zip(p_out, (p_conv, p_ssd, p_s5r, p_s5i, p_sc, p_pl, mk, mv)):
            lst.append(val)
        for lst, val in zip(s_out, (s_conv, s_ssd, s_s5r, s_s5i, s_sc, s_pl)):
            lst.append(val)

    y_prompt = final_norm(h, final_norm_g, rows=P_ROWS, first_block=0, tm=COL_TILE)
    y_sample = final_norm(h, final_norm_g, rows=DEC_BATCH, first_block=SAMPLE_BLOCK, tm=DEC_BATCH)
    p_shapes = ((BATCH, SSD_CONV - 1, XBC_WIDTH), (BATCH, SSD_HEADS, SSD_HEAD_DIM, SSD_STATE),
                (BATCH, S5_GROUPS, S5_STATE), (BATCH, S5_GROUPS, S5_STATE),
                (BATCH, SC_CONV - 1, GROUP_WIDTH), (BATCH, POOL_HIST, GROUP_WIDTH),
                (BATCH, N_MEM, XA_HEADS, XA_HEAD_DIM), (BATCH, N_MEM, XA_HEADS, XA_HEAD_DIM))
    s_shapes = tuple((DEC_BATCH,) + s[1:] for s in p_shapes[:6])
    stack = lambda vals, shape: jnp.stack(vals).reshape((DEPTH,) + shape)
    return ((y_prompt.reshape(BATCH, SEQ, D_MODEL), y_sample.reshape(DEC_BATCH, 1, D_MODEL))
            + tuple(stack(v, s) for v, s in zip(p_out, p_shapes))
            + tuple(stack(v, s) for v, s in zip(s_out, s_shapes)))


_forward_jit = jax.jit(_forward)


def kernel(x_prompt, x_sample, mem_prompt, state_ssd_conv, state_ssd, state_s5_re, state_s5_im, state_sconv, state_pool, cache_mem_k, cache_mem_v, norm_mix_g, w_in, ssd_conv_w, ssd_conv_b, ssd_dt_bias, ssd_a_log, ssd_d, s5_lam_re, s5_lam_im, s5_log_dt, s5_b_re, s5_b_im, s5_c_re, s5_c_im, s5_d, s5_w_glu, s5_b_glu, sc_conv_w, pool_w, pool_scale, mix_out_g, w_out, norm_xa_g, norm_mem_g, w_q, w_k, w_v, w_o, norm_mlp_g, w_up, w_down, final_norm_g):
    w = dict(norm_mix_g=norm_mix_g, w_in=w_in, ssd_conv_w=ssd_conv_w, ssd_conv_b=ssd_conv_b,
             ssd_dt_bias=ssd_dt_bias, ssd_a_log=ssd_a_log, ssd_d=ssd_d, s5_lam_re=s5_lam_re,
             s5_lam_im=s5_lam_im, s5_log_dt=s5_log_dt, s5_b_re=s5_b_re, s5_b_im=s5_b_im,
             s5_c_re=s5_c_re, s5_c_im=s5_c_im, s5_d=s5_d, s5_w_glu=s5_w_glu, s5_b_glu=s5_b_glu,
             sc_conv_w=sc_conv_w, pool_w=pool_w, pool_scale=pool_scale, mix_out_g=mix_out_g,
             w_out=w_out, norm_xa_g=norm_xa_g, norm_mem_g=norm_mem_g, w_q=w_q, w_k=w_k, w_v=w_v,
             w_o=w_o, norm_mlp_g=norm_mlp_g, w_up=w_up, w_down=w_down)
    return _forward_jit(x_prompt, x_sample, mem_prompt, state_ssd_conv, state_ssd, state_s5_re, state_s5_im,
                        state_sconv, state_pool, cache_mem_k, cache_mem_v, final_norm_g, w)
```

```python
import functools
import math

import numpy as np
import jax
import jax.numpy as jnp
from jax import lax
from jax.experimental import pallas as pl
from jax.experimental.pallas import tpu as pltpu

F32 = jnp.float32
BF16 = jnp.bfloat16

D_MODEL = 2048
BATCH = 4
SEQ = 2048
DEPTH = 2
DEC_BATCH = 128
PAST_LEN = 16384
GROUP_WIDTH = D_MODEL // 4
SSD_HEAD_DIM = 64
SSD_HEADS = GROUP_WIDTH // SSD_HEAD_DIM
SSD_GROUPS = 2
SSD_STATE = 64
SSD_CONV = 4
SSD_CHUNK = 128
XBC_WIDTH = GROUP_WIDTH + 2 * SSD_GROUPS * SSD_STATE
S5_CH = 16
S5_GROUPS = GROUP_WIDTH // S5_CH
S5_STATE = 64
S5_LANES = S5_GROUPS * S5_STATE
SC_CONV = 3
POOL_WINDOWS = (2, 4, 8, 16)
POOL_GROUP = GROUP_WIDTH // len(POOL_WINDOWS)
POOL_HIST = max(POOL_WINDOWS) - 1
N_MEM = 256
XA_HEADS = 4
XA_HEAD_DIM = D_MODEL // XA_HEADS
D_FF = 4 * D_MODEL
EPS = 1e-6
P_ROWS = BATCH * SEQ
ALL_ROWS = P_ROWS + DEC_BATCH

LANE = 128
SUBLANE = 8
VMEM_LIMIT = 56 * 1024 * 1024

COL_Z = 0
COL_XBC = 512
COL_DT = 1280
COL_U5 = 1536
COL_GB = 2048
COL_GC = 2560
COL_HV = 3072
COL_UP = 3584
IN_PAD = 4096
ZX_WIDTH = COL_DT

ROW_TILE = ALL_ROWS // 8
COL_TILE = 1024
MLP_ROW_TILE = ALL_ROWS // 10
FF_TILE = 512
SAMPLE_BLOCK = P_ROWS // DEC_BATCH

SAMPLE_ATTN_ROWS = 2
S5_CHUNK = 256
S5_SEG = S5_CHUNK // SUBLANE
S5_SCAN_LANES = 512


def _cparams(sem):
    return pltpu.CompilerParams(dimension_semantics=sem, vmem_limit_bytes=VMEM_LIMIT)


def _layer_spec(shape, layer):
    zeros = (0,) * len(shape)
    return pl.BlockSpec((None,) + tuple(shape), lambda *_: (layer,) + zeros)


def _whole_spec(shape):
    zeros = (0,) * len(shape)
    return pl.BlockSpec(tuple(shape), lambda *_: zeros)


def _weight_cols_spec(k, n, tn, layer):
    mode = pl.Buffered(1) if tn == n else None
    return pl.BlockSpec((None, k, tn), lambda i, j: (layer, 0, j), pipeline_mode=mode)


def _prompt_row_block(b, t, steps, rows):
    return jnp.minimum(b * steps + t, P_ROWS // rows)


def _prompt_batch(b):
    return jnp.minimum(b, BATCH - 1)


def _prompt_step(body, row_outputs):
    b = pl.program_id(0)
    pl.when(b < BATCH)(body)

    @pl.when(b == BATCH)
    def _():
        for ref in row_outputs:
            ref[...] = jnp.zeros_like(ref)


def _sigmoid(x):
    return 1.0 / (1.0 + jnp.exp(-x))


def _silu(x):
    return x * _sigmoid(x)


def _softplus(x):
    return jnp.maximum(x, 0.0) + jnp.log(1.0 + jnp.exp(-jnp.abs(x)))


def _gelu_tanh(x):
    return 0.5 * x * (1.0 + jnp.tanh(math.sqrt(2.0 / math.pi) * (x + 0.044715 * (x * x * x))))


def _rms(x, g):
    return x * lax.rsqrt(jnp.mean(x * x, axis=-1, keepdims=True) + EPS) * g


def _dot(a, b):
    return jnp.dot(a, b, preferred_element_type=F32)


def _split_bf16(x):
    hi = x.astype(BF16)
    lo = (x - hi.astype(F32)).astype(BF16)
    return hi, lo


def _norm_matmul_kernel(*refs, has_res):
    if has_res:
        x_ref, g_ref, w_ref, r_ref, o_ref, xn_ref = refs
    else:
        x_ref, g_ref, w_ref, o_ref, xn_ref = refs
        r_ref = None

    @pl.when(pl.program_id(1) == 0)
    def _():
        xn_ref[...] = _rms(x_ref[...], g_ref[...]).astype(BF16)

    acc = _dot(xn_ref[...], w_ref[...])
    if has_res:
        acc = acc + r_ref[...]
    o_ref[...] = acc.astype(o_ref.dtype)


def norm_matmul(x, g, w, layer, *, out_dtype=F32, tm, tn):
    m, k = x.shape
    n = w.shape[2]
    return pl.pallas_call(
        functools.partial(_norm_matmul_kernel, has_res=False),
        grid=(m // tm, n // tn),
        in_specs=[pl.BlockSpec((tm, k), lambda i, j: (i, 0)),
                  _layer_spec((1, k), layer),
                  _weight_cols_spec(k, n, tn, layer)],
        out_specs=pl.BlockSpec((tm, tn), lambda i, j: (i, j)),
        out_shape=jax.ShapeDtypeStruct((m, n), out_dtype),
        scratch_shapes=[pltpu.VMEM((tm, k), BF16)],
        compiler_params=_cparams(("parallel", "arbitrary")),
        name="norm_matmul",
    )(x, g, w)


def _res_matmul_kernel(x_ref, w_ref, r_ref, o_ref):
    o_ref[...] = r_ref[...] + _dot(x_ref[...], w_ref[...])


def res_matmul(x, w, res, layer, *, tm, tn):
    m, k = x.shape
    n = w.shape[2]
    return pl.pallas_call(
        _res_matmul_kernel,
        grid=(m // tm, n // tn),
        in_specs=[pl.BlockSpec((tm, k), lambda i, j: (i, 0)),
                  _weight_cols_spec(k, n, tn, layer),
                  pl.BlockSpec((tm, tn), lambda i, j: (i, j))],
        out_specs=pl.BlockSpec((tm, tn), lambda i, j: (i, j)),
        out_shape=jax.ShapeDtypeStruct((m, n), F32),
        compiler_params=_cparams(("parallel", "arbitrary")),
        name="res_matmul",
    )(x, w, res)


def _out_proj_kernel(m0_ref, m1_ref, m2_ref, m3_ref, w_ref, r_ref, o_ref):
    x = jnp.concatenate([m0_ref[...], m1_ref[...], m2_ref[...], m3_ref[...]], axis=-1)
    o_ref[...] = r_ref[...] + _dot(x, w_ref[...])


def out_proj(mixes, w, res, layer, *, tm):
    m = res.shape[0]
    mix_spec = pl.BlockSpec((tm, GROUP_WIDTH), lambda i, j: (i, 0))
    row_spec = pl.BlockSpec((tm, D_MODEL), lambda i, j: (i, 0))
    return pl.pallas_call(
        _out_proj_kernel,
        grid=(m // tm, 1),
        in_specs=[mix_spec] * 4 + [_weight_cols_spec(D_MODEL, D_MODEL, D_MODEL, layer), row_spec],
        out_specs=row_spec,
        out_shape=jax.ShapeDtypeStruct((m, D_MODEL), F32),
        compiler_params=_cparams(("parallel", "arbitrary")),
        name="out_proj",
    )(*mixes, w, res)


def _mlp_kernel(x_ref, g_ref, wu_ref, wd_ref, o_ref, xn_ref):
    j = pl.program_id(1)

    @pl.when(j == 0)
    def _():
        x = x_ref[...]
        xn_ref[...] = _rms(x, g_ref[...]).astype(BF16)
        o_ref[...] = x

    a = jnp.square(jnp.maximum(_dot(xn_ref[...], wu_ref[...].astype(BF16)), 0.0))
    o_ref[...] += _dot(a.astype(BF16), wd_ref[...].astype(BF16))


def mlp(x, g, w_up, w_down, layer, *, tm, tf):
    m = x.shape[0]
    return pl.pallas_call(
        _mlp_kernel,
        grid=(m // tm, D_FF // tf),
        in_specs=[pl.BlockSpec((tm, D_MODEL), lambda i, j: (i, 0)),
                  _layer_spec((1, D_MODEL), layer),
                  pl.BlockSpec((None, D_MODEL, tf), lambda i, j: (layer, 0, j)),
                  pl.BlockSpec((None, tf, D_MODEL), lambda i, j: (layer, j, 0))],
        out_specs=pl.BlockSpec((tm, D_MODEL), lambda i, j: (i, 0)),
        out_shape=jax.ShapeDtypeStruct((m, D_MODEL), F32),
        scratch_shapes=[pltpu.VMEM((tm, D_MODEL), BF16)],
        compiler_params=_cparams(("parallel", "arbitrary")),
        name="mlp",
    )(x, g, w_up, w_down)


def _final_norm_kernel(x_ref, g_ref, o_ref):
    o_ref[...] = _rms(x_ref[...], g_ref[...])


def final_norm(x, g, *, rows, first_block, tm):
    return pl.pallas_call(
        _final_norm_kernel,
        grid=(rows // tm,),
        in_specs=[pl.BlockSpec((tm, D_MODEL), lambda i: (first_block + i, 0)), _whole_spec((1, D_MODEL))],
        out_specs=pl.BlockSpec((tm, D_MODEL), lambda i: (i, 0)),
        out_shape=jax.ShapeDtypeStruct((rows, D_MODEL), F32),
        compiler_params=_cparams(("parallel",)),
        name="final_norm",
    )(x, g.reshape(1, D_MODEL))


def _prompt_attn_kernel(q_ref, k_ref, v_ref, o_ref):
    _prompt_step(lambda: _prompt_attn_body(q_ref, k_ref, v_ref, o_ref), (o_ref,))


def _prompt_attn_body(q_ref, k_ref, v_ref, o_ref):
    scale = XA_HEAD_DIM ** -0.5
    for h in range(XA_HEADS):
        cs = slice(h * XA_HEAD_DIM, (h + 1) * XA_HEAD_DIM)
        kh = k_ref[0, :, cs].astype(BF16)
        vh = v_ref[0, :, cs].astype(BF16)
        s = lax.dot_general(q_ref[:, cs], kh, (((1,), (1,)), ((), ())), preferred_element_type=F32) * scale
        p = jnp.exp(s - jnp.max(s, axis=-1, keepdims=True))
        p = p / jnp.sum(p, axis=-1, keepdims=True)
        o_ref[:, cs] = _dot(p.astype(BF16), vh).astype(o_ref.dtype)


def prompt_attn(q, k, v, *, tq):
    nq = SEQ // tq
    return pl.pallas_call(
        _prompt_attn_kernel,
        grid=(BATCH + 1, nq),
        in_specs=[pl.BlockSpec((tq, D_MODEL), lambda b, i: (_prompt_row_block(b, i, nq, tq), 0)),
                  pl.BlockSpec((1, N_MEM, D_MODEL), lambda b, i: (_prompt_batch(b), 0, 0)),
                  pl.BlockSpec((1, N_MEM, D_MODEL), lambda b, i: (_prompt_batch(b), 0, 0))],
        out_specs=pl.BlockSpec((tq, D_MODEL), lambda b, i: (_prompt_row_block(b, i, nq, tq), 0)),
        out_shape=jax.ShapeDtypeStruct((ALL_ROWS, D_MODEL), BF16),
        compiler_params=_cparams(("arbitrary", "arbitrary")),
        name="prompt_attn",
    )(q, k, v)


def _sample_attn_kernel(q_ref, k_ref, v_ref, o_ref):
    scale = XA_HEAD_DIM ** -0.5
    half, quarter = XA_HEAD_DIM // 2, XA_HEAD_DIM // 4
    for r in range(SAMPLE_ATTN_ROWS):
        prod = k_ref[r] * (q_ref[r] * scale)
        fold = prod[:, :, :half] + prod[:, :, half:]
        fold = fold[:, :, :quarter] + fold[:, :, quarter:]
        s = jnp.sum(fold, axis=-1, keepdims=True)
        e = jnp.exp(s - jnp.max(s, axis=0, keepdims=True))
        o = jnp.sum(e * v_ref[r], axis=0) / jnp.sum(e, axis=0)
        o_ref[r] = o.astype(o_ref.dtype)


def sample_attn(q, cache_k, cache_v, layer):
    nr = SAMPLE_ATTN_ROWS
    row = pl.BlockSpec((nr, XA_HEADS, XA_HEAD_DIM), lambda b: (b, 0, 0))
    mem = pl.BlockSpec((None, nr, N_MEM, XA_HEADS, XA_HEAD_DIM), lambda b: (layer, b, 0, 0, 0))
    return pl.pallas_call(
        _sample_attn_kernel,
        grid=(DEC_BATCH // nr,),
        in_specs=[row, mem, mem],
        out_specs=row,
        out_shape=jax.ShapeDtypeStruct((DEC_BATCH, XA_HEADS, XA_HEAD_DIM), BF16),
        compiler_params=_cparams(("parallel",)),
        name="sample_attn",
    )(q, cache_k, cache_v)


def _s5_abar(lr, li, ldt):
    delta = jnp.exp(ldt)
    mag = jnp.exp(lr * delta)
    return mag * jnp.cos(li * delta), mag * jnp.sin(li * delta)


def _s5_bbar_kernel(lr_ref, li_ref, ldt_ref, bre_ref, bim_ref, bbr_ref, bbi_ref):
    lr, li = lr_ref[...], li_ref[...]
    ar, ai = _s5_abar(lr, li, ldt_ref[...])
    den = lr * lr + li * li
    cr = ((ar - 1.0) * lr + ai * li) / den
    ci = (ai * lr - (ar - 1.0) * li) / den
    br, bi = bre_ref[...], bim_ref[...]
    bbr_ref[...] = cr * br - ci * bi
    bbi_ref[...] = cr * bi + ci * br


def _s5_pow_kernel(lr_ref, li_ref, ldt_ref, pr_ref, pi_ref):
    ar, ai = _s5_abar(lr_ref[...], li_ref[...], ldt_ref[...])
    qr, qi = ar, ai
    pr_ref[0] = qr
    pi_ref[0] = qi
    for e in range(1, S5_SEG):
        qr, qi = qr * ar - qi * ai, qr * ai + qi * ar
        pr_ref[e] = qr
        pi_ref[e] = qi


def s5_prepare(lam_re, lam_im, log_dt, b_re, b_im, c_re, c_im):
    dg = DEPTH * S5_GROUPS
    ldt = jnp.broadcast_to(log_dt[..., None], (DEPTH, S5_GROUPS, S5_STATE))
    rep = lambda a: jnp.repeat(a.reshape(dg, S5_STATE), S5_CH, axis=0)
    to_rows = lambda b: jnp.transpose(b, (0, 1, 3, 2)).reshape(dg * S5_CH, S5_STATE)
    shp = jax.ShapeDtypeStruct((dg * S5_CH, S5_STATE), F32)
    bbr, bbi = pl.pallas_call(_s5_bbar_kernel, out_shape=(shp, shp), name="s5_bbar")(
        rep(lam_re), rep(lam_im), rep(ldt), to_rows(b_re), to_rows(b_im))
    pshp = jax.ShapeDtypeStruct((S5_SEG, dg, S5_STATE), F32)
    pr, pi = pl.pallas_call(_s5_pow_kernel, out_shape=(pshp, pshp), name="s5_pow")(
        lam_re.reshape(dg, S5_STATE), lam_im.reshape(dg, S5_STATE), ldt.reshape(dg, S5_STATE))

    gpb = LANE // S5_CH
    nblk = S5_GROUPS // gpb
    eye = jnp.eye(gpb, dtype=F32)

    def b_blocks(bb):
        bb = bb.reshape(DEPTH, nblk, gpb, S5_CH, S5_STATE)
        return jnp.einsum("digkn,gh->digkhn", bb, eye).reshape(DEPTH, nblk, LANE, gpb * S5_STATE).astype(BF16)

    def c_blocks(cc):
        cc = cc.reshape(DEPTH, nblk, gpb, S5_CH, S5_STATE)
        return jnp.einsum("digkn,gh->dignhk", cc, eye).reshape(DEPTH, nblk, gpb * S5_STATE, LANE).astype(BF16)

    def pow_rows(p):
        return jnp.transpose(p.reshape(S5_SEG, DEPTH, S5_LANES), (1, 0, 2))

    pw_r, pw_i = pow_rows(pr), pow_rows(pi)
    tile = lambda p: jnp.broadcast_to(p[:, :, None, :], (DEPTH, S5_SEG, SUBLANE, S5_LANES))
    return dict(b_re=b_blocks(bbr), b_im=b_blocks(bbi), c_re=c_blocks(c_re), c_imn=c_blocks(-c_im),
                pw_re=pw_r, pw_im=pw_i, pwt_re=tile(pw_r), pwt_im=tile(pw_i))


def _s5_bu(ub, bre_ref, bim_ref):
    res_r, res_i = [], []
    for i in range(GROUP_WIDTH // LANE):
        ui = ub[:, i * LANE:(i + 1) * LANE]
        res_r.append(_dot(ui, bre_ref[i]))
        res_i.append(_dot(ui, bim_ref[i]))
    return res_r, res_i


def _s5_tail(u, hs_re, hs_im, cre_ref, cimn_ref, d_ref, wglu_ref, bglu_ref, g_ref):
    ys = []
    for i in range(GROUP_WIDTH // LANE):
        ys.append(_dot(hs_re(i).astype(BF16), cre_ref[i]) + _dot(hs_im(i).astype(BF16), cimn_ref[i]))
    y = jnp.concatenate(ys, axis=-1) + d_ref[...] * u
    y = _gelu_tanh(y)
    y = y * _sigmoid(_dot(y.astype(BF16), wglu_ref[...]) + bglu_ref[...])
    return _rms(y, g_ref[...])


def _s5_param_specs(layer):
    nblk = GROUP_WIDTH // LANE
    sblk = S5_LANES // nblk
    return [_layer_spec((nblk, LANE, sblk), layer),
            _layer_spec((nblk, LANE, sblk), layer),
            _layer_spec((nblk, sblk, LANE), layer),
            _layer_spec((nblk, sblk, LANE), layer),
            _layer_spec((1, GROUP_WIDTH), layer),
            _layer_spec((GROUP_WIDTH, GROUP_WIDTH), layer),
            _layer_spec((1, GROUP_WIDTH), layer),
            _layer_spec((1, GROUP_WIDTH), layer)]


def _s5_param_args(sp):
    return (sp["s5_b_re"], sp["s5_b_im"], sp["s5_c_re"], sp["s5_c_imn"], sp["s5_d"], sp["s5_w_glu"],
            sp["s5_b_glu"], sp["g_s5"])


def _cmul_add(x_r, x_i, a_r, a_i, h_r, h_i):
    return x_r + a_r * h_r - a_i * h_i, x_i + a_r * h_i + a_i * h_r


def _prompt_s5_kernel(*refs):
    _prompt_step(lambda: _prompt_s5_body(*refs), (refs[15],))


def _prompt_s5_body(u_ref, perm_ref, unperm_ref, pw_re_ref, pw_im_ref, pwt_re_ref, pwt_im_ref,
                    bre_ref, bim_ref, cre_ref, cimn_ref, d_ref, wglu_ref, bglu_ref, g_ref,
                    mix_ref, sre_ref, sim_ref, hre_ref, him_ref, cr_ref, ci_ref):
    sblk = S5_LANES // (GROUP_WIDTH // LANE)
    W = S5_SCAN_LANES

    @pl.when(pl.program_id(1) == 0)
    def _():
        cr_ref[...] = jnp.zeros_like(cr_ref)
        ci_ref[...] = jnp.zeros_like(ci_ref)

    u_hi, u_lo = _split_bf16(u_ref[...])
    ub = _dot(perm_ref[...], u_hi)
    u = ub + _dot(perm_ref[...], u_lo)
    bu_r, bu_i = _s5_bu(ub.astype(BF16), bre_ref, bim_ref)
    for i in range(len(bu_r)):
        hre_ref[:, i * sblk:(i + 1) * sblk] = bu_r[i]
        him_ref[:, i * sblk:(i + 1) * sblk] = bu_i[i]

    row = lax.broadcasted_iota(jnp.int32, (SUBLANE, W), 0)
    for lb in range(0, S5_LANES, W):
        ls = slice(lb, lb + W)
        a_r = jnp.broadcast_to(pw_re_ref[0:1, ls], (SUBLANE, W))
        a_i = jnp.broadcast_to(pw_im_ref[0:1, ls], (SUBLANE, W))

        def local_step(j, h, ls=ls, a_r=a_r, a_i=a_i):
            rs = pl.ds(pl.multiple_of(j * SUBLANE, SUBLANE), SUBLANE)
            n_r, n_i = _cmul_add(hre_ref[rs, ls], him_ref[rs, ls], a_r, a_i, h[0], h[1])
            hre_ref[rs, ls] = n_r
            him_ref[rs, ls] = n_i
            return n_r, n_i

        zero = jnp.zeros((SUBLANE, W), F32)
        e_r, e_i = lax.fori_loop(0, S5_SEG, local_step, (zero, zero))

        s_r = pw_re_ref[S5_SEG - 1:S5_SEG, ls]
        s_i = pw_im_ref[S5_SEG - 1:S5_SEG, ls]
        c_r, c_i = cr_ref[:, ls], ci_ref[:, ls]
        in_r, in_i = zero, zero
        for s in range(SUBLANE):
            in_r = jnp.where(row == s, c_r, in_r)
            in_i = jnp.where(row == s, c_i, in_i)
            c_r, c_i = _cmul_add(e_r[s:s + 1, :], e_i[s:s + 1, :], s_r, s_i, c_r, c_i)
        cr_ref[:, ls] = c_r
        ci_ref[:, ls] = c_i

        def fix_step(j, carry, ls=ls, in_r=in_r, in_i=in_i):
            rs = pl.ds(pl.multiple_of(j * SUBLANE, SUBLANE), SUBLANE)
            n_r, n_i = _cmul_add(hre_ref[rs, ls], him_ref[rs, ls], pwt_re_ref[j, :, ls], pwt_im_ref[j, :, ls],
                                 in_r, in_i)
            hre_ref[rs, ls] = n_r
            him_ref[rs, ls] = n_i
            return carry

        lax.fori_loop(0, S5_SEG, fix_step, 0)

    sre_ref[0] = cr_ref[...]
    sim_ref[0] = ci_ref[...]
    y = _s5_tail(u, lambda i: hre_ref[:, i * sblk:(i + 1) * sblk], lambda i: him_ref[:, i * sblk:(i + 1) * sblk],
                 cre_ref, cimn_ref, d_ref, wglu_ref, bglu_ref, g_ref)
    mix_ref[...] = _dot(unperm_ref[...], y.astype(mix_ref.dtype)).astype(mix_ref.dtype)


def prompt_s5(proj, sp, layer):
    nt = SEQ // S5_CHUNK
    t_of_row = (np.arange(S5_CHUNK) % SUBLANE) * S5_SEG + np.arange(S5_CHUNK) // SUBLANE
    perm = jnp.asarray(np.eye(S5_CHUNK, dtype=np.float32)[t_of_row], BF16)
    st = jax.ShapeDtypeStruct((BATCH, 1, S5_LANES), F32)
    st_spec = pl.BlockSpec((1, 1, S5_LANES), lambda b, t: (_prompt_batch(b), 0, 0))
    rb = lambda b, t: _prompt_row_block(b, t, nt, S5_CHUNK)
    return pl.pallas_call(
        _prompt_s5_kernel,
        grid=(BATCH + 1, nt),
        in_specs=[pl.BlockSpec((S5_CHUNK, GROUP_WIDTH), lambda b, t: (rb(b, t), COL_U5 // GROUP_WIDTH)),
                  _whole_spec((S5_CHUNK, S5_CHUNK)), _whole_spec((S5_CHUNK, S5_CHUNK)),
                  _layer_spec((S5_SEG, S5_LANES), layer), _layer_spec((S5_SEG, S5_LANES), layer),
                  _layer_spec((S5_SEG, SUBLANE, S5_LANES), layer), _layer_spec((S5_SEG, SUBLANE, S5_LANES), layer)]
                 + _s5_param_specs(layer),
        out_specs=(pl.BlockSpec((S5_CHUNK, GROUP_WIDTH), lambda b, t: (rb(b, t), 0)), st_spec, st_spec),
        out_shape=(jax.ShapeDtypeStruct((ALL_ROWS, GROUP_WIDTH), BF16), st, st),
        scratch_shapes=[pltpu.VMEM((S5_CHUNK, S5_LANES), F32), pltpu.VMEM((S5_CHUNK, S5_LANES), F32),
                        pltpu.VMEM((1, S5_LANES), F32), pltpu.VMEM((1, S5_LANES), F32)],
        compiler_params=_cparams(("arbitrary", "arbitrary")),
        name="prompt_s5",
    )(proj, perm, perm.T, sp["s5_pw_re"], sp["s5_pw_im"], sp["s5_pwt_re"], sp["s5_pwt_im"], *_s5_param_args(sp))


def _prompt_ssd_kernel(*refs):
    _prompt_step(lambda: _prompt_ssd_body(*refs), (refs[8],))


def _prompt_ssd_body(zx_ref, dt_ref, cw_ref, cb_ref, dtb_ref, alog_ref, dexp_ref, g_ref,
                     mix_ref, cst_ref, hst_ref, xbuf_ref, h_ref):
    L = SSD_CHUNK
    hist = SSD_CONV - 1
    base = SUBLANE

    @pl.when(pl.program_id(1) == 0)
    def _():
        xbuf_ref[0:base, :] = jnp.zeros((base, XBC_WIDTH), F32)
        h_ref[...] = jnp.zeros_like(h_ref)

    @pl.when(pl.program_id(1) > 0)
    def _():
        xbuf_ref[base - hist:base, :] = xbuf_ref[base + L - hist:base + L, :]

    xbc = zx_ref[:, COL_XBC:COL_XBC + XBC_WIDTH]
    xbuf_ref[base:base + L, :] = xbc
    cst_ref[0] = xbuf_ref[base + L - hist:base + L, :]
    conv = cb_ref[...] + cw_ref[hist:hist + 1, :] * xbc
    for k in range(hist):
        conv = conv + cw_ref[k:k + 1, :] * xbuf_ref[base - hist + k:base - hist + k + L, :]
    xc = _silu(conv)
    xs = xc[:, :GROUP_WIDTH]
    ng = SSD_GROUPS * SSD_STATE
    bm = xc[:, GROUP_WIDTH:GROUP_WIDTH + ng].astype(BF16)
    cm = xc[:, GROUP_WIDTH + ng:].astype(BF16)

    dt = _softplus(dt_ref[...] + dtb_ref[...])
    a = -jnp.exp(alog_ref[...])
    ri = lax.broadcasted_iota(jnp.int32, (L, L), 0)
    ci = lax.broadcasted_iota(jnp.int32, (L, L), 1)
    causal = ri >= ci
    acum = jnp.dot(causal.astype(F32), dt * a, preferred_element_type=F32, precision=lax.Precision.HIGHEST)
    acum_t = acum.T
    last = acum[L - 1:L, :]
    to_end = jnp.exp(last - acum)
    e_acum = jnp.exp(acum)
    chunk_decay = jnp.exp(last)

    ys = []
    rep = SSD_HEADS // SSD_GROUPS
    cb = [lax.dot_general(cm[:, g * SSD_STATE:(g + 1) * SSD_STATE], bm[:, g * SSD_STATE:(g + 1) * SSD_STATE],
                          (((1,), (1,)), ((), ())), preferred_element_type=F32) for g in range(SSD_GROUPS)]
    for h in range(SSD_HEADS):
        g = h // rep
        hs = slice(h * SSD_HEAD_DIM, (h + 1) * SSD_HEAD_DIM)
        gs = slice(g * SSD_STATE, (g + 1) * SSD_STATE)
        seg = acum[:, h:h + 1] - acum_t[h:h + 1, :]
        decay = jnp.exp(jnp.where(causal, seg, -jnp.inf))
        xs_h = xs[:, hs]
        xdt = xs_h * dt[:, h:h + 1]
        y = _dot((cb[g] * decay).astype(BF16), xdt.astype(BF16))
        h_prev = h_ref[h]
        y_off = lax.dot_general(cm[:, gs], h_prev.astype(BF16), (((1,), (1,)), ((), ())),
                                preferred_element_type=F32)
        y = y + y_off * e_acum[:, h:h + 1]
        st = lax.dot_general((xdt * to_end[:, h:h + 1]).astype(BF16), bm[:, gs], (((0,), (0,)), ((), ())),
                             preferred_element_type=F32)
        h_ref[h] = h_prev * chunk_decay[:, h:h + 1] + st
        ys.append(y)
    y = (jnp.concatenate(ys, axis=-1) + dexp_ref[...] * xs) * _silu(zx_ref[:, COL_Z:COL_Z + GROUP_WIDTH])
    mix_ref[...] = _rms(y, g_ref[...]).astype(mix_ref.dtype)
    hst_ref[0] = h_ref[...]


def prompt_ssd(proj, sp, layer):
    nc = SEQ // SSD_CHUNK
    rb = lambda b, c: _prompt_row_block(b, c, nc, SSD_CHUNK)
    return pl.pallas_call(
        _prompt_ssd_kernel,
        grid=(BATCH + 1, nc),
        in_specs=[pl.BlockSpec((SSD_CHUNK, ZX_WIDTH), lambda b, c: (rb(b, c), 0)),
                  pl.BlockSpec((SSD_CHUNK, LANE), lambda b, c: (rb(b, c), COL_DT // LANE)),
                  _layer_spec((SSD_CONV, XBC_WIDTH), layer),
                  _layer_spec((1, XBC_WIDTH), layer),
                  _layer_spec((1, LANE), layer),
                  _layer_spec((1, LANE), layer),
                  _layer_spec((1, GROUP_WIDTH), layer),
                  _layer_spec((1, GROUP_WIDTH), layer)],
        out_specs=(pl.BlockSpec((SSD_CHUNK, GROUP_WIDTH), lambda b, c: (rb(b, c), 0)),
                   pl.BlockSpec((1, SSD_CONV - 1, XBC_WIDTH), lambda b, c: (_prompt_batch(b), 0, 0)),
                   pl.BlockSpec((1, SSD_HEADS, SSD_HEAD_DIM, SSD_STATE), lambda b, c: (_prompt_batch(b), 0, 0, 0))),
        out_shape=(jax.ShapeDtypeStruct((ALL_ROWS, GROUP_WIDTH), BF16),
                   jax.ShapeDtypeStruct((BATCH, SSD_CONV - 1, XBC_WIDTH), F32),
                   jax.ShapeDtypeStruct((BATCH, SSD_HEADS, SSD_HEAD_DIM, SSD_STATE), F32)),
        scratch_shapes=[pltpu.VMEM((SUBLANE + SSD_CHUNK, XBC_WIDTH), F32),
                        pltpu.VMEM((SSD_HEADS, SSD_HEAD_DIM, SSD_STATE), F32)],
        compiler_params=_cparams(("arbitrary", "arbitrary")),
        name="prompt_ssd",
    )(proj, proj, sp["ssd_conv_w"], sp["ssd_conv_b"], sp["ssd_dt_bias"], sp["ssd_a_log"],
      sp["ssd_d_exp"], sp["g_ssd"])


def _pool_counts(pos, w):
    return jnp.minimum(w, pos + 1).astype(F32)


def _prompt_convpool_kernel(*refs):
    _prompt_step(lambda: _prompt_convpool_body(*refs), (refs[9], refs[10]))


def _prompt_convpool_body(gb_ref, gc_ref, hv_ref, up_ref, scw_ref, pw_ref, ps_ref, gsc_ref, gpl_ref,
                          msc_ref, mpl_ref, scst_ref, plst_ref, vbuf_ref, pbuf_ref):
    rows = gb_ref.shape[0]
    vb = SUBLANE
    pb = 2 * SUBLANE
    vh = SC_CONV - 1
    t = pl.program_id(1)

    @pl.when(t == 0)
    def _():
        vbuf_ref[0:vb, :] = jnp.zeros((vb, GROUP_WIDTH), F32)
        pbuf_ref[0:pb, :] = jnp.zeros((pb, GROUP_WIDTH), F32)

    @pl.when(t > 0)
    def _():
        vbuf_ref[vb - vh:vb, :] = vbuf_ref[vb + rows - vh:vb + rows, :]
        pbuf_ref[0:pb, :] = pbuf_ref[rows:rows + pb, :]

    v = gc_ref[...] * hv_ref[...]
    vbuf_ref[vb:vb + rows, :] = v
    acc = scw_ref[vh:vh + 1, :] * v
    for k in range(vh):
        acc = acc + scw_ref[k:k + 1, :] * vbuf_ref[vb - vh + k:vb - vh + k + rows, :]
    msc_ref[...] = _rms(gb_ref[...] * acc, gsc_ref[...]).astype(msc_ref.dtype)
    scst_ref[0] = vbuf_ref[vb + rows - vh:vb + rows, :]

    u = up_ref[...]
    pbuf_ref[pb:pb + rows, :] = u
    pos = t * rows + lax.broadcasted_iota(jnp.int32, (rows, 1), 0)
    ys = []
    for gi, w in enumerate(POOL_WINDOWS):
        cs = slice(gi * POOL_GROUP, (gi + 1) * POOL_GROUP)
        s = u[:, cs]
        for j in range(1, w):
            s = s + pbuf_ref[pb - j:pb - j + rows, cs]
        pooled = s / _pool_counts(pos, w) - u[:, cs]
        ys.append(_dot(pooled.astype(BF16), pw_ref[gi]))
    y = jnp.concatenate(ys, axis=-1) * ps_ref[...]
    mpl_ref[...] = _rms(y, gpl_ref[...]).astype(mpl_ref.dtype)
    plst_ref[0] = pbuf_ref[pb + rows - POOL_HIST:pb + rows, :]


def prompt_convpool(proj, sp, layer):
    nt = SEQ // S5_CHUNK
    rb = lambda b, t: _prompt_row_block(b, t, nt, S5_CHUNK)
    col = lambda c: pl.BlockSpec((S5_CHUNK, GROUP_WIDTH), lambda b, t: (rb(b, t), c // GROUP_WIDTH))
    mix_spec = pl.BlockSpec((S5_CHUNK, GROUP_WIDTH), lambda b, t: (rb(b, t), 0))
    mix_shape = jax.ShapeDtypeStruct((ALL_ROWS, GROUP_WIDTH), BF16)
    return pl.pallas_call(
        _prompt_convpool_kernel,
        grid=(BATCH + 1, nt),
        in_specs=[col(COL_GB), col(COL_GC), col(COL_HV), col(COL_UP),
                  _layer_spec((SC_CONV, GROUP_WIDTH), layer),
                  _layer_spec((len(POOL_WINDOWS), POOL_GROUP, POOL_GROUP), layer),
                  _layer_spec((1, GROUP_WIDTH), layer), _layer_spec((1, GROUP_WIDTH), layer),
                  _layer_spec((1, GROUP_WIDTH), layer)],
        out_specs=(mix_spec, mix_spec,
                   pl.BlockSpec((1, SC_CONV - 1, GROUP_WIDTH), lambda b, t: (_prompt_batch(b), 0, 0)),
                   pl.BlockSpec((1, POOL_HIST, GROUP_WIDTH), lambda b, t: (_prompt_batch(b), 0, 0))),
        out_shape=(mix_shape, mix_shape,
                   jax.ShapeDtypeStruct((BATCH, SC_CONV - 1, GROUP_WIDTH), F32),
                   jax.ShapeDtypeStruct((BATCH, POOL_HIST, GROUP_WIDTH), F32)),
        scratch_shapes=[pltpu.VMEM((SUBLANE + S5_CHUNK, GROUP_WIDTH), F32),
                        pltpu.VMEM((2 * SUBLANE + S5_CHUNK, GROUP_WIDTH), F32)],
        compiler_params=_cparams(("arbitrary", "arbitrary")),
        name="prompt_convpool",
    )(proj, proj, proj, proj, sp["sc_conv_w"], sp["pool_w"], sp["pool_scale"], sp["g_sc"], sp["g_pool"])


def _sample_mix_kernel(proj_ref, cprev_ref, s5r_ref, s5i_ref, scprev_ref, plprev_ref,
                       cw_ref, cb_ref, dtb_ref, alog_ref,
                       pwr_ref, pwi_ref, bre_ref, bim_ref, cre_ref, cimn_ref, d5_ref, wglu_ref, bglu_ref, g5_ref,
                       scw_ref, pw_ref, ps_ref, gsc_ref, gpl_ref,
                       m5_ref, msc_ref, mpl_ref, cst_ref, s5ro_ref, s5io_ref, scst_ref, plst_ref,
                       xs_ref, xdt_ref, da_ref, b2_ref, c2_ref):
    nb = proj_ref.shape[0]
    W = GROUP_WIDTH
    xbc = proj_ref[:, COL_XBC:COL_XBC + XBC_WIDTH]
    hist = SSD_CONV - 1
    conv = cb_ref[...] + cw_ref[hist:hist + 1, :] * xbc
    for k in range(hist):
        conv = conv + cw_ref[k:k + 1, :] * cprev_ref[:, k * XBC_WIDTH:(k + 1) * XBC_WIDTH]
    cst_ref[:, 0:(hist - 1) * XBC_WIDTH] = cprev_ref[:, XBC_WIDTH:hist * XBC_WIDTH]
    cst_ref[:, (hist - 1) * XBC_WIDTH:hist * XBC_WIDTH] = xbc
    xc = _silu(conv)
    xs = xc[:, :W]
    bm = xc[:, W:W + LANE]
    cm = xc[:, W + LANE:W + 2 * LANE]
    dt = _softplus(proj_ref[:, COL_DT:COL_DT + LANE] + dtb_ref[...])
    da = jnp.exp(dt * (-jnp.exp(alog_ref[...])))
    lane_w = lax.broadcasted_iota(jnp.int32, (nb, W), 1)
    dt_exp = jnp.zeros((nb, W), F32)
    for h in range(SSD_HEADS):
        dt_exp = jnp.where(lane_w // SSD_HEAD_DIM == h, dt[:, h:h + 1], dt_exp)
        da_ref[:, h * LANE:(h + 1) * LANE] = jnp.broadcast_to(da[:, h:h + 1], (nb, LANE))
    xs_ref[...] = xs
    xdt_ref[...] = xs * dt_exp
    lane = lax.broadcasted_iota(jnp.int32, (nb, LANE), 1)
    low = lane < SSD_STATE
    for src, dst in ((bm, b2_ref), (cm, c2_ref)):
        swapped = pltpu.roll(src, SSD_STATE, 1)
        dst[:, 0:LANE] = jnp.where(low, src, swapped)
        dst[:, LANE:2 * LANE] = jnp.where(low, swapped, src)

    u5 = proj_ref[:, COL_U5:COL_U5 + W]
    bu_r, bu_i = _s5_bu(u5.astype(BF16), bre_ref, bim_ref)
    sblk = S5_LANES // len(bu_r)
    for i in range(len(bu_r)):
        ls = slice(i * sblk, (i + 1) * sblk)
        n_r, n_i = _cmul_add(bu_r[i], bu_i[i], pwr_ref[0:1, ls], pwi_ref[0:1, ls], s5r_ref[:, ls], s5i_ref[:, ls])
        s5ro_ref[:, ls] = n_r
        s5io_ref[:, ls] = n_i
    y5 = _s5_tail(u5, lambda i: s5ro_ref[:, i * sblk:(i + 1) * sblk], lambda i: s5io_ref[:, i * sblk:(i + 1) * sblk],
                  cre_ref, cimn_ref, d5_ref, wglu_ref, bglu_ref, g5_ref)
    m5_ref[...] = y5.astype(m5_ref.dtype)

    v = proj_ref[:, COL_GC:COL_GC + W] * proj_ref[:, COL_HV:COL_HV + W]
    vh = SC_CONV - 1
    acc = scw_ref[vh:vh + 1, :] * v
    for k in range(vh):
        acc = acc + scw_ref[k:k + 1, :] * scprev_ref[:, k * W:(k + 1) * W]
    msc_ref[...] = _rms(proj_ref[:, COL_GB:COL_GB + W] * acc, gsc_ref[...]).astype(msc_ref.dtype)
    scst_ref[:, 0:(vh - 1) * W] = scprev_ref[:, W:vh * W]
    scst_ref[:, (vh - 1) * W:vh * W] = v

    up = proj_ref[:, COL_UP:COL_UP + W]
    ys = []
    for gi, w in enumerate(POOL_WINDOWS):
        cs = slice(gi * POOL_GROUP, (gi + 1) * POOL_GROUP)
        s = up[:, cs]
        for j in range(1, w):
            k = POOL_HIST - j
            s = s + plprev_ref[:, k * W + gi * POOL_GROUP:k * W + (gi + 1) * POOL_GROUP]
        pooled = s / float(min(w, PAST_LEN + 1)) - up[:, cs]
        ys.append(_dot(pooled.astype(BF16), pw_ref[gi]))
    y = jnp.concatenate(ys, axis=-1) * ps_ref[...]
    mpl_ref[...] = _rms(y, gpl_ref[...]).astype(mpl_ref.dtype)
    plst_ref[:, 0:(POOL_HIST - 1) * W] = plprev_ref[:, W:POOL_HIST * W]
    plst_ref[:, (POOL_HIST - 1) * W:POOL_HIST * W] = up


def sample_mix(proj, states, sp, layer):
    nb = DEC_BATCH
    W = GROUP_WIDTH
    f = lambda n: jax.ShapeDtypeStruct((nb, n), F32)
    b = lambda n: jax.ShapeDtypeStruct((nb, n), BF16)
    out_widths = ((SSD_CONV - 1) * XBC_WIDTH, S5_LANES, S5_LANES, (SC_CONV - 1) * W, POOL_HIST * W,
                  W, W, SSD_HEADS * LANE, SSD_GROUPS * LANE, SSD_GROUPS * LANE)
    out_shape = (b(W), b(W), b(W)) + tuple(f(n) for n in out_widths)
    nblk = GROUP_WIDTH // LANE
    in_specs = ([pl.BlockSpec((nb, IN_PAD), lambda i: (SAMPLE_BLOCK, 0))]
                + [_layer_spec((nb, s.shape[2]), layer) for s in states]
                + [_layer_spec((SSD_CONV, XBC_WIDTH), layer), _layer_spec((1, XBC_WIDTH), layer),
                   _layer_spec((1, LANE), layer), _layer_spec((1, LANE), layer),
                   _layer_spec((S5_SEG, S5_LANES), layer), _layer_spec((S5_SEG, S5_LANES), layer)]
                + _s5_param_specs(layer)
                + [_layer_spec((SC_CONV, W), layer), _layer_spec((len(POOL_WINDOWS), POOL_GROUP, POOL_GROUP), layer),
                   _layer_spec((1, W), layer), _layer_spec((1, W), layer), _layer_spec((1, W), layer)])
    return pl.pallas_call(
        _sample_mix_kernel,
        grid=(1,),
        in_specs=in_specs,
        out_specs=tuple(_whole_spec(s.shape) for s in out_shape),
        out_shape=out_shape,
        compiler_params=_cparams(("arbitrary",)),
        name="sample_mix",
    )(proj, *states, sp["ssd_conv_w"], sp["ssd_conv_b"], sp["ssd_dt_bias"], sp["ssd_a_log"],
      sp["s5_pw_re"], sp["s5_pw_im"], *_s5_param_args(sp),
      sp["sc_conv_w"], sp["pool_w"], sp["pool_scale"], sp["g_sc"], sp["g_pool"])


def _sample_ssd_kernel(h0_ref, xdt_ref, da_ref, b2_ref, c2_ref, e_ref, r_ref, xs_ref, z_ref, dexp_ref, g_ref,
                       hn_ref, mix_ref, xrep_ref, prod_ref, y_ref):
    hp = pl.program_id(0)
    hpl = 2 * SSD_HEAD_DIM * SSD_STATE
    per_head = SSD_HEAD_DIM * SSD_STATE
    x_hi, x_lo = _split_bf16(xdt_ref[...])
    xrep_ref[...] = _dot(x_hi, e_ref[...]) + _dot(x_lo, e_ref[...])
    b2, c2 = b2_ref[...], c2_ref[...]
    for j in range(hpl // LANE):
        ls = slice(j * LANE, (j + 1) * LANE)
        hl = (j * LANE) // per_head
        hn = da_ref[:, hl * LANE:(hl + 1) * LANE] * h0_ref[:, ls] + xrep_ref[:, ls] * b2
        hn_ref[:, ls] = hn
        prod_ref[:, ls] = hn * c2
    p_hi, p_lo = _split_bf16(prod_ref[...])
    y_ref[hp] = _dot(p_hi, r_ref[...]) + _dot(p_lo, r_ref[...])

    @pl.when(hp == pl.num_programs(0) - 1)
    def _():
        y = jnp.concatenate([y_ref[i] for i in range(SSD_HEADS // 2)], axis=-1)
        y = (y + dexp_ref[...] * xs_ref[...]) * _silu(z_ref[...])
        mix_ref[...] = _rms(y, g_ref[...]).astype(mix_ref.dtype)


def sample_ssd(h0_all, xdt, da, b2, c2, xs, proj, sp, layer):
    nb = DEC_BATCH
    npairs = SSD_HEADS // 2
    hpl = 2 * SSD_HEAD_DIM * SSD_STATE
    expand = jnp.repeat(jnp.eye(LANE, dtype=BF16), SSD_STATE, axis=1)
    return pl.pallas_call(
        _sample_ssd_kernel,
        grid=(npairs,),
        in_specs=[pl.BlockSpec((None, nb, hpl), lambda i: (layer, 0, i)),
                  pl.BlockSpec((nb, LANE), lambda i: (0, i)),
                  pl.BlockSpec((nb, 2 * LANE), lambda i: (0, i)),
                  pl.BlockSpec((nb, LANE), lambda i: (0, i // (npairs // SSD_GROUPS))),
                  pl.BlockSpec((nb, LANE), lambda i: (0, i // (npairs // SSD_GROUPS))),
                  _whole_spec((LANE, hpl)),
                  _whole_spec((hpl, LANE)),
                  _whole_spec((nb, GROUP_WIDTH)),
                  pl.BlockSpec((nb, GROUP_WIDTH), lambda i: (SAMPLE_BLOCK, COL_Z // GROUP_WIDTH)),
                  _layer_spec((1, GROUP_WIDTH), layer),
                  _layer_spec((1, GROUP_WIDTH), layer)],
        out_specs=(pl.BlockSpec((nb, hpl), lambda i: (0, i)),
                   _whole_spec((nb, GROUP_WIDTH))),
        out_shape=(jax.ShapeDtypeStruct((nb, SSD_HEADS * SSD_HEAD_DIM * SSD_STATE), F32),
                   jax.ShapeDtypeStruct((nb, GROUP_WIDTH), BF16)),
        scratch_shapes=[pltpu.VMEM((nb, hpl), F32), pltpu.VMEM((nb, hpl), F32),
                        pltpu.VMEM((npairs, nb, LANE), F32)],
        compiler_params=_cparams(("arbitrary",)),
        name="sample_ssd",
    )(h0_all, xdt, da, b2, c2, expand, expand.T, xs, proj, sp["ssd_d_exp"], sp["g_ssd"])


def _regroup_w_in(w_in):
    head = COL_DT + SSD_HEADS
    pad = jnp.zeros((DEPTH, D_MODEL, COL_U5 - head), w_in.dtype)
    return jnp.concatenate([w_in[..., :head], pad, w_in[..., head:]], axis=-1).astype(BF16)


def _stacked_params(w):
    row = lambda v: v.reshape(DEPTH, 1, -1)
    pad_heads = lambda v: row(jnp.pad(v, ((0, 0), (0, LANE - SSD_HEADS))))
    g_mix = w["mix_out_g"].reshape(DEPTH, 4, 1, GROUP_WIDTH)
    sp = dict(w_in=_regroup_w_in(w["w_in"]))
    for k in ("w_out", "w_q", "w_k", "w_v", "w_o", "s5_w_glu", "pool_w"):
        sp[k] = w[k].astype(BF16)
    sp.update(w_up=w["w_up"], w_down=w["w_down"])
    for k in ("norm_mix_g", "norm_xa_g", "norm_mem_g", "norm_mlp_g", "ssd_conv_b", "s5_d", "s5_b_glu", "pool_scale"):
        sp[k] = row(w[k])
    sp.update(ssd_conv_w=w["ssd_conv_w"], sc_conv_w=w["sc_conv_w"],
              ssd_dt_bias=pad_heads(w["ssd_dt_bias"]), ssd_a_log=pad_heads(w["ssd_a_log"]),
              ssd_d_exp=row(jnp.repeat(w["ssd_d"], SSD_HEAD_DIM, axis=1)),
              g_ssd=g_mix[:, 0], g_s5=g_mix[:, 1], g_sc=g_mix[:, 2], g_pool=g_mix[:, 3])
    s5 = s5_prepare(w["s5_lam_re"], w["s5_lam_im"], w["s5_log_dt"], w["s5_b_re"], w["s5_b_im"],
                    w["s5_c_re"], w["s5_c_im"])
    sp.update({"s5_" + k: v for k, v in s5.items()})
    return sp


def _forward(x_prompt, x_sample, mem_prompt, state_ssd_conv, state_ssd, state_s5_re, state_s5_im,
             state_sconv, state_pool, cache_mem_k, cache_mem_v, final_norm_g, w):
    sp = _stacked_params(w)
    flat = lambda s: s.reshape(DEPTH, DEC_BATCH, -1)
    s_states = tuple(flat(s) for s in (state_ssd_conv, state_s5_re, state_s5_im, state_sconv, state_pool))
    s_ssd0 = flat(state_ssd)
    h = jnp.concatenate([x_prompt.reshape(P_ROWS, D_MODEL), x_sample.reshape(DEC_BATCH, D_MODEL)], axis=0)
    mem = mem_prompt.reshape(BATCH * N_MEM, D_MODEL)
    p_out = [[] for _ in range(8)]
    s_out = [[] for _ in range(6)]
    put_sample = lambda full, rows: lax.dynamic_update_slice(full, rows, (P_ROWS, 0))
    for l in range(DEPTH):
        mk = norm_matmul(mem, sp["norm_mem_g"], sp["w_k"], l, tm=BATCH * N_MEM, tn=COL_TILE)
        mv = norm_matmul(mem, sp["norm_mem_g"], sp["w_v"], l, tm=BATCH * N_MEM, tn=COL_TILE)

        proj = norm_matmul(h, sp["norm_mix_g"], sp["w_in"], l, tm=ROW_TILE, tn=COL_TILE)
        m_ssd, p_conv, p_ssd = prompt_ssd(proj, sp, l)
        m_s5, p_s5r, p_s5i = prompt_s5(proj, sp, l)
        m_sc, m_pl, p_sc, p_pl = prompt_convpool(proj, sp, l)
        (s_m5, s_msc, s_mpl, s_conv, s_s5r, s_s5i, s_sc, s_pl, xs, xdt, da, b2, c2) = sample_mix(
            proj, s_states, sp, l)
        s_ssd, s_mssd = sample_ssd(s_ssd0, xdt, da, b2, c2, xs, proj, sp, l)
        mixes = (put_sample(m_ssd, s_mssd), put_sample(m_s5, s_m5),
                 put_sample(m_sc, s_msc), put_sample(m_pl, s_mpl))

        h = out_proj(mixes, sp["w_out"], h, l, tm=ROW_TILE)
        q = norm_matmul(h, sp["norm_xa_g"], sp["w_q"], l, out_dtype=BF16, tm=ROW_TILE, tn=D_MODEL)
        o = prompt_attn(q, mk.reshape(BATCH, N_MEM, D_MODEL), mv.reshape(BATCH, N_MEM, D_MODEL), tq=512)
        q_s = q[P_ROWS:].astype(F32).reshape(DEC_BATCH, XA_HEADS, XA_HEAD_DIM)
        o_s = sample_attn(q_s, cache_mem_k, cache_mem_v, l)
        o = put_sample(o, o_s.reshape(DEC_BATCH, D_MODEL))
        h = res_matmul(o, sp["w_o"], h, l, tm=ROW_TILE, tn=D_MODEL)
        h = mlp(h, sp["norm_mlp_g"], sp["w_up"], sp["w_down"], l, tm=MLP_ROW_TILE, tf=FF_TILE)

        for lst, val in zip(p_out, (p_conv, p_ssd, p_s5r, p_s5i, p_sc, p_pl, mk, mv)):
            lst.append(val)
        for lst, val in zip(s_out, (s_conv, s_ssd, s_s5r, s_s5i, s_sc, s_pl)):
            lst.append(val)

    y_prompt = final_norm(h, final_norm_g, rows=P_ROWS, first_block=0, tm=COL_TILE)
    y_sample = final_norm(h, final_norm_g, rows=DEC_BATCH, first_block=SAMPLE_BLOCK, tm=DEC_BATCH)
    p_shapes = ((BATCH, SSD_CONV - 1, XBC_WIDTH), (BATCH, SSD_HEADS, SSD_HEAD_DIM, SSD_STATE),
                (BATCH, S5_GROUPS, S5_STATE), (BATCH, S5_GROUPS, S5_STATE),
                (BATCH, SC_CONV - 1, GROUP_WIDTH), (BATCH, POOL_HIST, GROUP_WIDTH),
                (BATCH, N_MEM, XA_HEADS, XA_HEAD_DIM), (BATCH, N_MEM, XA_HEADS, XA_HEAD_DIM))
    s_shapes = tuple((DEC_BATCH,) + s[1:] for s in p_shapes[:6])
    stack = lambda vals, shape: jnp.stack(vals).reshape((DEPTH,) + shape)
    return ((y_prompt.reshape(BATCH, SEQ, D_MODEL), y_sample.reshape(DEC_BATCH, 1, D_MODEL))
            + tuple(stack(v, s) for v, s in zip(p_out, p_shapes))
            + tuple(stack(v, s) for v, s in zip(s_out, s_shapes)))


_forward_jit = jax.jit(_forward)


def kernel(x_prompt, x_sample, mem_prompt, state_ssd_conv, state_ssd, state_s5_re, state_s5_im, state_sconv, state_pool, cache_mem_k, cache_mem_v, norm_mix_g, w_in, ssd_conv_w, ssd_conv_b, ssd_dt_bias, ssd_a_log, ssd_d, s5_lam_re, s5_lam_im, s5_log_dt, s5_b_re, s5_b_im, s5_c_re, s5_c_im, s5_d, s5_w_glu, s5_b_glu, sc_conv_w, pool_w, pool_scale, mix_out_g, w_out, norm_xa_g, norm_mem_g, w_q, w_k, w_v, w_o, norm_mlp_g, w_up, w_down, final_norm_g):
    w = dict(norm_mix_g=norm_mix_g, w_in=w_in, ssd_conv_w=ssd_conv_w, ssd_conv_b=ssd_conv_b,
             ssd_dt_bias=ssd_dt_bias, ssd_a_log=ssd_a_log, ssd_d=ssd_d, s5_lam_re=s5_lam_re,
             s5_lam_im=s5_lam_im, s5_log_dt=s5_log_dt, s5_b_re=s5_b_re, s5_b_im=s5_b_im,
             s5_c_re=s5_c_re, s5_c_im=s5_c_im, s5_d=s5_d, s5_w_glu=s5_w_glu, s5_b_glu=s5_b_glu,
             sc_conv_w=sc_conv_w, pool_w=pool_w, pool_scale=pool_scale, mix_out_g=mix_out_g,
             w_out=w_out, norm_xa_g=norm_xa_g, norm_mem_g=norm_mem_g, w_q=w_q, w_k=w_k, w_v=w_v,
             w_o=w_o, norm_mlp_g=norm_mlp_g, w_up=w_up, w_down=w_down)
    return _forward_jit(x_prompt, x_sample, mem_prompt, state_ssd_conv, state_ssd, state_s5_re, state_s5_im,
                        state_sconv, state_pool, cache_mem_k, cache_mem_v, final_norm_g, w)
```

```python
import functools
import math

import numpy as np
import jax
import jax.numpy as jnp
from jax import lax
from jax.experimental import pallas as pl
from jax.experimental.pallas import tpu as pltpu

F32 = jnp.float32
BF16 = jnp.bfloat16

D_MODEL = 2048
BATCH = 4
SEQ = 2048
DEPTH = 2
DEC_BATCH = 128
PAST_LEN = 16384
GROUP_WIDTH = D_MODEL // 4
SSD_HEAD_DIM = 64
SSD_HEADS = GROUP_WIDTH // SSD_HEAD_DIM
SSD_GROUPS = 2
SSD_STATE = 64
SSD_CONV = 4
SSD_CHUNK = 128
XBC_WIDTH = GROUP_WIDTH + 2 * SSD_GROUPS * SSD_STATE
S5_CH = 16
S5_GROUPS = GROUP_WIDTH // S5_CH
S5_STATE = 64
S5_LANES = S5_GROUPS * S5_STATE
SC_CONV = 3
POOL_WINDOWS = (2, 4, 8, 16)
POOL_GROUP = GROUP_WIDTH // len(POOL_WINDOWS)
POOL_HIST = max(POOL_WINDOWS) - 1
N_MEM = 256
XA_HEADS = 4
XA_HEAD_DIM = D_MODEL // XA_HEADS
D_FF = 4 * D_MODEL
EPS = 1e-6
P_ROWS = BATCH * SEQ
ALL_ROWS = P_ROWS + DEC_BATCH

LANE = 128
SUBLANE = 8
VMEM_LIMIT = 56 * 1024 * 1024

COL_Z = 0
COL_XBC = 512
COL_DT = 1280
A_WIDTH = 1536
ZX_WIDTH = COL_DT
COL_U5 = 0
COL_GB = 512
COL_GC = 1024
COL_HV = 1536
COL_UP = 2048
B_WIDTH = 2560
IN_ROW_TILE = 640

ROW_TILE = ALL_ROWS // 8
COL_TILE = 1024
MLP_ROW_TILE = ALL_ROWS // 10
FF_TILE = 512
SAMPLE_BLOCK = P_ROWS // DEC_BATCH

SAMPLE_ATTN_ROWS = 2
S5_CHUNK = 256
S5_SEG = S5_CHUNK // SUBLANE
S5_SCAN_LANES = 512


def _cparams(sem):
    return pltpu.CompilerParams(dimension_semantics=sem, vmem_limit_bytes=VMEM_LIMIT)


def _layer_spec(shape, layer):
    zeros = (0,) * len(shape)
    return pl.BlockSpec((None,) + tuple(shape), lambda *_: (layer,) + zeros)


def _whole_spec(shape):
    zeros = (0,) * len(shape)
    return pl.BlockSpec(tuple(shape), lambda *_: zeros)


def _weight_cols_spec(k, n, tn, layer):
    mode = pl.Buffered(1) if tn == n else None
    return pl.BlockSpec((None, k, tn), lambda i, j: (layer, 0, j), pipeline_mode=mode)


def _prompt_row_block(b, t, steps, rows):
    return jnp.minimum(b * steps + t, P_ROWS // rows)


def _prompt_batch(b):
    return jnp.minimum(b, BATCH - 1)


def _prompt_step(body, row_outputs):
    b = pl.program_id(0)
    pl.when(b < BATCH)(body)

    @pl.when(b == BATCH)
    def _():
        for ref in row_outputs:
            ref[...] = jnp.zeros_like(ref)


def _sigmoid(x):
    return 1.0 / (1.0 + jnp.exp(-x))


def _silu(x):
    return x * _sigmoid(x)


def _softplus(x):
    return jnp.maximum(x, 0.0) + jnp.log(1.0 + jnp.exp(-jnp.abs(x)))


def _gelu_tanh(x):
    return 0.5 * x * (1.0 + jnp.tanh(math.sqrt(2.0 / math.pi) * (x + 0.044715 * (x * x * x))))


def _rms(x, g):
    return x * lax.rsqrt(jnp.mean(x * x, axis=-1, keepdims=True) + EPS) * g


def _dot(a, b):
    return jnp.dot(a, b, preferred_element_type=F32)


def _split_bf16(x):
    hi = x.astype(BF16)
    lo = (x - hi.astype(F32)).astype(BF16)
    return hi, lo


def _norm_matmul_kernel(*refs, has_res):
    if has_res:
        x_ref, g_ref, w_ref, r_ref, o_ref, xn_ref = refs
    else:
        x_ref, g_ref, w_ref, o_ref, xn_ref = refs
        r_ref = None

    @pl.when(pl.program_id(1) == 0)
    def _():
        xn_ref[...] = _rms(x_ref[...], g_ref[...]).astype(BF16)

    acc = _dot(xn_ref[...], w_ref[...])
    if has_res:
        acc = acc + r_ref[...]
    o_ref[...] = acc.astype(o_ref.dtype)


def norm_matmul(x, g, w, layer, *, out_dtype=F32, tm, tn):
    m, k = x.shape
    n = w.shape[2]
    return pl.pallas_call(
        functools.partial(_norm_matmul_kernel, has_res=False),
        grid=(m // tm, n // tn),
        in_specs=[pl.BlockSpec((tm, k), lambda i, j: (i, 0)),
                  _layer_spec((1, k), layer),
                  _weight_cols_spec(k, n, tn, layer)],
        out_specs=pl.BlockSpec((tm, tn), lambda i, j: (i, j)),
        out_shape=jax.ShapeDtypeStruct((m, n), out_dtype),
        scratch_shapes=[pltpu.VMEM((tm, k), BF16)],
        compiler_params=_cparams(("parallel", "arbitrary")),
        name="norm_matmul",
    )(x, g, w)


def _in_proj_kernel(x_ref, g_ref, wa_ref, wb_ref, oa_ref, ob_ref):
    xn = _rms(x_ref[...], g_ref[...]).astype(BF16)
    oa_ref[...] = _dot(xn, wa_ref[...])
    ob_ref[...] = _dot(xn, wb_ref[...])


def in_proj(x, g, wa, wb, layer, *, tm):
    m, k = x.shape
    resident = lambda n: pl.BlockSpec((None, k, n), lambda i: (layer, 0, 0), pipeline_mode=pl.Buffered(1))
    return pl.pallas_call(
        _in_proj_kernel,
        grid=(m // tm,),
        in_specs=[pl.BlockSpec((tm, k), lambda i: (i, 0)), _layer_spec((1, k), layer),
                  resident(A_WIDTH), resident(B_WIDTH)],
        out_specs=(pl.BlockSpec((tm, A_WIDTH), lambda i: (i, 0)), pl.BlockSpec((tm, B_WIDTH), lambda i: (i, 0))),
        out_shape=(jax.ShapeDtypeStruct((m, A_WIDTH), F32), jax.ShapeDtypeStruct((m, B_WIDTH), F32)),
        compiler_params=_cparams(("parallel",)),
        name="in_proj",
    )(x, g, wa, wb)


def _res_matmul_kernel(x_ref, w_ref, r_ref, o_ref):
    o_ref[...] = r_ref[...] + _dot(x_ref[...], w_ref[...])


def res_matmul(x, w, res, layer, *, tm, tn):
    m, k = x.shape
    n = w.shape[2]
    return pl.pallas_call(
        _res_matmul_kernel,
        grid=(m // tm, n // tn),
        in_specs=[pl.BlockSpec((tm, k), lambda i, j: (i, 0)),
                  _weight_cols_spec(k, n, tn, layer),
                  pl.BlockSpec((tm, tn), lambda i, j: (i, j))],
        out_specs=pl.BlockSpec((tm, tn), lambda i, j: (i, j)),
        out_shape=jax.ShapeDtypeStruct((m, n), F32),
        compiler_params=_cparams(("parallel", "arbitrary")),
        name="res_matmul",
    )(x, w, res)


def _out_proj_kernel(m0_ref, m1_ref, m2_ref, m3_ref, w_ref, r_ref, o_ref):
    x = jnp.concatenate([m0_ref[...], m1_ref[...], m2_ref[...], m3_ref[...]], axis=-1)
    o_ref[...] = r_ref[...] + _dot(x, w_ref[...])


def out_proj(mixes, w, res, layer, *, tm):
    m = res.shape[0]
    mix_spec = pl.BlockSpec((tm, GROUP_WIDTH), lambda i, j: (i, 0))
    row_spec = pl.BlockSpec((tm, D_MODEL), lambda i, j: (i, 0))
    return pl.pallas_call(
        _out_proj_kernel,
        grid=(m // tm, 1),
        in_specs=[mix_spec] * 4 + [_weight_cols_spec(D_MODEL, D_MODEL, D_MODEL, layer), row_spec],
        out_specs=row_spec,
        out_shape=jax.ShapeDtypeStruct((m, D_MODEL), F32),
        compiler_params=_cparams(("parallel", "arbitrary")),
        name="out_proj",
    )(*mixes, w, res)


def _mlp_kernel(x_ref, g_ref, wu_ref, wd_ref, o_ref, xn_ref):
    j = pl.program_id(1)

    @pl.when(j == 0)
    def _():
        x = x_ref[...]
        xn_ref[...] = _rms(x, g_ref[...]).astype(BF16)
        o_ref[...] = x

    a = jnp.square(jnp.maximum(_dot(xn_ref[...], wu_ref[...].astype(BF16)), 0.0))
    o_ref[...] += _dot(a.astype(BF16), wd_ref[...].astype(BF16))


def mlp(x, g, w_up, w_down, layer, *, tm, tf):
    m = x.shape[0]
    return pl.pallas_call(
        _mlp_kernel,
        grid=(m // tm, D_FF // tf),
        in_specs=[pl.BlockSpec((tm, D_MODEL), lambda i, j: (i, 0)),
                  _layer_spec((1, D_MODEL), layer),
                  pl.BlockSpec((None, D_MODEL, tf), lambda i, j: (layer, 0, j)),
                  pl.BlockSpec((None, tf, D_MODEL), lambda i, j: (layer, j, 0))],
        out_specs=pl.BlockSpec((tm, D_MODEL), lambda i, j: (i, 0)),
        out_shape=jax.ShapeDtypeStruct((m, D_MODEL), F32),
        scratch_shapes=[pltpu.VMEM((tm, D_MODEL), BF16)],
        compiler_params=_cparams(("parallel", "arbitrary")),
        name="mlp",
    )(x, g, w_up, w_down)


def _final_norm_kernel(x_ref, g_ref, o_ref):
    o_ref[...] = _rms(x_ref[...], g_ref[...])


def final_norm(x, g, *, rows, first_block, tm):
    return pl.pallas_call(
        _final_norm_kernel,
        grid=(rows // tm,),
        in_specs=[pl.BlockSpec((tm, D_MODEL), lambda i: (first_block + i, 0)), _whole_spec((1, D_MODEL))],
        out_specs=pl.BlockSpec((tm, D_MODEL), lambda i: (i, 0)),
        out_shape=jax.ShapeDtypeStruct((rows, D_MODEL), F32),
        compiler_params=_cparams(("parallel",)),
        name="final_norm",
    )(x, g.reshape(1, D_MODEL))


def _prompt_attn_kernel(q_ref, k_ref, v_ref, o_ref):
    _prompt_step(lambda: _prompt_attn_body(q_ref, k_ref, v_ref, o_ref), (o_ref,))


def _prompt_attn_body(q_ref, k_ref, v_ref, o_ref):
    scale = XA_HEAD_DIM ** -0.5
    for h in range(XA_HEADS):
        cs = slice(h * XA_HEAD_DIM, (h + 1) * XA_HEAD_DIM)
        kh = k_ref[0, :, cs].astype(BF16)
        vh = v_ref[0, :, cs].astype(BF16)
        s = lax.dot_general(q_ref[:, cs], kh, (((1,), (1,)), ((), ())), preferred_element_type=F32) * scale
        p = jnp.exp(s - jnp.max(s, axis=-1, keepdims=True))
        p = p / jnp.sum(p, axis=-1, keepdims=True)
        o_ref[:, cs] = _dot(p.astype(BF16), vh).astype(o_ref.dtype)


def prompt_attn(q, k, v, *, tq):
    nq = SEQ // tq
    return pl.pallas_call(
        _prompt_attn_kernel,
        grid=(BATCH + 1, nq),
        in_specs=[pl.BlockSpec((tq, D_MODEL), lambda b, i: (_prompt_row_block(b, i, nq, tq), 0)),
                  pl.BlockSpec((1, N_MEM, D_MODEL), lambda b, i: (_prompt_batch(b), 0, 0)),
                  pl.BlockSpec((1, N_MEM, D_MODEL), lambda b, i: (_prompt_batch(b), 0, 0))],
        out_specs=pl.BlockSpec((tq, D_MODEL), lambda b, i: (_prompt_row_block(b, i, nq, tq), 0)),
        out_shape=jax.ShapeDtypeStruct((ALL_ROWS, D_MODEL), BF16),
        compiler_params=_cparams(("arbitrary", "arbitrary")),
        name="prompt_attn",
    )(q, k, v)


def _sample_attn_kernel(q_ref, k_ref, v_ref, o_ref):
    scale = XA_HEAD_DIM ** -0.5
    half, quarter = XA_HEAD_DIM // 2, XA_HEAD_DIM // 4
    for r in range(SAMPLE_ATTN_ROWS):
        prod = k_ref[r] * (q_ref[r] * scale)
        fold = prod[:, :, :half] + prod[:, :, half:]
        fold = fold[:, :, :quarter] + fold[:, :, quarter:]
        s = jnp.sum(fold, axis=-1, keepdims=True)
        e = jnp.exp(s - jnp.max(s, axis=0, keepdims=True))
        o = jnp.sum(e * v_ref[r], axis=0) / jnp.sum(e, axis=0)
        o_ref[r] = o.astype(o_ref.dtype)


def sample_attn(q, cache_k, cache_v, layer):
    nr = SAMPLE_ATTN_ROWS
    row = pl.BlockSpec((nr, XA_HEADS, XA_HEAD_DIM), lambda b: (b, 0, 0))
    mem = pl.BlockSpec((None, nr, N_MEM, XA_HEADS, XA_HEAD_DIM), lambda b: (layer, b, 0, 0, 0))
    return pl.pallas_call(
        _sample_attn_kernel,
        grid=(DEC_BATCH // nr,),
        in_specs=[row, mem, mem],
        out_specs=row,
        out_shape=jax.ShapeDtypeStruct((DEC_BATCH, XA_HEADS, XA_HEAD_DIM), BF16),
        compiler_params=_cparams(("parallel",)),
        name="sample_attn",
    )(q, cache_k, cache_v)


def _s5_abar(lr, li, ldt):
    delta = jnp.exp(ldt)
    mag = jnp.exp(lr * delta)
    return mag * jnp.cos(li * delta), mag * jnp.sin(li * delta)


def _s5_bbar_kernel(lr_ref, li_ref, ldt_ref, bre_ref, bim_ref, bbr_ref, bbi_ref):
    lr, li = lr_ref[...], li_ref[...]
    ar, ai = _s5_abar(lr, li, ldt_ref[...])
    den = lr * lr + li * li
    cr = ((ar - 1.0) * lr + ai * li) / den
    ci = (ai * lr - (ar - 1.0) * li) / den
    br, bi = bre_ref[...], bim_ref[...]
    bbr_ref[...] = cr * br - ci * bi
    bbi_ref[...] = cr * bi + ci * br


def _s5_pow_kernel(lr_ref, li_ref, ldt_ref, pr_ref, pi_ref):
    ar, ai = _s5_abar(lr_ref[...], li_ref[...], ldt_ref[...])
    qr, qi = ar, ai
    pr_ref[0] = qr
    pi_ref[0] = qi
    for e in range(1, S5_SEG):
        qr, qi = qr * ar - qi * ai, qr * ai + qi * ar
        pr_ref[e] = qr
        pi_ref[e] = qi


def s5_prepare(lam_re, lam_im, log_dt, b_re, b_im, c_re, c_im):
    dg = DEPTH * S5_GROUPS
    ldt = jnp.broadcast_to(log_dt[..., None], (DEPTH, S5_GROUPS, S5_STATE))
    rep = lambda a: jnp.repeat(a.reshape(dg, S5_STATE), S5_CH, axis=0)
    to_rows = lambda b: jnp.transpose(b, (0, 1, 3, 2)).reshape(dg * S5_CH, S5_STATE)
    shp = jax.ShapeDtypeStruct((dg * S5_CH, S5_STATE), F32)
    bbr, bbi = pl.pallas_call(_s5_bbar_kernel, out_shape=(shp, shp), name="s5_bbar")(
        rep(lam_re), rep(lam_im), rep(ldt), to_rows(b_re), to_rows(b_im))
    pshp = jax.ShapeDtypeStruct((S5_SEG, dg, S5_STATE), F32)
    pr, pi = pl.pallas_call(_s5_pow_kernel, out_shape=(pshp, pshp), name="s5_pow")(
        lam_re.reshape(dg, S5_STATE), lam_im.reshape(dg, S5_STATE), ldt.reshape(dg, S5_STATE))

    gpb = LANE // S5_CH
    nblk = S5_GROUPS // gpb
    eye = jnp.eye(gpb, dtype=F32)

    def b_blocks(bb):
        bb = bb.reshape(DEPTH, nblk, gpb, S5_CH, S5_STATE)
        return jnp.einsum("digkn,gh->digkhn", bb, eye).reshape(DEPTH, nblk, LANE, gpb * S5_STATE).astype(BF16)

    def c_blocks(cc):
        cc = cc.reshape(DEPTH, nblk, gpb, S5_CH, S5_STATE)
        return jnp.einsum("digkn,gh->dignhk", cc, eye).reshape(DEPTH, nblk, gpb * S5_STATE, LANE).astype(BF16)

    def pow_rows(p):
        return jnp.transpose(p.reshape(S5_SEG, DEPTH, S5_LANES), (1, 0, 2))

    pw_r, pw_i = pow_rows(pr), pow_rows(pi)
    tile = lambda p: jnp.broadcast_to(p[:, :, None, :], (DEPTH, S5_SEG, SUBLANE, S5_LANES))
    return dict(b_re=b_blocks(bbr), b_im=b_blocks(bbi), c_re=c_blocks(c_re), c_imn=c_blocks(-c_im),
                pw_re=pw_r, pw_im=pw_i, pwt_re=tile(pw_r), pwt_im=tile(pw_i))


def _s5_bu(ub, bre_ref, bim_ref):
    res_r, res_i = [], []
    for i in range(GROUP_WIDTH // LANE):
        ui = ub[:, i * LANE:(i + 1) * LANE]
        res_r.append(_dot(ui, bre_ref[i]))
        res_i.append(_dot(ui, bim_ref[i]))
    return res_r, res_i


def _s5_tail(u, hs_re, hs_im, cre_ref, cimn_ref, d_ref, wglu_ref, bglu_ref, g_ref):
    ys = []
    for i in range(GROUP_WIDTH // LANE):
        ys.append(_dot(hs_re(i).astype(BF16), cre_ref[i]) + _dot(hs_im(i).astype(BF16), cimn_ref[i]))
    y = jnp.concatenate(ys, axis=-1) + d_ref[...] * u
    y = _gelu_tanh(y)
    y = y * _sigmoid(_dot(y.astype(BF16), wglu_ref[...]) + bglu_ref[...])
    return _rms(y, g_ref[...])


def _s5_param_specs(layer):
    nblk = GROUP_WIDTH // LANE
    sblk = S5_LANES // nblk
    return [_layer_spec((nblk, LANE, sblk), layer),
            _layer_spec((nblk, LANE, sblk), layer),
            _layer_spec((nblk, sblk, LANE), layer),
            _layer_spec((nblk, sblk, LANE), layer),
            _layer_spec((1, GROUP_WIDTH), layer),
            _layer_spec((GROUP_WIDTH, GROUP_WIDTH), layer),
            _layer_spec((1, GROUP_WIDTH), layer),
            _layer_spec((1, GROUP_WIDTH), layer)]


def _s5_param_args(sp):
    return (sp["s5_b_re"], sp["s5_b_im"], sp["s5_c_re"], sp["s5_c_imn"], sp["s5_d"], sp["s5_w_glu"],
            sp["s5_b_glu"], sp["g_s5"])


def _cmul_add(x_r, x_i, a_r, a_i, h_r, h_i):
    return x_r + a_r * h_r - a_i * h_i, x_i + a_r * h_i + a_i * h_r


def _prompt_s5_kernel(*refs):
    _prompt_step(lambda: _prompt_s5_body(*refs), (refs[15],))


def _prompt_s5_body(u_ref, perm_ref, unperm_ref, pw_re_ref, pw_im_ref, pwt_re_ref, pwt_im_ref,
                    bre_ref, bim_ref, cre_ref, cimn_ref, d_ref, wglu_ref, bglu_ref, g_ref,
                    mix_ref, sre_ref, sim_ref, hre_ref, him_ref, cr_ref, ci_ref):
    sblk = S5_LANES // (GROUP_WIDTH // LANE)
    W = S5_SCAN_LANES

    @pl.when(pl.program_id(1) == 0)
    def _():
        cr_ref[...] = jnp.zeros_like(cr_ref)
        ci_ref[...] = jnp.zeros_like(ci_ref)

    u_hi, u_lo = _split_bf16(u_ref[...])
    ub = _dot(perm_ref[...], u_hi)
    u = ub + _dot(perm_ref[...], u_lo)
    bu_r, bu_i = _s5_bu(ub.astype(BF16), bre_ref, bim_ref)
    for i in range(len(bu_r)):
        hre_ref[:, i * sblk:(i + 1) * sblk] = bu_r[i]
        him_ref[:, i * sblk:(i + 1) * sblk] = bu_i[i]

    row = lax.broadcasted_iota(jnp.int32, (SUBLANE, W), 0)
    for lb in range(0, S5_LANES, W):
        ls = slice(lb, lb + W)
        a_r = jnp.broadcast_to(pw_re_ref[0:1, ls], (SUBLANE, W))
        a_i = jnp.broadcast_to(pw_im_ref[0:1, ls], (SUBLANE, W))

        def local_step(j, h, ls=ls, a_r=a_r, a_i=a_i):
            rs = pl.ds(pl.multiple_of(j * SUBLANE, SUBLANE), SUBLANE)
            n_r, n_i = _cmul_add(hre_ref[rs, ls], him_ref[rs, ls], a_r, a_i, h[0], h[1])
            hre_ref[rs, ls] = n_r
            him_ref[rs, ls] = n_i
            return n_r, n_i

        zero = jnp.zeros((SUBLANE, W), F32)
        e_r, e_i = lax.fori_loop(0, S5_SEG, local_step, (zero, zero))

        s_r = pw_re_ref[S5_SEG - 1:S5_SEG, ls]
        s_i = pw_im_ref[S5_SEG - 1:S5_SEG, ls]
        c_r, c_i = cr_ref[:, ls], ci_ref[:, ls]
        in_r, in_i = zero, zero
        for s in range(SUBLANE):
            in_r = jnp.where(row == s, c_r, in_r)
            in_i = jnp.where(row == s, c_i, in_i)
            c_r, c_i = _cmul_add(e_r[s:s + 1, :], e_i[s:s + 1, :], s_r, s_i, c_r, c_i)
        cr_ref[:, ls] = c_r
        ci_ref[:, ls] = c_i

        def fix_step(j, carry, ls=ls, in_r=in_r, in_i=in_i):
            rs = pl.ds(pl.multiple_of(j * SUBLANE, SUBLANE), SUBLANE)
            n_r, n_i = _cmul_add(hre_ref[rs, ls], him_ref[rs, ls], pwt_re_ref[j, :, ls], pwt_im_ref[j, :, ls],
                                 in_r, in_i)
            hre_ref[rs, ls] = n_r
            him_ref[rs, ls] = n_i
            return carry

        lax.fori_loop(0, S5_SEG, fix_step, 0)

    sre_ref[0] = cr_ref[...]
    sim_ref[0] = ci_ref[...]
    y = _s5_tail(u, lambda i: hre_ref[:, i * sblk:(i + 1) * sblk], lambda i: him_ref[:, i * sblk:(i + 1) * sblk],
                 cre_ref, cimn_ref, d_ref, wglu_ref, bglu_ref, g_ref)
    mix_ref[...] = _dot(unperm_ref[...], y.astype(mix_ref.dtype)).astype(mix_ref.dtype)


def prompt_s5(proj, sp, layer):
    nt = SEQ // S5_CHUNK
    t_of_row = (np.arange(S5_CHUNK) % SUBLANE) * S5_SEG + np.arange(S5_CHUNK) // SUBLANE
    perm = jnp.asarray(np.eye(S5_CHUNK, dtype=np.float32)[t_of_row], BF16)
    st = jax.ShapeDtypeStruct((BATCH, 1, S5_LANES), F32)
    st_spec = pl.BlockSpec((1, 1, S5_LANES), lambda b, t: (_prompt_batch(b), 0, 0))
    rb = lambda b, t: _prompt_row_block(b, t, nt, S5_CHUNK)
    return pl.pallas_call(
        _prompt_s5_kernel,
        grid=(BATCH + 1, nt),
        in_specs=[pl.BlockSpec((S5_CHUNK, GROUP_WIDTH), lambda b, t: (rb(b, t), COL_U5 // GROUP_WIDTH)),
                  _whole_spec((S5_CHUNK, S5_CHUNK)), _whole_spec((S5_CHUNK, S5_CHUNK)),
                  _layer_spec((S5_SEG, S5_LANES), layer), _layer_spec((S5_SEG, S5_LANES), layer),
                  _layer_spec((S5_SEG, SUBLANE, S5_LANES), layer), _layer_spec((S5_SEG, SUBLANE, S5_LANES), layer)]
                 + _s5_param_specs(layer),
        out_specs=(pl.BlockSpec((S5_CHUNK, GROUP_WIDTH), lambda b, t: (rb(b, t), 0)), st_spec, st_spec),
        out_shape=(jax.ShapeDtypeStruct((ALL_ROWS, GROUP_WIDTH), BF16), st, st),
        scratch_shapes=[pltpu.VMEM((S5_CHUNK, S5_LANES), F32), pltpu.VMEM((S5_CHUNK, S5_LANES), F32),
                        pltpu.VMEM((1, S5_LANES), F32), pltpu.VMEM((1, S5_LANES), F32)],
        compiler_params=_cparams(("arbitrary", "arbitrary")),
        name="prompt_s5",
    )(proj, perm, perm.T, sp["s5_pw_re"], sp["s5_pw_im"], sp["s5_pwt_re"], sp["s5_pwt_im"], *_s5_param_args(sp))


def _prompt_ssd_kernel(*refs):
    _prompt_step(lambda: _prompt_ssd_body(*refs), (refs[8],))


def _prompt_ssd_body(zx_ref, dt_ref, cw_ref, cb_ref, dtb_ref, alog_ref, dexp_ref, g_ref,
                     mix_ref, cst_ref, hst_ref, xbuf_ref, h_ref):
    L = SSD_CHUNK
    hist = SSD_CONV - 1
    base = SUBLANE

    @pl.when(pl.program_id(1) == 0)
    def _():
        xbuf_ref[0:base, :] = jnp.zeros((base, XBC_WIDTH), F32)
        h_ref[...] = jnp.zeros_like(h_ref)

    @pl.when(pl.program_id(1) > 0)
    def _():
        xbuf_ref[base - hist:base, :] = xbuf_ref[base + L - hist:base + L, :]

    xbc = zx_ref[:, COL_XBC:COL_XBC + XBC_WIDTH]
    xbuf_ref[base:base + L, :] = xbc
    cst_ref[0] = xbuf_ref[base + L - hist:base + L, :]
    conv = cb_ref[...] + cw_ref[hist:hist + 1, :] * xbc
    for k in range(hist):
        conv = conv + cw_ref[k:k + 1, :] * xbuf_ref[base - hist + k:base - hist + k + L, :]
    xc = _silu(conv)
    xs = xc[:, :GROUP_WIDTH]
    ng = SSD_GROUPS * SSD_STATE
    bm = xc[:, GROUP_WIDTH:GROUP_WIDTH + ng].astype(BF16)
    cm = xc[:, GROUP_WIDTH + ng:].astype(BF16)

    dt = _softplus(dt_ref[...] + dtb_ref[...])
    a = -jnp.exp(alog_ref[...])
    ri = lax.broadcasted_iota(jnp.int32, (L, L), 0)
    ci = lax.broadcasted_iota(jnp.int32, (L, L), 1)
    causal = ri >= ci
    acum = jnp.dot(causal.astype(F32), dt * a, preferred_element_type=F32, precision=lax.Precision.HIGHEST)
    acum_t = acum.T
    last = acum[L - 1:L, :]
    to_end = jnp.exp(last - acum)
    e_acum = jnp.exp(acum)
    chunk_decay = jnp.exp(last)

    ys = []
    rep = SSD_HEADS // SSD_GROUPS
    cb = [lax.dot_general(cm[:, g * SSD_STATE:(g + 1) * SSD_STATE], bm[:, g * SSD_STATE:(g + 1) * SSD_STATE],
                          (((1,), (1,)), ((), ())), preferred_element_type=F32) for g in range(SSD_GROUPS)]
    for h in range(SSD_HEADS):
        g = h // rep
        hs = slice(h * SSD_HEAD_DIM, (h + 1) * SSD_HEAD_DIM)
        gs = slice(g * SSD_STATE, (g + 1) * SSD_STATE)
        seg = acum[:, h:h + 1] - acum_t[h:h + 1, :]
        decay = jnp.exp(jnp.where(causal, seg, -jnp.inf))
        xs_h = xs[:, hs]
        xdt = xs_h * dt[:, h:h + 1]
        y = _dot((cb[g] * decay).astype(BF16), xdt.astype(BF16))
        h_prev = h_ref[h]
        y_off = lax.dot_general(cm[:, gs], h_prev.astype(BF16), (((1,), (1,)), ((), ())),
                                preferred_element_type=F32)
        y = y + y_off * e_acum[:, h:h + 1]
        st = lax.dot_general((xdt * to_end[:, h:h + 1]).astype(BF16), bm[:, gs], (((0,), (0,)), ((), ())),
                             preferred_element_type=F32)
        h_ref[h] = h_prev * chunk_decay[:, h:h + 1] + st
        ys.append(y)
    y = (jnp.concatenate(ys, axis=-1) + dexp_ref[...] * xs) * _silu(zx_ref[:, COL_Z:COL_Z + GROUP_WIDTH])
    mix_ref[...] = _rms(y, g_ref[...]).astype(mix_ref.dtype)
    hst_ref[0] = h_ref[...]


def prompt_ssd(proj, sp, layer):
    nc = SEQ // SSD_CHUNK
    rb = lambda b, c: _prompt_row_block(b, c, nc, SSD_CHUNK)
    return pl.pallas_call(
        _prompt_ssd_kernel,
        grid=(BATCH + 1, nc),
        in_specs=[pl.BlockSpec((SSD_CHUNK, ZX_WIDTH), lambda b, c: (rb(b, c), 0)),
                  pl.BlockSpec((SSD_CHUNK, LANE), lambda b, c: (rb(b, c), COL_DT // LANE)),
                  _layer_spec((SSD_CONV, XBC_WIDTH), layer),
                  _layer_spec((1, XBC_WIDTH), layer),
                  _layer_spec((1, LANE), layer),
                  _layer_spec((1, LANE), layer),
                  _layer_spec((1, GROUP_WIDTH), layer),
                  _layer_spec((1, GROUP_WIDTH), layer)],
        out_specs=(pl.BlockSpec((SSD_CHUNK, GROUP_WIDTH), lambda b, c: (rb(b, c), 0)),
                   pl.BlockSpec((1, SSD_CONV - 1, XBC_WIDTH), lambda b, c: (_prompt_batch(b), 0, 0)),
                   pl.BlockSpec((1, SSD_HEADS, SSD_HEAD_DIM, SSD_STATE), lambda b, c: (_prompt_batch(b), 0, 0, 0))),
        out_shape=(jax.ShapeDtypeStruct((ALL_ROWS, GROUP_WIDTH), BF16),
                   jax.ShapeDtypeStruct((BATCH, SSD_CONV - 1, XBC_WIDTH), F32),
                   jax.ShapeDtypeStruct((BATCH, SSD_HEADS, SSD_HEAD_DIM, SSD_STATE), F32)),
        scratch_shapes=[pltpu.VMEM((SUBLANE + SSD_CHUNK, XBC_WIDTH), F32),
                        pltpu.VMEM((SSD_HEADS, SSD_HEAD_DIM, SSD_STATE), F32)],
        compiler_params=_cparams(("arbitrary", "arbitrary")),
        name="prompt_ssd",
    )(proj, proj, sp["ssd_conv_w"], sp["ssd_conv_b"], sp["ssd_dt_bias"], sp["ssd_a_log"],
      sp["ssd_d_exp"], sp["g_ssd"])


def _pool_counts(pos, w):
    return jnp.minimum(w, pos + 1).astype(F32)


def _prompt_convpool_kernel(*refs):
    _prompt_step(lambda: _prompt_convpool_body(*refs), (refs[9], refs[10]))


def _prompt_convpool_body(gb_ref, gc_ref, hv_ref, up_ref, scw_ref, pw_ref, ps_ref, gsc_ref, gpl_ref,
                          msc_ref, mpl_ref, scst_ref, plst_ref, vbuf_ref, pbuf_ref):
    rows = gb_ref.shape[0]
    vb = SUBLANE
    pb = 2 * SUBLANE
    vh = SC_CONV - 1
    t = pl.program_id(1)

    @pl.when(t == 0)
    def _():
        vbuf_ref[0:vb, :] = jnp.zeros((vb, GROUP_WIDTH), F32)
        pbuf_ref[0:pb, :] = jnp.zeros((pb, GROUP_WIDTH), F32)

    @pl.when(t > 0)
    def _():
        vbuf_ref[vb - vh:vb, :] = vbuf_ref[vb + rows - vh:vb + rows, :]
        pbuf_ref[0:pb, :] = pbuf_ref[rows:rows + pb, :]

    v = gc_ref[...] * hv_ref[...]
    vbuf_ref[vb:vb + rows, :] = v
    acc = scw_ref[vh:vh + 1, :] * v
    for k in range(vh):
        acc = acc + scw_ref[k:k + 1, :] * vbuf_ref[vb - vh + k:vb - vh + k + rows, :]
    msc_ref[...] = _rms(gb_ref[...] * acc, gsc_ref[...]).astype(msc_ref.dtype)
    scst_ref[0] = vbuf_ref[vb + rows - vh:vb + rows, :]

    u = up_ref[...]
    pbuf_ref[pb:pb + rows, :] = u
    pos = t * rows + lax.broadcasted_iota(jnp.int32, (rows, 1), 0)
    ys = []
    for gi, w in enumerate(POOL_WINDOWS):
        cs = slice(gi * POOL_GROUP, (gi + 1) * POOL_GROUP)
        s = u[:, cs]
        for j in range(1, w):
            s = s + pbuf_ref[pb - j:pb - j + rows, cs]
        pooled = s / _pool_counts(pos, w) - u[:, cs]
        ys.append(_dot(pooled.astype(BF16), pw_ref[gi]))
    y = jnp.concatenate(ys, axis=-1) * ps_ref[...]
    mpl_ref[...] = _rms(y, gpl_ref[...]).astype(mpl_ref.dtype)
    plst_ref[0] = pbuf_ref[pb + rows - POOL_HIST:pb + rows, :]


def prompt_convpool(proj, sp, layer):
    nt = SEQ // S5_CHUNK
    rb = lambda b, t: _prompt_row_block(b, t, nt, S5_CHUNK)
    col = lambda c: pl.BlockSpec((S5_CHUNK, GROUP_WIDTH), lambda b, t: (rb(b, t), c // GROUP_WIDTH))
    mix_spec = pl.BlockSpec((S5_CHUNK, GROUP_WIDTH), lambda b, t: (rb(b, t), 0))
    mix_shape = jax.ShapeDtypeStruct((ALL_ROWS, GROUP_WIDTH), BF16)
    return pl.pallas_call(
        _prompt_convpool_kernel,
        grid=(BATCH + 1, nt),
        in_specs=[col(COL_GB), col(COL_GC), col(COL_HV), col(COL_UP),
                  _layer_spec((SC_CONV, GROUP_WIDTH), layer),
                  _layer_spec((len(POOL_WINDOWS), POOL_GROUP, POOL_GROUP), layer),
                  _layer_spec((1, GROUP_WIDTH), layer), _layer_spec((1, GROUP_WIDTH), layer),
                  _layer_spec((1, GROUP_WIDTH), layer)],
        out_specs=(mix_spec, mix_spec,
                   pl.BlockSpec((1, SC_CONV - 1, GROUP_WIDTH), lambda b, t: (_prompt_batch(b), 0, 0)),
                   pl.BlockSpec((1, POOL_HIST, GROUP_WIDTH), lambda b, t: (_prompt_batch(b), 0, 0))),
        out_shape=(mix_shape, mix_shape,
                   jax.ShapeDtypeStruct((BATCH, SC_CONV - 1, GROUP_WIDTH), F32),
                   jax.ShapeDtypeStruct((BATCH, POOL_HIST, GROUP_WIDTH), F32)),
        scratch_shapes=[pltpu.VMEM((SUBLANE + S5_CHUNK, GROUP_WIDTH), F32),
                        pltpu.VMEM((2 * SUBLANE + S5_CHUNK, GROUP_WIDTH), F32)],
        compiler_params=_cparams(("arbitrary", "arbitrary")),
        name="prompt_convpool",
    )(proj, proj, proj, proj, sp["sc_conv_w"], sp["pool_w"], sp["pool_scale"], sp["g_sc"], sp["g_pool"])


def _sample_mix_kernel(proj_ref, pb_ref, cprev_ref, s5r_ref, s5i_ref, scprev_ref, plprev_ref,
                       cw_ref, cb_ref, dtb_ref, alog_ref,
                       pwr_ref, pwi_ref, bre_ref, bim_ref, cre_ref, cimn_ref, d5_ref, wglu_ref, bglu_ref, g5_ref,
                       scw_ref, pw_ref, ps_ref, gsc_ref, gpl_ref,
                       m5_ref, msc_ref, mpl_ref, cst_ref, s5ro_ref, s5io_ref, scst_ref, plst_ref,
                       xs_ref, xdt_ref, da_ref, b2_ref, c2_ref):
    nb = proj_ref.shape[0]
    W = GROUP_WIDTH
    xbc = proj_ref[:, COL_XBC:COL_XBC + XBC_WIDTH]
    hist = SSD_CONV - 1
    conv = cb_ref[...] + cw_ref[hist:hist + 1, :] * xbc
    for k in range(hist):
        conv = conv + cw_ref[k:k + 1, :] * cprev_ref[:, k * XBC_WIDTH:(k + 1) * XBC_WIDTH]
    cst_ref[:, 0:(hist - 1) * XBC_WIDTH] = cprev_ref[:, XBC_WIDTH:hist * XBC_WIDTH]
    cst_ref[:, (hist - 1) * XBC_WIDTH:hist * XBC_WIDTH] = xbc
    xc = _silu(conv)
    xs = xc[:, :W]
    bm = xc[:, W:W + LANE]
    cm = xc[:, W + LANE:W + 2 * LANE]
    dt = _softplus(proj_ref[:, COL_DT:COL_DT + LANE] + dtb_ref[...])
    da = jnp.exp(dt * (-jnp.exp(alog_ref[...])))
    lane_w = lax.broadcasted_iota(jnp.int32, (nb, W), 1)
    dt_exp = jnp.zeros((nb, W), F32)
    for h in range(SSD_HEADS):
        dt_exp = jnp.where(lane_w // SSD_HEAD_DIM == h, dt[:, h:h + 1], dt_exp)
        da_ref[:, h * LANE:(h + 1) * LANE] = jnp.broadcast_to(da[:, h:h + 1], (nb, LANE))
    xs_ref[...] = xs
    xdt_ref[...] = xs * dt_exp
    lane = lax.broadcasted_iota(jnp.int32, (nb, LANE), 1)
    low = lane < SSD_STATE
    for src, dst in ((bm, b2_ref), (cm, c2_ref)):
        swapped = pltpu.roll(src, SSD_STATE, 1)
        dst[:, 0:LANE] = jnp.where(low, src, swapped)
        dst[:, LANE:2 * LANE] = jnp.where(low, swapped, src)

    u5 = pb_ref[:, COL_U5:COL_U5 + W]
    bu_r, bu_i = _s5_bu(u5.astype(BF16), bre_ref, bim_ref)
    sblk = S5_LANES // len(bu_r)
    for i in range(len(bu_r)):
        ls = slice(i * sblk, (i + 1) * sblk)
        n_r, n_i = _cmul_add(bu_r[i], bu_i[i], pwr_ref[0:1, ls], pwi_ref[0:1, ls], s5r_ref[:, ls], s5i_ref[:, ls])
        s5ro_ref[:, ls] = n_r
        s5io_ref[:, ls] = n_i
    y5 = _s5_tail(u5, lambda i: s5ro_ref[:, i * sblk:(i + 1) * sblk], lambda i: s5io_ref[:, i * sblk:(i + 1) * sblk],
                  cre_ref, cimn_ref, d5_ref, wglu_ref, bglu_ref, g5_ref)
    m5_ref[...] = y5.astype(m5_ref.dtype)

    v = pb_ref[:, COL_GC:COL_GC + W] * pb_ref[:, COL_HV:COL_HV + W]
    vh = SC_CONV - 1
    acc = scw_ref[vh:vh + 1, :] * v
    for k in range(vh):
        acc = acc + scw_ref[k:k + 1, :] * scprev_ref[:, k * W:(k + 1) * W]
    msc_ref[...] = _rms(pb_ref[:, COL_GB:COL_GB + W] * acc, gsc_ref[...]).astype(msc_ref.dtype)
    scst_ref[:, 0:(vh - 1) * W] = scprev_ref[:, W:vh * W]
    scst_ref[:, (vh - 1) * W:vh * W] = v

    up = pb_ref[:, COL_UP:COL_UP + W]
    ys = []
    for gi, w in enumerate(POOL_WINDOWS):
        cs = slice(gi * POOL_GROUP, (gi + 1) * POOL_GROUP)
        s = up[:, cs]
        for j in range(1, w):
            k = POOL_HIST - j
            s = s + plprev_ref[:, k * W + gi * POOL_GROUP:k * W + (gi + 1) * POOL_GROUP]
        pooled = s / float(min(w, PAST_LEN + 1)) - up[:, cs]
        ys.append(_dot(pooled.astype(BF16), pw_ref[gi]))
    y = jnp.concatenate(ys, axis=-1) * ps_ref[...]
    mpl_ref[...] = _rms(y, gpl_ref[...]).astype(mpl_ref.dtype)
    plst_ref[:, 0:(POOL_HIST - 1) * W] = plprev_ref[:, W:POOL_HIST * W]
    plst_ref[:, (POOL_HIST - 1) * W:POOL_HIST * W] = up


def sample_mix(proj_a, proj_b, states, sp, layer):
    nb = DEC_BATCH
    W = GROUP_WIDTH
    f = lambda n: jax.ShapeDtypeStruct((nb, n), F32)
    b = lambda n: jax.ShapeDtypeStruct((nb, n), BF16)
    out_widths = ((SSD_CONV - 1) * XBC_WIDTH, S5_LANES, S5_LANES, (SC_CONV - 1) * W, POOL_HIST * W,
                  W, W, SSD_HEADS * LANE, SSD_GROUPS * LANE, SSD_GROUPS * LANE)
    out_shape = (b(W), b(W), b(W)) + tuple(f(n) for n in out_widths)
    nblk = GROUP_WIDTH // LANE
    in_specs = ([pl.BlockSpec((nb, A_WIDTH), lambda i: (SAMPLE_BLOCK, 0)),
                 pl.BlockSpec((nb, B_WIDTH), lambda i: (SAMPLE_BLOCK, 0))]
                + [_layer_spec((nb, s.shape[2]), layer) for s in states]
                + [_layer_spec((SSD_CONV, XBC_WIDTH), layer), _layer_spec((1, XBC_WIDTH), layer),
                   _layer_spec((1, LANE), layer), _layer_spec((1, LANE), layer),
                   _layer_spec((S5_SEG, S5_LANES), layer), _layer_spec((S5_SEG, S5_LANES), layer)]
                + _s5_param_specs(layer)
                + [_layer_spec((SC_CONV, W), layer), _layer_spec((len(POOL_WINDOWS), POOL_GROUP, POOL_GROUP), layer),
                   _layer_spec((1, W), layer), _layer_spec((1, W), layer), _layer_spec((1, W), layer)])
    return pl.pallas_call(
        _sample_mix_kernel,
        grid=(1,),
        in_specs=in_specs,
        out_specs=tuple(_whole_spec(s.shape) for s in out_shape),
        out_shape=out_shape,
        compiler_params=_cparams(("arbitrary",)),
        name="sample_mix",
    )(proj_a, proj_b, *states, sp["ssd_conv_w"], sp["ssd_conv_b"], sp["ssd_dt_bias"], sp["ssd_a_log"],
      sp["s5_pw_re"], sp["s5_pw_im"], *_s5_param_args(sp),
      sp["sc_conv_w"], sp["pool_w"], sp["pool_scale"], sp["g_sc"], sp["g_pool"])


def _sample_ssd_kernel(h0_ref, xdt_ref, da_ref, b2_ref, c2_ref, e_ref, r_ref, xs_ref, z_ref, dexp_ref, g_ref,
                       hn_ref, mix_ref, xrep_ref, prod_ref, y_ref):
    hp = pl.program_id(0)
    hpl = 2 * SSD_HEAD_DIM * SSD_STATE
    per_head = SSD_HEAD_DIM * SSD_STATE
    x_hi, x_lo = _split_bf16(xdt_ref[...])
    xrep_ref[...] = _dot(x_hi, e_ref[...]) + _dot(x_lo, e_ref[...])
    b2, c2 = b2_ref[...], c2_ref[...]
    for j in range(hpl // LANE):
        ls = slice(j * LANE, (j + 1) * LANE)
        hl = (j * LANE) // per_head
        hn = da_ref[:, hl * LANE:(hl + 1) * LANE] * h0_ref[:, ls] + xrep_ref[:, ls] * b2
        hn_ref[:, ls] = hn
        prod_ref[:, ls] = hn * c2
    p_hi, p_lo = _split_bf16(prod_ref[...])
    y_ref[hp] = _dot(p_hi, r_ref[...]) + _dot(p_lo, r_ref[...])

    @pl.when(hp == pl.num_programs(0) - 1)
    def _():
        y = jnp.concatenate([y_ref[i] for i in range(SSD_HEADS // 2)], axis=-1)
        y = (y + dexp_ref[...] * xs_ref[...]) * _silu(z_ref[...])
        mix_ref[...] = _rms(y, g_ref[...]).astype(mix_ref.dtype)


def sample_ssd(h0_all, xdt, da, b2, c2, xs, proj, sp, layer):
    nb = DEC_BATCH
    npairs = SSD_HEADS // 2
    hpl = 2 * SSD_HEAD_DIM * SSD_STATE
    expand = jnp.repeat(jnp.eye(LANE, dtype=BF16), SSD_STATE, axis=1)
    return pl.pallas_call(
        _sample_ssd_kernel,
        grid=(npairs,),
        in_specs=[pl.BlockSpec((None, nb, hpl), lambda i: (layer, 0, i)),
                  pl.BlockSpec((nb, LANE), lambda i: (0, i)),
                  pl.BlockSpec((nb, 2 * LANE), lambda i: (0, i)),
                  pl.BlockSpec((nb, LANE), lambda i: (0, i // (npairs // SSD_GROUPS))),
                  pl.BlockSpec((nb, LANE), lambda i: (0, i // (npairs // SSD_GROUPS))),
                  _whole_spec((LANE, hpl)),
                  _whole_spec((hpl, LANE)),
                  _whole_spec((nb, GROUP_WIDTH)),
                  pl.BlockSpec((nb, GROUP_WIDTH), lambda i: (SAMPLE_BLOCK, COL_Z // GROUP_WIDTH)),
                  _layer_spec((1, GROUP_WIDTH), layer),
                  _layer_spec((1, GROUP_WIDTH), layer)],
        out_specs=(pl.BlockSpec((nb, hpl), lambda i: (0, i)),
                   _whole_spec((nb, GROUP_WIDTH))),
        out_shape=(jax.ShapeDtypeStruct((nb, SSD_HEADS * SSD_HEAD_DIM * SSD_STATE), F32),
                   jax.ShapeDtypeStruct((nb, GROUP_WIDTH), BF16)),
        scratch_shapes=[pltpu.VMEM((nb, hpl), F32), pltpu.VMEM((nb, hpl), F32),
                        pltpu.VMEM((npairs, nb, LANE), F32)],
        compiler_params=_cparams(("arbitrary",)),
        name="sample_ssd",
    )(h0_all, xdt, da, b2, c2, expand, expand.T, xs, proj, sp["ssd_d_exp"], sp["g_ssd"])


def _split_w_in(w_in):
    head = COL_DT + SSD_HEADS
    part_a = jnp.pad(w_in[..., :head], ((0, 0), (0, 0), (0, A_WIDTH - head)))
    return part_a.astype(BF16), w_in[..., head:].astype(BF16)


def _stacked_params(w):
    row = lambda v: v.reshape(DEPTH, 1, -1)
    pad_heads = lambda v: row(jnp.pad(v, ((0, 0), (0, LANE - SSD_HEADS))))
    g_mix = w["mix_out_g"].reshape(DEPTH, 4, 1, GROUP_WIDTH)
    w_in_a, w_in_b = _split_w_in(w["w_in"])
    sp = dict(w_in_a=w_in_a, w_in_b=w_in_b)
    for k in ("w_out", "w_q", "w_k", "w_v", "w_o", "s5_w_glu", "pool_w"):
        sp[k] = w[k].astype(BF16)
    sp.update(w_up=w["w_up"], w_down=w["w_down"])
    for k in ("norm_mix_g", "norm_xa_g", "norm_mem_g", "norm_mlp_g", "ssd_conv_b", "s5_d", "s5_b_glu", "pool_scale"):
        sp[k] = row(w[k])
    sp.update(ssd_conv_w=w["ssd_conv_w"], sc_conv_w=w["sc_conv_w"],
              ssd_dt_bias=pad_heads(w["ssd_dt_bias"]), ssd_a_log=pad_heads(w["ssd_a_log"]),
              ssd_d_exp=row(jnp.repeat(w["ssd_d"], SSD_HEAD_DIM, axis=1)),
              g_ssd=g_mix[:, 0], g_s5=g_mix[:, 1], g_sc=g_mix[:, 2], g_pool=g_mix[:, 3])
    s5 = s5_prepare(w["s5_lam_re"], w["s5_lam_im"], w["s5_log_dt"], w["s5_b_re"], w["s5_b_im"],
                    w["s5_c_re"], w["s5_c_im"])
    sp.update({"s5_" + k: v for k, v in s5.items()})
    return sp


def _forward(x_prompt, x_sample, mem_prompt, state_ssd_conv, state_ssd, state_s5_re, state_s5_im,
             state_sconv, state_pool, cache_mem_k, cache_mem_v, final_norm_g, w):
    sp = _stacked_params(w)
    flat = lambda s: s.reshape(DEPTH, DEC_BATCH, -1)
    s_states = tuple(flat(s) for s in (state_ssd_conv, state_s5_re, state_s5_im, state_sconv, state_pool))
    s_ssd0 = flat(state_ssd)
    h = jnp.concatenate([x_prompt.reshape(P_ROWS, D_MODEL), x_sample.reshape(DEC_BATCH, D_MODEL)], axis=0)
    mem = mem_prompt.reshape(BATCH * N_MEM, D_MODEL)
    p_out = [[] for _ in range(8)]
    s_out = [[] for _ in range(6)]
    put_sample = lambda full, rows: lax.dynamic_update_slice(full, rows, (P_ROWS, 0))
    for l in range(DEPTH):
        mk = norm_matmul(mem, sp["norm_mem_g"], sp["w_k"], l, tm=BATCH * N_MEM, tn=COL_TILE)
        mv = norm_matmul(mem, sp["norm_mem_g"], sp["w_v"], l, tm=BATCH * N_MEM, tn=COL_TILE)

        proj_a, proj_b = in_proj(h, sp["norm_mix_g"], sp["w_in_a"], sp["w_in_b"], l, tm=IN_ROW_TILE)
        m_ssd, p_conv, p_ssd = prompt_ssd(proj_a, sp, l)
        m_s5, p_s5r, p_s5i = prompt_s5(proj_b, sp, l)
        m_sc, m_pl, p_sc, p_pl = prompt_convpool(proj_b, sp, l)
        (s_m5, s_msc, s_mpl, s_conv, s_s5r, s_s5i, s_sc, s_pl, xs, xdt, da, b2, c2) = sample_mix(
            proj_a, proj_b, s_states, sp, l)
        s_ssd, s_mssd = sample_ssd(s_ssd0, xdt, da, b2, c2, xs, proj_a, sp, l)
        mixes = (put_sample(m_ssd, s_mssd), put_sample(m_s5, s_m5),
                 put_sample(m_sc, s_msc), put_sample(m_pl, s_mpl))

        h = out_proj(mixes, sp["w_out"], h, l, tm=ROW_TILE)
        q = norm_matmul(h, sp["norm_xa_g"], sp["w_q"], l, out_dtype=BF16, tm=ROW_TILE, tn=D_MODEL)
        o = prompt_attn(q, mk.reshape(BATCH, N_MEM, D_MODEL), mv.reshape(BATCH, N_MEM, D_MODEL), tq=512)
        q_s = q[P_ROWS:].astype(F32).reshape(DEC_BATCH, XA_HEADS, XA_HEAD_DIM)
        o_s = sample_attn(q_s, cache_mem_k, cache_mem_v, l)
        o = put_sample(o, o_s.reshape(DEC_BATCH, D_MODEL))
        h = res_matmul(o, sp["w_o"], h, l, tm=ROW_TILE, tn=D_MODEL)
        h = mlp(h, sp["norm_mlp_g"], sp["w_up"], sp["w_down"], l, tm=MLP_ROW_TILE, tf=FF_TILE)

        for lst, val in zip(p_out, (p_conv, p_ssd, p_s5r, p_s5i, p_sc, p_pl, mk, mv)):
            lst.append(val)
        for lst, val in zip(s_out, (s_conv, s_ssd, s_s5r, s_s5i, s_sc, s_pl)):
            lst.append(val)

    y_prompt = final_norm(h, final_norm_g, rows=P_ROWS, first_block=0, tm=COL_TILE)
    y_sample = final_norm(h, final_norm_g, rows=DEC_BATCH, first_block=SAMPLE_BLOCK, tm=DEC_BATCH)
    p_shapes = ((BATCH, SSD_CONV - 1, XBC_WIDTH), (BATCH, SSD_HEADS, SSD_HEAD_DIM, SSD_STATE),
                (BATCH, S5_GROUPS, S5_STATE), (BATCH, S5_GROUPS, S5_STATE),
                (BATCH, SC_CONV - 1, GROUP_WIDTH), (BATCH, POOL_HIST, GROUP_WIDTH),
                (BATCH, N_MEM, XA_HEADS, XA_HEAD_DIM), (BATCH, N_MEM, XA_HEADS, XA_HEAD_DIM))
    s_shapes = tuple((DEC_BATCH,) + s[1:] for s in p_shapes[:6])
    stack = lambda vals, shape: jnp.stack(vals).reshape((DEPTH,) + shape)
    return ((y_prompt.reshape(BATCH, SEQ, D_MODEL), y_sample.reshape(DEC_BATCH, 1, D_MODEL))
            + tuple(stack(v, s) for v, s in zip(p_out, p_shapes))
            + tuple(stack(v, s) for v, s in zip(s_out, s_shapes)))


_forward_jit = jax.jit(_forward)


def kernel(x_prompt, x_sample, mem_prompt, state_ssd_conv, state_ssd, state_s5_re, state_s5_im, state_sconv, state_pool, cache_mem_k, cache_mem_v, norm_mix_g, w_in, ssd_conv_w, ssd_conv_b, ssd_dt_bias, ssd_a_log, ssd_d, s5_lam_re, s5_lam_im, s5_log_dt, s5_b_re, s5_b_im, s5_c_re, s5_c_im, s5_d, s5_w_glu, s5_b_glu, sc_conv_w, pool_w, pool_scale, mix_out_g, w_out, norm_xa_g, norm_mem_g, w_q, w_k, w_v, w_o, norm_mlp_g, w_up, w_down, final_norm_g):
    w = dict(norm_mix_g=norm_mix_g, w_in=w_in, ssd_conv_w=ssd_conv_w, ssd_conv_b=ssd_conv_b,
             ssd_dt_bias=ssd_dt_bias, ssd_a_log=ssd_a_log, ssd_d=ssd_d, s5_lam_re=s5_lam_re,
             s5_lam_im=s5_lam_im, s5_log_dt=s5_log_dt, s5_b_re=s5_b_re, s5_b_im=s5_b_im,
             s5_c_re=s5_c_re, s5_c_im=s5_c_im, s5_d=s5_d, s5_w_glu=s5_w_glu, s5_b_glu=s5_b_glu,
             sc_conv_w=sc_conv_w, pool_w=pool_w, pool_scale=pool_scale, mix_out_g=mix_out_g,
             w_out=w_out, norm_xa_g=norm_xa_g, norm_mem_g=norm_mem_g, w_q=w_q, w_k=w_k, w_v=w_v,
             w_o=w_o, norm_mlp_g=norm_mlp_g, w_up=w_up, w_down=w_down)
    return _forward_jit(x_prompt, x_sample, mem_prompt, state_ssd_conv, state_ssd, state_s5_re, state_s5_im,
                        state_sconv, state_pool, cache_mem_k, cache_mem_v, final_norm_g, w)
```

```python
import functools
import math

import numpy as np
import jax
import jax.numpy as jnp
from jax import lax
from jax.experimental import pallas as pl
from jax.experimental.pallas import tpu as pltpu

F32 = jnp.float32
BF16 = jnp.bfloat16

D_MODEL = 2048
BATCH = 4
SEQ = 2048
DEPTH = 2
DEC_BATCH = 128
PAST_LEN = 16384
GROUP_WIDTH = D_MODEL // 4
SSD_HEAD_DIM = 64
SSD_HEADS = GROUP_WIDTH // SSD_HEAD_DIM
SSD_GROUPS = 2
SSD_STATE = 64
SSD_CONV = 4
SSD_CHUNK = 128
XBC_WIDTH = GROUP_WIDTH + 2 * SSD_GROUPS * SSD_STATE
S5_CH = 16
S5_GROUPS = GROUP_WIDTH // S5_CH
S5_STATE = 64
S5_LANES = S5_GROUPS * S5_STATE
SC_CONV = 3
POOL_WINDOWS = (2, 4, 8, 16)
POOL_GROUP = GROUP_WIDTH // len(POOL_WINDOWS)
POOL_HIST = max(POOL_WINDOWS) - 1
N_MEM = 256
XA_HEADS = 4
XA_HEAD_DIM = D_MODEL // XA_HEADS
D_FF = 4 * D_MODEL
EPS = 1e-6
P_ROWS = BATCH * SEQ
ALL_ROWS = P_ROWS + DEC_BATCH

LANE = 128
SUBLANE = 8
VMEM_LIMIT = 56 * 1024 * 1024

COL_Z = 0
COL_XBC = 512
COL_DT = 1280
A_WIDTH = 1536
ZX_WIDTH = COL_DT
COL_U5 = 0
COL_GB = 512
COL_GC = 1024
COL_HV = 1536
COL_UP = 2048
B_WIDTH = 2560
IN_ROW_TILE = 640

ROW_TILE = ALL_ROWS // 8
COL_TILE = 1024
MLP_ROW_TILE = ALL_ROWS // 10
FF_TILE = 512
SAMPLE_BLOCK = P_ROWS // DEC_BATCH

SAMPLE_ATTN_ROWS = 2
S5_CHUNK = 256
S5_SEG = S5_CHUNK // SUBLANE
S5_SCAN_LANES = 512


def _cparams(sem):
    return pltpu.CompilerParams(dimension_semantics=sem, vmem_limit_bytes=VMEM_LIMIT)


def _layer_spec(shape, layer):
    zeros = (0,) * len(shape)
    return pl.BlockSpec((None,) + tuple(shape), lambda *_: (layer,) + zeros)


def _whole_spec(shape):
    zeros = (0,) * len(shape)
    return pl.BlockSpec(tuple(shape), lambda *_: zeros)


def _weight_cols_spec(k, n, tn, layer):
    mode = pl.Buffered(1) if tn == n else None
    return pl.BlockSpec((None, k, tn), lambda i, j: (layer, 0, j), pipeline_mode=mode)


def _prompt_row_block(b, t, steps, rows):
    return jnp.minimum(b * steps + t, P_ROWS // rows)


def _prompt_batch(b):
    return jnp.minimum(b, BATCH - 1)


def _prompt_step(body, row_outputs):
    b = pl.program_id(0)
    pl.when(b < BATCH)(body)

    @pl.when(b == BATCH)
    def _():
        for ref in row_outputs:
            ref[...] = jnp.zeros_like(ref)


def _sigmoid(x):
    return 1.0 / (1.0 + jnp.exp(-x))


def _silu(x):
    return x * _sigmoid(x)


def _softplus(x):
    return jnp.maximum(x, 0.0) + jnp.log(1.0 + jnp.exp(-jnp.abs(x)))


def _gelu_tanh(x):
    return 0.5 * x * (1.0 + jnp.tanh(math.sqrt(2.0 / math.pi) * (x + 0.044715 * (x * x * x))))


def _rms(x, g):
    return x * lax.rsqrt(jnp.mean(x * x, axis=-1, keepdims=True) + EPS) * g


def _dot(a, b):
    return jnp.dot(a, b, preferred_element_type=F32)


def _split_bf16(x):
    hi = x.astype(BF16)
    lo = (x - hi.astype(F32)).astype(BF16)
    return hi, lo


def _norm_matmul_kernel(*refs, has_res):
    if has_res:
        x_ref, g_ref, w_ref, r_ref, o_ref, xn_ref = refs
    else:
        x_ref, g_ref, w_ref, o_ref, xn_ref = refs
        r_ref = None

    @pl.when(pl.program_id(1) == 0)
    def _():
        xn_ref[...] = _rms(x_ref[...], g_ref[...]).astype(BF16)

    acc = _dot(xn_ref[...], w_ref[...])
    if has_res:
        acc = acc + r_ref[...]
    o_ref[...] = acc.astype(o_ref.dtype)


def norm_matmul(x, g, w, layer, *, out_dtype=F32, tm, tn):
    m, k = x.shape
    n = w.shape[2]
    return pl.pallas_call(
        functools.partial(_norm_matmul_kernel, has_res=False),
        grid=(m // tm, n // tn),
        in_specs=[pl.BlockSpec((tm, k), lambda i, j: (i, 0)),
                  _layer_spec((1, k), layer),
                  _weight_cols_spec(k, n, tn, layer)],
        out_specs=pl.BlockSpec((tm, tn), lambda i, j: (i, j)),
        out_shape=jax.ShapeDtypeStruct((m, n), out_dtype),
        scratch_shapes=[pltpu.VMEM((tm, k), BF16)],
        compiler_params=_cparams(("parallel", "arbitrary")),
        name="norm_matmul",
    )(x, g, w)


def _in_proj_kernel(x_ref, g_ref, wa_ref, wb_ref, oa_ref, ob_ref):
    xn = _rms(x_ref[...], g_ref[...]).astype(BF16)
    oa_ref[...] = _dot(xn, wa_ref[...])
    ob_ref[...] = _dot(xn, wb_ref[...])


def in_proj(x, g, wa, wb, layer, *, tm):
    m, k = x.shape
    resident = lambda n: pl.BlockSpec((None, k, n), lambda i: (layer, 0, 0), pipeline_mode=pl.Buffered(1))
    return pl.pallas_call(
        _in_proj_kernel,
        grid=(m // tm,),
        in_specs=[pl.BlockSpec((tm, k), lambda i: (i, 0)), _layer_spec((1, k), layer),
                  resident(A_WIDTH), resident(B_WIDTH)],
        out_specs=(pl.BlockSpec((tm, A_WIDTH), lambda i: (i, 0)), pl.BlockSpec((tm, B_WIDTH), lambda i: (i, 0))),
        out_shape=(jax.ShapeDtypeStruct((m, A_WIDTH), F32), jax.ShapeDtypeStruct((m, B_WIDTH), F32)),
        compiler_params=_cparams(("parallel",)),
        name="in_proj",
    )(x, g, wa, wb)


def _res_matmul_kernel(x_ref, w_ref, r_ref, o_ref):
    o_ref[...] = r_ref[...] + _dot(x_ref[...], w_ref[...])


def res_matmul(x, w, res, layer, *, tm, tn):
    m, k = x.shape
    n = w.shape[2]
    return pl.pallas_call(
        _res_matmul_kernel,
        grid=(m // tm, n // tn),
        in_specs=[pl.BlockSpec((tm, k), lambda i, j: (i, 0)),
                  _weight_cols_spec(k, n, tn, layer),
                  pl.BlockSpec((tm, tn), lambda i, j: (i, j))],
        out_specs=pl.BlockSpec((tm, tn), lambda i, j: (i, j)),
        out_shape=jax.ShapeDtypeStruct((m, n), F32),
        compiler_params=_cparams(("parallel", "arbitrary")),
        name="res_matmul",
    )(x, w, res)


def _out_proj_kernel(m0_ref, m1_ref, m2_ref, m3_ref, w_ref, r_ref, o_ref):
    x = jnp.concatenate([m0_ref[...], m1_ref[...], m2_ref[...], m3_ref[...]], axis=-1)
    o_ref[...] = r_ref[...] + _dot(x, w_ref[...])


def out_proj(mixes, w, res, layer, *, tm):
    m = res.shape[0]
    mix_spec = pl.BlockSpec((tm, GROUP_WIDTH), lambda i, j: (i, 0))
    row_spec = pl.BlockSpec((tm, D_MODEL), lambda i, j: (i, 0))
    return pl.pallas_call(
        _out_proj_kernel,
        grid=(m // tm, 1),
        in_specs=[mix_spec] * 4 + [_weight_cols_spec(D_MODEL, D_MODEL, D_MODEL, layer), row_spec],
        out_specs=row_spec,
        out_shape=jax.ShapeDtypeStruct((m, D_MODEL), F32),
        compiler_params=_cparams(("parallel", "arbitrary")),
        name="out_proj",
    )(*mixes, w, res)


def _mlp_kernel(x_ref, g_ref, wu_ref, wd_ref, o_ref, xn_ref):
    j = pl.program_id(1)

    @pl.when(j == 0)
    def _():
        x = x_ref[...]
        xn_ref[...] = _rms(x, g_ref[...]).astype(BF16)
        o_ref[...] = x

    a = jnp.square(jnp.maximum(_dot(xn_ref[...], wu_ref[...].astype(BF16)), 0.0))
    o_ref[...] += _dot(a.astype(BF16), wd_ref[...].astype(BF16))


def mlp(x, g, w_up, w_down, layer, *, tm, tf):
    m = x.shape[0]
    return pl.pallas_call(
        _mlp_kernel,
        grid=(m // tm, D_FF // tf),
        in_specs=[pl.BlockSpec((tm, D_MODEL), lambda i, j: (i, 0)),
                  _layer_spec((1, D_MODEL), layer),
                  pl.BlockSpec((None, D_MODEL, tf), lambda i, j: (layer, 0, j)),
                  pl.BlockSpec((None, tf, D_MODEL), lambda i, j: (layer, j, 0))],
        out_specs=pl.BlockSpec((tm, D_MODEL), lambda i, j: (i, 0)),
        out_shape=jax.ShapeDtypeStruct((m, D_MODEL), F32),
        scratch_shapes=[pltpu.VMEM((tm, D_MODEL), BF16)],
        compiler_params=_cparams(("parallel", "arbitrary")),
        name="mlp",
    )(x, g, w_up, w_down)


def _final_norm_kernel(x_ref, g_ref, o_ref):
    o_ref[...] = _rms(x_ref[...], g_ref[...])


def final_norm(x, g, *, rows, first_block, tm):
    return pl.pallas_call(
        _final_norm_kernel,
        grid=(rows // tm,),
        in_specs=[pl.BlockSpec((tm, D_MODEL), lambda i: (first_block + i, 0)), _whole_spec((1, D_MODEL))],
        out_specs=pl.BlockSpec((tm, D_MODEL), lambda i: (i, 0)),
        out_shape=jax.ShapeDtypeStruct((rows, D_MODEL), F32),
        compiler_params=_cparams(("parallel",)),
        name="final_norm",
    )(x, g.reshape(1, D_MODEL))


def _prompt_attn_kernel(q_ref, k_ref, v_ref, o_ref):
    _prompt_step(lambda: _prompt_attn_body(q_ref, k_ref, v_ref, o_ref), (o_ref,))


def _prompt_attn_body(q_ref, k_ref, v_ref, o_ref):
    scale = XA_HEAD_DIM ** -0.5
    for h in range(XA_HEADS):
        cs = slice(h * XA_HEAD_DIM, (h + 1) * XA_HEAD_DIM)
        kh = k_ref[0, :, cs].astype(BF16)
        vh = v_ref[0, :, cs].astype(BF16)
        s = lax.dot_general(q_ref[:, cs], kh, (((1,), (1,)), ((), ())), preferred_element_type=F32) * scale
        p = jnp.exp(s - jnp.max(s, axis=-1, keepdims=True))
        p = p / jnp.sum(p, axis=-1, keepdims=True)
        o_ref[:, cs] = _dot(p.astype(BF16), vh).astype(o_ref.dtype)


def prompt_attn(q, k, v, *, tq):
    nq = SEQ // tq
    return pl.pallas_call(
        _prompt_attn_kernel,
        grid=(BATCH + 1, nq),
        in_specs=[pl.BlockSpec((tq, D_MODEL), lambda b, i: (_prompt_row_block(b, i, nq, tq), 0)),
                  pl.BlockSpec((1, N_MEM, D_MODEL), lambda b, i: (_prompt_batch(b), 0, 0)),
                  pl.BlockSpec((1, N_MEM, D_MODEL), lambda b, i: (_prompt_batch(b), 0, 0))],
        out_specs=pl.BlockSpec((tq, D_MODEL), lambda b, i: (_prompt_row_block(b, i, nq, tq), 0)),
        out_shape=jax.ShapeDtypeStruct((ALL_ROWS, D_MODEL), BF16),
        compiler_params=_cparams(("arbitrary", "arbitrary")),
        name="prompt_attn",
    )(q, k, v)


def _sample_attn_kernel(q_ref, k_ref, v_ref, o_ref):
    scale = XA_HEAD_DIM ** -0.5
    half, quarter = XA_HEAD_DIM // 2, XA_HEAD_DIM // 4
    for r in range(SAMPLE_ATTN_ROWS):
        prod = k_ref[r] * (q_ref[r] * scale)
        fold = prod[:, :, :half] + prod[:, :, half:]
        fold = fold[:, :, :quarter] + fold[:, :, quarter:]
        s = jnp.sum(fold, axis=-1, keepdims=True)
        e = jnp.exp(s - jnp.max(s, axis=0, keepdims=True))
        o = jnp.sum(e * v_ref[r], axis=0) / jnp.sum(e, axis=0)
        o_ref[r] = o.astype(o_ref.dtype)


def sample_attn(q, cache_k, cache_v, layer):
    nr = SAMPLE_ATTN_ROWS
    row = pl.BlockSpec((nr, XA_HEADS, XA_HEAD_DIM), lambda b: (b, 0, 0))
    mem = pl.BlockSpec((None, nr, N_MEM, XA_HEADS, XA_HEAD_DIM), lambda b: (layer, b, 0, 0, 0))
    return pl.pallas_call(
        _sample_attn_kernel,
        grid=(DEC_BATCH // nr,),
        in_specs=[row, mem, mem],
        out_specs=row,
        out_shape=jax.ShapeDtypeStruct((DEC_BATCH, XA_HEADS, XA_HEAD_DIM), BF16),
        compiler_params=_cparams(("parallel",)),
        name="sample_attn",
    )(q, cache_k, cache_v)


def _s5_abar(lr, li, ldt):
    delta = jnp.exp(ldt)
    mag = jnp.exp(lr * delta)
    return mag * jnp.cos(li * delta), mag * jnp.sin(li * delta)


def _s5_bbar_kernel(lr_ref, li_ref, ldt_ref, bre_ref, bim_ref, bbr_ref, bbi_ref):
    lr, li = lr_ref[...], li_ref[...]
    ar, ai = _s5_abar(lr, li, ldt_ref[...])
    den = lr * lr + li * li
    cr = ((ar - 1.0) * lr + ai * li) / den
    ci = (ai * lr - (ar - 1.0) * li) / den
    br, bi = bre_ref[...], bim_ref[...]
    bbr_ref[...] = cr * br - ci * bi
    bbi_ref[...] = cr * bi + ci * br


def _s5_pow_kernel(lr_ref, li_ref, ldt_ref, pr_ref, pi_ref):
    ar, ai = _s5_abar(lr_ref[...], li_ref[...], ldt_ref[...])
    qr, qi = ar, ai
    pr_ref[0] = qr
    pi_ref[0] = qi
    for e in range(1, S5_SEG):
        qr, qi = qr * ar - qi * ai, qr * ai + qi * ar
        pr_ref[e] = qr
        pi_ref[e] = qi


def s5_prepare(lam_re, lam_im, log_dt, b_re, b_im, c_re, c_im):
    dg = DEPTH * S5_GROUPS
    ldt = jnp.broadcast_to(log_dt[..., None], (DEPTH, S5_GROUPS, S5_STATE))
    rep = lambda a: jnp.repeat(a.reshape(dg, S5_STATE), S5_CH, axis=0)
    to_rows = lambda b: jnp.transpose(b, (0, 1, 3, 2)).reshape(dg * S5_CH, S5_STATE)
    shp = jax.ShapeDtypeStruct((dg * S5_CH, S5_STATE), F32)
    bbr, bbi = pl.pallas_call(_s5_bbar_kernel, out_shape=(shp, shp), name="s5_bbar")(
        rep(lam_re), rep(lam_im), rep(ldt), to_rows(b_re), to_rows(b_im))
    pshp = jax.ShapeDtypeStruct((S5_SEG, dg, S5_STATE), F32)
    pr, pi = pl.pallas_call(_s5_pow_kernel, out_shape=(pshp, pshp), name="s5_pow")(
        lam_re.reshape(dg, S5_STATE), lam_im.reshape(dg, S5_STATE), ldt.reshape(dg, S5_STATE))

    gpb = LANE // S5_CH
    nblk = S5_GROUPS // gpb
    eye = jnp.eye(gpb, dtype=F32)

    def b_blocks(bb):
        bb = bb.reshape(DEPTH, nblk, gpb, S5_CH, S5_STATE)
        return jnp.einsum("digkn,gh->digkhn", bb, eye).reshape(DEPTH, nblk, LANE, gpb * S5_STATE).astype(BF16)

    def c_blocks(cc):
        cc = cc.reshape(DEPTH, nblk, gpb, S5_CH, S5_STATE)
        return jnp.einsum("digkn,gh->dignhk", cc, eye).reshape(DEPTH, nblk, gpb * S5_STATE, LANE).astype(BF16)

    def pow_rows(p):
        return jnp.transpose(p.reshape(S5_SEG, DEPTH, S5_LANES), (1, 0, 2))

    pw_r, pw_i = pow_rows(pr), pow_rows(pi)
    tile = lambda p: jnp.broadcast_to(p[:, :, None, :], (DEPTH, S5_SEG, SUBLANE, S5_LANES))
    return dict(b_re=b_blocks(bbr), b_im=b_blocks(bbi), c_re=c_blocks(c_re), c_imn=c_blocks(-c_im),
                pw_re=pw_r, pw_im=pw_i, pwt_re=tile(pw_r), pwt_im=tile(pw_i))


def _s5_bu(ub, bre_ref, bim_ref):
    res_r, res_i = [], []
    for i in range(GROUP_WIDTH // LANE):
        ui = ub[:, i * LANE:(i + 1) * LANE]
        res_r.append(_dot(ui, bre_ref[i]))
        res_i.append(_dot(ui, bim_ref[i]))
    return res_r, res_i


def _s5_tail(u, hs_re, hs_im, cre_ref, cimn_ref, d_ref, wglu_ref, bglu_ref, g_ref):
    ys = []
    for i in range(GROUP_WIDTH // LANE):
        ys.append(_dot(hs_re(i).astype(BF16), cre_ref[i]) + _dot(hs_im(i).astype(BF16), cimn_ref[i]))
    y = jnp.concatenate(ys, axis=-1) + d_ref[...] * u
    y = _gelu_tanh(y)
    y = y * _sigmoid(_dot(y.astype(BF16), wglu_ref[...]) + bglu_ref[...])
    return _rms(y, g_ref[...])


def _s5_param_specs(layer):
    nblk = GROUP_WIDTH // LANE
    sblk = S5_LANES // nblk
    return [_layer_spec((nblk, LANE, sblk), layer),
            _layer_spec((nblk, LANE, sblk), layer),
            _layer_spec((nblk, sblk, LANE), layer),
            _layer_spec((nblk, sblk, LANE), layer),
            _layer_spec((1, GROUP_WIDTH), layer),
            _layer_spec((GROUP_WIDTH, GROUP_WIDTH), layer),
            _layer_spec((1, GROUP_WIDTH), layer),
            _layer_spec((1, GROUP_WIDTH), layer)]


def _s5_param_args(sp):
    return (sp["s5_b_re"], sp["s5_b_im"], sp["s5_c_re"], sp["s5_c_imn"], sp["s5_d"], sp["s5_w_glu"],
            sp["s5_b_glu"], sp["g_s5"])


def _cmul_add(x_r, x_i, a_r, a_i, h_r, h_i):
    return x_r + a_r * h_r - a_i * h_i, x_i + a_r * h_i + a_i * h_r


def _prompt_s5_body(u_ref, perm_ref, unperm_ref, pw_re_ref, pw_im_ref, pwt_re_ref, pwt_im_ref,
                    bre_ref, bim_ref, cre_ref, cimn_ref, d_ref, wglu_ref, bglu_ref, g_ref,
                    mix_ref, sre_ref, sim_ref, hre_ref, him_ref, cr_ref, ci_ref):
    sblk = S5_LANES // (GROUP_WIDTH // LANE)
    W = S5_SCAN_LANES

    @pl.when(pl.program_id(1) == 0)
    def _():
        cr_ref[...] = jnp.zeros_like(cr_ref)
        ci_ref[...] = jnp.zeros_like(ci_ref)

    u_hi, u_lo = _split_bf16(u_ref[...])
    ub = _dot(perm_ref[...], u_hi)
    u = ub + _dot(perm_ref[...], u_lo)
    bu_r, bu_i = _s5_bu(ub.astype(BF16), bre_ref, bim_ref)
    for i in range(len(bu_r)):
        hre_ref[:, i * sblk:(i + 1) * sblk] = bu_r[i]
        him_ref[:, i * sblk:(i + 1) * sblk] = bu_i[i]

    row = lax.broadcasted_iota(jnp.int32, (SUBLANE, W), 0)
    for lb in range(0, S5_LANES, W):
        ls = slice(lb, lb + W)
        a_r = jnp.broadcast_to(pw_re_ref[0:1, ls], (SUBLANE, W))
        a_i = jnp.broadcast_to(pw_im_ref[0:1, ls], (SUBLANE, W))

        def local_step(j, h, ls=ls, a_r=a_r, a_i=a_i):
            rs = pl.ds(pl.multiple_of(j * SUBLANE, SUBLANE), SUBLANE)
            n_r, n_i = _cmul_add(hre_ref[rs, ls], him_ref[rs, ls], a_r, a_i, h[0], h[1])
            hre_ref[rs, ls] = n_r
            him_ref[rs, ls] = n_i
            return n_r, n_i

        zero = jnp.zeros((SUBLANE, W), F32)
        e_r, e_i = lax.fori_loop(0, S5_SEG, local_step, (zero, zero))

        s_r = pw_re_ref[S5_SEG - 1:S5_SEG, ls]
        s_i = pw_im_ref[S5_SEG - 1:S5_SEG, ls]
        c_r, c_i = cr_ref[:, ls], ci_ref[:, ls]
        in_r, in_i = zero, zero
        for s in range(SUBLANE):
            in_r = jnp.where(row == s, c_r, in_r)
            in_i = jnp.where(row == s, c_i, in_i)
            c_r, c_i = _cmul_add(e_r[s:s + 1, :], e_i[s:s + 1, :], s_r, s_i, c_r, c_i)
        cr_ref[:, ls] = c_r
        ci_ref[:, ls] = c_i

        def fix_step(j, carry, ls=ls, in_r=in_r, in_i=in_i):
            rs = pl.ds(pl.multiple_of(j * SUBLANE, SUBLANE), SUBLANE)
            n_r, n_i = _cmul_add(hre_ref[rs, ls], him_ref[rs, ls], pwt_re_ref[j, :, ls], pwt_im_ref[j, :, ls],
                                 in_r, in_i)
            hre_ref[rs, ls] = n_r
            him_ref[rs, ls] = n_i
            return carry

        lax.fori_loop(0, S5_SEG, fix_step, 0)

    sre_ref[0] = cr_ref[...]
    sim_ref[0] = ci_ref[...]
    y = _s5_tail(u, lambda i: hre_ref[:, i * sblk:(i + 1) * sblk], lambda i: him_ref[:, i * sblk:(i + 1) * sblk],
                 cre_ref, cimn_ref, d_ref, wglu_ref, bglu_ref, g_ref)
    mix_ref[...] = _dot(unperm_ref[...], y.astype(mix_ref.dtype)).astype(mix_ref.dtype)


def _prompt_ssd_kernel(*refs):
    _prompt_step(lambda: _prompt_ssd_body(*refs), (refs[8],))


def _prompt_ssd_body(zx_ref, dt_ref, cw_ref, cb_ref, dtb_ref, alog_ref, dexp_ref, g_ref,
                     mix_ref, cst_ref, hst_ref, xbuf_ref, h_ref):
    L = SSD_CHUNK
    hist = SSD_CONV - 1
    base = SUBLANE

    @pl.when(pl.program_id(1) == 0)
    def _():
        xbuf_ref[0:base, :] = jnp.zeros((base, XBC_WIDTH), F32)
        h_ref[...] = jnp.zeros_like(h_ref)

    @pl.when(pl.program_id(1) > 0)
    def _():
        xbuf_ref[base - hist:base, :] = xbuf_ref[base + L - hist:base + L, :]

    xbc = zx_ref[:, COL_XBC:COL_XBC + XBC_WIDTH]
    xbuf_ref[base:base + L, :] = xbc
    cst_ref[0] = xbuf_ref[base + L - hist:base + L, :]
    conv = cb_ref[...] + cw_ref[hist:hist + 1, :] * xbc
    for k in range(hist):
        conv = conv + cw_ref[k:k + 1, :] * xbuf_ref[base - hist + k:base - hist + k + L, :]
    xc = _silu(conv)
    xs = xc[:, :GROUP_WIDTH]
    ng = SSD_GROUPS * SSD_STATE
    bm = xc[:, GROUP_WIDTH:GROUP_WIDTH + ng].astype(BF16)
    cm = xc[:, GROUP_WIDTH + ng:].astype(BF16)

    dt = _softplus(dt_ref[...] + dtb_ref[...])
    a = -jnp.exp(alog_ref[...])
    ri = lax.broadcasted_iota(jnp.int32, (L, L), 0)
    ci = lax.broadcasted_iota(jnp.int32, (L, L), 1)
    causal = ri >= ci
    acum = jnp.dot(causal.astype(F32), dt * a, preferred_element_type=F32, precision=lax.Precision.HIGHEST)
    acum_t = acum.T
    last = acum[L - 1:L, :]
    to_end = jnp.exp(last - acum)
    e_acum = jnp.exp(acum)
    chunk_decay = jnp.exp(last)

    ys = []
    rep = SSD_HEADS // SSD_GROUPS
    cb = [lax.dot_general(cm[:, g * SSD_STATE:(g + 1) * SSD_STATE], bm[:, g * SSD_STATE:(g + 1) * SSD_STATE],
                          (((1,), (1,)), ((), ())), preferred_element_type=F32) for g in range(SSD_GROUPS)]
    for h in range(SSD_HEADS):
        g = h // rep
        hs = slice(h * SSD_HEAD_DIM, (h + 1) * SSD_HEAD_DIM)
        gs = slice(g * SSD_STATE, (g + 1) * SSD_STATE)
        seg = acum[:, h:h + 1] - acum_t[h:h + 1, :]
        decay = jnp.exp(jnp.where(causal, seg, -jnp.inf))
        xs_h = xs[:, hs]
        xdt = xs_h * dt[:, h:h + 1]
        y = _dot((cb[g] * decay).astype(BF16), xdt.astype(BF16))
        h_prev = h_ref[h]
        y_off = lax.dot_general(cm[:, gs], h_prev.astype(BF16), (((1,), (1,)), ((), ())),
                                preferred_element_type=F32)
        y = y + y_off * e_acum[:, h:h + 1]
        st = lax.dot_general((xdt * to_end[:, h:h + 1]).astype(BF16), bm[:, gs], (((0,), (0,)), ((), ())),
                             preferred_element_type=F32)
        h_ref[h] = h_prev * chunk_decay[:, h:h + 1] + st
        ys.append(y)
    y = (jnp.concatenate(ys, axis=-1) + dexp_ref[...] * xs) * _silu(zx_ref[:, COL_Z:COL_Z + GROUP_WIDTH])
    mix_ref[...] = _rms(y, g_ref[...]).astype(mix_ref.dtype)
    hst_ref[0] = h_ref[...]


def prompt_ssd(proj, sp, layer):
    nc = SEQ // SSD_CHUNK
    rb = lambda b, c: _prompt_row_block(b, c, nc, SSD_CHUNK)
    return pl.pallas_call(
        _prompt_ssd_kernel,
        grid=(BATCH + 1, nc),
        in_specs=[pl.BlockSpec((SSD_CHUNK, ZX_WIDTH), lambda b, c: (rb(b, c), 0)),
                  pl.BlockSpec((SSD_CHUNK, LANE), lambda b, c: (rb(b, c), COL_DT // LANE)),
                  _layer_spec((SSD_CONV, XBC_WIDTH), layer),
                  _layer_spec((1, XBC_WIDTH), layer),
                  _layer_spec((1, LANE), layer),
                  _layer_spec((1, LANE), layer),
                  _layer_spec((1, GROUP_WIDTH), layer),
                  _layer_spec((1, GROUP_WIDTH), layer)],
        out_specs=(pl.BlockSpec((SSD_CHUNK, GROUP_WIDTH), lambda b, c: (rb(b, c), 0)),
                   pl.BlockSpec((1, SSD_CONV - 1, XBC_WIDTH), lambda b, c: (_prompt_batch(b), 0, 0)),
                   pl.BlockSpec((1, SSD_HEADS, SSD_HEAD_DIM, SSD_STATE), lambda b, c: (_prompt_batch(b), 0, 0, 0))),
        out_shape=(jax.ShapeDtypeStruct((ALL_ROWS, GROUP_WIDTH), BF16),
                   jax.ShapeDtypeStruct((BATCH, SSD_CONV - 1, XBC_WIDTH), F32),
                   jax.ShapeDtypeStruct((BATCH, SSD_HEADS, SSD_HEAD_DIM, SSD_STATE), F32)),
        scratch_shapes=[pltpu.VMEM((SUBLANE + SSD_CHUNK, XBC_WIDTH), F32),
                        pltpu.VMEM((SSD_HEADS, SSD_HEAD_DIM, SSD_STATE), F32)],
        compiler_params=_cparams(("arbitrary", "arbitrary")),
        name="prompt_ssd",
    )(proj, proj, sp["ssd_conv_w"], sp["ssd_conv_b"], sp["ssd_dt_bias"], sp["ssd_a_log"],
      sp["ssd_d_exp"], sp["g_ssd"])


def _pool_counts(pos, w):
    return jnp.minimum(w, pos + 1).astype(F32)


def _prompt_convpool_body(gb_ref, gc_ref, hv_ref, up_ref, scw_ref, pw_ref, ps_ref, gsc_ref, gpl_ref,
                          msc_ref, mpl_ref, scst_ref, plst_ref, vbuf_ref, pbuf_ref):
    rows = gb_ref.shape[0]
    vb = SUBLANE
    pb = 2 * SUBLANE
    vh = SC_CONV - 1
    t = pl.program_id(1)

    @pl.when(t == 0)
    def _():
        vbuf_ref[0:vb, :] = jnp.zeros((vb, GROUP_WIDTH), F32)
        pbuf_ref[0:pb, :] = jnp.zeros((pb, GROUP_WIDTH), F32)

    @pl.when(t > 0)
    def _():
        vbuf_ref[vb - vh:vb, :] = vbuf_ref[vb + rows - vh:vb + rows, :]
        pbuf_ref[0:pb, :] = pbuf_ref[rows:rows + pb, :]

    v = gc_ref[...] * hv_ref[...]
    vbuf_ref[vb:vb + rows, :] = v
    acc = scw_ref[vh:vh + 1, :] * v
    for k in range(vh):
        acc = acc + scw_ref[k:k + 1, :] * vbuf_ref[vb - vh + k:vb - vh + k + rows, :]
    msc_ref[...] = _rms(gb_ref[...] * acc, gsc_ref[...]).astype(msc_ref.dtype)
    scst_ref[0] = vbuf_ref[vb + rows - vh:vb + rows, :]

    u = up_ref[...]
    pbuf_ref[pb:pb + rows, :] = u
    pos = t * rows + lax.broadcasted_iota(jnp.int32, (rows, 1), 0)
    ys = []
    for gi, w in enumerate(POOL_WINDOWS):
        cs = slice(gi * POOL_GROUP, (gi + 1) * POOL_GROUP)
        s = u[:, cs]
        for j in range(1, w):
            s = s + pbuf_ref[pb - j:pb - j + rows, cs]
        pooled = s / _pool_counts(pos, w) - u[:, cs]
        ys.append(_dot(pooled.astype(BF16), pw_ref[gi]))
    y = jnp.concatenate(ys, axis=-1) * ps_ref[...]
    mpl_ref[...] = _rms(y, gpl_ref[...]).astype(mpl_ref.dtype)
    plst_ref[0] = pbuf_ref[pb + rows - POOL_HIST:pb + rows, :]


N_S5_IN, N_S5_OUT, N_S5_SCRATCH = 15, 3, 4
N_CP_IN, N_CP_OUT = 9, 4


def _prompt_b_mixers_kernel(*refs):
    s5_in, refs = refs[:N_S5_IN], refs[N_S5_IN:]
    cp_in, refs = refs[:N_CP_IN], refs[N_CP_IN:]
    s5_out, refs = refs[:N_S5_OUT], refs[N_S5_OUT:]
    cp_out, refs = refs[:N_CP_OUT], refs[N_CP_OUT:]
    s5_scratch, cp_scratch = refs[:N_S5_SCRATCH], refs[N_S5_SCRATCH:]

    def body():
        _prompt_s5_body(*s5_in, *s5_out, *s5_scratch)
        _prompt_convpool_body(*cp_in, *cp_out, *cp_scratch)

    _prompt_step(body, (s5_out[0], cp_out[0], cp_out[1]))


def prompt_b_mixers(proj, sp, layer):
    nt = SEQ // S5_CHUNK
    t_of_row = (np.arange(S5_CHUNK) % SUBLANE) * S5_SEG + np.arange(S5_CHUNK) // SUBLANE
    perm = jnp.asarray(np.eye(S5_CHUNK, dtype=np.float32)[t_of_row], BF16)
    rb = lambda b, t: _prompt_row_block(b, t, nt, S5_CHUNK)
    col = lambda c: pl.BlockSpec((S5_CHUNK, GROUP_WIDTH), lambda b, t: (rb(b, t), c // GROUP_WIDTH))
    mix_spec = pl.BlockSpec((S5_CHUNK, GROUP_WIDTH), lambda b, t: (rb(b, t), 0))
    mix_shape = jax.ShapeDtypeStruct((ALL_ROWS, GROUP_WIDTH), BF16)
    seq_spec = lambda *s: pl.BlockSpec((1,) + s, lambda b, t: (_prompt_batch(b),) + (0,) * len(s))
    seq_shape = lambda *s: jax.ShapeDtypeStruct((BATCH,) + s, F32)
    s5_in_specs = [col(COL_U5), _whole_spec((S5_CHUNK, S5_CHUNK)), _whole_spec((S5_CHUNK, S5_CHUNK)),
                   _layer_spec((S5_SEG, S5_LANES), layer), _layer_spec((S5_SEG, S5_LANES), layer),
                   _layer_spec((S5_SEG, SUBLANE, S5_LANES), layer),
                   _layer_spec((S5_SEG, SUBLANE, S5_LANES), layer)] + _s5_param_specs(layer)
    cp_in_specs = [col(COL_GB), col(COL_GC), col(COL_HV), col(COL_UP),
                   _layer_spec((SC_CONV, GROUP_WIDTH), layer),
                   _layer_spec((len(POOL_WINDOWS), POOL_GROUP, POOL_GROUP), layer),
                   _layer_spec((1, GROUP_WIDTH), layer), _layer_spec((1, GROUP_WIDTH), layer),
                   _layer_spec((1, GROUP_WIDTH), layer)]
    assert len(s5_in_specs) == N_S5_IN and len(cp_in_specs) == N_CP_IN
    return pl.pallas_call(
        _prompt_b_mixers_kernel,
        grid=(BATCH + 1, nt),
        in_specs=s5_in_specs + cp_in_specs,
        out_specs=(mix_spec, seq_spec(1, S5_LANES), seq_spec(1, S5_LANES),
                   mix_spec, mix_spec, seq_spec(SC_CONV - 1, GROUP_WIDTH), seq_spec(POOL_HIST, GROUP_WIDTH)),
        out_shape=(mix_shape, seq_shape(1, S5_LANES), seq_shape(1, S5_LANES),
                   mix_shape, mix_shape, seq_shape(SC_CONV - 1, GROUP_WIDTH), seq_shape(POOL_HIST, GROUP_WIDTH)),
        scratch_shapes=[pltpu.VMEM((S5_CHUNK, S5_LANES), F32), pltpu.VMEM((S5_CHUNK, S5_LANES), F32),
                        pltpu.VMEM((1, S5_LANES), F32), pltpu.VMEM((1, S5_LANES), F32),
                        pltpu.VMEM((SUBLANE + S5_CHUNK, GROUP_WIDTH), F32),
                        pltpu.VMEM((2 * SUBLANE + S5_CHUNK, GROUP_WIDTH), F32)],
        compiler_params=_cparams(("arbitrary", "arbitrary")),
        name="prompt_b_mixers",
    )(proj, perm, perm.T, sp["s5_pw_re"], sp["s5_pw_im"], sp["s5_pwt_re"], sp["s5_pwt_im"], *_s5_param_args(sp),
      proj, proj, proj, proj, sp["sc_conv_w"], sp["pool_w"], sp["pool_scale"], sp["g_sc"], sp["g_pool"])


def _sample_mix_kernel(proj_ref, pb_ref, cprev_ref, s5r_ref, s5i_ref, scprev_ref, plprev_ref,
                       cw_ref, cb_ref, dtb_ref, alog_ref,
                       pwr_ref, pwi_ref, bre_ref, bim_ref, cre_ref, cimn_ref, d5_ref, wglu_ref, bglu_ref, g5_ref,
                       scw_ref, pw_ref, ps_ref, gsc_ref, gpl_ref,
                       m5_ref, msc_ref, mpl_ref, cst_ref, s5ro_ref, s5io_ref, scst_ref, plst_ref,
                       xs_ref, xdt_ref, da_ref, b2_ref, c2_ref):
    nb = proj_ref.shape[0]
    W = GROUP_WIDTH
    xbc = proj_ref[:, COL_XBC:COL_XBC + XBC_WIDTH]
    hist = SSD_CONV - 1
    conv = cb_ref[...] + cw_ref[hist:hist + 1, :] * xbc
    for k in range(hist):
        conv = conv + cw_ref[k:k + 1, :] * cprev_ref[:, k * XBC_WIDTH:(k + 1) * XBC_WIDTH]
    cst_ref[:, 0:(hist - 1) * XBC_WIDTH] = cprev_ref[:, XBC_WIDTH:hist * XBC_WIDTH]
    cst_ref[:, (hist - 1) * XBC_WIDTH:hist * XBC_WIDTH] = xbc
    xc = _silu(conv)
    xs = xc[:, :W]
    bm = xc[:, W:W + LANE]
    cm = xc[:, W + LANE:W + 2 * LANE]
    dt = _softplus(proj_ref[:, COL_DT:COL_DT + LANE] + dtb_ref[...])
    da = jnp.exp(dt * (-jnp.exp(alog_ref[...])))
    lane_w = lax.broadcasted_iota(jnp.int32, (nb, W), 1)
    dt_exp = jnp.zeros((nb, W), F32)
    for h in range(SSD_HEADS):
        dt_exp = jnp.where(lane_w // SSD_HEAD_DIM == h, dt[:, h:h + 1], dt_exp)
        da_ref[:, h * LANE:(h + 1) * LANE] = jnp.broadcast_to(da[:, h:h + 1], (nb, LANE))
    xs_ref[...] = xs
    xdt_ref[...] = xs * dt_exp
    lane = lax.broadcasted_iota(jnp.int32, (nb, LANE), 1)
    low = lane < SSD_STATE
    for src, dst in ((bm, b2_ref), (cm, c2_ref)):
        swapped = pltpu.roll(src, SSD_STATE, 1)
        dst[:, 0:LANE] = jnp.where(low, src, swapped)
        dst[:, LANE:2 * LANE] = jnp.where(low, swapped, src)

    u5 = pb_ref[:, COL_U5:COL_U5 + W]
    bu_r, bu_i = _s5_bu(u5.astype(BF16), bre_ref, bim_ref)
    sblk = S5_LANES // len(bu_r)
    for i in range(len(bu_r)):
        ls = slice(i * sblk, (i + 1) * sblk)
        n_r, n_i = _cmul_add(bu_r[i], bu_i[i], pwr_ref[0:1, ls], pwi_ref[0:1, ls], s5r_ref[:, ls], s5i_ref[:, ls])
        s5ro_ref[:, ls] = n_r
        s5io_ref[:, ls] = n_i
    y5 = _s5_tail(u5, lambda i: s5ro_ref[:, i * sblk:(i + 1) * sblk], lambda i: s5io_ref[:, i * sblk:(i + 1) * sblk],
                  cre_ref, cimn_ref, d5_ref, wglu_ref, bglu_ref, g5_ref)
    m5_ref[...] = y5.astype(m5_ref.dtype)

    v = pb_ref[:, COL_GC:COL_GC + W] * pb_ref[:, COL_HV:COL_HV + W]
    vh = SC_CONV - 1
    acc = scw_ref[vh:vh + 1, :] * v
    for k in range(vh):
        acc = acc + scw_ref[k:k + 1, :] * scprev_ref[:, k * W:(k + 1) * W]
    msc_ref[...] = _rms(pb_ref[:, COL_GB:COL_GB + W] * acc, gsc_ref[...]).astype(msc_ref.dtype)
    scst_ref[:, 0:(vh - 1) * W] = scprev_ref[:, W:vh * W]
    scst_ref[:, (vh - 1) * W:vh * W] = v

    up = pb_ref[:, COL_UP:COL_UP + W]
    ys = []
    for gi, w in enumerate(POOL_WINDOWS):
        cs = slice(gi * POOL_GROUP, (gi + 1) * POOL_GROUP)
        s = up[:, cs]
        for j in range(1, w):
            k = POOL_HIST - j
            s = s + plprev_ref[:, k * W + gi * POOL_GROUP:k * W + (gi + 1) * POOL_GROUP]
        pooled = s / float(min(w, PAST_LEN + 1)) - up[:, cs]
        ys.append(_dot(pooled.astype(BF16), pw_ref[gi]))
    y = jnp.concatenate(ys, axis=-1) * ps_ref[...]
    mpl_ref[...] = _rms(y, gpl_ref[...]).astype(mpl_ref.dtype)
    plst_ref[:, 0:(POOL_HIST - 1) * W] = plprev_ref[:, W:POOL_HIST * W]
    plst_ref[:, (POOL_HIST - 1) * W:POOL_HIST * W] = up


def sample_mix(proj_a, proj_b, states, sp, layer):
    nb = DEC_BATCH
    W = GROUP_WIDTH
    f = lambda n: jax.ShapeDtypeStruct((nb, n), F32)
    b = lambda n: jax.ShapeDtypeStruct((nb, n), BF16)
    out_widths = ((SSD_CONV - 1) * XBC_WIDTH, S5_LANES, S5_LANES, (SC_CONV - 1) * W, POOL_HIST * W,
                  W, W, SSD_HEADS * LANE, SSD_GROUPS * LANE, SSD_GROUPS * LANE)
    out_shape = (b(W), b(W), b(W)) + tuple(f(n) for n in out_widths)
    nblk = GROUP_WIDTH // LANE
    in_specs = ([pl.BlockSpec((nb, A_WIDTH), lambda i: (SAMPLE_BLOCK, 0)),
                 pl.BlockSpec((nb, B_WIDTH), lambda i: (SAMPLE_BLOCK, 0))]
                + [_layer_spec((nb, s.shape[2]), layer) for s in states]
                + [_layer_spec((SSD_CONV, XBC_WIDTH), layer), _layer_spec((1, XBC_WIDTH), layer),
                   _layer_spec((1, LANE), layer), _layer_spec((1, LANE), layer),
                   _layer_spec((S5_SEG, S5_LANES), layer), _layer_spec((S5_SEG, S5_LANES), layer)]
                + _s5_param_specs(layer)
                + [_layer_spec((SC_CONV, W), layer), _layer_spec((len(POOL_WINDOWS), POOL_GROUP, POOL_GROUP), layer),
                   _layer_spec((1, W), layer), _layer_spec((1, W), layer), _layer_spec((1, W), layer)])
    return pl.pallas_call(
        _sample_mix_kernel,
        grid=(1,),
        in_specs=in_specs,
        out_specs=tuple(_whole_spec(s.shape) for s in out_shape),
        out_shape=out_shape,
        compiler_params=_cparams(("arbitrary",)),
        name="sample_mix",
    )(proj_a, proj_b, *states, sp["ssd_conv_w"], sp["ssd_conv_b"], sp["ssd_dt_bias"], sp["ssd_a_log"],
      sp["s5_pw_re"], sp["s5_pw_im"], *_s5_param_args(sp),
      sp["sc_conv_w"], sp["pool_w"], sp["pool_scale"], sp["g_sc"], sp["g_pool"])


def _sample_ssd_kernel(h0_ref, xdt_ref, da_ref, b2_ref, c2_ref, e_ref, r_ref, xs_ref, z_ref, dexp_ref, g_ref,
                       hn_ref, mix_ref, xrep_ref, prod_ref, y_ref):
    hp = pl.program_id(0)
    hpl = 2 * SSD_HEAD_DIM * SSD_STATE
    per_head = SSD_HEAD_DIM * SSD_STATE
    x_hi, x_lo = _split_bf16(xdt_ref[...])
    xrep_ref[...] = _dot(x_hi, e_ref[...]) + _dot(x_lo, e_ref[...])
    b2, c2 = b2_ref[...], c2_ref[...]
    for j in range(hpl // LANE):
        ls = slice(j * LANE, (j + 1) * LANE)
        hl = (j * LANE) // per_head
        hn = da_ref[:, hl * LANE:(hl + 1) * LANE] * h0_ref[:, ls] + xrep_ref[:, ls] * b2
        hn_ref[:, ls] = hn
        prod_ref[:, ls] = hn * c2
    p_hi, p_lo = _split_bf16(prod_ref[...])
    y_ref[hp] = _dot(p_hi, r_ref[...]) + _dot(p_lo, r_ref[...])

    @pl.when(hp == pl.num_programs(0) - 1)
    def _():
        y = jnp.concatenate([y_ref[i] for i in range(SSD_HEADS // 2)], axis=-1)
        y = (y + dexp_ref[...] * xs_ref[...]) * _silu(z_ref[...])
        mix_ref[...] = _rms(y, g_ref[...]).astype(mix_ref.dtype)


def sample_ssd(h0_all, xdt, da, b2, c2, xs, proj, sp, layer):
    nb = DEC_BATCH
    npairs = SSD_HEADS // 2
    hpl = 2 * SSD_HEAD_DIM * SSD_STATE
    expand = jnp.repeat(jnp.eye(LANE, dtype=BF16), SSD_STATE, axis=1)
    return pl.pallas_call(
        _sample_ssd_kernel,
        grid=(npairs,),
        in_specs=[pl.BlockSpec((None, nb, hpl), lambda i: (layer, 0, i)),
                  pl.BlockSpec((nb, LANE), lambda i: (0, i)),
                  pl.BlockSpec((nb, 2 * LANE), lambda i: (0, i)),
                  pl.BlockSpec((nb, LANE), lambda i: (0, i // (npairs // SSD_GROUPS))),
                  pl.BlockSpec((nb, LANE), lambda i: (0, i // (npairs // SSD_GROUPS))),
                  _whole_spec((LANE, hpl)),
                  _whole_spec((hpl, LANE)),
                  _whole_spec((nb, GROUP_WIDTH)),
                  pl.BlockSpec((nb, GROUP_WIDTH), lambda i: (SAMPLE_BLOCK, COL_Z // GROUP_WIDTH)),
                  _layer_spec((1, GROUP_WIDTH), layer),
                  _layer_spec((1, GROUP_WIDTH), layer)],
        out_specs=(pl.BlockSpec((nb, hpl), lambda i: (0, i)),
                   _whole_spec((nb, GROUP_WIDTH))),
        out_shape=(jax.ShapeDtypeStruct((nb, SSD_HEADS * SSD_HEAD_DIM * SSD_STATE), F32),
                   jax.ShapeDtypeStruct((nb, GROUP_WIDTH), BF16)),
        scratch_shapes=[pltpu.VMEM((nb, hpl), F32), pltpu.VMEM((nb, hpl), F32),
                        pltpu.VMEM((npairs, nb, LANE), F32)],
        compiler_params=_cparams(("arbitrary",)),
        name="sample_ssd",
    )(h0_all, xdt, da, b2, c2, expand, expand.T, xs, proj, sp["ssd_d_exp"], sp["g_ssd"])


def _split_w_in(w_in):
    head = COL_DT + SSD_HEADS
    part_a = jnp.pad(w_in[..., :head], ((0, 0), (0, 0), (0, A_WIDTH - head)))
    return part_a.astype(BF16), w_in[..., head:].astype(BF16)


def _stacked_params(w):
    row = lambda v: v.reshape(DEPTH, 1, -1)
    pad_heads = lambda v: row(jnp.pad(v, ((0, 0), (0, LANE - SSD_HEADS))))
    g_mix = w["mix_out_g"].reshape(DEPTH, 4, 1, GROUP_WIDTH)
    w_in_a, w_in_b = _split_w_in(w["w_in"])
    sp = dict(w_in_a=w_in_a, w_in_b=w_in_b)
    for k in ("w_out", "w_q", "w_k", "w_v", "w_o", "s5_w_glu", "pool_w"):
        sp[k] = w[k].astype(BF16)
    sp.update(w_up=w["w_up"], w_down=w["w_down"])
    for k in ("norm_mix_g", "norm_xa_g", "norm_mem_g", "norm_mlp_g", "ssd_conv_b", "s5_d", "s5_b_glu", "pool_scale"):
        sp[k] = row(w[k])
    sp.update(ssd_conv_w=w["ssd_conv_w"], sc_conv_w=w["sc_conv_w"],
              ssd_dt_bias=pad_heads(w["ssd_dt_bias"]), ssd_a_log=pad_heads(w["ssd_a_log"]),
              ssd_d_exp=row(jnp.repeat(w["ssd_d"], SSD_HEAD_DIM, axis=1)),
              g_ssd=g_mix[:, 0], g_s5=g_mix[:, 1], g_sc=g_mix[:, 2], g_pool=g_mix[:, 3])
    s5 = s5_prepare(w["s5_lam_re"], w["s5_lam_im"], w["s5_log_dt"], w["s5_b_re"], w["s5_b_im"],
                    w["s5_c_re"], w["s5_c_im"])
    sp.update({"s5_" + k: v for k, v in s5.items()})
    return sp


def _forward(x_prompt, x_sample, mem_prompt, state_ssd_conv, state_ssd, state_s5_re, state_s5_im,
             state_sconv, state_pool, cache_mem_k, cache_mem_v, final_norm_g, w):
    sp = _stacked_params(w)
    flat = lambda s: s.reshape(DEPTH, DEC_BATCH, -1)
    s_states = tuple(flat(s) for s in (state_ssd_conv, state_s5_re, state_s5_im, state_sconv, state_pool))
    s_ssd0 = flat(state_ssd)
    h = jnp.concatenate([x_prompt.reshape(P_ROWS, D_MODEL), x_sample.reshape(DEC_BATCH, D_MODEL)], axis=0)
    mem = mem_prompt.reshape(BATCH * N_MEM, D_MODEL)
    p_out = [[] for _ in range(8)]
    s_out = [[] for _ in range(6)]
    put_sample = lambda full, rows: lax.dynamic_update_slice(full, rows, (P_ROWS, 0))
    for l in range(DEPTH):
        mk = norm_matmul(mem, sp["norm_mem_g"], sp["w_k"], l, tm=BATCH * N_MEM, tn=COL_TILE)
        mv = norm_matmul(mem, sp["norm_mem_g"], sp["w_v"], l, tm=BATCH * N_MEM, tn=COL_TILE)

        proj_a, proj_b = in_proj(h, sp["norm_mix_g"], sp["w_in_a"], sp["w_in_b"], l, tm=IN_ROW_TILE)
        m_ssd, p_conv, p_ssd = prompt_ssd(proj_a, sp, l)
        m_s5, p_s5r, p_s5i, m_sc, m_pl, p_sc, p_pl = prompt_b_mixers(proj_b, sp, l)
        (s_m5, s_msc, s_mpl, s_conv, s_s5r, s_s5i, s_sc, s_pl, xs, xdt, da, b2, c2) = sample_mix(
            proj_a, proj_b, s_states, sp, l)
        s_ssd, s_mssd = sample_ssd(s_ssd0, xdt, da, b2, c2, xs, proj_a, sp, l)
        mixes = (put_sample(m_ssd, s_mssd), put_sample(m_s5, s_m5),
                 put_sample(m_sc, s_msc), put_sample(m_pl, s_mpl))

        h = out_proj(mixes, sp["w_out"], h, l, tm=ROW_TILE)
        q = norm_matmul(h, sp["norm_xa_g"], sp["w_q"], l, out_dtype=BF16, tm=ROW_TILE, tn=D_MODEL)
        o = prompt_attn(q, mk.reshape(BATCH, N_MEM, D_MODEL), mv.reshape(BATCH, N_MEM, D_MODEL), tq=512)
        q_s = q[P_ROWS:].astype(F32).reshape(DEC_BATCH, XA_HEADS, XA_HEAD_DIM)
        o_s = sample_attn(q_s, cache_mem_k, cache_mem_v, l)
        o = put_sample(o, o_s.reshape(DEC_BATCH, D_MODEL))
        h = res_matmul(o, sp["w_o"], h, l, tm=ROW_TILE, tn=D_MODEL)
        h = mlp(h, sp["norm_mlp_g"], sp["w_up"], sp["w_down"], l, tm=MLP_ROW_TILE, tf=FF_TILE)

        for lst, val in zip(p_out, (p_conv, p_ssd, p_s5r, p_s5i, p_sc, p_pl, mk, mv)):
            lst.append(val)
        for lst, val in zip(s_out, (s_conv, s_ssd, s_s5r, s_s5i, s_sc, s_pl)):
            lst.append(val)

    y_prompt = final_norm(h, final_norm_g, rows=P_ROWS, first_block=0, tm=COL_TILE)
    y_sample = final_norm(h, final_norm_g, rows=DEC_BATCH, first_block=SAMPLE_BLOCK, tm=DEC_BATCH)
    p_shapes = ((BATCH, SSD_CONV - 1, XBC_WIDTH), (BATCH, SSD_HEADS, SSD_HEAD_DIM, SSD_STATE),
                (BATCH, S5_GROUPS, S5_STATE), (BATCH, S5_GROUPS, S5_STATE),
                (BATCH, SC_CONV - 1, GROUP_WIDTH), (BATCH, POOL_HIST, GROUP_WIDTH),
                (BATCH, N_MEM, XA_HEADS, XA_HEAD_DIM), (BATCH, N_MEM, XA_HEADS, XA_HEAD_DIM))
    s_shapes = tuple((DEC_BATCH,) + s[1:] for s in p_shapes[:6])
    stack = lambda vals, shape: jnp.stack(vals).reshape((DEPTH,) + shape)
    return ((y_prompt.reshape(BATCH, SEQ, D_MODEL), y_sample.reshape(DEC_BATCH, 1, D_MODEL))
            + tuple(stack(v, s) for v, s in zip(p_out, p_shapes))
            + tuple(stack(v, s) for v, s in zip(s_out, s_shapes)))


_forward_jit = jax.jit(_forward)


def kernel(x_prompt, x_sample, mem_prompt, state_ssd_conv, state_ssd, state_s5_re, state_s5_im, state_sconv, state_pool, cache_mem_k, cache_mem_v, norm_mix_g, w_in, ssd_conv_w, ssd_conv_b, ssd_dt_bias, ssd_a_log, ssd_d, s5_lam_re, s5_lam_im, s5_log_dt, s5_b_re, s5_b_im, s5_c_re, s5_c_im, s5_d, s5_w_glu, s5_b_glu, sc_conv_w, pool_w, pool_scale, mix_out_g, w_out, norm_xa_g, norm_mem_g, w_q, w_k, w_v, w_o, norm_mlp_g, w_up, w_down, final_norm_g):
    w = dict(norm_mix_g=norm_mix_g, w_in=w_in, ssd_conv_w=ssd_conv_w, ssd_conv_b=ssd_conv_b,
             ssd_dt_bias=ssd_dt_bias, ssd_a_log=ssd_a_log, ssd_d=ssd_d, s5_lam_re=s5_lam_re,
             s5_lam_im=s5_lam_im, s5_log_dt=s5_log_dt, s5_b_re=s5_b_re, s5_b_im=s5_b_im,
             s5_c_re=s5_c_re, s5_c_im=s5_c_im, s5_d=s5_d, s5_w_glu=s5_w_glu, s5_b_glu=s5_b_glu,
             sc_conv_w=sc_conv_w, pool_w=pool_w, pool_scale=pool_scale, mix_out_g=mix_out_g,
             w_out=w_out, norm_xa_g=norm_xa_g, norm_mem_g=norm_mem_g, w_q=w_q, w_k=w_k, w_v=w_v,
             w_o=w_o, norm_mlp_g=norm_mlp_g, w_up=w_up, w_down=w_down)
    return _forward_jit(x_prompt, x_sample, mem_prompt, state_ssd_conv, state_ssd, state_s5_re, state_s5_im,
                        state_sconv, state_pool, cache_mem_k, cache_mem_v, final_norm_g, w)
```

```python
import functools
import math

import numpy as np
import jax
import jax.numpy as jnp
from jax import lax
from jax.experimental import pallas as pl
from jax.experimental.pallas import tpu as pltpu

F32 = jnp.float32
BF16 = jnp.bfloat16

D_MODEL = 2048
BATCH = 4
SEQ = 2048
DEPTH = 2
DEC_BATCH = 128
PAST_LEN = 16384
GROUP_WIDTH = D_MODEL // 4
SSD_HEAD_DIM = 64
SSD_HEADS = GROUP_WIDTH // SSD_HEAD_DIM
SSD_GROUPS = 2
SSD_STATE = 64
SSD_CONV = 4
SSD_CHUNK = 128
XBC_WIDTH = GROUP_WIDTH + 2 * SSD_GROUPS * SSD_STATE
S5_CH = 16
S5_GROUPS = GROUP_WIDTH // S5_CH
S5_STATE = 64
S5_LANES = S5_GROUPS * S5_STATE
SC_CONV = 3
POOL_WINDOWS = (2, 4, 8, 16)
POOL_GROUP = GROUP_WIDTH // len(POOL_WINDOWS)
POOL_HIST = max(POOL_WINDOWS) - 1
N_MEM = 256
XA_HEADS = 4
XA_HEAD_DIM = D_MODEL // XA_HEADS
D_FF = 4 * D_MODEL
EPS = 1e-6
P_ROWS = BATCH * SEQ
ALL_ROWS = P_ROWS + DEC_BATCH

LANE = 128
SUBLANE = 8
VMEM_LIMIT = 56 * 1024 * 1024

COL_Z = 0
COL_XBC = 512
COL_DT = 1280
A_WIDTH = 1536
ZX_WIDTH = COL_DT
COL_U5 = 0
COL_GB = 512
COL_GC = 1024
COL_HV = 1536
COL_UP = 2048
B_WIDTH = 2560
IN_ROW_TILE = 640

ROW_TILE = ALL_ROWS // 8
COL_TILE = 1024
MLP_ROW_TILE = ALL_ROWS // 10
FF_TILE = 512
SAMPLE_BLOCK = P_ROWS // DEC_BATCH

SAMPLE_ATTN_ROWS = 2
S5_CHUNK = 256
S5_SEG = S5_CHUNK // SUBLANE
S5_SCAN_LANES = 512


def _cparams(sem):
    return pltpu.CompilerParams(dimension_semantics=sem, vmem_limit_bytes=VMEM_LIMIT)


def _layer_spec(shape, layer):
    zeros = (0,) * len(shape)
    return pl.BlockSpec((None,) + tuple(shape), lambda *_: (layer,) + zeros)


def _whole_spec(shape):
    zeros = (0,) * len(shape)
    return pl.BlockSpec(tuple(shape), lambda *_: zeros)


def _weight_cols_spec(k, n, tn, layer):
    mode = pl.Buffered(1) if tn == n else None
    return pl.BlockSpec((None, k, tn), lambda i, j: (layer, 0, j), pipeline_mode=mode)


def _prompt_row_block(b, t, steps, rows):
    return jnp.minimum(b * steps + t, P_ROWS // rows)


def _prompt_batch(b):
    return jnp.minimum(b, BATCH - 1)


def _prompt_step(body, row_outputs):
    b = pl.program_id(0)
    pl.when(b < BATCH)(body)

    @pl.when(b == BATCH)
    def _():
        for ref in row_outputs:
            ref[...] = jnp.zeros_like(ref)


def _sigmoid(x):
    return 1.0 / (1.0 + jnp.exp(-x))


def _silu(x):
    return x * _sigmoid(x)


def _softplus(x):
    return jnp.maximum(x, 0.0) + jnp.log(1.0 + jnp.exp(-jnp.abs(x)))


def _gelu_tanh(x):
    return 0.5 * x * (1.0 + jnp.tanh(math.sqrt(2.0 / math.pi) * (x + 0.044715 * (x * x * x))))


def _rms(x, g):
    return x * lax.rsqrt(jnp.mean(x * x, axis=-1, keepdims=True) + EPS) * g


def _dot(a, b):
    return jnp.dot(a, b, preferred_element_type=F32)


def _spread_exact(x, sel):
    hi = x.astype(BF16)
    r = x - hi.astype(F32)
    mid = r.astype(BF16)
    lo = (r - mid.astype(F32)).astype(BF16)
    return _dot(hi, sel) + _dot(mid, sel) + _dot(lo, sel)


def _split_bf16(x):
    hi = x.astype(BF16)
    lo = (x - hi.astype(F32)).astype(BF16)
    return hi, lo


def _norm_matmul_kernel(*refs, has_res):
    if has_res:
        x_ref, g_ref, w_ref, r_ref, o_ref, xn_ref = refs
    else:
        x_ref, g_ref, w_ref, o_ref, xn_ref = refs
        r_ref = None

    @pl.when(pl.program_id(1) == 0)
    def _():
        xn_ref[...] = _rms(x_ref[...], g_ref[...]).astype(BF16)

    acc = _dot(xn_ref[...], w_ref[...].astype(BF16))
    if has_res:
        acc = acc + r_ref[...]
    o_ref[...] = acc.astype(o_ref.dtype)


def norm_matmul(x, g, w, layer, *, out_dtype=F32, tm, tn):
    m, k = x.shape
    n = w.shape[2]
    return pl.pallas_call(
        functools.partial(_norm_matmul_kernel, has_res=False),
        grid=(m // tm, n // tn),
        in_specs=[pl.BlockSpec((tm, k), lambda i, j: (i, 0)),
                  _layer_spec((1, k), layer),
                  _weight_cols_spec(k, n, tn, layer)],
        out_specs=pl.BlockSpec((tm, tn), lambda i, j: (i, j)),
        out_shape=jax.ShapeDtypeStruct((m, n), out_dtype),
        scratch_shapes=[pltpu.VMEM((tm, k), BF16)],
        compiler_params=_cparams(("parallel", "arbitrary")),
        name="norm_matmul",
    )(x, g, w)


def _in_proj_kernel(x_ref, g_ref, wa_ref, wb_ref, oa_ref, ob_ref):
    xn = _rms(x_ref[...], g_ref[...]).astype(BF16)
    oa_ref[...] = _dot(xn, wa_ref[...])
    ob_ref[...] = _dot(xn, wb_ref[...])


def in_proj(x, g, wa, wb, layer, *, tm):
    m, k = x.shape
    resident = lambda n: pl.BlockSpec((None, k, n), lambda i: (layer, 0, 0), pipeline_mode=pl.Buffered(1))
    return pl.pallas_call(
        _in_proj_kernel,
        grid=(m // tm,),
        in_specs=[pl.BlockSpec((tm, k), lambda i: (i, 0)), _layer_spec((1, k), layer),
                  resident(A_WIDTH), resident(B_WIDTH)],
        out_specs=(pl.BlockSpec((tm, A_WIDTH), lambda i: (i, 0)), pl.BlockSpec((tm, B_WIDTH), lambda i: (i, 0))),
        out_shape=(jax.ShapeDtypeStruct((m, A_WIDTH), F32), jax.ShapeDtypeStruct((m, B_WIDTH), F32)),
        compiler_params=_cparams(("parallel",)),
        name="in_proj",
    )(x, g, wa, wb)


def _res_matmul_kernel(x_ref, w_ref, r_ref, o_ref):
    o_ref[...] = r_ref[...] + _dot(x_ref[...], w_ref[...].astype(BF16))


def res_matmul(x, w, res, layer, *, tm, tn):
    m, k = x.shape
    n = w.shape[2]
    return pl.pallas_call(
        _res_matmul_kernel,
        grid=(m // tm, n // tn),
        in_specs=[pl.BlockSpec((tm, k), lambda i, j: (i, 0)),
                  _weight_cols_spec(k, n, tn, layer),
                  pl.BlockSpec((tm, tn), lambda i, j: (i, j))],
        out_specs=pl.BlockSpec((tm, tn), lambda i, j: (i, j)),
        out_shape=jax.ShapeDtypeStruct((m, n), F32),
        compiler_params=_cparams(("parallel", "arbitrary")),
        name="res_matmul",
    )(x, w, res)


def _out_proj_kernel(m0_ref, m1_ref, m2_ref, m3_ref, w_ref, r_ref, o_ref):
    x = jnp.concatenate([m0_ref[...], m1_ref[...], m2_ref[...], m3_ref[...]], axis=-1)
    o_ref[...] = r_ref[...] + _dot(x, w_ref[...].astype(BF16))


def out_proj(mixes, w, res, layer, *, tm):
    m = res.shape[0]
    mix_spec = pl.BlockSpec((tm, GROUP_WIDTH), lambda i, j: (i, 0))
    row_spec = pl.BlockSpec((tm, D_MODEL), lambda i, j: (i, 0))
    return pl.pallas_call(
        _out_proj_kernel,
        grid=(m // tm, 1),
        in_specs=[mix_spec] * 4 + [_weight_cols_spec(D_MODEL, D_MODEL, D_MODEL, layer), row_spec],
        out_specs=row_spec,
        out_shape=jax.ShapeDtypeStruct((m, D_MODEL), F32),
        compiler_params=_cparams(("parallel", "arbitrary")),
        name="out_proj",
    )(*mixes, w, res)


def _mlp_kernel(x_ref, g_ref, wu_ref, wd_ref, o_ref, xn_ref):
    j = pl.program_id(1)

    @pl.when(j == 0)
    def _():
        x = x_ref[...]
        xn_ref[...] = _rms(x, g_ref[...]).astype(BF16)
        o_ref[...] = x

    a = jnp.square(jnp.maximum(_dot(xn_ref[...], wu_ref[...].astype(BF16)), 0.0))
    o_ref[...] += _dot(a.astype(BF16), wd_ref[...].astype(BF16))


def mlp(x, g, w_up, w_down, layer, *, tm, tf):
    m = x.shape[0]
    return pl.pallas_call(
        _mlp_kernel,
        grid=(m // tm, D_FF // tf),
        in_specs=[pl.BlockSpec((tm, D_MODEL), lambda i, j: (i, 0)),
                  _layer_spec((1, D_MODEL), layer),
                  pl.BlockSpec((None, D_MODEL, tf), lambda i, j: (layer, 0, j)),
                  pl.BlockSpec((None, tf, D_MODEL), lambda i, j: (layer, j, 0))],
        out_specs=pl.BlockSpec((tm, D_MODEL), lambda i, j: (i, 0)),
        out_shape=jax.ShapeDtypeStruct((m, D_MODEL), F32),
        scratch_shapes=[pltpu.VMEM((tm, D_MODEL), BF16)],
        compiler_params=_cparams(("parallel", "arbitrary")),
        name="mlp",
    )(x, g, w_up, w_down)


def _final_norm_kernel(x_ref, g_ref, o_ref):
    o_ref[...] = _rms(x_ref[...], g_ref[...])


def final_norm(x, g, *, rows, first_block, tm):
    return pl.pallas_call(
        _final_norm_kernel,
        grid=(rows // tm,),
        in_specs=[pl.BlockSpec((tm, D_MODEL), lambda i: (first_block + i, 0)), _whole_spec((1, D_MODEL))],
        out_specs=pl.BlockSpec((tm, D_MODEL), lambda i: (i, 0)),
        out_shape=jax.ShapeDtypeStruct((rows, D_MODEL), F32),
        compiler_params=_cparams(("parallel",)),
        name="final_norm",
    )(x, g.reshape(1, D_MODEL))


def _prompt_attn_kernel(q_ref, k_ref, v_ref, o_ref):
    _prompt_step(lambda: _prompt_attn_body(q_ref, k_ref, v_ref, o_ref), (o_ref,))


def _prompt_attn_body(q_ref, k_ref, v_ref, o_ref):
    scale = XA_HEAD_DIM ** -0.5
    for h in range(XA_HEADS):
        cs = slice(h * XA_HEAD_DIM, (h + 1) * XA_HEAD_DIM)
        kh = k_ref[0, :, cs].astype(BF16)
        vh = v_ref[0, :, cs].astype(BF16)
        s = lax.dot_general(q_ref[:, cs], kh, (((1,), (1,)), ((), ())), preferred_element_type=F32) * scale
        p = jnp.exp(s - jnp.max(s, axis=-1, keepdims=True))
        p = p / jnp.sum(p, axis=-1, keepdims=True)
        o_ref[:, cs] = _dot(p.astype(BF16), vh).astype(o_ref.dtype)


def prompt_attn(q, k, v, *, tq):
    nq = SEQ // tq
    return pl.pallas_call(
        _prompt_attn_kernel,
        grid=(BATCH + 1, nq),
        in_specs=[pl.BlockSpec((tq, D_MODEL), lambda b, i: (_prompt_row_block(b, i, nq, tq), 0)),
                  pl.BlockSpec((1, N_MEM, D_MODEL), lambda b, i: (_prompt_batch(b), 0, 0)),
                  pl.BlockSpec((1, N_MEM, D_MODEL), lambda b, i: (_prompt_batch(b), 0, 0))],
        out_specs=pl.BlockSpec((tq, D_MODEL), lambda b, i: (_prompt_row_block(b, i, nq, tq), 0)),
        out_shape=jax.ShapeDtypeStruct((ALL_ROWS, D_MODEL), BF16),
        compiler_params=_cparams(("arbitrary", "arbitrary")),
        name="prompt_attn",
    )(q, k, v)


def _sample_attn_kernel(q_ref, k_ref, v_ref, o_ref):
    scale = XA_HEAD_DIM ** -0.5
    half, quarter = XA_HEAD_DIM // 2, XA_HEAD_DIM // 4
    for r in range(SAMPLE_ATTN_ROWS):
        prod = k_ref[r] * (q_ref[r] * scale)
        fold = prod[:, :, :half] + prod[:, :, half:]
        fold = fold[:, :, :quarter] + fold[:, :, quarter:]
        s = jnp.sum(fold, axis=-1, keepdims=True)
        e = jnp.exp(s - jnp.max(s, axis=0, keepdims=True))
        o = jnp.sum(e * v_ref[r], axis=0) / jnp.sum(e, axis=0)
        o_ref[r] = o.astype(o_ref.dtype)


def sample_attn(q, cache_k, cache_v, layer):
    nr = SAMPLE_ATTN_ROWS
    row = pl.BlockSpec((nr, XA_HEADS, XA_HEAD_DIM), lambda b: (b, 0, 0))
    mem = pl.BlockSpec((None, nr, N_MEM, XA_HEADS, XA_HEAD_DIM), lambda b: (layer, b, 0, 0, 0))
    return pl.pallas_call(
        _sample_attn_kernel,
        grid=(DEC_BATCH // nr,),
        in_specs=[row, mem, mem],
        out_specs=row,
        out_shape=jax.ShapeDtypeStruct((DEC_BATCH, XA_HEADS, XA_HEAD_DIM), BF16),
        compiler_params=_cparams(("parallel",)),
        name="sample_attn",
    )(q, cache_k, cache_v)


def _s5_abar(lr, li, ldt):
    delta = jnp.exp(ldt)
    mag = jnp.exp(lr * delta)
    return mag * jnp.cos(li * delta), mag * jnp.sin(li * delta)


def _s5_bbar_kernel(lr_ref, li_ref, ldt_ref, bre_ref, bim_ref, bbr_ref, bbi_ref):
    lr, li = lr_ref[...], li_ref[...]
    ar, ai = _s5_abar(lr, li, ldt_ref[...])
    den = lr * lr + li * li
    cr = ((ar - 1.0) * lr + ai * li) / den
    ci = (ai * lr - (ar - 1.0) * li) / den
    br, bi = bre_ref[...], bim_ref[...]
    bbr_ref[...] = cr * br - ci * bi
    bbi_ref[...] = cr * bi + ci * br


def _s5_pow_kernel(lr_ref, li_ref, ldt_ref, pr_ref, pi_ref):
    ar, ai = _s5_abar(lr_ref[...], li_ref[...], ldt_ref[...])
    qr, qi = ar, ai
    pr_ref[0] = qr
    pi_ref[0] = qi
    for e in range(1, S5_SEG):
        qr, qi = qr * ar - qi * ai, qr * ai + qi * ar
        pr_ref[e] = qr
        pi_ref[e] = qi


def s5_prepare(lam_re, lam_im, log_dt, b_re, b_im, c_re, c_im):
    dg = DEPTH * S5_GROUPS
    ldt = jnp.broadcast_to(log_dt[..., None], (DEPTH, S5_GROUPS, S5_STATE))
    rep = lambda a: jnp.repeat(a.reshape(dg, S5_STATE), S5_CH, axis=0)
    to_rows = lambda b: jnp.transpose(b, (0, 1, 3, 2)).reshape(dg * S5_CH, S5_STATE)
    shp = jax.ShapeDtypeStruct((dg * S5_CH, S5_STATE), F32)
    bbr, bbi = pl.pallas_call(_s5_bbar_kernel, out_shape=(shp, shp), name="s5_bbar")(
        rep(lam_re), rep(lam_im), rep(ldt), to_rows(b_re), to_rows(b_im))
    pshp = jax.ShapeDtypeStruct((S5_SEG, dg, S5_STATE), F32)
    pr, pi = pl.pallas_call(_s5_pow_kernel, out_shape=(pshp, pshp), name="s5_pow")(
        lam_re.reshape(dg, S5_STATE), lam_im.reshape(dg, S5_STATE), ldt.reshape(dg, S5_STATE))

    gpb = LANE // S5_CH
    nblk = S5_GROUPS // gpb
    eye = jnp.eye(gpb, dtype=F32)

    def b_blocks(bb):
        bb = bb.reshape(DEPTH, nblk, gpb, S5_CH, S5_STATE)
        return jnp.einsum("digkn,gh->digkhn", bb, eye).reshape(DEPTH, nblk, LANE, gpb * S5_STATE).astype(BF16)

    def c_blocks(cc):
        cc = cc.reshape(DEPTH, nblk, gpb, S5_CH, S5_STATE)
        return jnp.einsum("digkn,gh->dignhk", cc, eye).reshape(DEPTH, nblk, gpb * S5_STATE, LANE).astype(BF16)

    def pow_rows(p):
        return jnp.transpose(p.reshape(S5_SEG, DEPTH, S5_LANES), (1, 0, 2))

    pw_r, pw_i = pow_rows(pr), pow_rows(pi)
    tile = lambda p: jnp.broadcast_to(p[:, :, None, :], (DEPTH, S5_SEG, SUBLANE, S5_LANES))
    return dict(b_re=b_blocks(bbr), b_im=b_blocks(bbi), c_re=c_blocks(c_re), c_imn=c_blocks(-c_im),
                pw_re=pw_r, pw_im=pw_i, pwt_re=tile(pw_r), pwt_im=tile(pw_i))


def _s5_bu(ub, bre_ref, bim_ref):
    res_r, res_i = [], []
    for i in range(GROUP_WIDTH // LANE):
        ui = ub[:, i * LANE:(i + 1) * LANE]
        res_r.append(_dot(ui, bre_ref[i]))
        res_i.append(_dot(ui, bim_ref[i]))
    return res_r, res_i


def _s5_tail(u, hs_re, hs_im, cre_ref, cimn_ref, d_ref, wglu_ref, bglu_ref, g_ref):
    ys = []
    for i in range(GROUP_WIDTH // LANE):
        ys.append(_dot(hs_re(i).astype(BF16), cre_ref[i]) + _dot(hs_im(i).astype(BF16), cimn_ref[i]))
    y = jnp.concatenate(ys, axis=-1) + d_ref[...] * u
    y = _gelu_tanh(y)
    y = y * _sigmoid(_dot(y.astype(BF16), wglu_ref[...]) + bglu_ref[...])
    return _rms(y, g_ref[...])


def _s5_param_specs(layer):
    nblk = GROUP_WIDTH // LANE
    sblk = S5_LANES // nblk
    return [_layer_spec((nblk, LANE, sblk), layer),
            _layer_spec((nblk, LANE, sblk), layer),
            _layer_spec((nblk, sblk, LANE), layer),
            _layer_spec((nblk, sblk, LANE), layer),
            _layer_spec((1, GROUP_WIDTH), layer),
            _layer_spec((GROUP_WIDTH, GROUP_WIDTH), layer),
            _layer_spec((1, GROUP_WIDTH), layer),
            _layer_spec((1, GROUP_WIDTH), layer)]


def _s5_param_args(sp):
    return (sp["s5_b_re"], sp["s5_b_im"], sp["s5_c_re"], sp["s5_c_imn"], sp["s5_d"], sp["s5_w_glu"],
            sp["s5_b_glu"], sp["g_s5"])


def _cmul_add(x_r, x_i, a_r, a_i, h_r, h_i):
    return x_r + a_r * h_r - a_i * h_i, x_i + a_r * h_i + a_i * h_r


def _prompt_s5_body(u_ref, perm_ref, unperm_ref, pw_re_ref, pw_im_ref, pwt_re_ref, pwt_im_ref,
                    bre_ref, bim_ref, cre_ref, cimn_ref, d_ref, wglu_ref, bglu_ref, g_ref,
                    mix_ref, sre_ref, sim_ref, hre_ref, him_ref, cr_ref, ci_ref):
    sblk = S5_LANES // (GROUP_WIDTH // LANE)
    W = S5_SCAN_LANES

    @pl.when(pl.program_id(1) == 0)
    def _():
        cr_ref[...] = jnp.zeros_like(cr_ref)
        ci_ref[...] = jnp.zeros_like(ci_ref)

    u_hi, u_lo = _split_bf16(u_ref[...])
    ub = _dot(perm_ref[...], u_hi)
    u = ub + _dot(perm_ref[...], u_lo)
    bu_r, bu_i = _s5_bu(ub.astype(BF16), bre_ref, bim_ref)
    for i in range(len(bu_r)):
        hre_ref[:, i * sblk:(i + 1) * sblk] = bu_r[i]
        him_ref[:, i * sblk:(i + 1) * sblk] = bu_i[i]

    row = lax.broadcasted_iota(jnp.int32, (SUBLANE, W), 0)
    for lb in range(0, S5_LANES, W):
        ls = slice(lb, lb + W)
        a_r = jnp.broadcast_to(pw_re_ref[0:1, ls], (SUBLANE, W))
        a_i = jnp.broadcast_to(pw_im_ref[0:1, ls], (SUBLANE, W))

        def local_step(j, h, ls=ls, a_r=a_r, a_i=a_i):
            rs = pl.ds(pl.multiple_of(j * SUBLANE, SUBLANE), SUBLANE)
            n_r, n_i = _cmul_add(hre_ref[rs, ls], him_ref[rs, ls], a_r, a_i, h[0], h[1])
            hre_ref[rs, ls] = n_r
            him_ref[rs, ls] = n_i
            return n_r, n_i

        zero = jnp.zeros((SUBLANE, W), F32)
        e_r, e_i = lax.fori_loop(0, S5_SEG, local_step, (zero, zero))

        s_r = pw_re_ref[S5_SEG - 1:S5_SEG, ls]
        s_i = pw_im_ref[S5_SEG - 1:S5_SEG, ls]
        c_r, c_i = cr_ref[:, ls], ci_ref[:, ls]
        in_r, in_i = zero, zero
        for s in range(SUBLANE):
            in_r = jnp.where(row == s, c_r, in_r)
            in_i = jnp.where(row == s, c_i, in_i)
            c_r, c_i = _cmul_add(e_r[s:s + 1, :], e_i[s:s + 1, :], s_r, s_i, c_r, c_i)
        cr_ref[:, ls] = c_r
        ci_ref[:, ls] = c_i

        def fix_step(j, carry, ls=ls, in_r=in_r, in_i=in_i):
            rs = pl.ds(pl.multiple_of(j * SUBLANE, SUBLANE), SUBLANE)
            n_r, n_i = _cmul_add(hre_ref[rs, ls], him_ref[rs, ls], pwt_re_ref[j, :, ls], pwt_im_ref[j, :, ls],
                                 in_r, in_i)
            hre_ref[rs, ls] = n_r
            him_ref[rs, ls] = n_i
            return carry

        lax.fori_loop(0, S5_SEG, fix_step, 0)

    sre_ref[0] = cr_ref[...]
    sim_ref[0] = ci_ref[...]
    y = _s5_tail(u, lambda i: hre_ref[:, i * sblk:(i + 1) * sblk], lambda i: him_ref[:, i * sblk:(i + 1) * sblk],
                 cre_ref, cimn_ref, d_ref, wglu_ref, bglu_ref, g_ref)
    mix_ref[...] = _dot(unperm_ref[...], y.astype(mix_ref.dtype)).astype(mix_ref.dtype)


def _prompt_ssd_kernel(*refs):
    _prompt_step(lambda: _prompt_ssd_body(*refs), (refs[10],))


def _prompt_ssd_body(zx_ref, dt_ref, cw_ref, cb_ref, dtb_ref, alog_ref, dexp_ref, g_ref, hexp_ref, hcol_ref,
                     mix_ref, cst_ref, hst_ref, xbuf_ref, h_ref):
    L = SSD_CHUNK
    hist = SSD_CONV - 1
    base = SUBLANE

    @pl.when(pl.program_id(1) == 0)
    def _():
        xbuf_ref[0:base, :] = jnp.zeros((base, XBC_WIDTH), F32)
        h_ref[...] = jnp.zeros_like(h_ref)

    @pl.when(pl.program_id(1) > 0)
    def _():
        xbuf_ref[base - hist:base, :] = xbuf_ref[base + L - hist:base + L, :]

    xbc = zx_ref[:, COL_XBC:COL_XBC + XBC_WIDTH]
    xbuf_ref[base:base + L, :] = xbc
    cst_ref[0] = xbuf_ref[base + L - hist:base + L, :]
    conv = cb_ref[...] + cw_ref[hist:hist + 1, :] * xbc
    for k in range(hist):
        conv = conv + cw_ref[k:k + 1, :] * xbuf_ref[base - hist + k:base - hist + k + L, :]
    xc = _silu(conv)
    xs = xc[:, :GROUP_WIDTH]
    ng = SSD_GROUPS * SSD_STATE
    bm = xc[:, GROUP_WIDTH:GROUP_WIDTH + ng].astype(BF16)
    cm = xc[:, GROUP_WIDTH + ng:].astype(BF16)

    dt = _softplus(dt_ref[...] + dtb_ref[...])
    a = -jnp.exp(alog_ref[...])
    ri = lax.broadcasted_iota(jnp.int32, (L, L), 0)
    ci = lax.broadcasted_iota(jnp.int32, (L, L), 1)
    causal = ri >= ci
    acum = jnp.dot(causal.astype(F32), dt * a, preferred_element_type=F32, precision=lax.Precision.HIGHEST)
    acum_t = acum.T
    last = acum[L - 1:L, :]
    to_end = jnp.exp(last - acum)
    e_acum = jnp.exp(acum)
    chunk_decay = jnp.exp(last)

    xdt_all = xs * _spread_exact(dt, hexp_ref[...])
    xdt_end = (xdt_all * _spread_exact(to_end, hexp_ref[...])).astype(BF16)
    xdt_all = xdt_all.astype(BF16)
    e_acum_all = _spread_exact(e_acum, hexp_ref[...])
    acum_cols = _spread_exact(acum, hcol_ref[...])

    ys_diag, ys_off = [], []
    rep = SSD_HEADS // SSD_GROUPS
    cb = [lax.dot_general(cm[:, g * SSD_STATE:(g + 1) * SSD_STATE], bm[:, g * SSD_STATE:(g + 1) * SSD_STATE],
                          (((1,), (1,)), ((), ())), preferred_element_type=F32) for g in range(SSD_GROUPS)]
    for h in range(SSD_HEADS):
        g = h // rep
        hs = slice(h * SSD_HEAD_DIM, (h + 1) * SSD_HEAD_DIM)
        gs = slice(g * SSD_STATE, (g + 1) * SSD_STATE)
        seg = acum_cols[:, h * L:(h + 1) * L] - acum_t[h:h + 1, :]
        decay = jnp.exp(jnp.where(causal, seg, -jnp.inf))
        ys_diag.append(_dot((cb[g] * decay).astype(BF16), xdt_all[:, hs]))
        h_prev = h_ref[h]
        ys_off.append(lax.dot_general(cm[:, gs], h_prev.astype(BF16), (((1,), (1,)), ((), ())),
                                      preferred_element_type=F32))
        st = lax.dot_general(xdt_end[:, hs], bm[:, gs], (((0,), (0,)), ((), ())), preferred_element_type=F32)
        h_ref[h] = h_prev * chunk_decay[:, h:h + 1] + st
    y = jnp.concatenate(ys_diag, axis=-1) + jnp.concatenate(ys_off, axis=-1) * e_acum_all
    y = (y + dexp_ref[...] * xs) * _silu(zx_ref[:, COL_Z:COL_Z + GROUP_WIDTH])
    mix_ref[...] = _rms(y, g_ref[...]).astype(mix_ref.dtype)
    hst_ref[0] = h_ref[...]


def prompt_ssd(proj, sp, layer):
    nc = SEQ // SSD_CHUNK
    rb = lambda b, c: _prompt_row_block(b, c, nc, SSD_CHUNK)
    head_of_row = np.arange(LANE)[:, None]
    head_lanes = jnp.asarray(head_of_row == np.arange(GROUP_WIDTH)[None, :] // SSD_HEAD_DIM, BF16)
    head_cols = jnp.asarray(head_of_row == np.arange(SSD_HEADS * SSD_CHUNK)[None, :] // SSD_CHUNK, BF16)
    return pl.pallas_call(
        _prompt_ssd_kernel,
        grid=(BATCH + 1, nc),
        in_specs=[pl.BlockSpec((SSD_CHUNK, ZX_WIDTH), lambda b, c: (rb(b, c), 0)),
                  pl.BlockSpec((SSD_CHUNK, LANE), lambda b, c: (rb(b, c), COL_DT // LANE)),
                  _layer_spec((SSD_CONV, XBC_WIDTH), layer),
                  _layer_spec((1, XBC_WIDTH), layer),
                  _layer_spec((1, LANE), layer),
                  _layer_spec((1, LANE), layer),
                  _layer_spec((1, GROUP_WIDTH), layer),
                  _layer_spec((1, GROUP_WIDTH), layer),
                  _whole_spec(head_lanes.shape), _whole_spec(head_cols.shape)],
        out_specs=(pl.BlockSpec((SSD_CHUNK, GROUP_WIDTH), lambda b, c: (rb(b, c), 0)),
                   pl.BlockSpec((1, SSD_CONV - 1, XBC_WIDTH), lambda b, c: (_prompt_batch(b), 0, 0)),
                   pl.BlockSpec((1, SSD_HEADS, SSD_HEAD_DIM, SSD_STATE), lambda b, c: (_prompt_batch(b), 0, 0, 0))),
        out_shape=(jax.ShapeDtypeStruct((ALL_ROWS, GROUP_WIDTH), BF16),
                   jax.ShapeDtypeStruct((BATCH, SSD_CONV - 1, XBC_WIDTH), F32),
                   jax.ShapeDtypeStruct((BATCH, SSD_HEADS, SSD_HEAD_DIM, SSD_STATE), F32)),
        scratch_shapes=[pltpu.VMEM((SUBLANE + SSD_CHUNK, XBC_WIDTH), F32),
                        pltpu.VMEM((SSD_HEADS, SSD_HEAD_DIM, SSD_STATE), F32)],
        compiler_params=_cparams(("arbitrary", "arbitrary")),
        name="prompt_ssd",
    )(proj, proj, sp["ssd_conv_w"], sp["ssd_conv_b"], sp["ssd_dt_bias"], sp["ssd_a_log"],
      sp["ssd_d_exp"], sp["g_ssd"], head_lanes, head_cols)


def _pool_counts(pos, w):
    return jnp.minimum(w, pos + 1).astype(F32)


def _prompt_convpool_body(gb_ref, gc_ref, hv_ref, up_ref, scw_ref, pw_ref, ps_ref, gsc_ref, gpl_ref,
                          msc_ref, mpl_ref, scst_ref, plst_ref, vbuf_ref, pbuf_ref):
    rows = gb_ref.shape[0]
    vb = SUBLANE
    pb = 2 * SUBLANE
    vh = SC_CONV - 1
    t = pl.program_id(1)

    @pl.when(t == 0)
    def _():
        vbuf_ref[0:vb, :] = jnp.zeros((vb, GROUP_WIDTH), F32)
        pbuf_ref[0:pb, :] = jnp.zeros((pb, GROUP_WIDTH), F32)

    @pl.when(t > 0)
    def _():
        vbuf_ref[vb - vh:vb, :] = vbuf_ref[vb + rows - vh:vb + rows, :]
        pbuf_ref[0:pb, :] = pbuf_ref[rows:rows + pb, :]

    v = gc_ref[...] * hv_ref[...]
    vbuf_ref[vb:vb + rows, :] = v
    acc = scw_ref[vh:vh + 1, :] * v
    for k in range(vh):
        acc = acc + scw_ref[k:k + 1, :] * vbuf_ref[vb - vh + k:vb - vh + k + rows, :]
    msc_ref[...] = _rms(gb_ref[...] * acc, gsc_ref[...]).astype(msc_ref.dtype)
    scst_ref[0] = vbuf_ref[vb + rows - vh:vb + rows, :]

    u = up_ref[...]
    pbuf_ref[pb:pb + rows, :] = u
    pos = t * rows + lax.broadcasted_iota(jnp.int32, (rows, 1), 0)
    ys = []
    for gi, w in enumerate(POOL_WINDOWS):
        cs = slice(gi * POOL_GROUP, (gi + 1) * POOL_GROUP)
        s = u[:, cs]
        for j in range(1, w):
            s = s + pbuf_ref[pb - j:pb - j + rows, cs]
        pooled = s / _pool_counts(pos, w) - u[:, cs]
        ys.append(_dot(pooled.astype(BF16), pw_ref[gi]))
    y = jnp.concatenate(ys, axis=-1) * ps_ref[...]
    mpl_ref[...] = _rms(y, gpl_ref[...]).astype(mpl_ref.dtype)
    plst_ref[0] = pbuf_ref[pb + rows - POOL_HIST:pb + rows, :]


N_S5_IN, N_S5_OUT, N_S5_SCRATCH = 15, 3, 4
N_CP_IN, N_CP_OUT = 9, 4


def _prompt_b_mixers_kernel(*refs):
    s5_in, refs = refs[:N_S5_IN], refs[N_S5_IN:]
    cp_in, refs = refs[:N_CP_IN], refs[N_CP_IN:]
    s5_out, refs = refs[:N_S5_OUT], refs[N_S5_OUT:]
    cp_out, refs = refs[:N_CP_OUT], refs[N_CP_OUT:]
    s5_scratch, cp_scratch = refs[:N_S5_SCRATCH], refs[N_S5_SCRATCH:]

    def body():
        _prompt_s5_body(*s5_in, *s5_out, *s5_scratch)
        _prompt_convpool_body(*cp_in, *cp_out, *cp_scratch)

    _prompt_step(body, (s5_out[0], cp_out[0], cp_out[1]))


def prompt_b_mixers(proj, sp, layer):
    nt = SEQ // S5_CHUNK
    t_of_row = (np.arange(S5_CHUNK) % SUBLANE) * S5_SEG + np.arange(S5_CHUNK) // SUBLANE
    perm = jnp.asarray(np.eye(S5_CHUNK, dtype=np.float32)[t_of_row], BF16)
    rb = lambda b, t: _prompt_row_block(b, t, nt, S5_CHUNK)
    col = lambda c: pl.BlockSpec((S5_CHUNK, GROUP_WIDTH), lambda b, t: (rb(b, t), c // GROUP_WIDTH))
    mix_spec = pl.BlockSpec((S5_CHUNK, GROUP_WIDTH), lambda b, t: (rb(b, t), 0))
    mix_shape = jax.ShapeDtypeStruct((ALL_ROWS, GROUP_WIDTH), BF16)
    seq_spec = lambda *s: pl.BlockSpec((1,) + s, lambda b, t: (_prompt_batch(b),) + (0,) * len(s))
    seq_shape = lambda *s: jax.ShapeDtypeStruct((BATCH,) + s, F32)
    s5_in_specs = [col(COL_U5), _whole_spec((S5_CHUNK, S5_CHUNK)), _whole_spec((S5_CHUNK, S5_CHUNK)),
                   _layer_spec((S5_SEG, S5_LANES), layer), _layer_spec((S5_SEG, S5_LANES), layer),
                   _layer_spec((S5_SEG, SUBLANE, S5_LANES), layer),
                   _layer_spec((S5_SEG, SUBLANE, S5_LANES), layer)] + _s5_param_specs(layer)
    cp_in_specs = [col(COL_GB), col(COL_GC), col(COL_HV), col(COL_UP),
                   _layer_spec((SC_CONV, GROUP_WIDTH), layer),
                   _layer_spec((len(POOL_WINDOWS), POOL_GROUP, POOL_GROUP), layer),
                   _layer_spec((1, GROUP_WIDTH), layer), _layer_spec((1, GROUP_WIDTH), layer),
                   _layer_spec((1, GROUP_WIDTH), layer)]
    assert len(s5_in_specs) == N_S5_IN and len(cp_in_specs) == N_CP_IN
    return pl.pallas_call(
        _prompt_b_mixers_kernel,
        grid=(BATCH + 1, nt),
        in_specs=s5_in_specs + cp_in_specs,
        out_specs=(mix_spec, seq_spec(1, S5_LANES), seq_spec(1, S5_LANES),
                   mix_spec, mix_spec, seq_spec(SC_CONV - 1, GROUP_WIDTH), seq_spec(POOL_HIST, GROUP_WIDTH)),
        out_shape=(mix_shape, seq_shape(1, S5_LANES), seq_shape(1, S5_LANES),
                   mix_shape, mix_shape, seq_shape(SC_CONV - 1, GROUP_WIDTH), seq_shape(POOL_HIST, GROUP_WIDTH)),
        scratch_shapes=[pltpu.VMEM((S5_CHUNK, S5_LANES), F32), pltpu.VMEM((S5_CHUNK, S5_LANES), F32),
                        pltpu.VMEM((1, S5_LANES), F32), pltpu.VMEM((1, S5_LANES), F32),
                        pltpu.VMEM((SUBLANE + S5_CHUNK, GROUP_WIDTH), F32),
                        pltpu.VMEM((2 * SUBLANE + S5_CHUNK, GROUP_WIDTH), F32)],
        compiler_params=_cparams(("arbitrary", "arbitrary")),
        name="prompt_b_mixers",
    )(proj, perm, perm.T, sp["s5_pw_re"], sp["s5_pw_im"], sp["s5_pwt_re"], sp["s5_pwt_im"], *_s5_param_args(sp),
      proj, proj, proj, proj, sp["sc_conv_w"], sp["pool_w"], sp["pool_scale"], sp["g_sc"], sp["g_pool"])


def _sample_mix_kernel(proj_ref, pb_ref, cprev_ref, s5r_ref, s5i_ref, scprev_ref, plprev_ref,
                       cw_ref, cb_ref, dtb_ref, alog_ref,
                       pwr_ref, pwi_ref, bre_ref, bim_ref, cre_ref, cimn_ref, d5_ref, wglu_ref, bglu_ref, g5_ref,
                       scw_ref, pw_ref, ps_ref, gsc_ref, gpl_ref,
                       m5_ref, msc_ref, mpl_ref, cst_ref, s5ro_ref, s5io_ref, scst_ref, plst_ref,
                       xs_ref, xdt_ref, da_ref, b2_ref, c2_ref):
    nb = proj_ref.shape[0]
    W = GROUP_WIDTH
    xbc = proj_ref[:, COL_XBC:COL_XBC + XBC_WIDTH]
    hist = SSD_CONV - 1
    conv = cb_ref[...] + cw_ref[hist:hist + 1, :] * xbc
    for k in range(hist):
        conv = conv + cw_ref[k:k + 1, :] * cprev_ref[:, k * XBC_WIDTH:(k + 1) * XBC_WIDTH]
    cst_ref[:, 0:(hist - 1) * XBC_WIDTH] = cprev_ref[:, XBC_WIDTH:hist * XBC_WIDTH]
    cst_ref[:, (hist - 1) * XBC_WIDTH:hist * XBC_WIDTH] = xbc
    xc = _silu(conv)
    xs = xc[:, :W]
    bm = xc[:, W:W + LANE]
    cm = xc[:, W + LANE:W + 2 * LANE]
    dt = _softplus(proj_ref[:, COL_DT:COL_DT + LANE] + dtb_ref[...])
    da = jnp.exp(dt * (-jnp.exp(alog_ref[...])))
    lane_w = lax.broadcasted_iota(jnp.int32, (nb, W), 1)
    dt_exp = jnp.zeros((nb, W), F32)
    for h in range(SSD_HEADS):
        dt_exp = jnp.where(lane_w // SSD_HEAD_DIM == h, dt[:, h:h + 1], dt_exp)
        da_ref[:, h * LANE:(h + 1) * LANE] = jnp.broadcast_to(da[:, h:h + 1], (nb, LANE))
    xs_ref[...] = xs
    xdt_ref[...] = xs * dt_exp
    lane = lax.broadcasted_iota(jnp.int32, (nb, LANE), 1)
    low = lane < SSD_STATE
    for src, dst in ((bm, b2_ref), (cm, c2_ref)):
        swapped = pltpu.roll(src, SSD_STATE, 1)
        dst[:, 0:LANE] = jnp.where(low, src, swapped)
        dst[:, LANE:2 * LANE] = jnp.where(low, swapped, src)

    u5 = pb_ref[:, COL_U5:COL_U5 + W]
    bu_r, bu_i = _s5_bu(u5.astype(BF16), bre_ref, bim_ref)
    sblk = S5_LANES // len(bu_r)
    for i in range(len(bu_r)):
        ls = slice(i * sblk, (i + 1) * sblk)
        n_r, n_i = _cmul_add(bu_r[i], bu_i[i], pwr_ref[0:1, ls], pwi_ref[0:1, ls], s5r_ref[:, ls], s5i_ref[:, ls])
        s5ro_ref[:, ls] = n_r
        s5io_ref[:, ls] = n_i
    y5 = _s5_tail(u5, lambda i: s5ro_ref[:, i * sblk:(i + 1) * sblk], lambda i: s5io_ref[:, i * sblk:(i + 1) * sblk],
                  cre_ref, cimn_ref, d5_ref, wglu_ref, bglu_ref, g5_ref)
    m5_ref[...] = y5.astype(m5_ref.dtype)

    v = pb_ref[:, COL_GC:COL_GC + W] * pb_ref[:, COL_HV:COL_HV + W]
    vh = SC_CONV - 1
    acc = scw_ref[vh:vh + 1, :] * v
    for k in range(vh):
        acc = acc + scw_ref[k:k + 1, :] * scprev_ref[:, k * W:(k + 1) * W]
    msc_ref[...] = _rms(pb_ref[:, COL_GB:COL_GB + W] * acc, gsc_ref[...]).astype(msc_ref.dtype)
    scst_ref[:, 0:(vh - 1) * W] = scprev_ref[:, W:vh * W]
    scst_ref[:, (vh - 1) * W:vh * W] = v

    up = pb_ref[:, COL_UP:COL_UP + W]
    ys = []
    for gi, w in enumerate(POOL_WINDOWS):
        cs = slice(gi * POOL_GROUP, (gi + 1) * POOL_GROUP)
        s = up[:, cs]
        for j in range(1, w):
            k = POOL_HIST - j
            s = s + plprev_ref[:, k * W + gi * POOL_GROUP:k * W + (gi + 1) * POOL_GROUP]
        pooled = s / float(min(w, PAST_LEN + 1)) - up[:, cs]
        ys.append(_dot(pooled.astype(BF16), pw_ref[gi]))
    y = jnp.concatenate(ys, axis=-1) * ps_ref[...]
    mpl_ref[...] = _rms(y, gpl_ref[...]).astype(mpl_ref.dtype)
    plst_ref[:, 0:(POOL_HIST - 1) * W] = plprev_ref[:, W:POOL_HIST * W]
    plst_ref[:, (POOL_HIST - 1) * W:POOL_HIST * W] = up


def sample_mix(proj_a, proj_b, states, sp, layer):
    nb = DEC_BATCH
    W = GROUP_WIDTH
    f = lambda n: jax.ShapeDtypeStruct((nb, n), F32)
    b = lambda n: jax.ShapeDtypeStruct((nb, n), BF16)
    out_widths = ((SSD_CONV - 1) * XBC_WIDTH, S5_LANES, S5_LANES, (SC_CONV - 1) * W, POOL_HIST * W,
                  W, W, SSD_HEADS * LANE, SSD_GROUPS * LANE, SSD_GROUPS * LANE)
    out_shape = (b(W), b(W), b(W)) + tuple(f(n) for n in out_widths)
    nblk = GROUP_WIDTH // LANE
    in_specs = ([pl.BlockSpec((nb, A_WIDTH), lambda i: (SAMPLE_BLOCK, 0)),
                 pl.BlockSpec((nb, B_WIDTH), lambda i: (SAMPLE_BLOCK, 0))]
                + [_layer_spec((nb, s.shape[2]), layer) for s in states]
                + [_layer_spec((SSD_CONV, XBC_WIDTH), layer), _layer_spec((1, XBC_WIDTH), layer),
                   _layer_spec((1, LANE), layer), _layer_spec((1, LANE), layer),
                   _layer_spec((S5_SEG, S5_LANES), layer), _layer_spec((S5_SEG, S5_LANES), layer)]
                + _s5_param_specs(layer)
                + [_layer_spec((SC_CONV, W), layer), _layer_spec((len(POOL_WINDOWS), POOL_GROUP, POOL_GROUP), layer),
                   _layer_spec((1, W), layer), _layer_spec((1, W), layer), _layer_spec((1, W), layer)])
    return pl.pallas_call(
        _sample_mix_kernel,
        grid=(1,),
        in_specs=in_specs,
        out_specs=tuple(_whole_spec(s.shape) for s in out_shape),
        out_shape=out_shape,
        compiler_params=_cparams(("arbitrary",)),
        name="sample_mix",
    )(proj_a, proj_b, *states, sp["ssd_conv_w"], sp["ssd_conv_b"], sp["ssd_dt_bias"], sp["ssd_a_log"],
      sp["s5_pw_re"], sp["s5_pw_im"], *_s5_param_args(sp),
      sp["sc_conv_w"], sp["pool_w"], sp["pool_scale"], sp["g_sc"], sp["g_pool"])


def _sample_ssd_kernel(h0_ref, xdt_ref, da_ref, b2_ref, c2_ref, e_ref, r_ref, xs_ref, z_ref, dexp_ref, g_ref,
                       hn_ref, mix_ref, xrep_ref, prod_ref, y_ref):
    hp = pl.program_id(0)
    hpl = 2 * SSD_HEAD_DIM * SSD_STATE
    per_head = SSD_HEAD_DIM * SSD_STATE
    x_hi, x_lo = _split_bf16(xdt_ref[...])
    xrep_ref[...] = _dot(x_hi, e_ref[...]) + _dot(x_lo, e_ref[...])
    b2, c2 = b2_ref[...], c2_ref[...]
    for j in range(hpl // LANE):
        ls = slice(j * LANE, (j + 1) * LANE)
        hl = (j * LANE) // per_head
        hn = da_ref[:, hl * LANE:(hl + 1) * LANE] * h0_ref[:, ls] + xrep_ref[:, ls] * b2
        hn_ref[:, ls] = hn
        prod_ref[:, ls] = hn * c2
    p_hi, p_lo = _split_bf16(prod_ref[...])
    y_ref[hp] = _dot(p_hi, r_ref[...]) + _dot(p_lo, r_ref[...])

    @pl.when(hp == pl.num_programs(0) - 1)
    def _():
        y = jnp.concatenate([y_ref[i] for i in range(SSD_HEADS // 2)], axis=-1)
        y = (y + dexp_ref[...] * xs_ref[...]) * _silu(z_ref[...])
        mix_ref[...] = _rms(y, g_ref[...]).astype(mix_ref.dtype)


def sample_ssd(h0_all, xdt, da, b2, c2, xs, proj, sp, layer):
    nb = DEC_BATCH
    npairs = SSD_HEADS // 2
    hpl = 2 * SSD_HEAD_DIM * SSD_STATE
    expand = jnp.repeat(jnp.eye(LANE, dtype=BF16), SSD_STATE, axis=1)
    return pl.pallas_call(
        _sample_ssd_kernel,
        grid=(npairs,),
        in_specs=[pl.BlockSpec((None, nb, hpl), lambda i: (layer, 0, i)),
                  pl.BlockSpec((nb, LANE), lambda i: (0, i)),
                  pl.BlockSpec((nb, 2 * LANE), lambda i: (0, i)),
                  pl.BlockSpec((nb, LANE), lambda i: (0, i // (npairs // SSD_GROUPS))),
                  pl.BlockSpec((nb, LANE), lambda i: (0, i // (npairs // SSD_GROUPS))),
                  _whole_spec((LANE, hpl)),
                  _whole_spec((hpl, LANE)),
                  _whole_spec((nb, GROUP_WIDTH)),
                  pl.BlockSpec((nb, GROUP_WIDTH), lambda i: (SAMPLE_BLOCK, COL_Z // GROUP_WIDTH)),
                  _layer_spec((1, GROUP_WIDTH), layer),
                  _layer_spec((1, GROUP_WIDTH), layer)],
        out_specs=(pl.BlockSpec((nb, hpl), lambda i: (0, i)),
                   _whole_spec((nb, GROUP_WIDTH))),
        out_shape=(jax.ShapeDtypeStruct((nb, SSD_HEADS * SSD_HEAD_DIM * SSD_STATE), F32),
                   jax.ShapeDtypeStruct((nb, GROUP_WIDTH), BF16)),
        scratch_shapes=[pltpu.VMEM((nb, hpl), F32), pltpu.VMEM((nb, hpl), F32),
                        pltpu.VMEM((npairs, nb, LANE), F32)],
        compiler_params=_cparams(("arbitrary",)),
        name="sample_ssd",
    )(h0_all, xdt, da, b2, c2, expand, expand.T, xs, proj, sp["ssd_d_exp"], sp["g_ssd"])


def _split_w_in(w_in):
    head = COL_DT + SSD_HEADS
    part_a = jnp.pad(w_in[..., :head], ((0, 0), (0, 0), (0, A_WIDTH - head)))
    return part_a.astype(BF16), w_in[..., head:].astype(BF16)


def _stacked_params(w):
    row = lambda v: v.reshape(DEPTH, 1, -1)
    pad_heads = lambda v: row(jnp.pad(v, ((0, 0), (0, LANE - SSD_HEADS))))
    g_mix = w["mix_out_g"].reshape(DEPTH, 4, 1, GROUP_WIDTH)
    w_in_a, w_in_b = _split_w_in(w["w_in"])
    sp = dict(w_in_a=w_in_a, w_in_b=w_in_b)
    for k in ("s5_w_glu", "pool_w"):
        sp[k] = w[k].astype(BF16)
    for k in ("w_out", "w_q", "w_k", "w_v", "w_o", "w_up", "w_down"):
        sp[k] = w[k]
    for k in ("norm_mix_g", "norm_xa_g", "norm_mem_g", "norm_mlp_g", "ssd_conv_b", "s5_d", "s5_b_glu", "pool_scale"):
        sp[k] = row(w[k])
    sp.update(ssd_conv_w=w["ssd_conv_w"], sc_conv_w=w["sc_conv_w"],
              ssd_dt_bias=pad_heads(w["ssd_dt_bias"]), ssd_a_log=pad_heads(w["ssd_a_log"]),
              ssd_d_exp=row(jnp.repeat(w["ssd_d"], SSD_HEAD_DIM, axis=1)),
              g_ssd=g_mix[:, 0], g_s5=g_mix[:, 1], g_sc=g_mix[:, 2], g_pool=g_mix[:, 3])
    s5 = s5_prepare(w["s5_lam_re"], w["s5_lam_im"], w["s5_log_dt"], w["s5_b_re"], w["s5_b_im"],
                    w["s5_c_re"], w["s5_c_im"])
    sp.update({"s5_" + k: v for k, v in s5.items()})
    return sp


def _forward(x_prompt, x_sample, mem_prompt, state_ssd_conv, state_ssd, state_s5_re, state_s5_im,
             state_sconv, state_pool, cache_mem_k, cache_mem_v, final_norm_g, w):
    sp = _stacked_params(w)
    flat = lambda s: s.reshape(DEPTH, DEC_BATCH, -1)
    s_states = tuple(flat(s) for s in (state_ssd_conv, state_s5_re, state_s5_im, state_sconv, state_pool))
    s_ssd0 = flat(state_ssd)
    h = jnp.concatenate([x_prompt.reshape(P_ROWS, D_MODEL), x_sample.reshape(DEC_BATCH, D_MODEL)], axis=0)
    mem = mem_prompt.reshape(BATCH * N_MEM, D_MODEL)
    p_out = [[] for _ in range(8)]
    s_out = [[] for _ in range(6)]
    put_sample = lambda full, rows: lax.dynamic_update_slice(full, rows, (P_ROWS, 0))
    for l in range(DEPTH):
        mk = norm_matmul(mem, sp["norm_mem_g"], sp["w_k"], l, tm=BATCH * N_MEM, tn=FF_TILE)
        mv = norm_matmul(mem, sp["norm_mem_g"], sp["w_v"], l, tm=BATCH * N_MEM, tn=FF_TILE)

        proj_a, proj_b = in_proj(h, sp["norm_mix_g"], sp["w_in_a"], sp["w_in_b"], l, tm=IN_ROW_TILE)
        m_ssd, p_conv, p_ssd = prompt_ssd(proj_a, sp, l)
        m_s5, p_s5r, p_s5i, m_sc, m_pl, p_sc, p_pl = prompt_b_mixers(proj_b, sp, l)
        (s_m5, s_msc, s_mpl, s_conv, s_s5r, s_s5i, s_sc, s_pl, xs, xdt, da, b2, c2) = sample_mix(
            proj_a, proj_b, s_states, sp, l)
        s_ssd, s_mssd = sample_ssd(s_ssd0, xdt, da, b2, c2, xs, proj_a, sp, l)
        mixes = (put_sample(m_ssd, s_mssd), put_sample(m_s5, s_m5),
                 put_sample(m_sc, s_msc), put_sample(m_pl, s_mpl))

        h = out_proj(mixes, sp["w_out"], h, l, tm=IN_ROW_TILE)
        q = norm_matmul(h, sp["norm_xa_g"], sp["w_q"], l, out_dtype=BF16, tm=IN_ROW_TILE, tn=D_MODEL)
        o = prompt_attn(q, mk.reshape(BATCH, N_MEM, D_MODEL), mv.reshape(BATCH, N_MEM, D_MODEL), tq=512)
        q_s = q[P_ROWS:].astype(F32).reshape(DEC_BATCH, XA_HEADS, XA_HEAD_DIM)
        o_s = sample_attn(q_s, cache_mem_k, cache_mem_v, l)
        o = put_sample(o, o_s.reshape(DEC_BATCH, D_MODEL))
        h = res_matmul(o, sp["w_o"], h, l, tm=IN_ROW_TILE, tn=D_MODEL)
        h = mlp(h, sp["norm_mlp_g"], sp["w_up"], sp["w_down"], l, tm=MLP_ROW_TILE, tf=FF_TILE)

        for lst, val in zip(p_out, (p_conv, p_ssd, p_s5r, p_s5i, p_sc, p_pl, mk, mv)):
            lst.append(val)
        for lst, val in zip(s_out, (s_conv, s_ssd, s_s5r, s_s5i, s_sc, s_pl)):
            lst.append(val)

    y_prompt = final_norm(h, final_norm_g, rows=P_ROWS, first_block=0, tm=COL_TILE)
    y_sample = final_norm(h, final_norm_g, rows=DEC_BATCH, first_block=SAMPLE_BLOCK, tm=DEC_BATCH)
    p_shapes = ((BATCH, SSD_CONV - 1, XBC_WIDTH), (BATCH, SSD_HEADS, SSD_HEAD_DIM, SSD_STATE),
                (BATCH, S5_GROUPS, S5_STATE), (BATCH, S5_GROUPS, S5_STATE),
                (BATCH, SC_CONV - 1, GROUP_WIDTH), (BATCH, POOL_HIST, GROUP_WIDTH),
                (BATCH, N_MEM, XA_HEADS, XA_HEAD_DIM), (BATCH, N_MEM, XA_HEADS, XA_HEAD_DIM))
    s_shapes = tuple((DEC_BATCH,) + s[1:] for s in p_shapes[:6])
    stack = lambda vals, shape: jnp.stack(vals).reshape((DEPTH,) + shape)
    return ((y_prompt.reshape(BATCH, SEQ, D_MODEL), y_sample.reshape(DEC_BATCH, 1, D_MODEL))
            + tuple(stack(v, s) for v, s in zip(p_out, p_shapes))
            + tuple(stack(v, s) for v, s in zip(s_out, s_shapes)))


_forward_jit = jax.jit(_forward)


def kernel(x_prompt, x_sample, mem_prompt, state_ssd_conv, state_ssd, state_s5_re, state_s5_im, state_sconv, state_pool, cache_mem_k, cache_mem_v, norm_mix_g, w_in, ssd_conv_w, ssd_conv_b, ssd_dt_bias, ssd_a_log, ssd_d, s5_lam_re, s5_lam_im, s5_log_dt, s5_b_re, s5_b_im, s5_c_re, s5_c_im, s5_d, s5_w_glu, s5_b_glu, sc_conv_w, pool_w, pool_scale, mix_out_g, w_out, norm_xa_g, norm_mem_g, w_q, w_k, w_v, w_o, norm_mlp_g, w_up, w_down, final_norm_g):
    w = dict(norm_mix_g=norm_mix_g, w_in=w_in, ssd_conv_w=ssd_conv_w, ssd_conv_b=ssd_conv_b,
             ssd_dt_bias=ssd_dt_bias, ssd_a_log=ssd_a_log, ssd_d=ssd_d, s5_lam_re=s5_lam_re,
             s5_lam_im=s5_lam_im, s5_log_dt=s5_log_dt, s5_b_re=s5_b_re, s5_b_im=s5_b_im,
             s5_c_re=s5_c_re, s5_c_im=s5_c_im, s5_d=s5_d, s5_w_glu=s5_w_glu, s5_b_glu=s5_b_glu,
             sc_conv_w=sc_conv_w, pool_w=pool_w, pool_scale=pool_scale, mix_out_g=mix_out_g,
             w_out=w_out, norm_xa_g=norm_xa_g, norm_mem_g=norm_mem_g, w_q=w_q, w_k=w_k, w_v=w_v,
             w_o=w_o, norm_mlp_g=norm_mlp_g, w_up=w_up, w_down=w_down)
    return _forward_jit(x_prompt, x_sample, mem_prompt, state_ssd_conv, state_ssd, state_s5_re, state_s5_im,
                        state_sconv, state_pool, cache_mem_k, cache_mem_v, final_norm_g, w)
```

```python
import math

import numpy as np
import jax
import jax.numpy as jnp
from jax import lax
from jax.experimental import pallas as pl
from jax.experimental.pallas import tpu as pltpu

F32 = jnp.float32
BF16 = jnp.bfloat16

D_MODEL = 2048
BATCH = 4
SEQ = 2048
DEPTH = 2
DEC_BATCH = 128
PAST_LEN = 16384
GROUP_WIDTH = D_MODEL // 4
SSD_HEAD_DIM = 64
SSD_HEADS = GROUP_WIDTH // SSD_HEAD_DIM
SSD_GROUPS = 2
SSD_STATE = 64
SSD_CONV = 4
SSD_CHUNK = 128
XBC_WIDTH = GROUP_WIDTH + 2 * SSD_GROUPS * SSD_STATE
S5_CH = 16
S5_GROUPS = GROUP_WIDTH // S5_CH
S5_STATE = 64
S5_LANES = S5_GROUPS * S5_STATE
SC_CONV = 3
POOL_WINDOWS = (2, 4, 8, 16)
POOL_GROUP = GROUP_WIDTH // len(POOL_WINDOWS)
POOL_HIST = max(POOL_WINDOWS) - 1
N_MEM = 256
XA_HEADS = 4
XA_HEAD_DIM = D_MODEL // XA_HEADS
D_FF = 4 * D_MODEL
EPS = 1e-6
P_ROWS = BATCH * SEQ
ALL_ROWS = P_ROWS + DEC_BATCH

LANE = 128
SUBLANE = 8
VMEM_LIMIT = 56 * 1024 * 1024

COL_Z = 0
COL_XBC = 512
COL_DT = 1280
A_WIDTH = 1536
ZX_WIDTH = COL_DT
COL_U5 = 0
COL_GB = 512
COL_GC = 1024
COL_HV = 1536
COL_UP = 2048
B_WIDTH = 2560

RESIDENT_ROW_TILE = ALL_ROWS // 13
MLP_ROW_TILE = ALL_ROWS // 10
FF_TILE = 512
MEM_COL_TILE = 512
FINAL_ROW_TILE = 1024
SAMPLE_BLOCK = P_ROWS // DEC_BATCH

SAMPLE_ATTN_ROWS = 2
S5_CHUNK = 256
S5_SEG = S5_CHUNK // SUBLANE
S5_SCAN_LANES = 512


def _cparams(sem):
    return pltpu.CompilerParams(dimension_semantics=sem, vmem_limit_bytes=VMEM_LIMIT)


def _layer_spec(shape, layer):
    zeros = (0,) * len(shape)
    return pl.BlockSpec((None,) + tuple(shape), lambda *_: (layer,) + zeros)


def _whole_spec(shape):
    zeros = (0,) * len(shape)
    return pl.BlockSpec(tuple(shape), lambda *_: zeros)


def _weight_cols_spec(k, n, tn, layer):
    mode = pl.Buffered(1) if tn == n else None
    return pl.BlockSpec((None, k, tn), lambda i, j: (layer, 0, j), pipeline_mode=mode)


def _prompt_row_block(b, t, steps, rows):
    return jnp.minimum(b * steps + t, P_ROWS // rows)


def _prompt_batch(b):
    return jnp.minimum(b, BATCH - 1)


def _prompt_step(body, row_outputs):
    b = pl.program_id(0)
    pl.when(b < BATCH)(body)

    @pl.when(b == BATCH)
    def _():
        for ref in row_outputs:
            ref[...] = jnp.zeros_like(ref)


def _sigmoid(x):
    return 1.0 / (1.0 + jnp.exp(-x))


def _silu(x):
    return x * _sigmoid(x)


def _softplus(x):
    return jnp.maximum(x, 0.0) + jnp.log(1.0 + jnp.exp(-jnp.abs(x)))


def _gelu_tanh(x):
    return 0.5 * x * (1.0 + jnp.tanh(math.sqrt(2.0 / math.pi) * (x + 0.044715 * (x * x * x))))


def _rms(x, g):
    return x * lax.rsqrt(jnp.mean(x * x, axis=-1, keepdims=True) + EPS) * g


def _dot(a, b):
    return jnp.dot(a, b, preferred_element_type=F32)


def _split3_bf16(x):
    hi = x.astype(BF16)
    r = x - hi.astype(F32)
    mid = r.astype(BF16)
    lo = (r - mid.astype(F32)).astype(BF16)
    return hi, mid, lo


def _spread_exact(x, sel):
    hi, mid, lo = _split3_bf16(x)
    return _dot(hi, sel) + _dot(mid, sel) + _dot(lo, sel)


def _sum_rows_exact(sel, x):
    hi, mid, lo = _split3_bf16(x)
    return _dot(sel, hi) + _dot(sel, mid) + _dot(sel, lo)


def _split_bf16(x):
    hi = x.astype(BF16)
    lo = (x - hi.astype(F32)).astype(BF16)
    return hi, lo


def _norm_matmul_kernel(x_ref, g_ref, w_ref, o_ref, xn_ref):
    @pl.when(pl.program_id(1) == 0)
    def _():
        xn_ref[...] = _rms(x_ref[...], g_ref[...]).astype(BF16)

    o_ref[...] = _dot(xn_ref[...], w_ref[...].astype(BF16)).astype(o_ref.dtype)


def norm_matmul(x, g, w, layer, *, out_dtype=F32, tm, tn):
    m, k = x.shape
    n = w.shape[2]
    return pl.pallas_call(
        _norm_matmul_kernel,
        grid=(m // tm, n // tn),
        in_specs=[pl.BlockSpec((tm, k), lambda i, j: (i, 0)),
                  _layer_spec((1, k), layer),
                  _weight_cols_spec(k, n, tn, layer)],
        out_specs=pl.BlockSpec((tm, tn), lambda i, j: (i, j)),
        out_shape=jax.ShapeDtypeStruct((m, n), out_dtype),
        scratch_shapes=[pltpu.VMEM((tm, k), BF16)],
        compiler_params=_cparams(("parallel", "arbitrary")),
        name="norm_matmul",
    )(x, g, w)


def _in_proj_kernel(x_ref, g_ref, wa_ref, wb_ref, oa_ref, ob_ref):
    xn = _rms(x_ref[...], g_ref[...]).astype(BF16)
    oa_ref[...] = _dot(xn, wa_ref[...])
    ob_ref[...] = _dot(xn, wb_ref[...])


def in_proj(x, g, wa, wb, layer, *, tm):
    m, k = x.shape
    resident = lambda n: pl.BlockSpec((None, k, n), lambda i: (layer, 0, 0), pipeline_mode=pl.Buffered(1))
    return pl.pallas_call(
        _in_proj_kernel,
        grid=(m // tm,),
        in_specs=[pl.BlockSpec((tm, k), lambda i: (i, 0)), _layer_spec((1, k), layer),
                  resident(A_WIDTH), resident(B_WIDTH)],
        out_specs=(pl.BlockSpec((tm, A_WIDTH), lambda i: (i, 0)), pl.BlockSpec((tm, B_WIDTH), lambda i: (i, 0))),
        out_shape=(jax.ShapeDtypeStruct((m, A_WIDTH), F32), jax.ShapeDtypeStruct((m, B_WIDTH), F32)),
        compiler_params=_cparams(("parallel",)),
        name="in_proj",
    )(x, g, wa, wb)


def _res_matmul_kernel(x_ref, w_ref, r_ref, o_ref):
    o_ref[...] = r_ref[...] + _dot(x_ref[...], w_ref[...].astype(BF16))


def res_matmul(x, w, res, layer, *, tm, tn):
    m, k = x.shape
    n = w.shape[2]
    return pl.pallas_call(
        _res_matmul_kernel,
        grid=(m // tm, n // tn),
        in_specs=[pl.BlockSpec((tm, k), lambda i, j: (i, 0)),
                  _weight_cols_spec(k, n, tn, layer),
                  pl.BlockSpec((tm, tn), lambda i, j: (i, j))],
        out_specs=pl.BlockSpec((tm, tn), lambda i, j: (i, j)),
        out_shape=jax.ShapeDtypeStruct((m, n), F32),
        compiler_params=_cparams(("parallel", "arbitrary")),
        name="res_matmul",
    )(x, w, res)


def _out_proj_kernel(m0_ref, m1_ref, m2_ref, m3_ref, w_ref, r_ref, o_ref):
    x = jnp.concatenate([m0_ref[...], m1_ref[...], m2_ref[...], m3_ref[...]], axis=-1)
    o_ref[...] = r_ref[...] + _dot(x, w_ref[...].astype(BF16))


def out_proj(mixes, w, res, layer, *, tm):
    m = res.shape[0]
    mix_spec = pl.BlockSpec((tm, GROUP_WIDTH), lambda i, j: (i, 0))
    row_spec = pl.BlockSpec((tm, D_MODEL), lambda i, j: (i, 0))
    return pl.pallas_call(
        _out_proj_kernel,
        grid=(m // tm, 1),
        in_specs=[mix_spec] * 4 + [_weight_cols_spec(D_MODEL, D_MODEL, D_MODEL, layer), row_spec],
        out_specs=row_spec,
        out_shape=jax.ShapeDtypeStruct((m, D_MODEL), F32),
        compiler_params=_cparams(("parallel", "arbitrary")),
        name="out_proj",
    )(*mixes, w, res)


def _mlp_kernel(x_ref, g_ref, wu_ref, wd_ref, o_ref, xn_ref):
    j = pl.program_id(1)

    @pl.when(j == 0)
    def _():
        x = x_ref[...]
        xn_ref[...] = _rms(x, g_ref[...]).astype(BF16)
        o_ref[...] = x

    a = jnp.square(jnp.maximum(_dot(xn_ref[...], wu_ref[...].astype(BF16)), 0.0))
    o_ref[...] += _dot(a.astype(BF16), wd_ref[...].astype(BF16))


def mlp(x, g, w_up, w_down, layer, *, tm, tf):
    m = x.shape[0]
    return pl.pallas_call(
        _mlp_kernel,
        grid=(m // tm, D_FF // tf),
        in_specs=[pl.BlockSpec((tm, D_MODEL), lambda i, j: (i, 0)),
                  _layer_spec((1, D_MODEL), layer),
                  pl.BlockSpec((None, D_MODEL, tf), lambda i, j: (layer, 0, j)),
                  pl.BlockSpec((None, tf, D_MODEL), lambda i, j: (layer, j, 0))],
        out_specs=pl.BlockSpec((tm, D_MODEL), lambda i, j: (i, 0)),
        out_shape=jax.ShapeDtypeStruct((m, D_MODEL), F32),
        scratch_shapes=[pltpu.VMEM((tm, D_MODEL), BF16)],
        compiler_params=_cparams(("parallel", "arbitrary")),
        name="mlp",
    )(x, g, w_up, w_down)


def _final_norm_kernel(x_ref, g_ref, o_ref):
    o_ref[...] = _rms(x_ref[...], g_ref[...])


def final_norm(x, g, *, rows, first_block, tm):
    return pl.pallas_call(
        _final_norm_kernel,
        grid=(rows // tm,),
        in_specs=[pl.BlockSpec((tm, D_MODEL), lambda i: (first_block + i, 0)), _whole_spec((1, D_MODEL))],
        out_specs=pl.BlockSpec((tm, D_MODEL), lambda i: (i, 0)),
        out_shape=jax.ShapeDtypeStruct((rows, D_MODEL), F32),
        compiler_params=_cparams(("parallel",)),
        name="final_norm",
    )(x, g.reshape(1, D_MODEL))


def _prompt_attn_kernel(q_ref, k_ref, v_ref, o_ref):
    _prompt_step(lambda: _prompt_attn_body(q_ref, k_ref, v_ref, o_ref), (o_ref,))


def _prompt_attn_body(q_ref, k_ref, v_ref, o_ref):
    scale = XA_HEAD_DIM ** -0.5
    for h in range(XA_HEADS):
        cs = slice(h * XA_HEAD_DIM, (h + 1) * XA_HEAD_DIM)
        kh = k_ref[0, :, cs].astype(BF16)
        vh = v_ref[0, :, cs].astype(BF16)
        s = lax.dot_general(q_ref[:, cs], kh, (((1,), (1,)), ((), ())), preferred_element_type=F32) * scale
        p = jnp.exp(s - jnp.max(s, axis=-1, keepdims=True))
        p = p / jnp.sum(p, axis=-1, keepdims=True)
        o_ref[:, cs] = _dot(p.astype(BF16), vh).astype(o_ref.dtype)


def prompt_attn(q, k, v, *, tq):
    nq = SEQ // tq
    return pl.pallas_call(
        _prompt_attn_kernel,
        grid=(BATCH + 1, nq),
        in_specs=[pl.BlockSpec((tq, D_MODEL), lambda b, i: (_prompt_row_block(b, i, nq, tq), 0)),
                  pl.BlockSpec((1, N_MEM, D_MODEL), lambda b, i: (_prompt_batch(b), 0, 0)),
                  pl.BlockSpec((1, N_MEM, D_MODEL), lambda b, i: (_prompt_batch(b), 0, 0))],
        out_specs=pl.BlockSpec((tq, D_MODEL), lambda b, i: (_prompt_row_block(b, i, nq, tq), 0)),
        out_shape=jax.ShapeDtypeStruct((ALL_ROWS, D_MODEL), BF16),
        compiler_params=_cparams(("arbitrary", "arbitrary")),
        name="prompt_attn",
    )(q, k, v)


def _sample_attn_kernel(q_ref, k_ref, v_ref, o_ref):
    scale = XA_HEAD_DIM ** -0.5
    half, quarter = XA_HEAD_DIM // 2, XA_HEAD_DIM // 4
    for r in range(SAMPLE_ATTN_ROWS):
        prod = k_ref[r] * (q_ref[r] * scale)
        fold = prod[:, :, :half] + prod[:, :, half:]
        fold = fold[:, :, :quarter] + fold[:, :, quarter:]
        s = jnp.sum(fold, axis=-1, keepdims=True)
        e = jnp.exp(s - jnp.max(s, axis=0, keepdims=True))
        o = jnp.sum(e * v_ref[r], axis=0) / jnp.sum(e, axis=0)
        o_ref[r] = o.astype(o_ref.dtype)


def sample_attn(q, cache_k, cache_v, layer):
    nr = SAMPLE_ATTN_ROWS
    row = pl.BlockSpec((nr, XA_HEADS, XA_HEAD_DIM), lambda b: (b, 0, 0))
    mem = pl.BlockSpec((None, nr, N_MEM, XA_HEADS, XA_HEAD_DIM), lambda b: (layer, b, 0, 0, 0))
    return pl.pallas_call(
        _sample_attn_kernel,
        grid=(DEC_BATCH // nr,),
        in_specs=[row, mem, mem],
        out_specs=row,
        out_shape=jax.ShapeDtypeStruct((DEC_BATCH, XA_HEADS, XA_HEAD_DIM), BF16),
        compiler_params=_cparams(("parallel",)),
        name="sample_attn",
    )(q, cache_k, cache_v)


def _s5_abar(lr, li, ldt):
    delta = jnp.exp(ldt)
    mag = jnp.exp(lr * delta)
    return mag * jnp.cos(li * delta), mag * jnp.sin(li * delta)


def _s5_bbar_kernel(lr_ref, li_ref, ldt_ref, bre_ref, bim_ref, bbr_ref, bbi_ref):
    lr, li = lr_ref[...], li_ref[...]
    ar, ai = _s5_abar(lr, li, ldt_ref[...])
    den = lr * lr + li * li
    cr = ((ar - 1.0) * lr + ai * li) / den
    ci = (ai * lr - (ar - 1.0) * li) / den
    br, bi = bre_ref[...], bim_ref[...]
    bbr_ref[...] = cr * br - ci * bi
    bbi_ref[...] = cr * bi + ci * br


def _s5_pow_kernel(lr_ref, li_ref, ldt_ref, pr_ref, pi_ref):
    ar, ai = _s5_abar(lr_ref[...], li_ref[...], ldt_ref[...])
    qr, qi = ar, ai
    pr_ref[0] = qr
    pi_ref[0] = qi
    for e in range(1, S5_SEG):
        qr, qi = qr * ar - qi * ai, qr * ai + qi * ar
        pr_ref[e] = qr
        pi_ref[e] = qi


def s5_prepare(lam_re, lam_im, log_dt, b_re, b_im, c_re, c_im):
    dg = DEPTH * S5_GROUPS
    ldt = jnp.broadcast_to(log_dt[..., None], (DEPTH, S5_GROUPS, S5_STATE))
    rep = lambda a: jnp.repeat(a.reshape(dg, S5_STATE), S5_CH, axis=0)
    to_rows = lambda b: jnp.transpose(b, (0, 1, 3, 2)).reshape(dg * S5_CH, S5_STATE)
    shp = jax.ShapeDtypeStruct((dg * S5_CH, S5_STATE), F32)
    bbr, bbi = pl.pallas_call(_s5_bbar_kernel, out_shape=(shp, shp), name="s5_bbar")(
        rep(lam_re), rep(lam_im), rep(ldt), to_rows(b_re), to_rows(b_im))
    pshp = jax.ShapeDtypeStruct((S5_SEG, dg, S5_STATE), F32)
    pr, pi = pl.pallas_call(_s5_pow_kernel, out_shape=(pshp, pshp), name="s5_pow")(
        lam_re.reshape(dg, S5_STATE), lam_im.reshape(dg, S5_STATE), ldt.reshape(dg, S5_STATE))

    gpb = LANE // S5_CH
    nblk = S5_GROUPS // gpb
    eye = jnp.eye(gpb, dtype=F32)

    def b_blocks(bb):
        bb = bb.reshape(DEPTH, nblk, gpb, S5_CH, S5_STATE)
        return jnp.einsum("digkn,gh->digkhn", bb, eye).reshape(DEPTH, nblk, LANE, gpb * S5_STATE).astype(BF16)

    def c_blocks(cc):
        cc = cc.reshape(DEPTH, nblk, gpb, S5_CH, S5_STATE)
        return jnp.einsum("digkn,gh->dignhk", cc, eye).reshape(DEPTH, nblk, gpb * S5_STATE, LANE).astype(BF16)

    def pow_rows(p):
        return jnp.transpose(p.reshape(S5_SEG, DEPTH, S5_LANES), (1, 0, 2))

    pw_r, pw_i = pow_rows(pr), pow_rows(pi)
    tile = lambda p: jnp.broadcast_to(p[:, :, None, :], (DEPTH, S5_SEG, SUBLANE, S5_LANES))
    return dict(b_re=b_blocks(bbr), b_im=b_blocks(bbi), c_re=c_blocks(c_re), c_imn=c_blocks(-c_im),
                pw_re=pw_r, pw_im=pw_i, pwt_re=tile(pw_r), pwt_im=tile(pw_i))


def _s5_bu(ub, bre_ref, bim_ref):
    res_r, res_i = [], []
    for i in range(GROUP_WIDTH // LANE):
        ui = ub[:, i * LANE:(i + 1) * LANE]
        res_r.append(_dot(ui, bre_ref[i]))
        res_i.append(_dot(ui, bim_ref[i]))
    return res_r, res_i


def _s5_tail(u, hs_re, hs_im, cre_ref, cimn_ref, d_ref, wglu_ref, bglu_ref, g_ref):
    ys = []
    for i in range(GROUP_WIDTH // LANE):
        ys.append(_dot(hs_re(i).astype(BF16), cre_ref[i]) + _dot(hs_im(i).astype(BF16), cimn_ref[i]))
    y = jnp.concatenate(ys, axis=-1) + d_ref[...] * u
    y = _gelu_tanh(y)
    y = y * _sigmoid(_dot(y.astype(BF16), wglu_ref[...]) + bglu_ref[...])
    return _rms(y, g_ref[...])


def _s5_param_specs(layer):
    nblk = GROUP_WIDTH // LANE
    sblk = S5_LANES // nblk
    return [_layer_spec((nblk, LANE, sblk), layer),
            _layer_spec((nblk, LANE, sblk), layer),
            _layer_spec((nblk, sblk, LANE), layer),
            _layer_spec((nblk, sblk, LANE), layer),
            _layer_spec((1, GROUP_WIDTH), layer),
            _layer_spec((GROUP_WIDTH, GROUP_WIDTH), layer),
            _layer_spec((1, GROUP_WIDTH), layer),
            _layer_spec((1, GROUP_WIDTH), layer)]


def _s5_param_args(sp):
    return (sp["s5_b_re"], sp["s5_b_im"], sp["s5_c_re"], sp["s5_c_imn"], sp["s5_d"], sp["s5_w_glu"],
            sp["s5_b_glu"], sp["g_s5"])


def _cmul_add(x_r, x_i, a_r, a_i, h_r, h_i):
    return x_r + a_r * h_r - a_i * h_i, x_i + a_r * h_i + a_i * h_r


def _prompt_s5_body(u_ref, perm_ref, unperm_ref, pw_re_ref, pw_im_ref, pwt_re_ref, pwt_im_ref,
                    bre_ref, bim_ref, cre_ref, cimn_ref, d_ref, wglu_ref, bglu_ref, g_ref,
                    mix_ref, sre_ref, sim_ref, hre_ref, him_ref, cr_ref, ci_ref):
    sblk = S5_LANES // (GROUP_WIDTH // LANE)
    W = S5_SCAN_LANES

    @pl.when(pl.program_id(1) == 0)
    def _():
        cr_ref[...] = jnp.zeros_like(cr_ref)
        ci_ref[...] = jnp.zeros_like(ci_ref)

    u_hi, u_lo = _split_bf16(u_ref[...])
    ub = _dot(perm_ref[...], u_hi)
    u = ub + _dot(perm_ref[...], u_lo)
    bu_r, bu_i = _s5_bu(ub.astype(BF16), bre_ref, bim_ref)
    for i in range(len(bu_r)):
        hre_ref[:, i * sblk:(i + 1) * sblk] = bu_r[i]
        him_ref[:, i * sblk:(i + 1) * sblk] = bu_i[i]

    row = lax.broadcasted_iota(jnp.int32, (SUBLANE, W), 0)
    for lb in range(0, S5_LANES, W):
        ls = slice(lb, lb + W)
        a_r = jnp.broadcast_to(pw_re_ref[0:1, ls], (SUBLANE, W))
        a_i = jnp.broadcast_to(pw_im_ref[0:1, ls], (SUBLANE, W))

        def local_step(j, h, ls=ls, a_r=a_r, a_i=a_i):
            rs = pl.ds(pl.multiple_of(j * SUBLANE, SUBLANE), SUBLANE)
            n_r, n_i = _cmul_add(hre_ref[rs, ls], him_ref[rs, ls], a_r, a_i, h[0], h[1])
            hre_ref[rs, ls] = n_r
            him_ref[rs, ls] = n_i
            return n_r, n_i

        zero = jnp.zeros((SUBLANE, W), F32)
        e_r, e_i = lax.fori_loop(0, S5_SEG, local_step, (zero, zero))

        s_r = pw_re_ref[S5_SEG - 1:S5_SEG, ls]
        s_i = pw_im_ref[S5_SEG - 1:S5_SEG, ls]
        c_r, c_i = cr_ref[:, ls], ci_ref[:, ls]
        in_r, in_i = zero, zero
        for s in range(SUBLANE):
            in_r = jnp.where(row == s, c_r, in_r)
            in_i = jnp.where(row == s, c_i, in_i)
            c_r, c_i = _cmul_add(e_r[s:s + 1, :], e_i[s:s + 1, :], s_r, s_i, c_r, c_i)
        cr_ref[:, ls] = c_r
        ci_ref[:, ls] = c_i

        def fix_step(j, carry, ls=ls, in_r=in_r, in_i=in_i):
            rs = pl.ds(pl.multiple_of(j * SUBLANE, SUBLANE), SUBLANE)
            n_r, n_i = _cmul_add(hre_ref[rs, ls], him_ref[rs, ls], pwt_re_ref[j, :, ls], pwt_im_ref[j, :, ls],
                                 in_r, in_i)
            hre_ref[rs, ls] = n_r
            him_ref[rs, ls] = n_i
            return carry

        lax.fori_loop(0, S5_SEG, fix_step, 0)

    sre_ref[0] = cr_ref[...]
    sim_ref[0] = ci_ref[...]
    y = _s5_tail(u, lambda i: hre_ref[:, i * sblk:(i + 1) * sblk], lambda i: him_ref[:, i * sblk:(i + 1) * sblk],
                 cre_ref, cimn_ref, d_ref, wglu_ref, bglu_ref, g_ref)
    mix_ref[...] = _dot(unperm_ref[...], y.astype(mix_ref.dtype)).astype(mix_ref.dtype)


def _prompt_ssd_kernel(*refs):
    _prompt_step(lambda: _prompt_ssd_body(*refs), (refs[10],))


def _prompt_ssd_body(zx_ref, dt_ref, cw_ref, cb_ref, dtb_ref, alog_ref, dexp_ref, g_ref, hexp_ref, hcol_ref,
                     mix_ref, cst_ref, hst_ref, xbuf_ref, h_ref):
    L = SSD_CHUNK
    hist = SSD_CONV - 1
    base = SUBLANE

    @pl.when(pl.program_id(1) == 0)
    def _():
        xbuf_ref[0:base, :] = jnp.zeros((base, XBC_WIDTH), F32)
        h_ref[...] = jnp.zeros_like(h_ref)

    @pl.when(pl.program_id(1) > 0)
    def _():
        xbuf_ref[base - hist:base, :] = xbuf_ref[base + L - hist:base + L, :]

    xbc = zx_ref[:, COL_XBC:COL_XBC + XBC_WIDTH]
    xbuf_ref[base:base + L, :] = xbc
    cst_ref[0] = xbuf_ref[base + L - hist:base + L, :]
    conv = cb_ref[...] + cw_ref[hist:hist + 1, :] * xbc
    for k in range(hist):
        conv = conv + cw_ref[k:k + 1, :] * xbuf_ref[base - hist + k:base - hist + k + L, :]
    xc = _silu(conv)
    xs = xc[:, :GROUP_WIDTH]
    ng = SSD_GROUPS * SSD_STATE
    bm = xc[:, GROUP_WIDTH:GROUP_WIDTH + ng].astype(BF16)
    cm = xc[:, GROUP_WIDTH + ng:].astype(BF16)

    dt = _softplus(dt_ref[...] + dtb_ref[...])
    a = -jnp.exp(alog_ref[...])
    ri = lax.broadcasted_iota(jnp.int32, (L, L), 0)
    ci = lax.broadcasted_iota(jnp.int32, (L, L), 1)
    causal = ri >= ci
    acum = _sum_rows_exact(causal.astype(BF16), dt * a)
    acum_t = acum.T
    last = acum[L - 1:L, :]
    to_end = jnp.exp(last - acum)
    e_acum = jnp.exp(acum)
    chunk_decay = jnp.exp(last)

    xdt_all = xs * _spread_exact(dt, hexp_ref[...])
    xdt_end = (xdt_all * _spread_exact(to_end, hexp_ref[...])).astype(BF16)
    xdt_all = xdt_all.astype(BF16)
    e_acum_all = _spread_exact(e_acum, hexp_ref[...])
    acum_cols = _spread_exact(acum, hcol_ref[...])

    ys_diag, ys_off = [], []
    rep = SSD_HEADS // SSD_GROUPS
    cb = [lax.dot_general(cm[:, g * SSD_STATE:(g + 1) * SSD_STATE], bm[:, g * SSD_STATE:(g + 1) * SSD_STATE],
                          (((1,), (1,)), ((), ())), preferred_element_type=F32) for g in range(SSD_GROUPS)]
    for h in range(SSD_HEADS):
        g = h // rep
        hs = slice(h * SSD_HEAD_DIM, (h + 1) * SSD_HEAD_DIM)
        gs = slice(g * SSD_STATE, (g + 1) * SSD_STATE)
        seg = acum_cols[:, h * L:(h + 1) * L] - acum_t[h:h + 1, :]
        decay = jnp.exp(jnp.where(causal, seg, -jnp.inf))
        ys_diag.append(_dot((cb[g] * decay).astype(BF16), xdt_all[:, hs]))
        h_prev = h_ref[h]
        ys_off.append(lax.dot_general(cm[:, gs], h_prev.astype(BF16), (((1,), (1,)), ((), ())),
                                      preferred_element_type=F32))
        st = lax.dot_general(xdt_end[:, hs], bm[:, gs], (((0,), (0,)), ((), ())), preferred_element_type=F32)
        h_ref[h] = h_prev * chunk_decay[:, h:h + 1] + st
    y = jnp.concatenate(ys_diag, axis=-1) + jnp.concatenate(ys_off, axis=-1) * e_acum_all
    y = (y + dexp_ref[...] * xs) * _silu(zx_ref[:, COL_Z:COL_Z + GROUP_WIDTH])
    mix_ref[...] = _rms(y, g_ref[...]).astype(mix_ref.dtype)
    hst_ref[0] = h_ref[...]


def prompt_ssd(proj, sp, layer):
    nc = SEQ // SSD_CHUNK
    rb = lambda b, c: _prompt_row_block(b, c, nc, SSD_CHUNK)
    head_of_row = np.arange(LANE)[:, None]
    head_lanes = jnp.asarray(head_of_row == np.arange(GROUP_WIDTH)[None, :] // SSD_HEAD_DIM, BF16)
    head_cols = jnp.asarray(head_of_row == np.arange(SSD_HEADS * SSD_CHUNK)[None, :] // SSD_CHUNK, BF16)
    return pl.pallas_call(
        _prompt_ssd_kernel,
        grid=(BATCH + 1, nc),
        in_specs=[pl.BlockSpec((SSD_CHUNK, ZX_WIDTH), lambda b, c: (rb(b, c), 0)),
                  pl.BlockSpec((SSD_CHUNK, LANE), lambda b, c: (rb(b, c), COL_DT // LANE)),
                  _layer_spec((SSD_CONV, XBC_WIDTH), layer),
                  _layer_spec((1, XBC_WIDTH), layer),
                  _layer_spec((1, LANE), layer),
                  _layer_spec((1, LANE), layer),
                  _layer_spec((1, GROUP_WIDTH), layer),
                  _layer_spec((1, GROUP_WIDTH), layer),
                  _whole_spec(head_lanes.shape), _whole_spec(head_cols.shape)],
        out_specs=(pl.BlockSpec((SSD_CHUNK, GROUP_WIDTH), lambda b, c: (rb(b, c), 0)),
                   pl.BlockSpec((1, SSD_CONV - 1, XBC_WIDTH), lambda b, c: (_prompt_batch(b), 0, 0)),
                   pl.BlockSpec((1, SSD_HEADS, SSD_HEAD_DIM, SSD_STATE), lambda b, c: (_prompt_batch(b), 0, 0, 0))),
        out_shape=(jax.ShapeDtypeStruct((ALL_ROWS, GROUP_WIDTH), BF16),
                   jax.ShapeDtypeStruct((BATCH, SSD_CONV - 1, XBC_WIDTH), F32),
                   jax.ShapeDtypeStruct((BATCH, SSD_HEADS, SSD_HEAD_DIM, SSD_STATE), F32)),
        scratch_shapes=[pltpu.VMEM((SUBLANE + SSD_CHUNK, XBC_WIDTH), F32),
                        pltpu.VMEM((SSD_HEADS, SSD_HEAD_DIM, SSD_STATE), F32)],
        compiler_params=_cparams(("arbitrary", "arbitrary")),
        name="prompt_ssd",
    )(proj, proj, sp["ssd_conv_w"], sp["ssd_conv_b"], sp["ssd_dt_bias"], sp["ssd_a_log"],
      sp["ssd_d_exp"], sp["g_ssd"], head_lanes, head_cols)


def _pool_counts(pos, w):
    return jnp.minimum(w, pos + 1).astype(F32)


def _prompt_convpool_body(gb_ref, gc_ref, hv_ref, up_ref, scw_ref, pw_ref, ps_ref, gsc_ref, gpl_ref,
                          msc_ref, mpl_ref, scst_ref, plst_ref, vbuf_ref, pbuf_ref):
    rows = gb_ref.shape[0]
    vb = SUBLANE
    pb = 2 * SUBLANE
    vh = SC_CONV - 1
    t = pl.program_id(1)

    @pl.when(t == 0)
    def _():
        vbuf_ref[0:vb, :] = jnp.zeros((vb, GROUP_WIDTH), F32)
        pbuf_ref[0:pb, :] = jnp.zeros((pb, GROUP_WIDTH), F32)

    @pl.when(t > 0)
    def _():
        vbuf_ref[vb - vh:vb, :] = vbuf_ref[vb + rows - vh:vb + rows, :]
        pbuf_ref[0:pb, :] = pbuf_ref[rows:rows + pb, :]

    v = gc_ref[...] * hv_ref[...]
    vbuf_ref[vb:vb + rows, :] = v
    acc = scw_ref[vh:vh + 1, :] * v
    for k in range(vh):
        acc = acc + scw_ref[k:k + 1, :] * vbuf_ref[vb - vh + k:vb - vh + k + rows, :]
    msc_ref[...] = _rms(gb_ref[...] * acc, gsc_ref[...]).astype(msc_ref.dtype)
    scst_ref[0] = vbuf_ref[vb + rows - vh:vb + rows, :]

    u = up_ref[...]
    pbuf_ref[pb:pb + rows, :] = u
    pos = t * rows + lax.broadcasted_iota(jnp.int32, (rows, 1), 0)
    ys = []
    for gi, w in enumerate(POOL_WINDOWS):
        cs = slice(gi * POOL_GROUP, (gi + 1) * POOL_GROUP)
        s = u[:, cs]
        for j in range(1, w):
            s = s + pbuf_ref[pb - j:pb - j + rows, cs]
        pooled = s / _pool_counts(pos, w) - u[:, cs]
        ys.append(_dot(pooled.astype(BF16), pw_ref[gi]))
    y = jnp.concatenate(ys, axis=-1) * ps_ref[...]
    mpl_ref[...] = _rms(y, gpl_ref[...]).astype(mpl_ref.dtype)
    plst_ref[0] = pbuf_ref[pb + rows - POOL_HIST:pb + rows, :]


N_S5_IN, N_S5_OUT, N_S5_SCRATCH = 15, 3, 4
N_CP_IN, N_CP_OUT = 9, 4


def _prompt_b_mixers_kernel(*refs):
    s5_in, refs = refs[:N_S5_IN], refs[N_S5_IN:]
    cp_in, refs = refs[:N_CP_IN], refs[N_CP_IN:]
    s5_out, refs = refs[:N_S5_OUT], refs[N_S5_OUT:]
    cp_out, refs = refs[:N_CP_OUT], refs[N_CP_OUT:]
    s5_scratch, cp_scratch = refs[:N_S5_SCRATCH], refs[N_S5_SCRATCH:]

    def body():
        _prompt_s5_body(*s5_in, *s5_out, *s5_scratch)
        _prompt_convpool_body(*cp_in, *cp_out, *cp_scratch)

    _prompt_step(body, (s5_out[0], cp_out[0], cp_out[1]))


def prompt_b_mixers(proj, sp, layer):
    nt = SEQ // S5_CHUNK
    t_of_row = (np.arange(S5_CHUNK) % SUBLANE) * S5_SEG + np.arange(S5_CHUNK) // SUBLANE
    perm = jnp.asarray(np.eye(S5_CHUNK, dtype=np.float32)[t_of_row], BF16)
    rb = lambda b, t: _prompt_row_block(b, t, nt, S5_CHUNK)
    col = lambda c: pl.BlockSpec((S5_CHUNK, GROUP_WIDTH), lambda b, t: (rb(b, t), c // GROUP_WIDTH))
    mix_spec = pl.BlockSpec((S5_CHUNK, GROUP_WIDTH), lambda b, t: (rb(b, t), 0))
    mix_shape = jax.ShapeDtypeStruct((ALL_ROWS, GROUP_WIDTH), BF16)
    seq_spec = lambda *s: pl.BlockSpec((1,) + s, lambda b, t: (_prompt_batch(b),) + (0,) * len(s))
    seq_shape = lambda *s: jax.ShapeDtypeStruct((BATCH,) + s, F32)
    s5_in_specs = [col(COL_U5), _whole_spec((S5_CHUNK, S5_CHUNK)), _whole_spec((S5_CHUNK, S5_CHUNK)),
                   _layer_spec((S5_SEG, S5_LANES), layer), _layer_spec((S5_SEG, S5_LANES), layer),
                   _layer_spec((S5_SEG, SUBLANE, S5_LANES), layer),
                   _layer_spec((S5_SEG, SUBLANE, S5_LANES), layer)] + _s5_param_specs(layer)
    cp_in_specs = [col(COL_GB), col(COL_GC), col(COL_HV), col(COL_UP),
                   _layer_spec((SC_CONV, GROUP_WIDTH), layer),
                   _layer_spec((len(POOL_WINDOWS), POOL_GROUP, POOL_GROUP), layer),
                   _layer_spec((1, GROUP_WIDTH), layer), _layer_spec((1, GROUP_WIDTH), layer),
                   _layer_spec((1, GROUP_WIDTH), layer)]
    assert len(s5_in_specs) == N_S5_IN and len(cp_in_specs) == N_CP_IN
    return pl.pallas_call(
        _prompt_b_mixers_kernel,
        grid=(BATCH + 1, nt),
        in_specs=s5_in_specs + cp_in_specs,
        out_specs=(mix_spec, seq_spec(1, S5_LANES), seq_spec(1, S5_LANES),
                   mix_spec, mix_spec, seq_spec(SC_CONV - 1, GROUP_WIDTH), seq_spec(POOL_HIST, GROUP_WIDTH)),
        out_shape=(mix_shape, seq_shape(1, S5_LANES), seq_shape(1, S5_LANES),
                   mix_shape, mix_shape, seq_shape(SC_CONV - 1, GROUP_WIDTH), seq_shape(POOL_HIST, GROUP_WIDTH)),
        scratch_shapes=[pltpu.VMEM((S5_CHUNK, S5_LANES), F32), pltpu.VMEM((S5_CHUNK, S5_LANES), F32),
                        pltpu.VMEM((1, S5_LANES), F32), pltpu.VMEM((1, S5_LANES), F32),
                        pltpu.VMEM((SUBLANE + S5_CHUNK, GROUP_WIDTH), F32),
                        pltpu.VMEM((2 * SUBLANE + S5_CHUNK, GROUP_WIDTH), F32)],
        compiler_params=_cparams(("arbitrary", "arbitrary")),
        name="prompt_b_mixers",
    )(proj, perm, perm.T, sp["s5_pw_re"], sp["s5_pw_im"], sp["s5_pwt_re"], sp["s5_pwt_im"], *_s5_param_args(sp),
      proj, proj, proj, proj, sp["sc_conv_w"], sp["pool_w"], sp["pool_scale"], sp["g_sc"], sp["g_pool"])


def _sample_mix_kernel(proj_ref, pb_ref, cprev_ref, s5r_ref, s5i_ref, scprev_ref, plprev_ref,
                       cw_ref, cb_ref, dtb_ref, alog_ref,
                       pwr_ref, pwi_ref, bre_ref, bim_ref, cre_ref, cimn_ref, d5_ref, wglu_ref, bglu_ref, g5_ref,
                       scw_ref, pw_ref, ps_ref, gsc_ref, gpl_ref,
                       m5_ref, msc_ref, mpl_ref, cst_ref, s5ro_ref, s5io_ref, scst_ref, plst_ref,
                       xs_ref, xdt_ref, da_ref, b2_ref, c2_ref):
    nb = proj_ref.shape[0]
    W = GROUP_WIDTH
    xbc = proj_ref[:, COL_XBC:COL_XBC + XBC_WIDTH]
    hist = SSD_CONV - 1
    conv = cb_ref[...] + cw_ref[hist:hist + 1, :] * xbc
    for k in range(hist):
        conv = conv + cw_ref[k:k + 1, :] * cprev_ref[:, k * XBC_WIDTH:(k + 1) * XBC_WIDTH]
    cst_ref[:, 0:(hist - 1) * XBC_WIDTH] = cprev_ref[:, XBC_WIDTH:hist * XBC_WIDTH]
    cst_ref[:, (hist - 1) * XBC_WIDTH:hist * XBC_WIDTH] = xbc
    xc = _silu(conv)
    xs = xc[:, :W]
    bm = xc[:, W:W + LANE]
    cm = xc[:, W + LANE:W + 2 * LANE]
    dt = _softplus(proj_ref[:, COL_DT:COL_DT + LANE] + dtb_ref[...])
    da = jnp.exp(dt * (-jnp.exp(alog_ref[...])))
    lane_w = lax.broadcasted_iota(jnp.int32, (nb, W), 1)
    dt_exp = jnp.zeros((nb, W), F32)
    for h in range(SSD_HEADS):
        dt_exp = jnp.where(lane_w // SSD_HEAD_DIM == h, dt[:, h:h + 1], dt_exp)
        da_ref[:, h * LANE:(h + 1) * LANE] = jnp.broadcast_to(da[:, h:h + 1], (nb, LANE))
    xs_ref[...] = xs
    xdt_ref[...] = xs * dt_exp
    lane = lax.broadcasted_iota(jnp.int32, (nb, LANE), 1)
    low = lane < SSD_STATE
    for src, dst in ((bm, b2_ref), (cm, c2_ref)):
        swapped = pltpu.roll(src, SSD_STATE, 1)
        dst[:, 0:LANE] = jnp.where(low, src, swapped)
        dst[:, LANE:2 * LANE] = jnp.where(low, swapped, src)

    u5 = pb_ref[:, COL_U5:COL_U5 + W]
    bu_r, bu_i = _s5_bu(u5.astype(BF16), bre_ref, bim_ref)
    sblk = S5_LANES // len(bu_r)
    for i in range(len(bu_r)):
        ls = slice(i * sblk, (i + 1) * sblk)
        n_r, n_i = _cmul_add(bu_r[i], bu_i[i], pwr_ref[0:1, ls], pwi_ref[0:1, ls], s5r_ref[:, ls], s5i_ref[:, ls])
        s5ro_ref[:, ls] = n_r
        s5io_ref[:, ls] = n_i
    y5 = _s5_tail(u5, lambda i: s5ro_ref[:, i * sblk:(i + 1) * sblk], lambda i: s5io_ref[:, i * sblk:(i + 1) * sblk],
                  cre_ref, cimn_ref, d5_ref, wglu_ref, bglu_ref, g5_ref)
    m5_ref[...] = y5.astype(m5_ref.dtype)

    v = pb_ref[:, COL_GC:COL_GC + W] * pb_ref[:, COL_HV:COL_HV + W]
    vh = SC_CONV - 1
    acc = scw_ref[vh:vh + 1, :] * v
    for k in range(vh):
        acc = acc + scw_ref[k:k + 1, :] * scprev_ref[:, k * W:(k + 1) * W]
    msc_ref[...] = _rms(pb_ref[:, COL_GB:COL_GB + W] * acc, gsc_ref[...]).astype(msc_ref.dtype)
    scst_ref[:, 0:(vh - 1) * W] = scprev_ref[:, W:vh * W]
    scst_ref[:, (vh - 1) * W:vh * W] = v

    up = pb_ref[:, COL_UP:COL_UP + W]
    ys = []
    for gi, w in enumerate(POOL_WINDOWS):
        cs = slice(gi * POOL_GROUP, (gi + 1) * POOL_GROUP)
        s = up[:, cs]
        for j in range(1, w):
            k = POOL_HIST - j
            s = s + plprev_ref[:, k * W + gi * POOL_GROUP:k * W + (gi + 1) * POOL_GROUP]
        pooled = s / float(min(w, PAST_LEN + 1)) - up[:, cs]
        ys.append(_dot(pooled.astype(BF16), pw_ref[gi]))
    y = jnp.concatenate(ys, axis=-1) * ps_ref[...]
    mpl_ref[...] = _rms(y, gpl_ref[...]).astype(mpl_ref.dtype)
    plst_ref[:, 0:(POOL_HIST - 1) * W] = plprev_ref[:, W:POOL_HIST * W]
    plst_ref[:, (POOL_HIST - 1) * W:POOL_HIST * W] = up


def sample_mix(proj_a, proj_b, states, sp, layer):
    nb = DEC_BATCH
    W = GROUP_WIDTH
    f = lambda n: jax.ShapeDtypeStruct((nb, n), F32)
    b = lambda n: jax.ShapeDtypeStruct((nb, n), BF16)
    out_widths = ((SSD_CONV - 1) * XBC_WIDTH, S5_LANES, S5_LANES, (SC_CONV - 1) * W, POOL_HIST * W,
                  W, W, SSD_HEADS * LANE, SSD_GROUPS * LANE, SSD_GROUPS * LANE)
    out_shape = (b(W), b(W), b(W)) + tuple(f(n) for n in out_widths)
    nblk = GROUP_WIDTH // LANE
    in_specs = ([pl.BlockSpec((nb, A_WIDTH), lambda i: (SAMPLE_BLOCK, 0)),
                 pl.BlockSpec((nb, B_WIDTH), lambda i: (SAMPLE_BLOCK, 0))]
                + [_layer_spec((nb, s.shape[2]), layer) for s in states]
                + [_layer_spec((SSD_CONV, XBC_WIDTH), layer), _layer_spec((1, XBC_WIDTH), layer),
                   _layer_spec((1, LANE), layer), _layer_spec((1, LANE), layer),
                   _layer_spec((S5_SEG, S5_LANES), layer), _layer_spec((S5_SEG, S5_LANES), layer)]
                + _s5_param_specs(layer)
                + [_layer_spec((SC_CONV, W), layer), _layer_spec((len(POOL_WINDOWS), POOL_GROUP, POOL_GROUP), layer),
                   _layer_spec((1, W), layer), _layer_spec((1, W), layer), _layer_spec((1, W), layer)])
    return pl.pallas_call(
        _sample_mix_kernel,
        grid=(1,),
        in_specs=in_specs,
        out_specs=tuple(_whole_spec(s.shape) for s in out_shape),
        out_shape=out_shape,
        compiler_params=_cparams(("arbitrary",)),
        name="sample_mix",
    )(proj_a, proj_b, *states, sp["ssd_conv_w"], sp["ssd_conv_b"], sp["ssd_dt_bias"], sp["ssd_a_log"],
      sp["s5_pw_re"], sp["s5_pw_im"], *_s5_param_args(sp),
      sp["sc_conv_w"], sp["pool_w"], sp["pool_scale"], sp["g_sc"], sp["g_pool"])


def _sample_ssd_kernel(h0_ref, xdt_ref, da_ref, b2_ref, c2_ref, e_ref, r_ref, xs_ref, z_ref, dexp_ref, g_ref,
                       hn_ref, mix_ref, xrep_ref, prod_ref, y_ref):
    hp = pl.program_id(0)
    hpl = 2 * SSD_HEAD_DIM * SSD_STATE
    per_head = SSD_HEAD_DIM * SSD_STATE
    x_hi, x_lo = _split_bf16(xdt_ref[...])
    xrep_ref[...] = _dot(x_hi, e_ref[...]) + _dot(x_lo, e_ref[...])
    b2, c2 = b2_ref[...], c2_ref[...]
    for j in range(hpl // LANE):
        ls = slice(j * LANE, (j + 1) * LANE)
        hl = (j * LANE) // per_head
        hn = da_ref[:, hl * LANE:(hl + 1) * LANE] * h0_ref[:, ls] + xrep_ref[:, ls] * b2
        hn_ref[:, ls] = hn
        prod_ref[:, ls] = hn * c2
    p_hi, p_lo = _split_bf16(prod_ref[...])
    y_ref[hp] = _dot(p_hi, r_ref[...]) + _dot(p_lo, r_ref[...])

    @pl.when(hp == pl.num_programs(0) - 1)
    def _():
        y = jnp.concatenate([y_ref[i] for i in range(SSD_HEADS // 2)], axis=-1)
        y = (y + dexp_ref[...] * xs_ref[...]) * _silu(z_ref[...])
        mix_ref[...] = _rms(y, g_ref[...]).astype(mix_ref.dtype)


def sample_ssd(h0_all, xdt, da, b2, c2, xs, proj, sp, layer):
    nb = DEC_BATCH
    npairs = SSD_HEADS // 2
    hpl = 2 * SSD_HEAD_DIM * SSD_STATE
    expand = jnp.repeat(jnp.eye(LANE, dtype=BF16), SSD_STATE, axis=1)
    return pl.pallas_call(
        _sample_ssd_kernel,
        grid=(npairs,),
        in_specs=[pl.BlockSpec((None, nb, hpl), lambda i: (layer, 0, i)),
                  pl.BlockSpec((nb, LANE), lambda i: (0, i)),
                  pl.BlockSpec((nb, 2 * LANE), lambda i: (0, i)),
                  pl.BlockSpec((nb, LANE), lambda i: (0, i // (npairs // SSD_GROUPS))),
                  pl.BlockSpec((nb, LANE), lambda i: (0, i // (npairs // SSD_GROUPS))),
                  _whole_spec((LANE, hpl)),
                  _whole_spec((hpl, LANE)),
                  _whole_spec((nb, GROUP_WIDTH)),
                  pl.BlockSpec((nb, GROUP_WIDTH), lambda i: (SAMPLE_BLOCK, COL_Z // GROUP_WIDTH)),
                  _layer_spec((1, GROUP_WIDTH), layer),
                  _layer_spec((1, GROUP_WIDTH), layer)],
        out_specs=(pl.BlockSpec((nb, hpl), lambda i: (0, i)),
                   _whole_spec((nb, GROUP_WIDTH))),
        out_shape=(jax.ShapeDtypeStruct((nb, SSD_HEADS * SSD_HEAD_DIM * SSD_STATE), F32),
                   jax.ShapeDtypeStruct((nb, GROUP_WIDTH), BF16)),
        scratch_shapes=[pltpu.VMEM((nb, hpl), F32), pltpu.VMEM((nb, hpl), F32),
                        pltpu.VMEM((npairs, nb, LANE), F32)],
        compiler_params=_cparams(("arbitrary",)),
        name="sample_ssd",
    )(h0_all, xdt, da, b2, c2, expand, expand.T, xs, proj, sp["ssd_d_exp"], sp["g_ssd"])


def _split_w_in(w_in):
    head = COL_DT + SSD_HEADS
    part_a = jnp.pad(w_in[..., :head], ((0, 0), (0, 0), (0, A_WIDTH - head)))
    return part_a.astype(BF16), w_in[..., head:].astype(BF16)


def _stacked_params(w):
    row = lambda v: v.reshape(DEPTH, 1, -1)
    pad_heads = lambda v: row(jnp.pad(v, ((0, 0), (0, LANE - SSD_HEADS))))
    g_mix = w["mix_out_g"].reshape(DEPTH, 4, 1, GROUP_WIDTH)
    w_in_a, w_in_b = _split_w_in(w["w_in"])
    sp = dict(w_in_a=w_in_a, w_in_b=w_in_b)
    for k in ("s5_w_glu", "pool_w"):
        sp[k] = w[k].astype(BF16)
    for k in ("w_out", "w_q", "w_k", "w_v", "w_o", "w_up", "w_down"):
        sp[k] = w[k]
    for k in ("norm_mix_g", "norm_xa_g", "norm_mem_g", "norm_mlp_g", "ssd_conv_b", "s5_d", "s5_b_glu", "pool_scale"):
        sp[k] = row(w[k])
    sp.update(ssd_conv_w=w["ssd_conv_w"], sc_conv_w=w["sc_conv_w"],
              ssd_dt_bias=pad_heads(w["ssd_dt_bias"]), ssd_a_log=pad_heads(w["ssd_a_log"]),
              ssd_d_exp=row(jnp.repeat(w["ssd_d"], SSD_HEAD_DIM, axis=1)),
              g_ssd=g_mix[:, 0], g_s5=g_mix[:, 1], g_sc=g_mix[:, 2], g_pool=g_mix[:, 3])
    s5 = s5_prepare(w["s5_lam_re"], w["s5_lam_im"], w["s5_log_dt"], w["s5_b_re"], w["s5_b_im"],
                    w["s5_c_re"], w["s5_c_im"])
    sp.update({"s5_" + k: v for k, v in s5.items()})
    return sp


def _forward(x_prompt, x_sample, mem_prompt, state_ssd_conv, state_ssd, state_s5_re, state_s5_im,
             state_sconv, state_pool, cache_mem_k, cache_mem_v, final_norm_g, w):
    sp = _stacked_params(w)
    flat = lambda s: s.reshape(DEPTH, DEC_BATCH, -1)
    s_states = tuple(flat(s) for s in (state_ssd_conv, state_s5_re, state_s5_im, state_sconv, state_pool))
    s_ssd0 = flat(state_ssd)
    h = jnp.concatenate([x_prompt.reshape(P_ROWS, D_MODEL), x_sample.reshape(DEC_BATCH, D_MODEL)], axis=0)
    mem = mem_prompt.reshape(BATCH * N_MEM, D_MODEL)
    p_out = [[] for _ in range(8)]
    s_out = [[] for _ in range(6)]
    put_sample = lambda full, rows: lax.dynamic_update_slice(full, rows, (P_ROWS, 0))
    for l in range(DEPTH):
        mk = norm_matmul(mem, sp["norm_mem_g"], sp["w_k"], l, tm=BATCH * N_MEM, tn=MEM_COL_TILE)
        mv = norm_matmul(mem, sp["norm_mem_g"], sp["w_v"], l, tm=BATCH * N_MEM, tn=MEM_COL_TILE)

        proj_a, proj_b = in_proj(h, sp["norm_mix_g"], sp["w_in_a"], sp["w_in_b"], l, tm=RESIDENT_ROW_TILE)
        m_ssd, p_conv, p_ssd = prompt_ssd(proj_a, sp, l)
        m_s5, p_s5r, p_s5i, m_sc, m_pl, p_sc, p_pl = prompt_b_mixers(proj_b, sp, l)
        (s_m5, s_msc, s_mpl, s_conv, s_s5r, s_s5i, s_sc, s_pl, xs, xdt, da, b2, c2) = sample_mix(
            proj_a, proj_b, s_states, sp, l)
        s_ssd, s_mssd = sample_ssd(s_ssd0, xdt, da, b2, c2, xs, proj_a, sp, l)
        mixes = (put_sample(m_ssd, s_mssd), put_sample(m_s5, s_m5),
                 put_sample(m_sc, s_msc), put_sample(m_pl, s_mpl))

        h = out_proj(mixes, sp["w_out"], h, l, tm=RESIDENT_ROW_TILE)
        q = norm_matmul(h, sp["norm_xa_g"], sp["w_q"], l, out_dtype=BF16, tm=RESIDENT_ROW_TILE, tn=D_MODEL)
        o = prompt_attn(q, mk.reshape(BATCH, N_MEM, D_MODEL), mv.reshape(BATCH, N_MEM, D_MODEL), tq=512)
        q_s = q[P_ROWS:].astype(F32).reshape(DEC_BATCH, XA_HEADS, XA_HEAD_DIM)
        o_s = sample_attn(q_s, cache_mem_k, cache_mem_v, l)
        o = put_sample(o, o_s.reshape(DEC_BATCH, D_MODEL))
        h = res_matmul(o, sp["w_o"], h, l, tm=RESIDENT_ROW_TILE, tn=D_MODEL)
        h = mlp(h, sp["norm_mlp_g"], sp["w_up"], sp["w_down"], l, tm=MLP_ROW_TILE, tf=FF_TILE)

        for lst, val in zip(p_out, (p_conv, p_ssd, p_s5r, p_s5i, p_sc, p_pl, mk, mv)):
            lst.append(val)
        for lst, val in zip(s_out, (s_conv, s_ssd, s_s5r, s_s5i, s_sc, s_pl)):
            lst.append(val)

    y_prompt = final_norm(h, final_norm_g, rows=P_ROWS, first_block=0, tm=FINAL_ROW_TILE)
    y_sample = final_norm(h, final_norm_g, rows=DEC_BATCH, first_block=SAMPLE_BLOCK, tm=DEC_BATCH)
    p_shapes = ((BATCH, SSD_CONV - 1, XBC_WIDTH), (BATCH, SSD_HEADS, SSD_HEAD_DIM, SSD_STATE),
                (BATCH, S5_GROUPS, S5_STATE), (BATCH, S5_GROUPS, S5_STATE),
                (BATCH, SC_CONV - 1, GROUP_WIDTH), (BATCH, POOL_HIST, GROUP_WIDTH),
                (BATCH, N_MEM, XA_HEADS, XA_HEAD_DIM), (BATCH, N_MEM, XA_HEADS, XA_HEAD_DIM))
    s_shapes = tuple((DEC_BATCH,) + s[1:] for s in p_shapes[:6])
    stack = lambda vals, shape: jnp.stack(vals).reshape((DEPTH,) + shape)
    return ((y_prompt.reshape(BATCH, SEQ, D_MODEL), y_sample.reshape(DEC_BATCH, 1, D_MODEL))
            + tuple(stack(v, s) for v, s in zip(p_out, p_shapes))
            + tuple(stack(v, s) for v, s in zip(s_out, s_shapes)))


_forward_jit = jax.jit(_forward)


def kernel(x_prompt, x_sample, mem_prompt, state_ssd_conv, state_ssd, state_s5_re, state_s5_im, state_sconv, state_pool, cache_mem_k, cache_mem_v, norm_mix_g, w_in, ssd_conv_w, ssd_conv_b, ssd_dt_bias, ssd_a_log, ssd_d, s5_lam_re, s5_lam_im, s5_log_dt, s5_b_re, s5_b_im, s5_c_re, s5_c_im, s5_d, s5_w_glu, s5_b_glu, sc_conv_w, pool_w, pool_scale, mix_out_g, w_out, norm_xa_g, norm_mem_g, w_q, w_k, w_v, w_o, norm_mlp_g, w_up, w_down, final_norm_g):
    w = dict(norm_mix_g=norm_mix_g, w_in=w_in, ssd_conv_w=ssd_conv_w, ssd_conv_b=ssd_conv_b,
             ssd_dt_bias=ssd_dt_bias, ssd_a_log=ssd_a_log, ssd_d=ssd_d, s5_lam_re=s5_lam_re,
             s5_lam_im=s5_lam_im, s5_log_dt=s5_log_dt, s5_b_re=s5_b_re, s5_b_im=s5_b_im,
             s5_c_re=s5_c_re, s5_c_im=s5_c_im, s5_d=s5_d, s5_w_glu=s5_w_glu, s5_b_glu=s5_b_glu,
             sc_conv_w=sc_conv_w, pool_w=pool_w, pool_scale=pool_scale, mix_out_g=mix_out_g,
             w_out=w_out, norm_xa_g=norm_xa_g, norm_mem_g=norm_mem_g, w_q=w_q, w_k=w_k, w_v=w_v,
             w_o=w_o, norm_mlp_g=norm_mlp_g, w_up=w_up, w_down=w_down)
    return _forward_jit(x_prompt, x_sample, mem_prompt, state_ssd_conv, state_ssd, state_s5_re, state_s5_im,
                        state_sconv, state_pool, cache_mem_k, cache_mem_v, final_norm_g, w)
```

```python
import math

import numpy as np
import jax
import jax.numpy as jnp
from jax import lax
from jax.experimental import pallas as pl
from jax.experimental.pallas import tpu as pltpu

F32 = jnp.float32
BF16 = jnp.bfloat16

D_MODEL = 2048
BATCH = 4
SEQ = 2048
DEPTH = 2
DEC_BATCH = 128
PAST_LEN = 16384
GROUP_WIDTH = D_MODEL // 4
SSD_HEAD_DIM = 64
SSD_HEADS = GROUP_WIDTH // SSD_HEAD_DIM
SSD_GROUPS = 2
SSD_STATE = 64
SSD_CONV = 4
SSD_CHUNK = 128
XBC_WIDTH = GROUP_WIDTH + 2 * SSD_GROUPS * SSD_STATE
S5_CH = 16
S5_GROUPS = GROUP_WIDTH // S5_CH
S5_STATE = 64
S5_LANES = S5_GROUPS * S5_STATE
SC_CONV = 3
POOL_WINDOWS = (2, 4, 8, 16)
POOL_GROUP = GROUP_WIDTH // len(POOL_WINDOWS)
POOL_HIST = max(POOL_WINDOWS) - 1
N_MEM = 256
XA_HEADS = 4
XA_HEAD_DIM = D_MODEL // XA_HEADS
D_FF = 4 * D_MODEL
EPS = 1e-6
P_ROWS = BATCH * SEQ
ALL_ROWS = P_ROWS + DEC_BATCH

LANE = 128
SUBLANE = 8
VMEM_LIMIT = 60 * 1024 * 1024

COL_Z = 0
COL_XBC = 512
COL_DT = 1280
A_WIDTH = 1536
ZX_WIDTH = COL_DT
COL_U5 = 0
COL_GB = 512
COL_GC = 1024
COL_HV = 1536
COL_UP = 2048
B_WIDTH = 2560

RESIDENT_ROW_TILE = ALL_ROWS // 13
MLP_ROW_TILE = ALL_ROWS // 8
FF_TILE = 512
MEM_COL_TILE = 512
FINAL_ROW_TILE = 1024
SAMPLE_BLOCK = P_ROWS // DEC_BATCH

SAMPLE_ATTN_ROWS = 2
S5_CHUNK = 256
S5_SEG = S5_CHUNK // SUBLANE
S5_SCAN_LANES = 512


def _cparams(sem):
    return pltpu.CompilerParams(dimension_semantics=sem, vmem_limit_bytes=VMEM_LIMIT)


def _layer_spec(shape, layer):
    zeros = (0,) * len(shape)
    return pl.BlockSpec((None,) + tuple(shape), lambda *_: (layer,) + zeros)


def _whole_spec(shape):
    zeros = (0,) * len(shape)
    return pl.BlockSpec(tuple(shape), lambda *_: zeros)


def _weight_cols_spec(k, n, tn, layer):
    mode = pl.Buffered(1) if tn == n else None
    return pl.BlockSpec((None, k, tn), lambda i, j: (layer, 0, j), pipeline_mode=mode)


def _prompt_row_block(b, t, steps, rows):
    return jnp.minimum(b * steps + t, P_ROWS // rows)


def _prompt_batch(b):
    return jnp.minimum(b, BATCH - 1)


def _prompt_step(body, row_outputs):
    b = pl.program_id(0)
    pl.when(b < BATCH)(body)

    @pl.when(b == BATCH)
    def _():
        for ref in row_outputs:
            ref[...] = jnp.zeros_like(ref)


def _sigmoid(x):
    return 1.0 / (1.0 + jnp.exp(-x))


def _silu(x):
    return x * _sigmoid(x)


def _softplus(x):
    return jnp.maximum(x, 0.0) + jnp.log(1.0 + jnp.exp(-jnp.abs(x)))


def _gelu_tanh(x):
    return 0.5 * x * (1.0 + jnp.tanh(math.sqrt(2.0 / math.pi) * (x + 0.044715 * (x * x * x))))


def _rms(x, g):
    return x * lax.rsqrt(jnp.mean(x * x, axis=-1, keepdims=True) + EPS) * g


def _dot(a, b):
    return jnp.dot(a, b, preferred_element_type=F32)


def _split3_bf16(x):
    hi = x.astype(BF16)
    r = x - hi.astype(F32)
    mid = r.astype(BF16)
    lo = (r - mid.astype(F32)).astype(BF16)
    return hi, mid, lo


def _spread_exact(x, sel):
    hi, mid, lo = _split3_bf16(x)
    return _dot(hi, sel) + _dot(mid, sel) + _dot(lo, sel)


def _sum_rows_exact(sel, x):
    hi, mid, lo = _split3_bf16(x)
    return _dot(sel, hi) + _dot(sel, mid) + _dot(sel, lo)


def _split_bf16(x):
    hi = x.astype(BF16)
    lo = (x - hi.astype(F32)).astype(BF16)
    return hi, lo


def _norm_matmul_kernel(x_ref, g_ref, w_ref, o_ref, xn_ref):
    @pl.when(pl.program_id(1) == 0)
    def _():
        xn_ref[...] = _rms(x_ref[...], g_ref[...]).astype(BF16)

    o_ref[...] = _dot(xn_ref[...], w_ref[...].astype(BF16)).astype(o_ref.dtype)


def norm_matmul(x, g, w, layer, *, out_dtype=F32, tm, tn):
    m, k = x.shape
    n = w.shape[2]
    return pl.pallas_call(
        _norm_matmul_kernel,
        grid=(m // tm, n // tn),
        in_specs=[pl.BlockSpec((tm, k), lambda i, j: (i, 0)),
                  _layer_spec((1, k), layer),
                  _weight_cols_spec(k, n, tn, layer)],
        out_specs=pl.BlockSpec((tm, tn), lambda i, j: (i, j)),
        out_shape=jax.ShapeDtypeStruct((m, n), out_dtype),
        scratch_shapes=[pltpu.VMEM((tm, k), BF16)],
        compiler_params=_cparams(("parallel", "arbitrary")),
        name="norm_matmul",
    )(x, g, w)


def _in_proj_kernel(x_ref, g_ref, wa_ref, wb_ref, oa_ref, ob_ref):
    xn = _rms(x_ref[...], g_ref[...]).astype(BF16)
    oa_ref[...] = _dot(xn, wa_ref[...])
    ob_ref[...] = _dot(xn, wb_ref[...])


def in_proj(x, g, wa, wb, layer, *, tm):
    m, k = x.shape
    resident = lambda n: pl.BlockSpec((None, k, n), lambda i: (layer, 0, 0), pipeline_mode=pl.Buffered(1))
    return pl.pallas_call(
        _in_proj_kernel,
        grid=(m // tm,),
        in_specs=[pl.BlockSpec((tm, k), lambda i: (i, 0)), _layer_spec((1, k), layer),
                  resident(A_WIDTH), resident(B_WIDTH)],
        out_specs=(pl.BlockSpec((tm, A_WIDTH), lambda i: (i, 0)), pl.BlockSpec((tm, B_WIDTH), lambda i: (i, 0))),
        out_shape=(jax.ShapeDtypeStruct((m, A_WIDTH), F32), jax.ShapeDtypeStruct((m, B_WIDTH), F32)),
        compiler_params=_cparams(("parallel",)),
        name="in_proj",
    )(x, g, wa, wb)


def _res_matmul_kernel(x_ref, w_ref, r_ref, o_ref):
    o_ref[...] = r_ref[...] + _dot(x_ref[...], w_ref[...].astype(BF16))


def res_matmul(x, w, res, layer, *, tm, tn):
    m, k = x.shape
    n = w.shape[2]
    return pl.pallas_call(
        _res_matmul_kernel,
        grid=(m // tm, n // tn),
        in_specs=[pl.BlockSpec((tm, k), lambda i, j: (i, 0)),
                  _weight_cols_spec(k, n, tn, layer),
                  pl.BlockSpec((tm, tn), lambda i, j: (i, j))],
        out_specs=pl.BlockSpec((tm, tn), lambda i, j: (i, j)),
        out_shape=jax.ShapeDtypeStruct((m, n), F32),
        compiler_params=_cparams(("parallel", "arbitrary")),
        name="res_matmul",
    )(x, w, res)


def _out_proj_kernel(m0_ref, m1_ref, m2_ref, m3_ref, w_ref, r_ref, o_ref):
    x = jnp.concatenate([m0_ref[...], m1_ref[...], m2_ref[...], m3_ref[...]], axis=-1)
    o_ref[...] = r_ref[...] + _dot(x, w_ref[...].astype(BF16))


def out_proj(mixes, w, res, layer, *, tm):
    m = res.shape[0]
    mix_spec = pl.BlockSpec((tm, GROUP_WIDTH), lambda i, j: (i, 0))
    row_spec = pl.BlockSpec((tm, D_MODEL), lambda i, j: (i, 0))
    return pl.pallas_call(
        _out_proj_kernel,
        grid=(m // tm, 1),
        in_specs=[mix_spec] * 4 + [_weight_cols_spec(D_MODEL, D_MODEL, D_MODEL, layer), row_spec],
        out_specs=row_spec,
        out_shape=jax.ShapeDtypeStruct((m, D_MODEL), F32),
        compiler_params=_cparams(("parallel", "arbitrary")),
        name="out_proj",
    )(*mixes, w, res)


def _mlp_kernel(x_ref, g_ref, wu_ref, wd_ref, o_ref, xn_ref):
    j = pl.program_id(1)

    @pl.when(j == 0)
    def _():
        x = x_ref[...]
        xn_ref[...] = _rms(x, g_ref[...]).astype(BF16)
        o_ref[...] = x

    a = jnp.square(jnp.maximum(_dot(xn_ref[...], wu_ref[...].astype(BF16)), 0.0))
    o_ref[...] += _dot(a.astype(BF16), wd_ref[...].astype(BF16))


def mlp(x, g, w_up, w_down, layer, *, tm, tf):
    m = x.shape[0]
    return pl.pallas_call(
        _mlp_kernel,
        grid=(m // tm, D_FF // tf),
        in_specs=[pl.BlockSpec((tm, D_MODEL), lambda i, j: (i, 0)),
                  _layer_spec((1, D_MODEL), layer),
                  pl.BlockSpec((None, D_MODEL, tf), lambda i, j: (layer, 0, j)),
                  pl.BlockSpec((None, tf, D_MODEL), lambda i, j: (layer, j, 0))],
        out_specs=pl.BlockSpec((tm, D_MODEL), lambda i, j: (i, 0)),
        out_shape=jax.ShapeDtypeStruct((m, D_MODEL), F32),
        scratch_shapes=[pltpu.VMEM((tm, D_MODEL), BF16)],
        compiler_params=_cparams(("parallel", "arbitrary")),
        name="mlp",
    )(x, g, w_up, w_down)


def _final_norm_kernel(x_ref, g_ref, o_ref):
    o_ref[...] = _rms(x_ref[...], g_ref[...])


def final_norm(x, g, *, rows, first_block, tm):
    return pl.pallas_call(
        _final_norm_kernel,
        grid=(rows // tm,),
        in_specs=[pl.BlockSpec((tm, D_MODEL), lambda i: (first_block + i, 0)), _whole_spec((1, D_MODEL))],
        out_specs=pl.BlockSpec((tm, D_MODEL), lambda i: (i, 0)),
        out_shape=jax.ShapeDtypeStruct((rows, D_MODEL), F32),
        compiler_params=_cparams(("parallel",)),
        name="final_norm",
    )(x, g.reshape(1, D_MODEL))


def _prompt_attn_kernel(q_ref, k_ref, v_ref, o_ref):
    _prompt_step(lambda: _prompt_attn_body(q_ref, k_ref, v_ref, o_ref), (o_ref,))


def _prompt_attn_body(q_ref, k_ref, v_ref, o_ref):
    scale = XA_HEAD_DIM ** -0.5
    for h in range(XA_HEADS):
        cs = slice(h * XA_HEAD_DIM, (h + 1) * XA_HEAD_DIM)
        kh = k_ref[0, :, cs].astype(BF16)
        vh = v_ref[0, :, cs].astype(BF16)
        s = lax.dot_general(q_ref[:, cs], kh, (((1,), (1,)), ((), ())), preferred_element_type=F32) * scale
        p = jnp.exp(s - jnp.max(s, axis=-1, keepdims=True))
        p = p / jnp.sum(p, axis=-1, keepdims=True)
        o_ref[:, cs] = _dot(p.astype(BF16), vh).astype(o_ref.dtype)


def prompt_attn(q, k, v, *, tq):
    nq = SEQ // tq
    return pl.pallas_call(
        _prompt_attn_kernel,
        grid=(BATCH + 1, nq),
        in_specs=[pl.BlockSpec((tq, D_MODEL), lambda b, i: (_prompt_row_block(b, i, nq, tq), 0)),
                  pl.BlockSpec((1, N_MEM, D_MODEL), lambda b, i: (_prompt_batch(b), 0, 0)),
                  pl.BlockSpec((1, N_MEM, D_MODEL), lambda b, i: (_prompt_batch(b), 0, 0))],
        out_specs=pl.BlockSpec((tq, D_MODEL), lambda b, i: (_prompt_row_block(b, i, nq, tq), 0)),
        out_shape=jax.ShapeDtypeStruct((ALL_ROWS, D_MODEL), BF16),
        compiler_params=_cparams(("arbitrary", "arbitrary")),
        name="prompt_attn",
    )(q, k, v)


def _sample_attn_kernel(q_ref, k_ref, v_ref, o_ref):
    scale = XA_HEAD_DIM ** -0.5
    half, quarter = XA_HEAD_DIM // 2, XA_HEAD_DIM // 4
    for r in range(SAMPLE_ATTN_ROWS):
        prod = k_ref[r] * (q_ref[r] * scale)
        fold = prod[:, :, :half] + prod[:, :, half:]
        fold = fold[:, :, :quarter] + fold[:, :, quarter:]
        s = jnp.sum(fold, axis=-1, keepdims=True)
        e = jnp.exp(s - jnp.max(s, axis=0, keepdims=True))
        o = jnp.sum(e * v_ref[r], axis=0) / jnp.sum(e, axis=0)
        o_ref[r] = o.astype(o_ref.dtype)


def sample_attn(q, cache_k, cache_v, layer):
    nr = SAMPLE_ATTN_ROWS
    row = pl.BlockSpec((nr, XA_HEADS, XA_HEAD_DIM), lambda b: (b, 0, 0))
    mem = pl.BlockSpec((None, nr, N_MEM, XA_HEADS, XA_HEAD_DIM), lambda b: (layer, b, 0, 0, 0))
    return pl.pallas_call(
        _sample_attn_kernel,
        grid=(DEC_BATCH // nr,),
        in_specs=[row, mem, mem],
        out_specs=row,
        out_shape=jax.ShapeDtypeStruct((DEC_BATCH, XA_HEADS, XA_HEAD_DIM), BF16),
        compiler_params=_cparams(("parallel",)),
        name="sample_attn",
    )(q, cache_k, cache_v)


def _s5_abar(lr, li, ldt):
    delta = jnp.exp(ldt)
    mag = jnp.exp(lr * delta)
    return mag * jnp.cos(li * delta), mag * jnp.sin(li * delta)


def _s5_bbar_kernel(lr_ref, li_ref, ldt_ref, bre_ref, bim_ref, bbr_ref, bbi_ref):
    lr, li = lr_ref[...], li_ref[...]
    ar, ai = _s5_abar(lr, li, ldt_ref[...])
    den = lr * lr + li * li
    cr = ((ar - 1.0) * lr + ai * li) / den
    ci = (ai * lr - (ar - 1.0) * li) / den
    br, bi = bre_ref[...], bim_ref[...]
    bbr_ref[...] = cr * br - ci * bi
    bbi_ref[...] = cr * bi + ci * br


def _s5_pow_kernel(lr_ref, li_ref, ldt_ref, pr_ref, pi_ref):
    ar, ai = _s5_abar(lr_ref[...], li_ref[...], ldt_ref[...])
    qr, qi = ar, ai
    pr_ref[0] = qr
    pi_ref[0] = qi
    for e in range(1, S5_SEG):
        qr, qi = qr * ar - qi * ai, qr * ai + qi * ar
        pr_ref[e] = qr
        pi_ref[e] = qi


def s5_prepare(lam_re, lam_im, log_dt, b_re, b_im, c_re, c_im):
    dg = DEPTH * S5_GROUPS
    ldt = jnp.broadcast_to(log_dt[..., None], (DEPTH, S5_GROUPS, S5_STATE))
    rep = lambda a: jnp.repeat(a.reshape(dg, S5_STATE), S5_CH, axis=0)
    to_rows = lambda b: jnp.transpose(b, (0, 1, 3, 2)).reshape(dg * S5_CH, S5_STATE)
    shp = jax.ShapeDtypeStruct((dg * S5_CH, S5_STATE), F32)
    bbr, bbi = pl.pallas_call(_s5_bbar_kernel, out_shape=(shp, shp), name="s5_bbar")(
        rep(lam_re), rep(lam_im), rep(ldt), to_rows(b_re), to_rows(b_im))
    pshp = jax.ShapeDtypeStruct((S5_SEG, dg, S5_STATE), F32)
    pr, pi = pl.pallas_call(_s5_pow_kernel, out_shape=(pshp, pshp), name="s5_pow")(
        lam_re.reshape(dg, S5_STATE), lam_im.reshape(dg, S5_STATE), ldt.reshape(dg, S5_STATE))

    gpb = LANE // S5_CH
    nblk = S5_GROUPS // gpb
    eye = jnp.eye(gpb, dtype=F32)

    def b_blocks(bb):
        bb = bb.reshape(DEPTH, nblk, gpb, S5_CH, S5_STATE)
        return jnp.einsum("digkn,gh->digkhn", bb, eye).reshape(DEPTH, nblk, LANE, gpb * S5_STATE).astype(BF16)

    def c_blocks(cc):
        cc = cc.reshape(DEPTH, nblk, gpb, S5_CH, S5_STATE)
        return jnp.einsum("digkn,gh->dignhk", cc, eye).reshape(DEPTH, nblk, gpb * S5_STATE, LANE).astype(BF16)

    def pow_rows(p):
        return jnp.transpose(p.reshape(S5_SEG, DEPTH, S5_LANES), (1, 0, 2))

    pw_r, pw_i = pow_rows(pr), pow_rows(pi)
    tile = lambda p: jnp.broadcast_to(p[:, :, None, :], (DEPTH, S5_SEG, SUBLANE, S5_LANES))
    return dict(b_re=b_blocks(bbr), b_im=b_blocks(bbi), c_re=c_blocks(c_re), c_imn=c_blocks(-c_im),
                pw_re=pw_r, pw_im=pw_i, pwt_re=tile(pw_r), pwt_im=tile(pw_i))


def _s5_bu(ub, bre_ref, bim_ref):
    res_r, res_i = [], []
    for i in range(GROUP_WIDTH // LANE):
        ui = ub[:, i * LANE:(i + 1) * LANE]
        res_r.append(_dot(ui, bre_ref[i]))
        res_i.append(_dot(ui, bim_ref[i]))
    return res_r, res_i


def _s5_tail(u, hs_re, hs_im, cre_ref, cimn_ref, d_ref, wglu_ref, bglu_ref, g_ref):
    ys = []
    for i in range(GROUP_WIDTH // LANE):
        ys.append(_dot(hs_re(i).astype(BF16), cre_ref[i]) + _dot(hs_im(i).astype(BF16), cimn_ref[i]))
    y = jnp.concatenate(ys, axis=-1) + d_ref[...] * u
    y = _gelu_tanh(y)
    y = y * _sigmoid(_dot(y.astype(BF16), wglu_ref[...]) + bglu_ref[...])
    return _rms(y, g_ref[...])


def _s5_param_specs(layer):
    nblk = GROUP_WIDTH // LANE
    sblk = S5_LANES // nblk
    return [_layer_spec((nblk, LANE, sblk), layer),
            _layer_spec((nblk, LANE, sblk), layer),
            _layer_spec((nblk, sblk, LANE), layer),
            _layer_spec((nblk, sblk, LANE), layer),
            _layer_spec((1, GROUP_WIDTH), layer),
            _layer_spec((GROUP_WIDTH, GROUP_WIDTH), layer),
            _layer_spec((1, GROUP_WIDTH), layer),
            _layer_spec((1, GROUP_WIDTH), layer)]


def _s5_param_args(sp):
    return (sp["s5_b_re"], sp["s5_b_im"], sp["s5_c_re"], sp["s5_c_imn"], sp["s5_d"], sp["s5_w_glu"],
            sp["s5_b_glu"], sp["g_s5"])


def _cmul_add(x_r, x_i, a_r, a_i, h_r, h_i):
    return x_r + a_r * h_r - a_i * h_i, x_i + a_r * h_i + a_i * h_r


def _prompt_s5_body(u_ref, perm_ref, unperm_ref, pw_re_ref, pw_im_ref, pwt_re_ref, pwt_im_ref,
                    bre_ref, bim_ref, cre_ref, cimn_ref, d_ref, wglu_ref, bglu_ref, g_ref,
                    mix_ref, sre_ref, sim_ref, hre_ref, him_ref, cr_ref, ci_ref):
    sblk = S5_LANES // (GROUP_WIDTH // LANE)
    W = S5_SCAN_LANES

    @pl.when(pl.program_id(1) == 0)
    def _():
        cr_ref[...] = jnp.zeros_like(cr_ref)
        ci_ref[...] = jnp.zeros_like(ci_ref)

    u_hi, u_lo = _split_bf16(u_ref[...])
    ub = _dot(perm_ref[...], u_hi)
    u = ub + _dot(perm_ref[...], u_lo)
    bu_r, bu_i = _s5_bu(ub.astype(BF16), bre_ref, bim_ref)
    for i in range(len(bu_r)):
        hre_ref[:, i * sblk:(i + 1) * sblk] = bu_r[i]
        him_ref[:, i * sblk:(i + 1) * sblk] = bu_i[i]

    row = lax.broadcasted_iota(jnp.int32, (SUBLANE, W), 0)
    for lb in range(0, S5_LANES, W):
        ls = slice(lb, lb + W)
        a_r = jnp.broadcast_to(pw_re_ref[0:1, ls], (SUBLANE, W))
        a_i = jnp.broadcast_to(pw_im_ref[0:1, ls], (SUBLANE, W))

        def local_step(j, h, ls=ls, a_r=a_r, a_i=a_i):
            rs = pl.ds(pl.multiple_of(j * SUBLANE, SUBLANE), SUBLANE)
            n_r, n_i = _cmul_add(hre_ref[rs, ls], him_ref[rs, ls], a_r, a_i, h[0], h[1])
            hre_ref[rs, ls] = n_r
            him_ref[rs, ls] = n_i
            return n_r, n_i

        zero = jnp.zeros((SUBLANE, W), F32)
        e_r, e_i = lax.fori_loop(0, S5_SEG, local_step, (zero, zero))

        s_r = pw_re_ref[S5_SEG - 1:S5_SEG, ls]
        s_i = pw_im_ref[S5_SEG - 1:S5_SEG, ls]
        c_r, c_i = cr_ref[:, ls], ci_ref[:, ls]
        in_r, in_i = zero, zero
        for s in range(SUBLANE):
            in_r = jnp.where(row == s, c_r, in_r)
            in_i = jnp.where(row == s, c_i, in_i)
            c_r, c_i = _cmul_add(e_r[s:s + 1, :], e_i[s:s + 1, :], s_r, s_i, c_r, c_i)
        cr_ref[:, ls] = c_r
        ci_ref[:, ls] = c_i

        def fix_step(j, carry, ls=ls, in_r=in_r, in_i=in_i):
            rs = pl.ds(pl.multiple_of(j * SUBLANE, SUBLANE), SUBLANE)
            n_r, n_i = _cmul_add(hre_ref[rs, ls], him_ref[rs, ls], pwt_re_ref[j, :, ls], pwt_im_ref[j, :, ls],
                                 in_r, in_i)
            hre_ref[rs, ls] = n_r
            him_ref[rs, ls] = n_i
            return carry

        lax.fori_loop(0, S5_SEG, fix_step, 0)

    sre_ref[0] = cr_ref[...]
    sim_ref[0] = ci_ref[...]
    y = _s5_tail(u, lambda i: hre_ref[:, i * sblk:(i + 1) * sblk], lambda i: him_ref[:, i * sblk:(i + 1) * sblk],
                 cre_ref, cimn_ref, d_ref, wglu_ref, bglu_ref, g_ref)
    mix_ref[...] = _dot(unperm_ref[...], y.astype(mix_ref.dtype)).astype(mix_ref.dtype)


def _prompt_ssd_kernel(*refs):
    _prompt_step(lambda: _prompt_ssd_body(*refs), (refs[10],))


def _prompt_ssd_body(zx_ref, dt_ref, cw_ref, cb_ref, dtb_ref, alog_ref, dexp_ref, g_ref, hexp_ref, hcol_ref,
                     mix_ref, cst_ref, hst_ref, xbuf_ref, h_ref):
    L = SSD_CHUNK
    hist = SSD_CONV - 1
    base = SUBLANE

    @pl.when(pl.program_id(1) == 0)
    def _():
        xbuf_ref[0:base, :] = jnp.zeros((base, XBC_WIDTH), F32)
        h_ref[...] = jnp.zeros_like(h_ref)

    @pl.when(pl.program_id(1) > 0)
    def _():
        xbuf_ref[base - hist:base, :] = xbuf_ref[base + L - hist:base + L, :]

    xbc = zx_ref[:, COL_XBC:COL_XBC + XBC_WIDTH]
    xbuf_ref[base:base + L, :] = xbc
    cst_ref[0] = xbuf_ref[base + L - hist:base + L, :]
    conv = cb_ref[...] + cw_ref[hist:hist + 1, :] * xbc
    for k in range(hist):
        conv = conv + cw_ref[k:k + 1, :] * xbuf_ref[base - hist + k:base - hist + k + L, :]
    xc = _silu(conv)
    xs = xc[:, :GROUP_WIDTH]
    ng = SSD_GROUPS * SSD_STATE
    bm = xc[:, GROUP_WIDTH:GROUP_WIDTH + ng].astype(BF16)
    cm = xc[:, GROUP_WIDTH + ng:].astype(BF16)

    dt = _softplus(dt_ref[...] + dtb_ref[...])
    a = -jnp.exp(alog_ref[...])
    ri = lax.broadcasted_iota(jnp.int32, (L, L), 0)
    ci = lax.broadcasted_iota(jnp.int32, (L, L), 1)
    causal = ri >= ci
    acum = _sum_rows_exact(causal.astype(BF16), dt * a)
    acum_t = acum.T
    last = acum[L - 1:L, :]
    to_end = jnp.exp(last - acum)
    e_acum = jnp.exp(acum)
    chunk_decay = jnp.exp(last)

    xdt_all = xs * _spread_exact(dt, hexp_ref[...])
    xdt_end = (xdt_all * _spread_exact(to_end, hexp_ref[...])).astype(BF16)
    xdt_all = xdt_all.astype(BF16)
    e_acum_all = _spread_exact(e_acum, hexp_ref[...])
    acum_cols = _spread_exact(acum, hcol_ref[...])

    ys_diag, ys_off = [], []
    rep = SSD_HEADS // SSD_GROUPS
    cb = [lax.dot_general(cm[:, g * SSD_STATE:(g + 1) * SSD_STATE], bm[:, g * SSD_STATE:(g + 1) * SSD_STATE],
                          (((1,), (1,)), ((), ())), preferred_element_type=F32) for g in range(SSD_GROUPS)]
    for h in range(SSD_HEADS):
        g = h // rep
        hs = slice(h * SSD_HEAD_DIM, (h + 1) * SSD_HEAD_DIM)
        gs = slice(g * SSD_STATE, (g + 1) * SSD_STATE)
        seg = acum_cols[:, h * L:(h + 1) * L] - acum_t[h:h + 1, :]
        decay = jnp.exp(jnp.where(causal, seg, -jnp.inf))
        ys_diag.append(_dot((cb[g] * decay).astype(BF16), xdt_all[:, hs]))
        h_prev = h_ref[h]
        ys_off.append(lax.dot_general(cm[:, gs], h_prev.astype(BF16), (((1,), (1,)), ((), ())),
                                      preferred_element_type=F32))
        st = lax.dot_general(xdt_end[:, hs], bm[:, gs], (((0,), (0,)), ((), ())), preferred_element_type=F32)
        h_ref[h] = h_prev * chunk_decay[:, h:h + 1] + st
    y = jnp.concatenate(ys_diag, axis=-1) + jnp.concatenate(ys_off, axis=-1) * e_acum_all
    y = (y + dexp_ref[...] * xs) * _silu(zx_ref[:, COL_Z:COL_Z + GROUP_WIDTH])
    mix_ref[...] = _rms(y, g_ref[...]).astype(mix_ref.dtype)
    hst_ref[0] = h_ref[...]


def prompt_ssd(proj, sp, layer):
    nc = SEQ // SSD_CHUNK
    rb = lambda b, c: _prompt_row_block(b, c, nc, SSD_CHUNK)
    head_of_row = np.arange(LANE)[:, None]
    head_lanes = jnp.asarray(head_of_row == np.arange(GROUP_WIDTH)[None, :] // SSD_HEAD_DIM, BF16)
    head_cols = jnp.asarray(head_of_row == np.arange(SSD_HEADS * SSD_CHUNK)[None, :] // SSD_CHUNK, BF16)
    return pl.pallas_call(
        _prompt_ssd_kernel,
        grid=(BATCH + 1, nc),
        in_specs=[pl.BlockSpec((SSD_CHUNK, ZX_WIDTH), lambda b, c: (rb(b, c), 0)),
                  pl.BlockSpec((SSD_CHUNK, LANE), lambda b, c: (rb(b, c), COL_DT // LANE)),
                  _layer_spec((SSD_CONV, XBC_WIDTH), layer),
                  _layer_spec((1, XBC_WIDTH), layer),
                  _layer_spec((1, LANE), layer),
                  _layer_spec((1, LANE), layer),
                  _layer_spec((1, GROUP_WIDTH), layer),
                  _layer_spec((1, GROUP_WIDTH), layer),
                  _whole_spec(head_lanes.shape), _whole_spec(head_cols.shape)],
        out_specs=(pl.BlockSpec((SSD_CHUNK, GROUP_WIDTH), lambda b, c: (rb(b, c), 0)),
                   pl.BlockSpec((1, SSD_CONV - 1, XBC_WIDTH), lambda b, c: (_prompt_batch(b), 0, 0)),
                   pl.BlockSpec((1, SSD_HEADS, SSD_HEAD_DIM, SSD_STATE), lambda b, c: (_prompt_batch(b), 0, 0, 0))),
        out_shape=(jax.ShapeDtypeStruct((ALL_ROWS, GROUP_WIDTH), BF16),
                   jax.ShapeDtypeStruct((BATCH, SSD_CONV - 1, XBC_WIDTH), F32),
                   jax.ShapeDtypeStruct((BATCH, SSD_HEADS, SSD_HEAD_DIM, SSD_STATE), F32)),
        scratch_shapes=[pltpu.VMEM((SUBLANE + SSD_CHUNK, XBC_WIDTH), F32),
                        pltpu.VMEM((SSD_HEADS, SSD_HEAD_DIM, SSD_STATE), F32)],
        compiler_params=_cparams(("arbitrary", "arbitrary")),
        name="prompt_ssd",
    )(proj, proj, sp["ssd_conv_w"], sp["ssd_conv_b"], sp["ssd_dt_bias"], sp["ssd_a_log"],
      sp["ssd_d_exp"], sp["g_ssd"], head_lanes, head_cols)


def _pool_counts(pos, w):
    return jnp.minimum(w, pos + 1).astype(F32)


def _prompt_convpool_body(gb_ref, gc_ref, hv_ref, up_ref, scw_ref, pw_ref, ps_ref, gsc_ref, gpl_ref,
                          msc_ref, mpl_ref, scst_ref, plst_ref, vbuf_ref, pbuf_ref):
    rows = gb_ref.shape[0]
    vb = SUBLANE
    pb = 2 * SUBLANE
    vh = SC_CONV - 1
    t = pl.program_id(1)

    @pl.when(t == 0)
    def _():
        vbuf_ref[0:vb, :] = jnp.zeros((vb, GROUP_WIDTH), F32)
        pbuf_ref[0:pb, :] = jnp.zeros((pb, GROUP_WIDTH), F32)

    @pl.when(t > 0)
    def _():
        vbuf_ref[vb - vh:vb, :] = vbuf_ref[vb + rows - vh:vb + rows, :]
        pbuf_ref[0:pb, :] = pbuf_ref[rows:rows + pb, :]

    v = gc_ref[...] * hv_ref[...]
    vbuf_ref[vb:vb + rows, :] = v
    acc = scw_ref[vh:vh + 1, :] * v
    for k in range(vh):
        acc = acc + scw_ref[k:k + 1, :] * vbuf_ref[vb - vh + k:vb - vh + k + rows, :]
    msc_ref[...] = _rms(gb_ref[...] * acc, gsc_ref[...]).astype(msc_ref.dtype)
    scst_ref[0] = vbuf_ref[vb + rows - vh:vb + rows, :]

    u = up_ref[...]
    pbuf_ref[pb:pb + rows, :] = u
    pos = t * rows + lax.broadcasted_iota(jnp.int32, (rows, 1), 0)
    ys = []
    for gi, w in enumerate(POOL_WINDOWS):
        cs = slice(gi * POOL_GROUP, (gi + 1) * POOL_GROUP)
        s = u[:, cs]
        for j in range(1, w):
            s = s + pbuf_ref[pb - j:pb - j + rows, cs]
        pooled = s / _pool_counts(pos, w) - u[:, cs]
        ys.append(_dot(pooled.astype(BF16), pw_ref[gi]))
    y = jnp.concatenate(ys, axis=-1) * ps_ref[...]
    mpl_ref[...] = _rms(y, gpl_ref[...]).astype(mpl_ref.dtype)
    plst_ref[0] = pbuf_ref[pb + rows - POOL_HIST:pb + rows, :]


N_S5_IN, N_S5_OUT, N_S5_SCRATCH = 15, 3, 4
N_CP_IN, N_CP_OUT = 9, 4


def _prompt_b_mixers_kernel(*refs):
    s5_in, refs = refs[:N_S5_IN], refs[N_S5_IN:]
    cp_in, refs = refs[:N_CP_IN], refs[N_CP_IN:]
    s5_out, refs = refs[:N_S5_OUT], refs[N_S5_OUT:]
    cp_out, refs = refs[:N_CP_OUT], refs[N_CP_OUT:]
    s5_scratch, cp_scratch = refs[:N_S5_SCRATCH], refs[N_S5_SCRATCH:]

    def body():
        _prompt_s5_body(*s5_in, *s5_out, *s5_scratch)
        _prompt_convpool_body(*cp_in, *cp_out, *cp_scratch)

    _prompt_step(body, (s5_out[0], cp_out[0], cp_out[1]))


def prompt_b_mixers(proj, sp, layer):
    nt = SEQ // S5_CHUNK
    t_of_row = (np.arange(S5_CHUNK) % SUBLANE) * S5_SEG + np.arange(S5_CHUNK) // SUBLANE
    perm = jnp.asarray(np.eye(S5_CHUNK, dtype=np.float32)[t_of_row], BF16)
    rb = lambda b, t: _prompt_row_block(b, t, nt, S5_CHUNK)
    col = lambda c: pl.BlockSpec((S5_CHUNK, GROUP_WIDTH), lambda b, t: (rb(b, t), c // GROUP_WIDTH))
    mix_spec = pl.BlockSpec((S5_CHUNK, GROUP_WIDTH), lambda b, t: (rb(b, t), 0))
    mix_shape = jax.ShapeDtypeStruct((ALL_ROWS, GROUP_WIDTH), BF16)
    seq_spec = lambda *s: pl.BlockSpec((1,) + s, lambda b, t: (_prompt_batch(b),) + (0,) * len(s))
    seq_shape = lambda *s: jax.ShapeDtypeStruct((BATCH,) + s, F32)
    s5_in_specs = [col(COL_U5), _whole_spec((S5_CHUNK, S5_CHUNK)), _whole_spec((S5_CHUNK, S5_CHUNK)),
                   _layer_spec((S5_SEG, S5_LANES), layer), _layer_spec((S5_SEG, S5_LANES), layer),
                   _layer_spec((S5_SEG, SUBLANE, S5_LANES), layer),
                   _layer_spec((S5_SEG, SUBLANE, S5_LANES), layer)] + _s5_param_specs(layer)
    cp_in_specs = [col(COL_GB), col(COL_GC), col(COL_HV), col(COL_UP),
                   _layer_spec((SC_CONV, GROUP_WIDTH), layer),
                   _layer_spec((len(POOL_WINDOWS), POOL_GROUP, POOL_GROUP), layer),
                   _layer_spec((1, GROUP_WIDTH), layer), _layer_spec((1, GROUP_WIDTH), layer),
                   _layer_spec((1, GROUP_WIDTH), layer)]
    assert len(s5_in_specs) == N_S5_IN and len(cp_in_specs) == N_CP_IN
    return pl.pallas_call(
        _prompt_b_mixers_kernel,
        grid=(BATCH + 1, nt),
        in_specs=s5_in_specs + cp_in_specs,
        out_specs=(mix_spec, seq_spec(1, S5_LANES), seq_spec(1, S5_LANES),
                   mix_spec, mix_spec, seq_spec(SC_CONV - 1, GROUP_WIDTH), seq_spec(POOL_HIST, GROUP_WIDTH)),
        out_shape=(mix_shape, seq_shape(1, S5_LANES), seq_shape(1, S5_LANES),
                   mix_shape, mix_shape, seq_shape(SC_CONV - 1, GROUP_WIDTH), seq_shape(POOL_HIST, GROUP_WIDTH)),
        scratch_shapes=[pltpu.VMEM((S5_CHUNK, S5_LANES), F32), pltpu.VMEM((S5_CHUNK, S5_LANES), F32),
                        pltpu.VMEM((1, S5_LANES), F32), pltpu.VMEM((1, S5_LANES), F32),
                        pltpu.VMEM((SUBLANE + S5_CHUNK, GROUP_WIDTH), F32),
                        pltpu.VMEM((2 * SUBLANE + S5_CHUNK, GROUP_WIDTH), F32)],
        compiler_params=_cparams(("arbitrary", "arbitrary")),
        name="prompt_b_mixers",
    )(proj, perm, perm.T, sp["s5_pw_re"], sp["s5_pw_im"], sp["s5_pwt_re"], sp["s5_pwt_im"], *_s5_param_args(sp),
      proj, proj, proj, proj, sp["sc_conv_w"], sp["pool_w"], sp["pool_scale"], sp["g_sc"], sp["g_pool"])


def _sample_mix_kernel(proj_ref, pb_ref, cprev_ref, s5r_ref, s5i_ref, scprev_ref, plprev_ref,
                       cw_ref, cb_ref, dtb_ref, alog_ref,
                       pwr_ref, pwi_ref, bre_ref, bim_ref, cre_ref, cimn_ref, d5_ref, wglu_ref, bglu_ref, g5_ref,
                       scw_ref, pw_ref, ps_ref, gsc_ref, gpl_ref,
                       m5_ref, msc_ref, mpl_ref, cst_ref, s5ro_ref, s5io_ref, scst_ref, plst_ref,
                       xs_ref, xdt_ref, da_ref, b2_ref, c2_ref):
    nb = proj_ref.shape[0]
    W = GROUP_WIDTH
    xbc = proj_ref[:, COL_XBC:COL_XBC + XBC_WIDTH]
    hist = SSD_CONV - 1
    conv = cb_ref[...] + cw_ref[hist:hist + 1, :] * xbc
    for k in range(hist):
        conv = conv + cw_ref[k:k + 1, :] * cprev_ref[:, k * XBC_WIDTH:(k + 1) * XBC_WIDTH]
    cst_ref[:, 0:(hist - 1) * XBC_WIDTH] = cprev_ref[:, XBC_WIDTH:hist * XBC_WIDTH]
    cst_ref[:, (hist - 1) * XBC_WIDTH:hist * XBC_WIDTH] = xbc
    xc = _silu(conv)
    xs = xc[:, :W]
    bm = xc[:, W:W + LANE]
    cm = xc[:, W + LANE:W + 2 * LANE]
    dt = _softplus(proj_ref[:, COL_DT:COL_DT + LANE] + dtb_ref[...])
    da = jnp.exp(dt * (-jnp.exp(alog_ref[...])))
    lane_w = lax.broadcasted_iota(jnp.int32, (nb, W), 1)
    dt_exp = jnp.zeros((nb, W), F32)
    for h in range(SSD_HEADS):
        dt_exp = jnp.where(lane_w // SSD_HEAD_DIM == h, dt[:, h:h + 1], dt_exp)
        da_ref[:, h * LANE:(h + 1) * LANE] = jnp.broadcast_to(da[:, h:h + 1], (nb, LANE))
    xs_ref[...] = xs
    xdt_ref[...] = xs * dt_exp
    lane = lax.broadcasted_iota(jnp.int32, (nb, LANE), 1)
    low = lane < SSD_STATE
    for src, dst in ((bm, b2_ref), (cm, c2_ref)):
        swapped = pltpu.roll(src, SSD_STATE, 1)
        dst[:, 0:LANE] = jnp.where(low, src, swapped)
        dst[:, LANE:2 * LANE] = jnp.where(low, swapped, src)

    u5 = pb_ref[:, COL_U5:COL_U5 + W]
    bu_r, bu_i = _s5_bu(u5.astype(BF16), bre_ref, bim_ref)
    sblk = S5_LANES // len(bu_r)
    for i in range(len(bu_r)):
        ls = slice(i * sblk, (i + 1) * sblk)
        n_r, n_i = _cmul_add(bu_r[i], bu_i[i], pwr_ref[0:1, ls], pwi_ref[0:1, ls], s5r_ref[:, ls], s5i_ref[:, ls])
        s5ro_ref[:, ls] = n_r
        s5io_ref[:, ls] = n_i
    y5 = _s5_tail(u5, lambda i: s5ro_ref[:, i * sblk:(i + 1) * sblk], lambda i: s5io_ref[:, i * sblk:(i + 1) * sblk],
                  cre_ref, cimn_ref, d5_ref, wglu_ref, bglu_ref, g5_ref)
    m5_ref[...] = y5.astype(m5_ref.dtype)

    v = pb_ref[:, COL_GC:COL_GC + W] * pb_ref[:, COL_HV:COL_HV + W]
    vh = SC_CONV - 1
    acc = scw_ref[vh:vh + 1, :] * v
    for k in range(vh):
        acc = acc + scw_ref[k:k + 1, :] * scprev_ref[:, k * W:(k + 1) * W]
    msc_ref[...] = _rms(pb_ref[:, COL_GB:COL_GB + W] * acc, gsc_ref[...]).astype(msc_ref.dtype)
    scst_ref[:, 0:(vh - 1) * W] = scprev_ref[:, W:vh * W]
    scst_ref[:, (vh - 1) * W:vh * W] = v

    up = pb_ref[:, COL_UP:COL_UP + W]
    ys = []
    for gi, w in enumerate(POOL_WINDOWS):
        cs = slice(gi * POOL_GROUP, (gi + 1) * POOL_GROUP)
        s = up[:, cs]
        for j in range(1, w):
            k = POOL_HIST - j
            s = s + plprev_ref[:, k * W + gi * POOL_GROUP:k * W + (gi + 1) * POOL_GROUP]
        pooled = s / float(min(w, PAST_LEN + 1)) - up[:, cs]
        ys.append(_dot(pooled.astype(BF16), pw_ref[gi]))
    y = jnp.concatenate(ys, axis=-1) * ps_ref[...]
    mpl_ref[...] = _rms(y, gpl_ref[...]).astype(mpl_ref.dtype)
    plst_ref[:, 0:(POOL_HIST - 1) * W] = plprev_ref[:, W:POOL_HIST * W]
    plst_ref[:, (POOL_HIST - 1) * W:POOL_HIST * W] = up


def sample_mix(proj_a, proj_b, states, sp, layer):
    nb = DEC_BATCH
    W = GROUP_WIDTH
    f = lambda n: jax.ShapeDtypeStruct((nb, n), F32)
    b = lambda n: jax.ShapeDtypeStruct((nb, n), BF16)
    out_widths = ((SSD_CONV - 1) * XBC_WIDTH, S5_LANES, S5_LANES, (SC_CONV - 1) * W, POOL_HIST * W,
                  W, W, SSD_HEADS * LANE, SSD_GROUPS * LANE, SSD_GROUPS * LANE)
    out_shape = (b(W), b(W), b(W)) + tuple(f(n) for n in out_widths)
    nblk = GROUP_WIDTH // LANE
    in_specs = ([pl.BlockSpec((nb, A_WIDTH), lambda i: (SAMPLE_BLOCK, 0)),
                 pl.BlockSpec((nb, B_WIDTH), lambda i: (SAMPLE_BLOCK, 0))]
                + [_layer_spec((nb, s.shape[2]), layer) for s in states]
                + [_layer_spec((SSD_CONV, XBC_WIDTH), layer), _layer_spec((1, XBC_WIDTH), layer),
                   _layer_spec((1, LANE), layer), _layer_spec((1, LANE), layer),
                   _layer_spec((S5_SEG, S5_LANES), layer), _layer_spec((S5_SEG, S5_LANES), layer)]
                + _s5_param_specs(layer)
                + [_layer_spec((SC_CONV, W), layer), _layer_spec((len(POOL_WINDOWS), POOL_GROUP, POOL_GROUP), layer),
                   _layer_spec((1, W), layer), _layer_spec((1, W), layer), _layer_spec((1, W), layer)])
    return pl.pallas_call(
        _sample_mix_kernel,
        grid=(1,),
        in_specs=in_specs,
        out_specs=tuple(_whole_spec(s.shape) for s in out_shape),
        out_shape=out_shape,
        compiler_params=_cparams(("arbitrary",)),
        name="sample_mix",
    )(proj_a, proj_b, *states, sp["ssd_conv_w"], sp["ssd_conv_b"], sp["ssd_dt_bias"], sp["ssd_a_log"],
      sp["s5_pw_re"], sp["s5_pw_im"], *_s5_param_args(sp),
      sp["sc_conv_w"], sp["pool_w"], sp["pool_scale"], sp["g_sc"], sp["g_pool"])


def _sample_ssd_kernel(h0_ref, xdt_ref, da_ref, b2_ref, c2_ref, e_ref, r_ref, xs_ref, z_ref, dexp_ref, g_ref,
                       hn_ref, mix_ref, xrep_ref, prod_ref, y_ref):
    hp = pl.program_id(0)
    hpl = 2 * SSD_HEAD_DIM * SSD_STATE
    per_head = SSD_HEAD_DIM * SSD_STATE
    x_hi, x_lo = _split_bf16(xdt_ref[...])
    xrep_ref[...] = _dot(x_hi, e_ref[...]) + _dot(x_lo, e_ref[...])
    b2, c2 = b2_ref[...], c2_ref[...]
    for j in range(hpl // LANE):
        ls = slice(j * LANE, (j + 1) * LANE)
        hl = (j * LANE) // per_head
        hn = da_ref[:, hl * LANE:(hl + 1) * LANE] * h0_ref[:, ls] + xrep_ref[:, ls] * b2
        hn_ref[:, ls] = hn
        prod_ref[:, ls] = hn * c2
    p_hi, p_lo = _split_bf16(prod_ref[...])
    y_ref[hp] = _dot(p_hi, r_ref[...]) + _dot(p_lo, r_ref[...])

    @pl.when(hp == pl.num_programs(0) - 1)
    def _():
        y = jnp.concatenate([y_ref[i] for i in range(SSD_HEADS // 2)], axis=-1)
        y = (y + dexp_ref[...] * xs_ref[...]) * _silu(z_ref[...])
        mix_ref[...] = _rms(y, g_ref[...]).astype(mix_ref.dtype)


def sample_ssd(h0_all, xdt, da, b2, c2, xs, proj, sp, layer):
    nb = DEC_BATCH
    npairs = SSD_HEADS // 2
    hpl = 2 * SSD_HEAD_DIM * SSD_STATE
    expand = jnp.repeat(jnp.eye(LANE, dtype=BF16), SSD_STATE, axis=1)
    return pl.pallas_call(
        _sample_ssd_kernel,
        grid=(npairs,),
        in_specs=[pl.BlockSpec((None, nb, hpl), lambda i: (layer, 0, i)),
                  pl.BlockSpec((nb, LANE), lambda i: (0, i)),
                  pl.BlockSpec((nb, 2 * LANE), lambda i: (0, i)),
                  pl.BlockSpec((nb, LANE), lambda i: (0, i // (npairs // SSD_GROUPS))),
                  pl.BlockSpec((nb, LANE), lambda i: (0, i // (npairs // SSD_GROUPS))),
                  _whole_spec((LANE, hpl)),
                  _whole_spec((hpl, LANE)),
                  _whole_spec((nb, GROUP_WIDTH)),
                  pl.BlockSpec((nb, GROUP_WIDTH), lambda i: (SAMPLE_BLOCK, COL_Z // GROUP_WIDTH)),
                  _layer_spec((1, GROUP_WIDTH), layer),
                  _layer_spec((1, GROUP_WIDTH), layer)],
        out_specs=(pl.BlockSpec((nb, hpl), lambda i: (0, i)),
                   _whole_spec((nb, GROUP_WIDTH))),
        out_shape=(jax.ShapeDtypeStruct((nb, SSD_HEADS * SSD_HEAD_DIM * SSD_STATE), F32),
                   jax.ShapeDtypeStruct((nb, GROUP_WIDTH), BF16)),
        scratch_shapes=[pltpu.VMEM((nb, hpl), F32), pltpu.VMEM((nb, hpl), F32),
                        pltpu.VMEM((npairs, nb, LANE), F32)],
        compiler_params=_cparams(("arbitrary",)),
        name="sample_ssd",
    )(h0_all, xdt, da, b2, c2, expand, expand.T, xs, proj, sp["ssd_d_exp"], sp["g_ssd"])


def _split_w_in(w_in):
    head = COL_DT + SSD_HEADS
    part_a = jnp.pad(w_in[..., :head], ((0, 0), (0, 0), (0, A_WIDTH - head)))
    return part_a.astype(BF16), w_in[..., head:].astype(BF16)


def _stacked_params(w):
    row = lambda v: v.reshape(DEPTH, 1, -1)
    pad_heads = lambda v: row(jnp.pad(v, ((0, 0), (0, LANE - SSD_HEADS))))
    g_mix = w["mix_out_g"].reshape(DEPTH, 4, 1, GROUP_WIDTH)
    w_in_a, w_in_b = _split_w_in(w["w_in"])
    sp = dict(w_in_a=w_in_a, w_in_b=w_in_b)
    for k in ("s5_w_glu", "pool_w"):
        sp[k] = w[k].astype(BF16)
    for k in ("w_out", "w_q", "w_k", "w_v", "w_o", "w_up", "w_down"):
        sp[k] = w[k]
    for k in ("norm_mix_g", "norm_xa_g", "norm_mem_g", "norm_mlp_g", "ssd_conv_b", "s5_d", "s5_b_glu", "pool_scale"):
        sp[k] = row(w[k])
    sp.update(ssd_conv_w=w["ssd_conv_w"], sc_conv_w=w["sc_conv_w"],
              ssd_dt_bias=pad_heads(w["ssd_dt_bias"]), ssd_a_log=pad_heads(w["ssd_a_log"]),
              ssd_d_exp=row(jnp.repeat(w["ssd_d"], SSD_HEAD_DIM, axis=1)),
              g_ssd=g_mix[:, 0], g_s5=g_mix[:, 1], g_sc=g_mix[:, 2], g_pool=g_mix[:, 3])
    s5 = s5_prepare(w["s5_lam_re"], w["s5_lam_im"], w["s5_log_dt"], w["s5_b_re"], w["s5_b_im"],
                    w["s5_c_re"], w["s5_c_im"])
    sp.update({"s5_" + k: v for k, v in s5.items()})
    return sp


def _forward(x_prompt, x_sample, mem_prompt, state_ssd_conv, state_ssd, state_s5_re, state_s5_im,
             state_sconv, state_pool, cache_mem_k, cache_mem_v, final_norm_g, w):
    sp = _stacked_params(w)
    flat = lambda s: s.reshape(DEPTH, DEC_BATCH, -1)
    s_states = tuple(flat(s) for s in (state_ssd_conv, state_s5_re, state_s5_im, state_sconv, state_pool))
    s_ssd0 = flat(state_ssd)
    h = jnp.concatenate([x_prompt.reshape(P_ROWS, D_MODEL), x_sample.reshape(DEC_BATCH, D_MODEL)], axis=0)
    mem = mem_prompt.reshape(BATCH * N_MEM, D_MODEL)
    p_out = [[] for _ in range(8)]
    s_out = [[] for _ in range(6)]
    put_sample = lambda full, rows: lax.dynamic_update_slice(full, rows, (P_ROWS, 0))
    for l in range(DEPTH):
        mk = norm_matmul(mem, sp["norm_mem_g"], sp["w_k"], l, tm=BATCH * N_MEM, tn=MEM_COL_TILE)
        mv = norm_matmul(mem, sp["norm_mem_g"], sp["w_v"], l, tm=BATCH * N_MEM, tn=MEM_COL_TILE)

        proj_a, proj_b = in_proj(h, sp["norm_mix_g"], sp["w_in_a"], sp["w_in_b"], l, tm=RESIDENT_ROW_TILE)
        m_ssd, p_conv, p_ssd = prompt_ssd(proj_a, sp, l)
        m_s5, p_s5r, p_s5i, m_sc, m_pl, p_sc, p_pl = prompt_b_mixers(proj_b, sp, l)
        (s_m5, s_msc, s_mpl, s_conv, s_s5r, s_s5i, s_sc, s_pl, xs, xdt, da, b2, c2) = sample_mix(
            proj_a, proj_b, s_states, sp, l)
        s_ssd, s_mssd = sample_ssd(s_ssd0, xdt, da, b2, c2, xs, proj_a, sp, l)
        mixes = (put_sample(m_ssd, s_mssd), put_sample(m_s5, s_m5),
                 put_sample(m_sc, s_msc), put_sample(m_pl, s_mpl))

        h = out_proj(mixes, sp["w_out"], h, l, tm=RESIDENT_ROW_TILE)
        q = norm_matmul(h, sp["norm_xa_g"], sp["w_q"], l, out_dtype=BF16, tm=RESIDENT_ROW_TILE, tn=D_MODEL)
        o = prompt_attn(q, mk.reshape(BATCH, N_MEM, D_MODEL), mv.reshape(BATCH, N_MEM, D_MODEL), tq=512)
        q_s = q[P_ROWS:].astype(F32).reshape(DEC_BATCH, XA_HEADS, XA_HEAD_DIM)
        o_s = sample_attn(q_s, cache_mem_k, cache_mem_v, l)
        o = put_sample(o, o_s.reshape(DEC_BATCH, D_MODEL))
        h = res_matmul(o, sp["w_o"], h, l, tm=RESIDENT_ROW_TILE, tn=D_MODEL)
        h = mlp(h, sp["norm_mlp_g"], sp["w_up"], sp["w_down"], l, tm=MLP_ROW_TILE, tf=FF_TILE)

        for lst, val in zip(p_out, (p_conv, p_ssd, p_s5r, p_s5i, p_sc, p_pl, mk, mv)):
            lst.append(val)
        for lst, val in zip(s_out, (s_conv, s_ssd, s_s5r, s_s5i, s_sc, s_pl)):
            lst.append(val)

    y_prompt = final_norm(h, final_norm_g, rows=P_ROWS, first_block=0, tm=FINAL_ROW_TILE)
    y_sample = final_norm(h, final_norm_g, rows=DEC_BATCH, first_block=SAMPLE_BLOCK, tm=DEC_BATCH)
    p_shapes = ((BATCH, SSD_CONV - 1, XBC_WIDTH), (BATCH, SSD_HEADS, SSD_HEAD_DIM, SSD_STATE),
                (BATCH, S5_GROUPS, S5_STATE), (BATCH, S5_GROUPS, S5_STATE),
                (BATCH, SC_CONV - 1, GROUP_WIDTH), (BATCH, POOL_HIST, GROUP_WIDTH),
                (BATCH, N_MEM, XA_HEADS, XA_HEAD_DIM), (BATCH, N_MEM, XA_HEADS, XA_HEAD_DIM))
    s_shapes = tuple((DEC_BATCH,) + s[1:] for s in p_shapes[:6])
    stack = lambda vals, shape: jnp.stack(vals).reshape((DEPTH,) + shape)
    return ((y_prompt.reshape(BATCH, SEQ, D_MODEL), y_sample.reshape(DEC_BATCH, 1, D_MODEL))
            + tuple(stack(v, s) for v, s in zip(p_out, p_shapes))
            + tuple(stack(v, s) for v, s in zip(s_out, s_shapes)))


_forward_jit = jax.jit(_forward)


def kernel(x_prompt, x_sample, mem_prompt, state_ssd_conv, state_ssd, state_s5_re, state_s5_im, state_sconv, state_pool, cache_mem_k, cache_mem_v, norm_mix_g, w_in, ssd_conv_w, ssd_conv_b, ssd_dt_bias, ssd_a_log, ssd_d, s5_lam_re, s5_lam_im, s5_log_dt, s5_b_re, s5_b_im, s5_c_re, s5_c_im, s5_d, s5_w_glu, s5_b_glu, sc_conv_w, pool_w, pool_scale, mix_out_g, w_out, norm_xa_g, norm_mem_g, w_q, w_k, w_v, w_o, norm_mlp_g, w_up, w_down, final_norm_g):
    w = dict(norm_mix_g=norm_mix_g, w_in=w_in, ssd_conv_w=ssd_conv_w, ssd_conv_b=ssd_conv_b,
             ssd_dt_bias=ssd_dt_bias, ssd_a_log=ssd_a_log, ssd_d=ssd_d, s5_lam_re=s5_lam_re,
             s5_lam_im=s5_lam_im, s5_log_dt=s5_log_dt, s5_b_re=s5_b_re, s5_b_im=s5_b_im,
             s5_c_re=s5_c_re, s5_c_im=s5_c_im, s5_d=s5_d, s5_w_glu=s5_w_glu, s5_b_glu=s5_b_glu,
             sc_conv_w=sc_conv_w, pool_w=pool_w, pool_scale=pool_scale, mix_out_g=mix_out_g,
             w_out=w_out, norm_xa_g=norm_xa_g, norm_mem_g=norm_mem_g, w_q=w_q, w_k=w_k, w_v=w_v,
             w_o=w_o, norm_mlp_g=norm_mlp_g, w_up=w_up, w_down=w_down)
    return _forward_jit(x_prompt, x_sample, mem_prompt, state_ssd_conv, state_ssd, state_s5_re, state_s5_im,
                        state_sconv, state_pool, cache_mem_k, cache_mem_v, final_norm_g, w)
```

```python
import math

import numpy as np
import jax
import jax.numpy as jnp
from jax import lax
from jax.experimental import pallas as pl
from jax.experimental.pallas import tpu as pltpu

F32 = jnp.float32
BF16 = jnp.bfloat16

D_MODEL = 2048
BATCH = 4
SEQ = 2048
DEPTH = 2
DEC_BATCH = 128
PAST_LEN = 16384
GROUP_WIDTH = D_MODEL // 4
SSD_HEAD_DIM = 64
SSD_HEADS = GROUP_WIDTH // SSD_HEAD_DIM
SSD_GROUPS = 2
SSD_STATE = 64
SSD_CONV = 4
SSD_CHUNK = 128
XBC_WIDTH = GROUP_WIDTH + 2 * SSD_GROUPS * SSD_STATE
S5_CH = 16
S5_GROUPS = GROUP_WIDTH // S5_CH
S5_STATE = 64
S5_LANES = S5_GROUPS * S5_STATE
SC_CONV = 3
POOL_WINDOWS = (2, 4, 8, 16)
POOL_GROUP = GROUP_WIDTH // len(POOL_WINDOWS)
POOL_HIST = max(POOL_WINDOWS) - 1
N_MEM = 256
XA_HEADS = 4
XA_HEAD_DIM = D_MODEL // XA_HEADS
D_FF = 4 * D_MODEL
EPS = 1e-6
P_ROWS = BATCH * SEQ
ALL_ROWS = P_ROWS + DEC_BATCH

LANE = 128
SUBLANE = 8
VMEM_LIMIT = 60 * 1024 * 1024

COL_Z = 0
COL_XBC = 512
COL_DT = 1280
A_WIDTH = 1536
ZX_WIDTH = COL_DT
COL_U5 = 0
COL_GB = 512
COL_GC = 1024
COL_HV = 1536
COL_UP = 2048
B_WIDTH = 2560

RESIDENT_ROW_TILE = ALL_ROWS // 13
PROJ_ROW_TILE = ALL_ROWS // 10
MLP_ROW_TILE = ALL_ROWS // 8
FF_TILE = 512
MEM_COL_TILE = 512
FINAL_ROW_TILE = 1024
SAMPLE_BLOCK = P_ROWS // DEC_BATCH

SAMPLE_ATTN_ROWS = 2
S5_CHUNK = 256
S5_SEG = S5_CHUNK // SUBLANE
S5_SCAN_LANES = 512


def _cparams(sem):
    return pltpu.CompilerParams(dimension_semantics=sem, vmem_limit_bytes=VMEM_LIMIT)


def _layer_spec(shape, layer):
    zeros = (0,) * len(shape)
    return pl.BlockSpec((None,) + tuple(shape), lambda *_: (layer,) + zeros)


def _whole_spec(shape):
    zeros = (0,) * len(shape)
    return pl.BlockSpec(tuple(shape), lambda *_: zeros)


def _weight_cols_spec(k, n, tn, layer):
    mode = pl.Buffered(1) if tn == n else None
    return pl.BlockSpec((None, k, tn), lambda i, j: (layer, 0, j), pipeline_mode=mode)


def _prompt_row_block(b, t, steps, rows):
    return jnp.minimum(b * steps + t, P_ROWS // rows)


def _prompt_batch(b):
    return jnp.minimum(b, BATCH - 1)


def _prompt_step(body, row_outputs):
    b = pl.program_id(0)
    pl.when(b < BATCH)(body)

    @pl.when(b == BATCH)
    def _():
        for ref in row_outputs:
            ref[...] = jnp.zeros_like(ref)


def _sigmoid(x):
    return 1.0 / (1.0 + jnp.exp(-x))


def _silu(x):
    return x * _sigmoid(x)


def _softplus(x):
    return jnp.maximum(x, 0.0) + jnp.log(1.0 + jnp.exp(-jnp.abs(x)))


def _gelu_tanh(x):
    return 0.5 * x * (1.0 + jnp.tanh(math.sqrt(2.0 / math.pi) * (x + 0.044715 * (x * x * x))))


def _rms(x, g):
    return x * lax.rsqrt(jnp.mean(x * x, axis=-1, keepdims=True) + EPS) * g


def _dot(a, b):
    return jnp.dot(a, b, preferred_element_type=F32)


def _split3_bf16(x):
    hi = x.astype(BF16)
    r = x - hi.astype(F32)
    mid = r.astype(BF16)
    lo = (r - mid.astype(F32)).astype(BF16)
    return hi, mid, lo


def _spread_exact(x, sel):
    hi, mid, lo = _split3_bf16(x)
    return _dot(hi, sel) + _dot(mid, sel) + _dot(lo, sel)


def _sum_rows_exact(sel, x):
    hi, mid, lo = _split3_bf16(x)
    return _dot(sel, hi) + _dot(sel, mid) + _dot(sel, lo)


def _split_bf16(x):
    hi = x.astype(BF16)
    lo = (x - hi.astype(F32)).astype(BF16)
    return hi, lo


def _norm_matmul_kernel(x_ref, g_ref, w_ref, o_ref, xn_ref):
    @pl.when(pl.program_id(1) == 0)
    def _():
        xn_ref[...] = _rms(x_ref[...], g_ref[...]).astype(BF16)

    o_ref[...] = _dot(xn_ref[...], w_ref[...].astype(BF16)).astype(o_ref.dtype)


def norm_matmul(x, g, w, layer, *, out_dtype=F32, tm, tn):
    m, k = x.shape
    n = w.shape[2]
    return pl.pallas_call(
        _norm_matmul_kernel,
        grid=(m // tm, n // tn),
        in_specs=[pl.BlockSpec((tm, k), lambda i, j: (i, 0)),
                  _layer_spec((1, k), layer),
                  _weight_cols_spec(k, n, tn, layer)],
        out_specs=pl.BlockSpec((tm, tn), lambda i, j: (i, j)),
        out_shape=jax.ShapeDtypeStruct((m, n), out_dtype),
        scratch_shapes=[pltpu.VMEM((tm, k), BF16)],
        compiler_params=_cparams(("parallel", "arbitrary")),
        name="norm_matmul",
    )(x, g, w)


def _in_proj_kernel(x_ref, g_ref, wa_ref, wb_ref, oa_ref, ob_ref):
    xn = _rms(x_ref[...], g_ref[...]).astype(BF16)
    oa_ref[...] = _dot(xn, wa_ref[...])
    ob_ref[...] = _dot(xn, wb_ref[...])


def in_proj(x, g, wa, wb, layer, *, tm):
    m, k = x.shape
    resident = lambda n: pl.BlockSpec((None, k, n), lambda i: (layer, 0, 0), pipeline_mode=pl.Buffered(1))
    return pl.pallas_call(
        _in_proj_kernel,
        grid=(m // tm,),
        in_specs=[pl.BlockSpec((tm, k), lambda i: (i, 0)), _layer_spec((1, k), layer),
                  resident(A_WIDTH), resident(B_WIDTH)],
        out_specs=(pl.BlockSpec((tm, A_WIDTH), lambda i: (i, 0)), pl.BlockSpec((tm, B_WIDTH), lambda i: (i, 0))),
        out_shape=(jax.ShapeDtypeStruct((m, A_WIDTH), F32), jax.ShapeDtypeStruct((m, B_WIDTH), F32)),
        compiler_params=_cparams(("parallel",)),
        name="in_proj",
    )(x, g, wa, wb)


def _res_matmul_kernel(x_ref, w_ref, r_ref, o_ref):
    o_ref[...] = r_ref[...] + _dot(x_ref[...], w_ref[...].astype(BF16))


def res_matmul(x, w, res, layer, *, tm, tn):
    m, k = x.shape
    n = w.shape[2]
    return pl.pallas_call(
        _res_matmul_kernel,
        grid=(m // tm, n // tn),
        in_specs=[pl.BlockSpec((tm, k), lambda i, j: (i, 0)),
                  _weight_cols_spec(k, n, tn, layer),
                  pl.BlockSpec((tm, tn), lambda i, j: (i, j))],
        out_specs=pl.BlockSpec((tm, tn), lambda i, j: (i, j)),
        out_shape=jax.ShapeDtypeStruct((m, n), F32),
        compiler_params=_cparams(("parallel", "arbitrary")),
        name="res_matmul",
    )(x, w, res)


def _out_proj_kernel(m0_ref, m1_ref, m2_ref, m3_ref, w_ref, r_ref, o_ref):
    x = jnp.concatenate([m0_ref[...], m1_ref[...], m2_ref[...], m3_ref[...]], axis=-1)
    o_ref[...] = r_ref[...] + _dot(x, w_ref[...].astype(BF16))


def out_proj(mixes, w, res, layer, *, tm):
    m = res.shape[0]
    mix_spec = pl.BlockSpec((tm, GROUP_WIDTH), lambda i, j: (i, 0))
    row_spec = pl.BlockSpec((tm, D_MODEL), lambda i, j: (i, 0))
    return pl.pallas_call(
        _out_proj_kernel,
        grid=(m // tm, 1),
        in_specs=[mix_spec] * 4 + [_weight_cols_spec(D_MODEL, D_MODEL, D_MODEL, layer), row_spec],
        out_specs=row_spec,
        out_shape=jax.ShapeDtypeStruct((m, D_MODEL), F32),
        compiler_params=_cparams(("parallel", "arbitrary")),
        name="out_proj",
    )(*mixes, w, res)


def _mlp_kernel(x_ref, g_ref, wu_ref, wd_ref, o_ref, xn_ref):
    j = pl.program_id(1)

    @pl.when(j == 0)
    def _():
        x = x_ref[...]
        xn_ref[...] = _rms(x, g_ref[...]).astype(BF16)
        o_ref[...] = x

    a = jnp.square(jnp.maximum(_dot(xn_ref[...], wu_ref[...].astype(BF16)), 0.0))
    o_ref[...] += _dot(a.astype(BF16), wd_ref[...].astype(BF16))


def mlp(x, g, w_up, w_down, layer, *, tm, tf):
    m = x.shape[0]
    return pl.pallas_call(
        _mlp_kernel,
        grid=(m // tm, D_FF // tf),
        in_specs=[pl.BlockSpec((tm, D_MODEL), lambda i, j: (i, 0)),
                  _layer_spec((1, D_MODEL), layer),
                  pl.BlockSpec((None, D_MODEL, tf), lambda i, j: (layer, 0, j)),
                  pl.BlockSpec((None, tf, D_MODEL), lambda i, j: (layer, j, 0))],
        out_specs=pl.BlockSpec((tm, D_MODEL), lambda i, j: (i, 0)),
        out_shape=jax.ShapeDtypeStruct((m, D_MODEL), F32),
        scratch_shapes=[pltpu.VMEM((tm, D_MODEL), BF16)],
        compiler_params=_cparams(("parallel", "arbitrary")),
        name="mlp",
    )(x, g, w_up, w_down)


def _final_norm_kernel(x_ref, g_ref, o_ref):
    o_ref[...] = _rms(x_ref[...], g_ref[...])


def final_norm(x, g, *, rows, first_block, tm):
    return pl.pallas_call(
        _final_norm_kernel,
        grid=(rows // tm,),
        in_specs=[pl.BlockSpec((tm, D_MODEL), lambda i: (first_block + i, 0)), _whole_spec((1, D_MODEL))],
        out_specs=pl.BlockSpec((tm, D_MODEL), lambda i: (i, 0)),
        out_shape=jax.ShapeDtypeStruct((rows, D_MODEL), F32),
        compiler_params=_cparams(("parallel",)),
        name="final_norm",
    )(x, g.reshape(1, D_MODEL))


def _prompt_attn_kernel(q_ref, k_ref, v_ref, o_ref):
    _prompt_step(lambda: _prompt_attn_body(q_ref, k_ref, v_ref, o_ref), (o_ref,))


def _prompt_attn_body(q_ref, k_ref, v_ref, o_ref):
    scale = XA_HEAD_DIM ** -0.5
    for h in range(XA_HEADS):
        cs = slice(h * XA_HEAD_DIM, (h + 1) * XA_HEAD_DIM)
        kh = k_ref[0, :, cs].astype(BF16)
        vh = v_ref[0, :, cs].astype(BF16)
        s = lax.dot_general(q_ref[:, cs], kh, (((1,), (1,)), ((), ())), preferred_element_type=F32) * scale
        p = jnp.exp(s - jnp.max(s, axis=-1, keepdims=True))
        p = p / jnp.sum(p, axis=-1, keepdims=True)
        o_ref[:, cs] = _dot(p.astype(BF16), vh).astype(o_ref.dtype)


def prompt_attn(q, k, v, *, tq):
    nq = SEQ // tq
    return pl.pallas_call(
        _prompt_attn_kernel,
        grid=(BATCH + 1, nq),
        in_specs=[pl.BlockSpec((tq, D_MODEL), lambda b, i: (_prompt_row_block(b, i, nq, tq), 0)),
                  pl.BlockSpec((1, N_MEM, D_MODEL), lambda b, i: (_prompt_batch(b), 0, 0)),
                  pl.BlockSpec((1, N_MEM, D_MODEL), lambda b, i: (_prompt_batch(b), 0, 0))],
        out_specs=pl.BlockSpec((tq, D_MODEL), lambda b, i: (_prompt_row_block(b, i, nq, tq), 0)),
        out_shape=jax.ShapeDtypeStruct((ALL_ROWS, D_MODEL), BF16),
        compiler_params=_cparams(("arbitrary", "arbitrary")),
        name="prompt_attn",
    )(q, k, v)


def _sample_attn_kernel(q_ref, k_ref, v_ref, o_ref):
    scale = XA_HEAD_DIM ** -0.5
    half, quarter = XA_HEAD_DIM // 2, XA_HEAD_DIM // 4
    for r in range(SAMPLE_ATTN_ROWS):
        prod = k_ref[r] * (q_ref[r] * scale)
        fold = prod[:, :, :half] + prod[:, :, half:]
        fold = fold[:, :, :quarter] + fold[:, :, quarter:]
        s = jnp.sum(fold, axis=-1, keepdims=True)
        e = jnp.exp(s - jnp.max(s, axis=0, keepdims=True))
        o = jnp.sum(e * v_ref[r], axis=0) / jnp.sum(e, axis=0)
        o_ref[r] = o.astype(o_ref.dtype)


def sample_attn(q, cache_k, cache_v, layer):
    nr = SAMPLE_ATTN_ROWS
    row = pl.BlockSpec((nr, XA_HEADS, XA_HEAD_DIM), lambda b: (b, 0, 0))
    mem = pl.BlockSpec((None, nr, N_MEM, XA_HEADS, XA_HEAD_DIM), lambda b: (layer, b, 0, 0, 0))
    return pl.pallas_call(
        _sample_attn_kernel,
        grid=(DEC_BATCH // nr,),
        in_specs=[row, mem, mem],
        out_specs=row,
        out_shape=jax.ShapeDtypeStruct((DEC_BATCH, XA_HEADS, XA_HEAD_DIM), BF16),
        compiler_params=_cparams(("parallel",)),
        name="sample_attn",
    )(q, cache_k, cache_v)


def _s5_abar(lr, li, ldt):
    delta = jnp.exp(ldt)
    mag = jnp.exp(lr * delta)
    return mag * jnp.cos(li * delta), mag * jnp.sin(li * delta)


def _s5_bbar_kernel(lr_ref, li_ref, ldt_ref, bre_ref, bim_ref, bbr_ref, bbi_ref):
    lr, li = lr_ref[...], li_ref[...]
    ar, ai = _s5_abar(lr, li, ldt_ref[...])
    den = lr * lr + li * li
    cr = ((ar - 1.0) * lr + ai * li) / den
    ci = (ai * lr - (ar - 1.0) * li) / den
    br, bi = bre_ref[...], bim_ref[...]
    bbr_ref[...] = cr * br - ci * bi
    bbi_ref[...] = cr * bi + ci * br


def _s5_pow_kernel(lr_ref, li_ref, ldt_ref, pr_ref, pi_ref):
    ar, ai = _s5_abar(lr_ref[...], li_ref[...], ldt_ref[...])
    qr, qi = ar, ai
    pr_ref[0] = qr
    pi_ref[0] = qi
    for e in range(1, S5_SEG):
        qr, qi = qr * ar - qi * ai, qr * ai + qi * ar
        pr_ref[e] = qr
        pi_ref[e] = qi


def s5_prepare(lam_re, lam_im, log_dt, b_re, b_im, c_re, c_im):
    dg = DEPTH * S5_GROUPS
    ldt = jnp.broadcast_to(log_dt[..., None], (DEPTH, S5_GROUPS, S5_STATE))
    rep = lambda a: jnp.repeat(a.reshape(dg, S5_STATE), S5_CH, axis=0)
    to_rows = lambda b: jnp.transpose(b, (0, 1, 3, 2)).reshape(dg * S5_CH, S5_STATE)
    shp = jax.ShapeDtypeStruct((dg * S5_CH, S5_STATE), F32)
    bbr, bbi = pl.pallas_call(_s5_bbar_kernel, out_shape=(shp, shp), name="s5_bbar")(
        rep(lam_re), rep(lam_im), rep(ldt), to_rows(b_re), to_rows(b_im))
    pshp = jax.ShapeDtypeStruct((S5_SEG, dg, S5_STATE), F32)
    pr, pi = pl.pallas_call(_s5_pow_kernel, out_shape=(pshp, pshp), name="s5_pow")(
        lam_re.reshape(dg, S5_STATE), lam_im.reshape(dg, S5_STATE), ldt.reshape(dg, S5_STATE))

    gpb = LANE // S5_CH
    nblk = S5_GROUPS // gpb
    eye = jnp.eye(gpb, dtype=F32)

    def b_blocks(bb):
        bb = bb.reshape(DEPTH, nblk, gpb, S5_CH, S5_STATE)
        return jnp.einsum("digkn,gh->digkhn", bb, eye).reshape(DEPTH, nblk, LANE, gpb * S5_STATE).astype(BF16)

    def c_blocks(cc):
        cc = cc.reshape(DEPTH, nblk, gpb, S5_CH, S5_STATE)
        return jnp.einsum("digkn,gh->dignhk", cc, eye).reshape(DEPTH, nblk, gpb * S5_STATE, LANE).astype(BF16)

    def pow_rows(p):
        return jnp.transpose(p.reshape(S5_SEG, DEPTH, S5_LANES), (1, 0, 2))

    pw_r, pw_i = pow_rows(pr), pow_rows(pi)
    tile = lambda p: jnp.broadcast_to(p[:, :, None, :], (DEPTH, S5_SEG, SUBLANE, S5_LANES))
    return dict(b_re=b_blocks(bbr), b_im=b_blocks(bbi), c_re=c_blocks(c_re), c_imn=c_blocks(-c_im),
                pw_re=pw_r, pw_im=pw_i, pwt_re=tile(pw_r), pwt_im=tile(pw_i))


def _s5_bu(ub, bre_ref, bim_ref):
    res_r, res_i = [], []
    for i in range(GROUP_WIDTH // LANE):
        ui = ub[:, i * LANE:(i + 1) * LANE]
        res_r.append(_dot(ui, bre_ref[i]))
        res_i.append(_dot(ui, bim_ref[i]))
    return res_r, res_i


def _s5_tail(u, hs_re, hs_im, cre_ref, cimn_ref, d_ref, wglu_ref, bglu_ref, g_ref):
    ys = []
    for i in range(GROUP_WIDTH // LANE):
        ys.append(_dot(hs_re(i).astype(BF16), cre_ref[i]) + _dot(hs_im(i).astype(BF16), cimn_ref[i]))
    y = jnp.concatenate(ys, axis=-1) + d_ref[...] * u
    y = _gelu_tanh(y)
    y = y * _sigmoid(_dot(y.astype(BF16), wglu_ref[...]) + bglu_ref[...])
    return _rms(y, g_ref[...])


def _s5_param_specs(layer):
    nblk = GROUP_WIDTH // LANE
    sblk = S5_LANES // nblk
    return [_layer_spec((nblk, LANE, sblk), layer),
            _layer_spec((nblk, LANE, sblk), layer),
            _layer_spec((nblk, sblk, LANE), layer),
            _layer_spec((nblk, sblk, LANE), layer),
            _layer_spec((1, GROUP_WIDTH), layer),
            _layer_spec((GROUP_WIDTH, GROUP_WIDTH), layer),
            _layer_spec((1, GROUP_WIDTH), layer),
            _layer_spec((1, GROUP_WIDTH), layer)]


def _s5_param_args(sp):
    return (sp["s5_b_re"], sp["s5_b_im"], sp["s5_c_re"], sp["s5_c_imn"], sp["s5_d"], sp["s5_w_glu"],
            sp["s5_b_glu"], sp["g_s5"])


def _cmul_add(x_r, x_i, a_r, a_i, h_r, h_i):
    return x_r + a_r * h_r - a_i * h_i, x_i + a_r * h_i + a_i * h_r


def _prompt_s5_body(u_ref, perm_ref, unperm_ref, pw_re_ref, pw_im_ref, pwt_re_ref, pwt_im_ref,
                    bre_ref, bim_ref, cre_ref, cimn_ref, d_ref, wglu_ref, bglu_ref, g_ref,
                    mix_ref, sre_ref, sim_ref, hre_ref, him_ref, cr_ref, ci_ref):
    sblk = S5_LANES // (GROUP_WIDTH // LANE)
    W = S5_SCAN_LANES

    @pl.when(pl.program_id(1) == 0)
    def _():
        cr_ref[...] = jnp.zeros_like(cr_ref)
        ci_ref[...] = jnp.zeros_like(ci_ref)

    u_hi, u_lo = _split_bf16(u_ref[...])
    ub = _dot(perm_ref[...], u_hi)
    u = ub + _dot(perm_ref[...], u_lo)
    bu_r, bu_i = _s5_bu(ub.astype(BF16), bre_ref, bim_ref)
    for i in range(len(bu_r)):
        hre_ref[:, i * sblk:(i + 1) * sblk] = bu_r[i]
        him_ref[:, i * sblk:(i + 1) * sblk] = bu_i[i]

    row = lax.broadcasted_iota(jnp.int32, (SUBLANE, W), 0)
    for lb in range(0, S5_LANES, W):
        ls = slice(lb, lb + W)
        a_r = jnp.broadcast_to(pw_re_ref[0:1, ls], (SUBLANE, W))
        a_i = jnp.broadcast_to(pw_im_ref[0:1, ls], (SUBLANE, W))

        def local_step(j, h, ls=ls, a_r=a_r, a_i=a_i):
            rs = pl.ds(pl.multiple_of(j * SUBLANE, SUBLANE), SUBLANE)
            n_r, n_i = _cmul_add(hre_ref[rs, ls], him_ref[rs, ls], a_r, a_i, h[0], h[1])
            hre_ref[rs, ls] = n_r
            him_ref[rs, ls] = n_i
            return n_r, n_i

        zero = jnp.zeros((SUBLANE, W), F32)
        e_r, e_i = lax.fori_loop(0, S5_SEG, local_step, (zero, zero))

        s_r = pw_re_ref[S5_SEG - 1:S5_SEG, ls]
        s_i = pw_im_ref[S5_SEG - 1:S5_SEG, ls]
        c_r, c_i = cr_ref[:, ls], ci_ref[:, ls]
        in_r, in_i = zero, zero
        for s in range(SUBLANE):
            in_r = jnp.where(row == s, c_r, in_r)
            in_i = jnp.where(row == s, c_i, in_i)
            c_r, c_i = _cmul_add(e_r[s:s + 1, :], e_i[s:s + 1, :], s_r, s_i, c_r, c_i)
        cr_ref[:, ls] = c_r
        ci_ref[:, ls] = c_i

        def fix_step(j, carry, ls=ls, in_r=in_r, in_i=in_i):
            rs = pl.ds(pl.multiple_of(j * SUBLANE, SUBLANE), SUBLANE)
            n_r, n_i = _cmul_add(hre_ref[rs, ls], him_ref[rs, ls], pwt_re_ref[j, :, ls], pwt_im_ref[j, :, ls],
                                 in_r, in_i)
            hre_ref[rs, ls] = n_r
            him_ref[rs, ls] = n_i
            return carry

        lax.fori_loop(0, S5_SEG, fix_step, 0)

    sre_ref[0] = cr_ref[...]
    sim_ref[0] = ci_ref[...]
    y = _s5_tail(u, lambda i: hre_ref[:, i * sblk:(i + 1) * sblk], lambda i: him_ref[:, i * sblk:(i + 1) * sblk],
                 cre_ref, cimn_ref, d_ref, wglu_ref, bglu_ref, g_ref)
    mix_ref[...] = _dot(unperm_ref[...], y.astype(mix_ref.dtype)).astype(mix_ref.dtype)


def _prompt_ssd_kernel(*refs):
    _prompt_step(lambda: _prompt_ssd_body(*refs), (refs[10],))


def _prompt_ssd_body(zx_ref, dt_ref, cw_ref, cb_ref, dtb_ref, alog_ref, dexp_ref, g_ref, hexp_ref, hcol_ref,
                     mix_ref, cst_ref, hst_ref, xbuf_ref, h_ref):
    L = SSD_CHUNK
    hist = SSD_CONV - 1
    base = SUBLANE

    @pl.when(pl.program_id(1) == 0)
    def _():
        xbuf_ref[0:base, :] = jnp.zeros((base, XBC_WIDTH), F32)
        h_ref[...] = jnp.zeros_like(h_ref)

    @pl.when(pl.program_id(1) > 0)
    def _():
        xbuf_ref[base - hist:base, :] = xbuf_ref[base + L - hist:base + L, :]

    xbc = zx_ref[:, COL_XBC:COL_XBC + XBC_WIDTH]
    xbuf_ref[base:base + L, :] = xbc
    cst_ref[0] = xbuf_ref[base + L - hist:base + L, :]
    conv = cb_ref[...] + cw_ref[hist:hist + 1, :] * xbc
    for k in range(hist):
        conv = conv + cw_ref[k:k + 1, :] * xbuf_ref[base - hist + k:base - hist + k + L, :]
    xc = _silu(conv)
    xs = xc[:, :GROUP_WIDTH]
    ng = SSD_GROUPS * SSD_STATE
    bm = xc[:, GROUP_WIDTH:GROUP_WIDTH + ng].astype(BF16)
    cm = xc[:, GROUP_WIDTH + ng:].astype(BF16)

    dt = _softplus(dt_ref[...] + dtb_ref[...])
    a = -jnp.exp(alog_ref[...])
    ri = lax.broadcasted_iota(jnp.int32, (L, L), 0)
    ci = lax.broadcasted_iota(jnp.int32, (L, L), 1)
    causal = ri >= ci
    acum = _sum_rows_exact(causal.astype(BF16), dt * a)
    acum_t = acum.T
    last = acum[L - 1:L, :]
    to_end = jnp.exp(last - acum)
    e_acum = jnp.exp(acum)
    chunk_decay = jnp.exp(last)

    xdt_all = xs * _spread_exact(dt, hexp_ref[...])
    xdt_end = (xdt_all * _spread_exact(to_end, hexp_ref[...])).astype(BF16)
    xdt_all = xdt_all.astype(BF16)
    e_acum_all = _spread_exact(e_acum, hexp_ref[...])
    acum_cols = _spread_exact(acum, hcol_ref[...])

    ys_diag, ys_off = [], []
    rep = SSD_HEADS // SSD_GROUPS
    cb = [lax.dot_general(cm[:, g * SSD_STATE:(g + 1) * SSD_STATE], bm[:, g * SSD_STATE:(g + 1) * SSD_STATE],
                          (((1,), (1,)), ((), ())), preferred_element_type=F32) for g in range(SSD_GROUPS)]
    for h in range(SSD_HEADS):
        g = h // rep
        hs = slice(h * SSD_HEAD_DIM, (h + 1) * SSD_HEAD_DIM)
        gs = slice(g * SSD_STATE, (g + 1) * SSD_STATE)
        seg = acum_cols[:, h * L:(h + 1) * L] - acum_t[h:h + 1, :]
        decay = jnp.exp(jnp.where(causal, seg, -jnp.inf))
        ys_diag.append(_dot((cb[g] * decay).astype(BF16), xdt_all[:, hs]))
        h_prev = h_ref[h]
        ys_off.append(lax.dot_general(cm[:, gs], h_prev.astype(BF16), (((1,), (1,)), ((), ())),
                                      preferred_element_type=F32))
        st = lax.dot_general(xdt_end[:, hs], bm[:, gs], (((0,), (0,)), ((), ())), preferred_element_type=F32)
        h_ref[h] = h_prev * chunk_decay[:, h:h + 1] + st
    y = jnp.concatenate(ys_diag, axis=-1) + jnp.concatenate(ys_off, axis=-1) * e_acum_all
    y = (y + dexp_ref[...] * xs) * _silu(zx_ref[:, COL_Z:COL_Z + GROUP_WIDTH])
    mix_ref[...] = _rms(y, g_ref[...]).astype(mix_ref.dtype)
    hst_ref[0] = h_ref[...]


def prompt_ssd(proj, sp, layer):
    nc = SEQ // SSD_CHUNK
    rb = lambda b, c: _prompt_row_block(b, c, nc, SSD_CHUNK)
    head_of_row = np.arange(LANE)[:, None]
    head_lanes = jnp.asarray(head_of_row == np.arange(GROUP_WIDTH)[None, :] // SSD_HEAD_DIM, BF16)
    head_cols = jnp.asarray(head_of_row == np.arange(SSD_HEADS * SSD_CHUNK)[None, :] // SSD_CHUNK, BF16)
    return pl.pallas_call(
        _prompt_ssd_kernel,
        grid=(BATCH + 1, nc),
        in_specs=[pl.BlockSpec((SSD_CHUNK, ZX_WIDTH), lambda b, c: (rb(b, c), 0)),
                  pl.BlockSpec((SSD_CHUNK, LANE), lambda b, c: (rb(b, c), COL_DT // LANE)),
                  _layer_spec((SSD_CONV, XBC_WIDTH), layer),
                  _layer_spec((1, XBC_WIDTH), layer),
                  _layer_spec((1, LANE), layer),
                  _layer_spec((1, LANE), layer),
                  _layer_spec((1, GROUP_WIDTH), layer),
                  _layer_spec((1, GROUP_WIDTH), layer),
                  _whole_spec(head_lanes.shape), _whole_spec(head_cols.shape)],
        out_specs=(pl.BlockSpec((SSD_CHUNK, GROUP_WIDTH), lambda b, c: (rb(b, c), 0)),
                   pl.BlockSpec((1, SSD_CONV - 1, XBC_WIDTH), lambda b, c: (_prompt_batch(b), 0, 0)),
                   pl.BlockSpec((1, SSD_HEADS, SSD_HEAD_DIM, SSD_STATE), lambda b, c: (_prompt_batch(b), 0, 0, 0))),
        out_shape=(jax.ShapeDtypeStruct((ALL_ROWS, GROUP_WIDTH), BF16),
                   jax.ShapeDtypeStruct((BATCH, SSD_CONV - 1, XBC_WIDTH), F32),
                   jax.ShapeDtypeStruct((BATCH, SSD_HEADS, SSD_HEAD_DIM, SSD_STATE), F32)),
        scratch_shapes=[pltpu.VMEM((SUBLANE + SSD_CHUNK, XBC_WIDTH), F32),
                        pltpu.VMEM((SSD_HEADS, SSD_HEAD_DIM, SSD_STATE), F32)],
        compiler_params=_cparams(("arbitrary", "arbitrary")),
        name="prompt_ssd",
    )(proj, proj, sp["ssd_conv_w"], sp["ssd_conv_b"], sp["ssd_dt_bias"], sp["ssd_a_log"],
      sp["ssd_d_exp"], sp["g_ssd"], head_lanes, head_cols)


def _pool_counts(pos, w):
    return jnp.minimum(w, pos + 1).astype(F32)


def _prompt_convpool_body(gb_ref, gc_ref, hv_ref, up_ref, scw_ref, pw_ref, ps_ref, gsc_ref, gpl_ref,
                          msc_ref, mpl_ref, scst_ref, plst_ref, vbuf_ref, pbuf_ref):
    rows = gb_ref.shape[0]
    vb = SUBLANE
    pb = 2 * SUBLANE
    vh = SC_CONV - 1
    t = pl.program_id(1)

    @pl.when(t == 0)
    def _():
        vbuf_ref[0:vb, :] = jnp.zeros((vb, GROUP_WIDTH), F32)
        pbuf_ref[0:pb, :] = jnp.zeros((pb, GROUP_WIDTH), F32)

    @pl.when(t > 0)
    def _():
        vbuf_ref[vb - vh:vb, :] = vbuf_ref[vb + rows - vh:vb + rows, :]
        pbuf_ref[0:pb, :] = pbuf_ref[rows:rows + pb, :]

    v = gc_ref[...] * hv_ref[...]
    vbuf_ref[vb:vb + rows, :] = v
    acc = scw_ref[vh:vh + 1, :] * v
    for k in range(vh):
        acc = acc + scw_ref[k:k + 1, :] * vbuf_ref[vb - vh + k:vb - vh + k + rows, :]
    msc_ref[...] = _rms(gb_ref[...] * acc, gsc_ref[...]).astype(msc_ref.dtype)
    scst_ref[0] = vbuf_ref[vb + rows - vh:vb + rows, :]

    u = up_ref[...]
    pbuf_ref[pb:pb + rows, :] = u
    pos = t * rows + lax.broadcasted_iota(jnp.int32, (rows, 1), 0)
    ys = []
    for gi, w in enumerate(POOL_WINDOWS):
        cs = slice(gi * POOL_GROUP, (gi + 1) * POOL_GROUP)
        s = u[:, cs]
        for j in range(1, w):
            s = s + pbuf_ref[pb - j:pb - j + rows, cs]
        pooled = s / _pool_counts(pos, w) - u[:, cs]
        ys.append(_dot(pooled.astype(BF16), pw_ref[gi]))
    y = jnp.concatenate(ys, axis=-1) * ps_ref[...]
    mpl_ref[...] = _rms(y, gpl_ref[...]).astype(mpl_ref.dtype)
    plst_ref[0] = pbuf_ref[pb + rows - POOL_HIST:pb + rows, :]


N_S5_IN, N_S5_OUT, N_S5_SCRATCH = 15, 3, 4
N_CP_IN, N_CP_OUT = 9, 4


def _prompt_b_mixers_kernel(*refs):
    s5_in, refs = refs[:N_S5_IN], refs[N_S5_IN:]
    cp_in, refs = refs[:N_CP_IN], refs[N_CP_IN:]
    s5_out, refs = refs[:N_S5_OUT], refs[N_S5_OUT:]
    cp_out, refs = refs[:N_CP_OUT], refs[N_CP_OUT:]
    s5_scratch, cp_scratch = refs[:N_S5_SCRATCH], refs[N_S5_SCRATCH:]

    def body():
        _prompt_s5_body(*s5_in, *s5_out, *s5_scratch)
        _prompt_convpool_body(*cp_in, *cp_out, *cp_scratch)

    _prompt_step(body, (s5_out[0], cp_out[0], cp_out[1]))


def prompt_b_mixers(proj, sp, layer):
    nt = SEQ // S5_CHUNK
    t_of_row = (np.arange(S5_CHUNK) % SUBLANE) * S5_SEG + np.arange(S5_CHUNK) // SUBLANE
    perm = jnp.asarray(np.eye(S5_CHUNK, dtype=np.float32)[t_of_row], BF16)
    rb = lambda b, t: _prompt_row_block(b, t, nt, S5_CHUNK)
    col = lambda c: pl.BlockSpec((S5_CHUNK, GROUP_WIDTH), lambda b, t: (rb(b, t), c // GROUP_WIDTH))
    mix_spec = pl.BlockSpec((S5_CHUNK, GROUP_WIDTH), lambda b, t: (rb(b, t), 0))
    mix_shape = jax.ShapeDtypeStruct((ALL_ROWS, GROUP_WIDTH), BF16)
    seq_spec = lambda *s: pl.BlockSpec((1,) + s, lambda b, t: (_prompt_batch(b),) + (0,) * len(s))
    seq_shape = lambda *s: jax.ShapeDtypeStruct((BATCH,) + s, F32)
    s5_in_specs = [col(COL_U5), _whole_spec((S5_CHUNK, S5_CHUNK)), _whole_spec((S5_CHUNK, S5_CHUNK)),
                   _layer_spec((S5_SEG, S5_LANES), layer), _layer_spec((S5_SEG, S5_LANES), layer),
                   _layer_spec((S5_SEG, SUBLANE, S5_LANES), layer),
                   _layer_spec((S5_SEG, SUBLANE, S5_LANES), layer)] + _s5_param_specs(layer)
    cp_in_specs = [col(COL_GB), col(COL_GC), col(COL_HV), col(COL_UP),
                   _layer_spec((SC_CONV, GROUP_WIDTH), layer),
                   _layer_spec((len(POOL_WINDOWS), POOL_GROUP, POOL_GROUP), layer),
                   _layer_spec((1, GROUP_WIDTH), layer), _layer_spec((1, GROUP_WIDTH), layer),
                   _layer_spec((1, GROUP_WIDTH), layer)]
    assert len(s5_in_specs) == N_S5_IN and len(cp_in_specs) == N_CP_IN
    return pl.pallas_call(
        _prompt_b_mixers_kernel,
        grid=(BATCH + 1, nt),
        in_specs=s5_in_specs + cp_in_specs,
        out_specs=(mix_spec, seq_spec(1, S5_LANES), seq_spec(1, S5_LANES),
                   mix_spec, mix_spec, seq_spec(SC_CONV - 1, GROUP_WIDTH), seq_spec(POOL_HIST, GROUP_WIDTH)),
        out_shape=(mix_shape, seq_shape(1, S5_LANES), seq_shape(1, S5_LANES),
                   mix_shape, mix_shape, seq_shape(SC_CONV - 1, GROUP_WIDTH), seq_shape(POOL_HIST, GROUP_WIDTH)),
        scratch_shapes=[pltpu.VMEM((S5_CHUNK, S5_LANES), F32), pltpu.VMEM((S5_CHUNK, S5_LANES), F32),
                        pltpu.VMEM((1, S5_LANES), F32), pltpu.VMEM((1, S5_LANES), F32),
                        pltpu.VMEM((SUBLANE + S5_CHUNK, GROUP_WIDTH), F32),
                        pltpu.VMEM((2 * SUBLANE + S5_CHUNK, GROUP_WIDTH), F32)],
        compiler_params=_cparams(("arbitrary", "arbitrary")),
        name="prompt_b_mixers",
    )(proj, perm, perm.T, sp["s5_pw_re"], sp["s5_pw_im"], sp["s5_pwt_re"], sp["s5_pwt_im"], *_s5_param_args(sp),
      proj, proj, proj, proj, sp["sc_conv_w"], sp["pool_w"], sp["pool_scale"], sp["g_sc"], sp["g_pool"])


def _sample_mix_kernel(proj_ref, pb_ref, cprev_ref, s5r_ref, s5i_ref, scprev_ref, plprev_ref,
                       cw_ref, cb_ref, dtb_ref, alog_ref,
                       pwr_ref, pwi_ref, bre_ref, bim_ref, cre_ref, cimn_ref, d5_ref, wglu_ref, bglu_ref, g5_ref,
                       scw_ref, pw_ref, ps_ref, gsc_ref, gpl_ref,
                       m5_ref, msc_ref, mpl_ref, cst_ref, s5ro_ref, s5io_ref, scst_ref, plst_ref,
                       xs_ref, xdt_ref, da_ref, b2_ref, c2_ref):
    nb = proj_ref.shape[0]
    W = GROUP_WIDTH
    xbc = proj_ref[:, COL_XBC:COL_XBC + XBC_WIDTH]
    hist = SSD_CONV - 1
    conv = cb_ref[...] + cw_ref[hist:hist + 1, :] * xbc
    for k in range(hist):
        conv = conv + cw_ref[k:k + 1, :] * cprev_ref[:, k * XBC_WIDTH:(k + 1) * XBC_WIDTH]
    cst_ref[:, 0:(hist - 1) * XBC_WIDTH] = cprev_ref[:, XBC_WIDTH:hist * XBC_WIDTH]
    cst_ref[:, (hist - 1) * XBC_WIDTH:hist * XBC_WIDTH] = xbc
    xc = _silu(conv)
    xs = xc[:, :W]
    bm = xc[:, W:W + LANE]
    cm = xc[:, W + LANE:W + 2 * LANE]
    dt = _softplus(proj_ref[:, COL_DT:COL_DT + LANE] + dtb_ref[...])
    da = jnp.exp(dt * (-jnp.exp(alog_ref[...])))
    lane_w = lax.broadcasted_iota(jnp.int32, (nb, W), 1)
    dt_exp = jnp.zeros((nb, W), F32)
    for h in range(SSD_HEADS):
        dt_exp = jnp.where(lane_w // SSD_HEAD_DIM == h, dt[:, h:h + 1], dt_exp)
        da_ref[:, h * LANE:(h + 1) * LANE] = jnp.broadcast_to(da[:, h:h + 1], (nb, LANE))
    xs_ref[...] = xs
    xdt_ref[...] = xs * dt_exp
    lane = lax.broadcasted_iota(jnp.int32, (nb, LANE), 1)
    low = lane < SSD_STATE
    for src, dst in ((bm, b2_ref), (cm, c2_ref)):
        swapped = pltpu.roll(src, SSD_STATE, 1)
        dst[:, 0:LANE] = jnp.where(low, src, swapped)
        dst[:, LANE:2 * LANE] = jnp.where(low, swapped, src)

    u5 = pb_ref[:, COL_U5:COL_U5 + W]
    bu_r, bu_i = _s5_bu(u5.astype(BF16), bre_ref, bim_ref)
    sblk = S5_LANES // len(bu_r)
    for i in range(len(bu_r)):
        ls = slice(i * sblk, (i + 1) * sblk)
        n_r, n_i = _cmul_add(bu_r[i], bu_i[i], pwr_ref[0:1, ls], pwi_ref[0:1, ls], s5r_ref[:, ls], s5i_ref[:, ls])
        s5ro_ref[:, ls] = n_r
        s5io_ref[:, ls] = n_i
    y5 = _s5_tail(u5, lambda i: s5ro_ref[:, i * sblk:(i + 1) * sblk], lambda i: s5io_ref[:, i * sblk:(i + 1) * sblk],
                  cre_ref, cimn_ref, d5_ref, wglu_ref, bglu_ref, g5_ref)
    m5_ref[...] = y5.astype(m5_ref.dtype)

    v = pb_ref[:, COL_GC:COL_GC + W] * pb_ref[:, COL_HV:COL_HV + W]
    vh = SC_CONV - 1
    acc = scw_ref[vh:vh + 1, :] * v
    for k in range(vh):
        acc = acc + scw_ref[k:k + 1, :] * scprev_ref[:, k * W:(k + 1) * W]
    msc_ref[...] = _rms(pb_ref[:, COL_GB:COL_GB + W] * acc, gsc_ref[...]).astype(msc_ref.dtype)
    scst_ref[:, 0:(vh - 1) * W] = scprev_ref[:, W:vh * W]
    scst_ref[:, (vh - 1) * W:vh * W] = v

    up = pb_ref[:, COL_UP:COL_UP + W]
    ys = []
    for gi, w in enumerate(POOL_WINDOWS):
        cs = slice(gi * POOL_GROUP, (gi + 1) * POOL_GROUP)
        s = up[:, cs]
        for j in range(1, w):
            k = POOL_HIST - j
            s = s + plprev_ref[:, k * W + gi * POOL_GROUP:k * W + (gi + 1) * POOL_GROUP]
        pooled = s / float(min(w, PAST_LEN + 1)) - up[:, cs]
        ys.append(_dot(pooled.astype(BF16), pw_ref[gi]))
    y = jnp.concatenate(ys, axis=-1) * ps_ref[...]
    mpl_ref[...] = _rms(y, gpl_ref[...]).astype(mpl_ref.dtype)
    plst_ref[:, 0:(POOL_HIST - 1) * W] = plprev_ref[:, W:POOL_HIST * W]
    plst_ref[:, (POOL_HIST - 1) * W:POOL_HIST * W] = up


def sample_mix(proj_a, proj_b, states, sp, layer):
    nb = DEC_BATCH
    W = GROUP_WIDTH
    f = lambda n: jax.ShapeDtypeStruct((nb, n), F32)
    b = lambda n: jax.ShapeDtypeStruct((nb, n), BF16)
    out_widths = ((SSD_CONV - 1) * XBC_WIDTH, S5_LANES, S5_LANES, (SC_CONV - 1) * W, POOL_HIST * W,
                  W, W, SSD_HEADS * LANE, SSD_GROUPS * LANE, SSD_GROUPS * LANE)
    out_shape = (b(W), b(W), b(W)) + tuple(f(n) for n in out_widths)
    nblk = GROUP_WIDTH // LANE
    in_specs = ([pl.BlockSpec((nb, A_WIDTH), lambda i: (SAMPLE_BLOCK, 0)),
                 pl.BlockSpec((nb, B_WIDTH), lambda i: (SAMPLE_BLOCK, 0))]
                + [_layer_spec((nb, s.shape[2]), layer) for s in states]
                + [_layer_spec((SSD_CONV, XBC_WIDTH), layer), _layer_spec((1, XBC_WIDTH), layer),
                   _layer_spec((1, LANE), layer), _layer_spec((1, LANE), layer),
                   _layer_spec((S5_SEG, S5_LANES), layer), _layer_spec((S5_SEG, S5_LANES), layer)]
                + _s5_param_specs(layer)
                + [_layer_spec((SC_CONV, W), layer), _layer_spec((len(POOL_WINDOWS), POOL_GROUP, POOL_GROUP), layer),
                   _layer_spec((1, W), layer), _layer_spec((1, W), layer), _layer_spec((1, W), layer)])
    return pl.pallas_call(
        _sample_mix_kernel,
        grid=(1,),
        in_specs=in_specs,
        out_specs=tuple(_whole_spec(s.shape) for s in out_shape),
        out_shape=out_shape,
        compiler_params=_cparams(("arbitrary",)),
        name="sample_mix",
    )(proj_a, proj_b, *states, sp["ssd_conv_w"], sp["ssd_conv_b"], sp["ssd_dt_bias"], sp["ssd_a_log"],
      sp["s5_pw_re"], sp["s5_pw_im"], *_s5_param_args(sp),
      sp["sc_conv_w"], sp["pool_w"], sp["pool_scale"], sp["g_sc"], sp["g_pool"])


def _sample_ssd_kernel(h0_ref, xdt_ref, da_ref, b2_ref, c2_ref, e_ref, r_ref, xs_ref, z_ref, dexp_ref, g_ref,
                       hn_ref, mix_ref, xrep_ref, prod_ref, y_ref):
    hp = pl.program_id(0)
    hpl = 2 * SSD_HEAD_DIM * SSD_STATE
    per_head = SSD_HEAD_DIM * SSD_STATE
    x_hi, x_lo = _split_bf16(xdt_ref[...])
    xrep_ref[...] = _dot(x_hi, e_ref[...]) + _dot(x_lo, e_ref[...])
    b2, c2 = b2_ref[...], c2_ref[...]
    for j in range(hpl // LANE):
        ls = slice(j * LANE, (j + 1) * LANE)
        hl = (j * LANE) // per_head
        hn = da_ref[:, hl * LANE:(hl + 1) * LANE] * h0_ref[:, ls] + xrep_ref[:, ls] * b2
        hn_ref[:, ls] = hn
        prod_ref[:, ls] = hn * c2
    p_hi, p_lo = _split_bf16(prod_ref[...])
    y_ref[hp] = _dot(p_hi, r_ref[...]) + _dot(p_lo, r_ref[...])

    @pl.when(hp == pl.num_programs(0) - 1)
    def _():
        y = jnp.concatenate([y_ref[i] for i in range(SSD_HEADS // 2)], axis=-1)
        y = (y + dexp_ref[...] * xs_ref[...]) * _silu(z_ref[...])
        mix_ref[...] = _rms(y, g_ref[...]).astype(mix_ref.dtype)


def sample_ssd(h0_all, xdt, da, b2, c2, xs, proj, sp, layer):
    nb = DEC_BATCH
    npairs = SSD_HEADS // 2
    hpl = 2 * SSD_HEAD_DIM * SSD_STATE
    expand = jnp.repeat(jnp.eye(LANE, dtype=BF16), SSD_STATE, axis=1)
    return pl.pallas_call(
        _sample_ssd_kernel,
        grid=(npairs,),
        in_specs=[pl.BlockSpec((None, nb, hpl), lambda i: (layer, 0, i)),
                  pl.BlockSpec((nb, LANE), lambda i: (0, i)),
                  pl.BlockSpec((nb, 2 * LANE), lambda i: (0, i)),
                  pl.BlockSpec((nb, LANE), lambda i: (0, i // (npairs // SSD_GROUPS))),
                  pl.BlockSpec((nb, LANE), lambda i: (0, i // (npairs // SSD_GROUPS))),
                  _whole_spec((LANE, hpl)),
                  _whole_spec((hpl, LANE)),
                  _whole_spec((nb, GROUP_WIDTH)),
                  pl.BlockSpec((nb, GROUP_WIDTH), lambda i: (SAMPLE_BLOCK, COL_Z // GROUP_WIDTH)),
                  _layer_spec((1, GROUP_WIDTH), layer),
                  _layer_spec((1, GROUP_WIDTH), layer)],
        out_specs=(pl.BlockSpec((nb, hpl), lambda i: (0, i)),
                   _whole_spec((nb, GROUP_WIDTH))),
        out_shape=(jax.ShapeDtypeStruct((nb, SSD_HEADS * SSD_HEAD_DIM * SSD_STATE), F32),
                   jax.ShapeDtypeStruct((nb, GROUP_WIDTH), BF16)),
        scratch_shapes=[pltpu.VMEM((nb, hpl), F32), pltpu.VMEM((nb, hpl), F32),
                        pltpu.VMEM((npairs, nb, LANE), F32)],
        compiler_params=_cparams(("arbitrary",)),
        name="sample_ssd",
    )(h0_all, xdt, da, b2, c2, expand, expand.T, xs, proj, sp["ssd_d_exp"], sp["g_ssd"])


def _split_w_in(w_in):
    head = COL_DT + SSD_HEADS
    part_a = jnp.pad(w_in[..., :head], ((0, 0), (0, 0), (0, A_WIDTH - head)))
    return part_a.astype(BF16), w_in[..., head:].astype(BF16)


def _stacked_params(w):
    row = lambda v: v.reshape(DEPTH, 1, -1)
    pad_heads = lambda v: row(jnp.pad(v, ((0, 0), (0, LANE - SSD_HEADS))))
    g_mix = w["mix_out_g"].reshape(DEPTH, 4, 1, GROUP_WIDTH)
    w_in_a, w_in_b = _split_w_in(w["w_in"])
    sp = dict(w_in_a=w_in_a, w_in_b=w_in_b)
    for k in ("s5_w_glu", "pool_w"):
        sp[k] = w[k].astype(BF16)
    for k in ("w_out", "w_q", "w_k", "w_v", "w_o", "w_up", "w_down"):
        sp[k] = w[k]
    for k in ("norm_mix_g", "norm_xa_g", "norm_mem_g", "norm_mlp_g", "ssd_conv_b", "s5_d", "s5_b_glu", "pool_scale"):
        sp[k] = row(w[k])
    sp.update(ssd_conv_w=w["ssd_conv_w"], sc_conv_w=w["sc_conv_w"],
              ssd_dt_bias=pad_heads(w["ssd_dt_bias"]), ssd_a_log=pad_heads(w["ssd_a_log"]),
              ssd_d_exp=row(jnp.repeat(w["ssd_d"], SSD_HEAD_DIM, axis=1)),
              g_ssd=g_mix[:, 0], g_s5=g_mix[:, 1], g_sc=g_mix[:, 2], g_pool=g_mix[:, 3])
    s5 = s5_prepare(w["s5_lam_re"], w["s5_lam_im"], w["s5_log_dt"], w["s5_b_re"], w["s5_b_im"],
                    w["s5_c_re"], w["s5_c_im"])
    sp.update({"s5_" + k: v for k, v in s5.items()})
    return sp


def _forward(x_prompt, x_sample, mem_prompt, state_ssd_conv, state_ssd, state_s5_re, state_s5_im,
             state_sconv, state_pool, cache_mem_k, cache_mem_v, final_norm_g, w):
    sp = _stacked_params(w)
    flat = lambda s: s.reshape(DEPTH, DEC_BATCH, -1)
    s_states = tuple(flat(s) for s in (state_ssd_conv, state_s5_re, state_s5_im, state_sconv, state_pool))
    s_ssd0 = flat(state_ssd)
    h = jnp.concatenate([x_prompt.reshape(P_ROWS, D_MODEL), x_sample.reshape(DEC_BATCH, D_MODEL)], axis=0)
    mem = mem_prompt.reshape(BATCH * N_MEM, D_MODEL)
    p_out = [[] for _ in range(8)]
    s_out = [[] for _ in range(6)]
    put_sample = lambda full, rows: lax.dynamic_update_slice(full, rows, (P_ROWS, 0))
    for l in range(DEPTH):
        mk = norm_matmul(mem, sp["norm_mem_g"], sp["w_k"], l, tm=BATCH * N_MEM, tn=MEM_COL_TILE)
        mv = norm_matmul(mem, sp["norm_mem_g"], sp["w_v"], l, tm=BATCH * N_MEM, tn=MEM_COL_TILE)

        proj_a, proj_b = in_proj(h, sp["norm_mix_g"], sp["w_in_a"], sp["w_in_b"], l, tm=RESIDENT_ROW_TILE)
        m_ssd, p_conv, p_ssd = prompt_ssd(proj_a, sp, l)
        m_s5, p_s5r, p_s5i, m_sc, m_pl, p_sc, p_pl = prompt_b_mixers(proj_b, sp, l)
        (s_m5, s_msc, s_mpl, s_conv, s_s5r, s_s5i, s_sc, s_pl, xs, xdt, da, b2, c2) = sample_mix(
            proj_a, proj_b, s_states, sp, l)
        s_ssd, s_mssd = sample_ssd(s_ssd0, xdt, da, b2, c2, xs, proj_a, sp, l)
        mixes = (put_sample(m_ssd, s_mssd), put_sample(m_s5, s_m5),
                 put_sample(m_sc, s_msc), put_sample(m_pl, s_mpl))

        h = out_proj(mixes, sp["w_out"], h, l, tm=PROJ_ROW_TILE)
        q = norm_matmul(h, sp["norm_xa_g"], sp["w_q"], l, out_dtype=BF16, tm=PROJ_ROW_TILE, tn=D_MODEL)
        o = prompt_attn(q, mk.reshape(BATCH, N_MEM, D_MODEL), mv.reshape(BATCH, N_MEM, D_MODEL), tq=512)
        q_s = q[P_ROWS:].astype(F32).reshape(DEC_BATCH, XA_HEADS, XA_HEAD_DIM)
        o_s = sample_attn(q_s, cache_mem_k, cache_mem_v, l)
        o = put_sample(o, o_s.reshape(DEC_BATCH, D_MODEL))
        h = res_matmul(o, sp["w_o"], h, l, tm=PROJ_ROW_TILE, tn=D_MODEL)
        h = mlp(h, sp["norm_mlp_g"], sp["w_up"], sp["w_down"], l, tm=MLP_ROW_TILE, tf=FF_TILE)

        for lst, val in zip(p_out, (p_conv, p_ssd, p_s5r, p_s5i, p_sc, p_pl, mk, mv)):
            lst.append(val)
        for lst, val in zip(s_out, (s_conv, s_ssd, s_s5r, s_s5i, s_sc, s_pl)):
            lst.append(val)

    y_prompt = final_norm(h, final_norm_g, rows=P_ROWS, first_block=0, tm=FINAL_ROW_TILE)
    y_sample = final_norm(h, final_norm_g, rows=DEC_BATCH, first_block=SAMPLE_BLOCK, tm=DEC_BATCH)
    p_shapes = ((BATCH, SSD_CONV - 1, XBC_WIDTH), (BATCH, SSD_HEADS, SSD_HEAD_DIM, SSD_STATE),
                (BATCH, S5_GROUPS, S5_STATE), (BATCH, S5_GROUPS, S5_STATE),
                (BATCH, SC_CONV - 1, GROUP_WIDTH), (BATCH, POOL_HIST, GROUP_WIDTH),
                (BATCH, N_MEM, XA_HEADS, XA_HEAD_DIM), (BATCH, N_MEM, XA_HEADS, XA_HEAD_DIM))
    s_shapes = tuple((DEC_BATCH,) + s[1:] for s in p_shapes[:6])
    stack = lambda vals, shape: jnp.stack(vals).reshape((DEPTH,) + shape)
    return ((y_prompt.reshape(BATCH, SEQ, D_MODEL), y_sample.reshape(DEC_BATCH, 1, D_MODEL))
            + tuple(stack(v, s) for v, s in zip(p_out, p_shapes))
            + tuple(stack(v, s) for v, s in zip(s_out, s_shapes)))


_forward_jit = jax.jit(_forward)


def kernel(x_prompt, x_sample, mem_prompt, state_ssd_conv, state_ssd, state_s5_re, state_s5_im, state_sconv, state_pool, cache_mem_k, cache_mem_v, norm_mix_g, w_in, ssd_conv_w, ssd_conv_b, ssd_dt_bias, ssd_a_log, ssd_d, s5_lam_re, s5_lam_im, s5_log_dt, s5_b_re, s5_b_im, s5_c_re, s5_c_im, s5_d, s5_w_glu, s5_b_glu, sc_conv_w, pool_w, pool_scale, mix_out_g, w_out, norm_xa_g, norm_mem_g, w_q, w_k, w_v, w_o, norm_mlp_g, w_up, w_down, final_norm_g):
    w = dict(norm_mix_g=norm_mix_g, w_in=w_in, ssd_conv_w=ssd_conv_w, ssd_conv_b=ssd_conv_b,
             ssd_dt_bias=ssd_dt_bias, ssd_a_log=ssd_a_log, ssd_d=ssd_d, s5_lam_re=s5_lam_re,
             s5_lam_im=s5_lam_im, s5_log_dt=s5_log_dt, s5_b_re=s5_b_re, s5_b_im=s5_b_im,
             s5_c_re=s5_c_re, s5_c_im=s5_c_im, s5_d=s5_d, s5_w_glu=s5_w_glu, s5_b_glu=s5_b_glu,
             sc_conv_w=sc_conv_w, pool_w=pool_w, pool_scale=pool_scale, mix_out_g=mix_out_g,
             w_out=w_out, norm_xa_g=norm_xa_g, norm_mem_g=norm_mem_g, w_q=w_q, w_k=w_k, w_v=w_v,
             w_o=w_o, norm_mlp_g=norm_mlp_g, w_up=w_up, w_down=w_down)
    return _forward_jit(x_prompt, x_sample, mem_prompt, state_ssd_conv, state_ssd, state_s5_re, state_s5_im,
                        state_sconv, state_pool, cache_mem_k, cache_mem_v, final_norm_g, w)
```

```python
import math

import numpy as np
import jax
import jax.numpy as jnp
from jax import lax
from jax.experimental import pallas as pl
from jax.experimental.pallas import tpu as pltpu

F32 = jnp.float32
BF16 = jnp.bfloat16

D_MODEL = 2048
BATCH = 4
SEQ = 2048
DEPTH = 2
DEC_BATCH = 128
PAST_LEN = 16384
GROUP_WIDTH = D_MODEL // 4
SSD_HEAD_DIM = 64
SSD_HEADS = GROUP_WIDTH // SSD_HEAD_DIM
SSD_GROUPS = 2
SSD_STATE = 64
SSD_CONV = 4
SSD_CHUNK = 128
XBC_WIDTH = GROUP_WIDTH + 2 * SSD_GROUPS * SSD_STATE
S5_CH = 16
S5_GROUPS = GROUP_WIDTH // S5_CH
S5_STATE = 64
S5_LANES = S5_GROUPS * S5_STATE
SC_CONV = 3
POOL_WINDOWS = (2, 4, 8, 16)
POOL_GROUP = GROUP_WIDTH // len(POOL_WINDOWS)
POOL_HIST = max(POOL_WINDOWS) - 1
N_MEM = 256
XA_HEADS = 4
XA_HEAD_DIM = D_MODEL // XA_HEADS
D_FF = 4 * D_MODEL
EPS = 1e-6
P_ROWS = BATCH * SEQ
ALL_ROWS = P_ROWS + DEC_BATCH

LANE = 128
SUBLANE = 8
VMEM_LIMIT = 60 * 1024 * 1024

COL_Z = 0
COL_XBC = 512
COL_DT = 1280
A_WIDTH = 1536
ZX_WIDTH = COL_DT
COL_U5 = 0
COL_GB = 512
COL_GC = 1024
COL_HV = 1536
COL_UP = 2048
B_WIDTH = 2560

RESIDENT_ROW_TILE = ALL_ROWS // 13
MLP_ROW_TILE = ALL_ROWS // 8
FF_TILE = 512
MEM_COL_TILE = 512
FINAL_ROW_TILE = 1024
SAMPLE_BLOCK = P_ROWS // DEC_BATCH

SAMPLE_ATTN_ROWS = 2
S5_CHUNK = 256
S5_SEG = S5_CHUNK // SUBLANE
S5_SCAN_LANES = 512


def _cparams(sem):
    return pltpu.CompilerParams(dimension_semantics=sem, vmem_limit_bytes=VMEM_LIMIT)


def _layer_spec(shape, layer):
    zeros = (0,) * len(shape)
    return pl.BlockSpec((None,) + tuple(shape), lambda *_: (layer,) + zeros)


def _whole_spec(shape):
    zeros = (0,) * len(shape)
    return pl.BlockSpec(tuple(shape), lambda *_: zeros)


def _weight_cols_spec(k, n, tn, layer):
    mode = pl.Buffered(1) if tn == n else None
    return pl.BlockSpec((None, k, tn), lambda i, j: (layer, 0, j), pipeline_mode=mode)


def _prompt_row_block(b, t, steps, rows):
    return jnp.minimum(b * steps + t, P_ROWS // rows)


def _prompt_batch(b):
    return jnp.minimum(b, BATCH - 1)


def _prompt_step(body, row_outputs):
    b = pl.program_id(0)
    pl.when(b < BATCH)(body)

    @pl.when(b == BATCH)
    def _():
        for ref in row_outputs:
            ref[...] = jnp.zeros_like(ref)


def _sigmoid(x):
    return 1.0 / (1.0 + jnp.exp(-x))


def _silu(x):
    return x * _sigmoid(x)


def _softplus(x):
    return jnp.maximum(x, 0.0) + jnp.log(1.0 + jnp.exp(-jnp.abs(x)))


def _gelu_tanh(x):
    return 0.5 * x * (1.0 + jnp.tanh(math.sqrt(2.0 / math.pi) * (x + 0.044715 * (x * x * x))))


def _rms(x, g):
    return x * lax.rsqrt(jnp.mean(x * x, axis=-1, keepdims=True) + EPS) * g


def _dot(a, b):
    return jnp.dot(a, b, preferred_element_type=F32)


def _split3_bf16(x):
    hi = x.astype(BF16)
    r = x - hi.astype(F32)
    mid = r.astype(BF16)
    lo = (r - mid.astype(F32)).astype(BF16)
    return hi, mid, lo


def _spread_exact(x, sel):
    hi, mid, lo = _split3_bf16(x)
    return _dot(hi, sel) + _dot(mid, sel) + _dot(lo, sel)


def _sum_rows_exact(sel, x):
    hi, mid, lo = _split3_bf16(x)
    return _dot(sel, hi) + _dot(sel, mid) + _dot(sel, lo)


def _split_bf16(x):
    hi = x.astype(BF16)
    lo = (x - hi.astype(F32)).astype(BF16)
    return hi, lo


def _norm_matmul_kernel(x_ref, g_ref, w_ref, o_ref, xn_ref):
    @pl.when(pl.program_id(1) == 0)
    def _():
        xn_ref[...] = _rms(x_ref[...], g_ref[...]).astype(BF16)

    o_ref[...] = _dot(xn_ref[...], w_ref[...].astype(BF16)).astype(o_ref.dtype)


def norm_matmul(x, g, w, layer, *, out_dtype=F32, tm, tn):
    m, k = x.shape
    n = w.shape[2]
    return pl.pallas_call(
        _norm_matmul_kernel,
        grid=(m // tm, n // tn),
        in_specs=[pl.BlockSpec((tm, k), lambda i, j: (i, 0)),
                  _layer_spec((1, k), layer),
                  _weight_cols_spec(k, n, tn, layer)],
        out_specs=pl.BlockSpec((tm, tn), lambda i, j: (i, j)),
        out_shape=jax.ShapeDtypeStruct((m, n), out_dtype),
        scratch_shapes=[pltpu.VMEM((tm, k), BF16)],
        compiler_params=_cparams(("parallel", "arbitrary")),
        name="norm_matmul",
    )(x, g, w)


def _in_proj_kernel(x_ref, g_ref, wa_ref, wb_ref, oa_ref, ob_ref):
    xn = _rms(x_ref[...], g_ref[...]).astype(BF16)
    oa_ref[...] = _dot(xn, wa_ref[...])
    ob_ref[...] = _dot(xn, wb_ref[...])


def in_proj(x, g, wa, wb, layer, *, tm):
    m, k = x.shape
    resident = lambda n: pl.BlockSpec((None, k, n), lambda i: (layer, 0, 0), pipeline_mode=pl.Buffered(1))
    return pl.pallas_call(
        _in_proj_kernel,
        grid=(m // tm,),
        in_specs=[pl.BlockSpec((tm, k), lambda i: (i, 0)), _layer_spec((1, k), layer),
                  resident(A_WIDTH), resident(B_WIDTH)],
        out_specs=(pl.BlockSpec((tm, A_WIDTH), lambda i: (i, 0)), pl.BlockSpec((tm, B_WIDTH), lambda i: (i, 0))),
        out_shape=(jax.ShapeDtypeStruct((m, A_WIDTH), F32), jax.ShapeDtypeStruct((m, B_WIDTH), F32)),
        compiler_params=_cparams(("parallel",)),
        name="in_proj",
    )(x, g, wa, wb)


def _res_matmul_kernel(x_ref, w_ref, r_ref, o_ref):
    o_ref[...] = r_ref[...] + _dot(x_ref[...], w_ref[...].astype(BF16))


def res_matmul(x, w, res, layer, *, tm, tn):
    m, k = x.shape
    n = w.shape[2]
    return pl.pallas_call(
        _res_matmul_kernel,
        grid=(m // tm, n // tn),
        in_specs=[pl.BlockSpec((tm, k), lambda i, j: (i, 0)),
                  _weight_cols_spec(k, n, tn, layer),
                  pl.BlockSpec((tm, tn), lambda i, j: (i, j))],
        out_specs=pl.BlockSpec((tm, tn), lambda i, j: (i, j)),
        out_shape=jax.ShapeDtypeStruct((m, n), F32),
        compiler_params=_cparams(("parallel", "arbitrary")),
        name="res_matmul",
    )(x, w, res)


def _out_proj_kernel(m0_ref, m1_ref, m2_ref, m3_ref, w_ref, r_ref, o_ref):
    x = jnp.concatenate([m0_ref[...], m1_ref[...], m2_ref[...], m3_ref[...]], axis=-1)
    o_ref[...] = r_ref[...] + _dot(x, w_ref[...].astype(BF16))


def out_proj(mixes, w, res, layer, *, tm):
    m = res.shape[0]
    mix_spec = pl.BlockSpec((tm, GROUP_WIDTH), lambda i, j: (i, 0))
    row_spec = pl.BlockSpec((tm, D_MODEL), lambda i, j: (i, 0))
    return pl.pallas_call(
        _out_proj_kernel,
        grid=(m // tm, 1),
        in_specs=[mix_spec] * 4 + [_weight_cols_spec(D_MODEL, D_MODEL, D_MODEL, layer), row_spec],
        out_specs=row_spec,
        out_shape=jax.ShapeDtypeStruct((m, D_MODEL), F32),
        compiler_params=_cparams(("parallel", "arbitrary")),
        name="out_proj",
    )(*mixes, w, res)


def _mlp_kernel(x_ref, g_ref, wu_ref, wd_ref, o_ref, xn_ref):
    j = pl.program_id(1)

    @pl.when(j == 0)
    def _():
        x = x_ref[...]
        xn_ref[...] = _rms(x, g_ref[...]).astype(BF16)
        o_ref[...] = x

    a = jnp.square(jnp.maximum(_dot(xn_ref[...], wu_ref[...].astype(BF16)), 0.0))
    o_ref[...] += _dot(a.astype(BF16), wd_ref[...].astype(BF16))


def mlp(x, g, w_up, w_down, layer, *, tm, tf):
    m = x.shape[0]
    return pl.pallas_call(
        _mlp_kernel,
        grid=(m // tm, D_FF // tf),
        in_specs=[pl.BlockSpec((tm, D_MODEL), lambda i, j: (i, 0)),
                  _layer_spec((1, D_MODEL), layer),
                  pl.BlockSpec((None, D_MODEL, tf), lambda i, j: (layer, 0, j)),
                  pl.BlockSpec((None, tf, D_MODEL), lambda i, j: (layer, j, 0))],
        out_specs=pl.BlockSpec((tm, D_MODEL), lambda i, j: (i, 0)),
        out_shape=jax.ShapeDtypeStruct((m, D_MODEL), F32),
        scratch_shapes=[pltpu.VMEM((tm, D_MODEL), BF16)],
        compiler_params=_cparams(("parallel", "arbitrary")),
        name="mlp",
    )(x, g, w_up, w_down)


def _final_norm_kernel(x_ref, g_ref, o_ref):
    o_ref[...] = _rms(x_ref[...], g_ref[...])


def final_norm(x, g, *, rows, first_block, tm):
    return pl.pallas_call(
        _final_norm_kernel,
        grid=(rows // tm,),
        in_specs=[pl.BlockSpec((tm, D_MODEL), lambda i: (first_block + i, 0)), _whole_spec((1, D_MODEL))],
        out_specs=pl.BlockSpec((tm, D_MODEL), lambda i: (i, 0)),
        out_shape=jax.ShapeDtypeStruct((rows, D_MODEL), F32),
        compiler_params=_cparams(("parallel",)),
        name="final_norm",
    )(x, g.reshape(1, D_MODEL))


def _prompt_attn_kernel(q_ref, k_ref, v_ref, o_ref, kb_ref, vb_ref):
    _prompt_step(lambda: _prompt_attn_body(q_ref, k_ref, v_ref, o_ref, kb_ref, vb_ref), (o_ref,))


def _prompt_attn_body(q_ref, k_ref, v_ref, o_ref, kb_ref, vb_ref):
    scale = XA_HEAD_DIM ** -0.5

    @pl.when(pl.program_id(1) == 0)
    def _():
        kb_ref[...] = k_ref[0].astype(BF16)
        vb_ref[...] = v_ref[0].astype(BF16)

    for h in range(XA_HEADS):
        cs = slice(h * XA_HEAD_DIM, (h + 1) * XA_HEAD_DIM)
        kh = kb_ref[:, cs]
        vh = vb_ref[:, cs]
        s = lax.dot_general(q_ref[:, cs], kh, (((1,), (1,)), ((), ())), preferred_element_type=F32) * scale
        p = jnp.exp(s - jnp.max(s, axis=-1, keepdims=True))
        p = p / jnp.sum(p, axis=-1, keepdims=True)
        o_ref[:, cs] = _dot(p.astype(BF16), vh).astype(o_ref.dtype)


def prompt_attn(q, k, v, *, tq):
    nq = SEQ // tq
    return pl.pallas_call(
        _prompt_attn_kernel,
        grid=(BATCH + 1, nq),
        in_specs=[pl.BlockSpec((tq, D_MODEL), lambda b, i: (_prompt_row_block(b, i, nq, tq), 0)),
                  pl.BlockSpec((1, N_MEM, D_MODEL), lambda b, i: (_prompt_batch(b), 0, 0)),
                  pl.BlockSpec((1, N_MEM, D_MODEL), lambda b, i: (_prompt_batch(b), 0, 0))],
        out_specs=pl.BlockSpec((tq, D_MODEL), lambda b, i: (_prompt_row_block(b, i, nq, tq), 0)),
        out_shape=jax.ShapeDtypeStruct((ALL_ROWS, D_MODEL), BF16),
        scratch_shapes=[pltpu.VMEM((N_MEM, D_MODEL), BF16), pltpu.VMEM((N_MEM, D_MODEL), BF16)],
        compiler_params=_cparams(("arbitrary", "arbitrary")),
        name="prompt_attn",
    )(q, k, v)


def _sample_attn_kernel(q_ref, k_ref, v_ref, o_ref):
    scale = XA_HEAD_DIM ** -0.5
    half, quarter = XA_HEAD_DIM // 2, XA_HEAD_DIM // 4
    for r in range(SAMPLE_ATTN_ROWS):
        prod = k_ref[r] * (q_ref[r] * scale)
        fold = prod[:, :, :half] + prod[:, :, half:]
        fold = fold[:, :, :quarter] + fold[:, :, quarter:]
        s = jnp.sum(fold, axis=-1, keepdims=True)
        e = jnp.exp(s - jnp.max(s, axis=0, keepdims=True))
        o = jnp.sum(e * v_ref[r], axis=0) / jnp.sum(e, axis=0)
        o_ref[r] = o.astype(o_ref.dtype)


def sample_attn(q, cache_k, cache_v, layer):
    nr = SAMPLE_ATTN_ROWS
    row = pl.BlockSpec((nr, XA_HEADS, XA_HEAD_DIM), lambda b: (b, 0, 0))
    mem = pl.BlockSpec((None, nr, N_MEM, XA_HEADS, XA_HEAD_DIM), lambda b: (layer, b, 0, 0, 0))
    return pl.pallas_call(
        _sample_attn_kernel,
        grid=(DEC_BATCH // nr,),
        in_specs=[row, mem, mem],
        out_specs=row,
        out_shape=jax.ShapeDtypeStruct((DEC_BATCH, XA_HEADS, XA_HEAD_DIM), BF16),
        compiler_params=_cparams(("parallel",)),
        name="sample_attn",
    )(q, cache_k, cache_v)


def _s5_abar(lr, li, ldt):
    delta = jnp.exp(ldt)
    mag = jnp.exp(lr * delta)
    return mag * jnp.cos(li * delta), mag * jnp.sin(li * delta)


def _s5_bbar_kernel(lr_ref, li_ref, ldt_ref, bre_ref, bim_ref, bbr_ref, bbi_ref):
    lr, li = lr_ref[...], li_ref[...]
    ar, ai = _s5_abar(lr, li, ldt_ref[...])
    den = lr * lr + li * li
    cr = ((ar - 1.0) * lr + ai * li) / den
    ci = (ai * lr - (ar - 1.0) * li) / den
    br, bi = bre_ref[...], bim_ref[...]
    bbr_ref[...] = cr * br - ci * bi
    bbi_ref[...] = cr * bi + ci * br


def _s5_pow_kernel(lr_ref, li_ref, ldt_ref, pr_ref, pi_ref):
    ar, ai = _s5_abar(lr_ref[...], li_ref[...], ldt_ref[...])
    qr, qi = ar, ai
    pr_ref[0] = qr
    pi_ref[0] = qi
    for e in range(1, S5_SEG):
        qr, qi = qr * ar - qi * ai, qr * ai + qi * ar
        pr_ref[e] = qr
        pi_ref[e] = qi


def s5_prepare(lam_re, lam_im, log_dt, b_re, b_im, c_re, c_im):
    dg = DEPTH * S5_GROUPS
    ldt = jnp.broadcast_to(log_dt[..., None], (DEPTH, S5_GROUPS, S5_STATE))
    rep = lambda a: jnp.repeat(a.reshape(dg, S5_STATE), S5_CH, axis=0)
    to_rows = lambda b: jnp.transpose(b, (0, 1, 3, 2)).reshape(dg * S5_CH, S5_STATE)
    shp = jax.ShapeDtypeStruct((dg * S5_CH, S5_STATE), F32)
    bbr, bbi = pl.pallas_call(_s5_bbar_kernel, out_shape=(shp, shp), name="s5_bbar")(
        rep(lam_re), rep(lam_im), rep(ldt), to_rows(b_re), to_rows(b_im))
    pshp = jax.ShapeDtypeStruct((S5_SEG, dg, S5_STATE), F32)
    pr, pi = pl.pallas_call(_s5_pow_kernel, out_shape=(pshp, pshp), name="s5_pow")(
        lam_re.reshape(dg, S5_STATE), lam_im.reshape(dg, S5_STATE), ldt.reshape(dg, S5_STATE))

    gpb = LANE // S5_CH
    nblk = S5_GROUPS // gpb
    eye = jnp.eye(gpb, dtype=F32)

    def b_blocks(bb):
        bb = bb.reshape(DEPTH, nblk, gpb, S5_CH, S5_STATE)
        return jnp.einsum("digkn,gh->digkhn", bb, eye).reshape(DEPTH, nblk, LANE, gpb * S5_STATE).astype(BF16)

    def c_blocks(cc):
        cc = cc.reshape(DEPTH, nblk, gpb, S5_CH, S5_STATE)
        return jnp.einsum("digkn,gh->dignhk", cc, eye).reshape(DEPTH, nblk, gpb * S5_STATE, LANE).astype(BF16)

    def pow_rows(p):
        return jnp.transpose(p.reshape(S5_SEG, DEPTH, S5_LANES), (1, 0, 2))

    pw_r, pw_i = pow_rows(pr), pow_rows(pi)
    tile = lambda p: jnp.broadcast_to(p[:, :, None, :], (DEPTH, S5_SEG, SUBLANE, S5_LANES))
    return dict(b_re=b_blocks(bbr), b_im=b_blocks(bbi), c_re=c_blocks(c_re), c_imn=c_blocks(-c_im),
                pw_re=pw_r, pw_im=pw_i, pwt_re=tile(pw_r), pwt_im=tile(pw_i))


def _s5_bu(ub, bre_ref, bim_ref):
    res_r, res_i = [], []
    for i in range(GROUP_WIDTH // LANE):
        ui = ub[:, i * LANE:(i + 1) * LANE]
        res_r.append(_dot(ui, bre_ref[i]))
        res_i.append(_dot(ui, bim_ref[i]))
    return res_r, res_i


def _s5_tail(u, hs_re, hs_im, cre_ref, cimn_ref, d_ref, wglu_ref, bglu_ref, g_ref):
    ys = []
    for i in range(GROUP_WIDTH // LANE):
        ys.append(_dot(hs_re(i).astype(BF16), cre_ref[i]) + _dot(hs_im(i).astype(BF16), cimn_ref[i]))
    y = jnp.concatenate(ys, axis=-1) + d_ref[...] * u
    y = _gelu_tanh(y)
    y = y * _sigmoid(_dot(y.astype(BF16), wglu_ref[...]) + bglu_ref[...])
    return _rms(y, g_ref[...])


def _s5_param_specs(layer):
    nblk = GROUP_WIDTH // LANE
    sblk = S5_LANES // nblk
    return [_layer_spec((nblk, LANE, sblk), layer),
            _layer_spec((nblk, LANE, sblk), layer),
            _layer_spec((nblk, sblk, LANE), layer),
            _layer_spec((nblk, sblk, LANE), layer),
            _layer_spec((1, GROUP_WIDTH), layer),
            _layer_spec((GROUP_WIDTH, GROUP_WIDTH), layer),
            _layer_spec((1, GROUP_WIDTH), layer),
            _layer_spec((1, GROUP_WIDTH), layer)]


def _s5_param_args(sp):
    return (sp["s5_b_re"], sp["s5_b_im"], sp["s5_c_re"], sp["s5_c_imn"], sp["s5_d"], sp["s5_w_glu"],
            sp["s5_b_glu"], sp["g_s5"])


def _cmul_add(x_r, x_i, a_r, a_i, h_r, h_i):
    return x_r + a_r * h_r - a_i * h_i, x_i + a_r * h_i + a_i * h_r


def _prompt_s5_body(u_ref, perm_ref, unperm_ref, pw_re_ref, pw_im_ref, pwt_re_ref, pwt_im_ref,
                    bre_ref, bim_ref, cre_ref, cimn_ref, d_ref, wglu_ref, bglu_ref, g_ref,
                    mix_ref, sre_ref, sim_ref, hre_ref, him_ref, cr_ref, ci_ref):
    sblk = S5_LANES // (GROUP_WIDTH // LANE)
    W = S5_SCAN_LANES

    @pl.when(pl.program_id(1) == 0)
    def _():
        cr_ref[...] = jnp.zeros_like(cr_ref)
        ci_ref[...] = jnp.zeros_like(ci_ref)

    u_hi, u_lo = _split_bf16(u_ref[...])
    ub = _dot(perm_ref[...], u_hi)
    u = ub + _dot(perm_ref[...], u_lo)
    bu_r, bu_i = _s5_bu(ub.astype(BF16), bre_ref, bim_ref)
    for i in range(len(bu_r)):
        hre_ref[:, i * sblk:(i + 1) * sblk] = bu_r[i]
        him_ref[:, i * sblk:(i + 1) * sblk] = bu_i[i]

    row = lax.broadcasted_iota(jnp.int32, (SUBLANE, W), 0)
    for lb in range(0, S5_LANES, W):
        ls = slice(lb, lb + W)
        a_r = jnp.broadcast_to(pw_re_ref[0:1, ls], (SUBLANE, W))
        a_i = jnp.broadcast_to(pw_im_ref[0:1, ls], (SUBLANE, W))

        def local_step(j, h, ls=ls, a_r=a_r, a_i=a_i):
            rs = pl.ds(pl.multiple_of(j * SUBLANE, SUBLANE), SUBLANE)
            n_r, n_i = _cmul_add(hre_ref[rs, ls], him_ref[rs, ls], a_r, a_i, h[0], h[1])
            hre_ref[rs, ls] = n_r
            him_ref[rs, ls] = n_i
            return n_r, n_i

        zero = jnp.zeros((SUBLANE, W), F32)
        e_r, e_i = lax.fori_loop(0, S5_SEG, local_step, (zero, zero))

        s_r = pw_re_ref[S5_SEG - 1:S5_SEG, ls]
        s_i = pw_im_ref[S5_SEG - 1:S5_SEG, ls]
        c_r, c_i = cr_ref[:, ls], ci_ref[:, ls]
        in_r, in_i = zero, zero
        for s in range(SUBLANE):
            in_r = jnp.where(row == s, c_r, in_r)
            in_i = jnp.where(row == s, c_i, in_i)
            c_r, c_i = _cmul_add(e_r[s:s + 1, :], e_i[s:s + 1, :], s_r, s_i, c_r, c_i)
        cr_ref[:, ls] = c_r
        ci_ref[:, ls] = c_i

        def fix_step(j, carry, ls=ls, in_r=in_r, in_i=in_i):
            rs = pl.ds(pl.multiple_of(j * SUBLANE, SUBLANE), SUBLANE)
            n_r, n_i = _cmul_add(hre_ref[rs, ls], him_ref[rs, ls], pwt_re_ref[j, :, ls], pwt_im_ref[j, :, ls],
                                 in_r, in_i)
            hre_ref[rs, ls] = n_r
            him_ref[rs, ls] = n_i
            return carry

        lax.fori_loop(0, S5_SEG, fix_step, 0)

    sre_ref[0] = cr_ref[...]
    sim_ref[0] = ci_ref[...]
    y = _s5_tail(u, lambda i: hre_ref[:, i * sblk:(i + 1) * sblk], lambda i: him_ref[:, i * sblk:(i + 1) * sblk],
                 cre_ref, cimn_ref, d_ref, wglu_ref, bglu_ref, g_ref)
    mix_ref[...] = _dot(unperm_ref[...], y.astype(mix_ref.dtype)).astype(mix_ref.dtype)


def _prompt_ssd_kernel(*refs):
    _prompt_step(lambda: _prompt_ssd_body(*refs), (refs[10],))


def _prompt_ssd_body(zx_ref, dt_ref, cw_ref, cb_ref, dtb_ref, alog_ref, dexp_ref, g_ref, hexp_ref, hcol_ref,
                     mix_ref, cst_ref, hst_ref, xbuf_ref, h_ref):
    L = SSD_CHUNK
    hist = SSD_CONV - 1
    base = SUBLANE

    @pl.when(pl.program_id(1) == 0)
    def _():
        xbuf_ref[0:base, :] = jnp.zeros((base, XBC_WIDTH), F32)
        h_ref[...] = jnp.zeros_like(h_ref)

    @pl.when(pl.program_id(1) > 0)
    def _():
        xbuf_ref[base - hist:base, :] = xbuf_ref[base + L - hist:base + L, :]

    xbc = zx_ref[:, COL_XBC:COL_XBC + XBC_WIDTH]
    xbuf_ref[base:base + L, :] = xbc
    cst_ref[0] = xbuf_ref[base + L - hist:base + L, :]
    conv = cb_ref[...] + cw_ref[hist:hist + 1, :] * xbc
    for k in range(hist):
        conv = conv + cw_ref[k:k + 1, :] * xbuf_ref[base - hist + k:base - hist + k + L, :]
    xc = _silu(conv)
    xs = xc[:, :GROUP_WIDTH]
    ng = SSD_GROUPS * SSD_STATE
    bm = xc[:, GROUP_WIDTH:GROUP_WIDTH + ng].astype(BF16)
    cm = xc[:, GROUP_WIDTH + ng:].astype(BF16)

    dt = _softplus(dt_ref[...] + dtb_ref[...])
    a = -jnp.exp(alog_ref[...])
    ri = lax.broadcasted_iota(jnp.int32, (L, L), 0)
    ci = lax.broadcasted_iota(jnp.int32, (L, L), 1)
    causal = ri >= ci
    acum = _sum_rows_exact(causal.astype(BF16), dt * a)
    acum_t = acum.T
    last = acum[L - 1:L, :]
    to_end = jnp.exp(last - acum)
    e_acum = jnp.exp(acum)
    chunk_decay = jnp.exp(last)

    xdt_all = xs * _spread_exact(dt, hexp_ref[...])
    xdt_end = (xdt_all * _spread_exact(to_end, hexp_ref[...])).astype(BF16)
    xdt_all = xdt_all.astype(BF16)
    e_acum_all = _spread_exact(e_acum, hexp_ref[...])
    acum_cols = _spread_exact(acum, hcol_ref[...])

    ys_diag, ys_off = [], []
    rep = SSD_HEADS // SSD_GROUPS
    cb = [lax.dot_general(cm[:, g * SSD_STATE:(g + 1) * SSD_STATE], bm[:, g * SSD_STATE:(g + 1) * SSD_STATE],
                          (((1,), (1,)), ((), ())), preferred_element_type=F32) for g in range(SSD_GROUPS)]
    for h in range(SSD_HEADS):
        g = h // rep
        hs = slice(h * SSD_HEAD_DIM, (h + 1) * SSD_HEAD_DIM)
        gs = slice(g * SSD_STATE, (g + 1) * SSD_STATE)
        seg = acum_cols[:, h * L:(h + 1) * L] - acum_t[h:h + 1, :]
        decay = jnp.exp(jnp.where(causal, seg, -jnp.inf))
        ys_diag.append(_dot((cb[g] * decay).astype(BF16), xdt_all[:, hs]))
        h_prev = h_ref[h]
        ys_off.append(lax.dot_general(cm[:, gs], h_prev.astype(BF16), (((1,), (1,)), ((), ())),
                                      preferred_element_type=F32))
        st = lax.dot_general(xdt_end[:, hs], bm[:, gs], (((0,), (0,)), ((), ())), preferred_element_type=F32)
        h_ref[h] = h_prev * chunk_decay[:, h:h + 1] + st
    y = jnp.concatenate(ys_diag, axis=-1) + jnp.concatenate(ys_off, axis=-1) * e_acum_all
    y = (y + dexp_ref[...] * xs) * _silu(zx_ref[:, COL_Z:COL_Z + GROUP_WIDTH])
    mix_ref[...] = _rms(y, g_ref[...]).astype(mix_ref.dtype)
    hst_ref[0] = h_ref[...]


def prompt_ssd(proj, sp, layer):
    nc = SEQ // SSD_CHUNK
    rb = lambda b, c: _prompt_row_block(b, c, nc, SSD_CHUNK)
    head_of_row = np.arange(LANE)[:, None]
    head_lanes = jnp.asarray(head_of_row == np.arange(GROUP_WIDTH)[None, :] // SSD_HEAD_DIM, BF16)
    head_cols = jnp.asarray(head_of_row == np.arange(SSD_HEADS * SSD_CHUNK)[None, :] // SSD_CHUNK, BF16)
    return pl.pallas_call(
        _prompt_ssd_kernel,
        grid=(BATCH + 1, nc),
        in_specs=[pl.BlockSpec((SSD_CHUNK, ZX_WIDTH), lambda b, c: (rb(b, c), 0)),
                  pl.BlockSpec((SSD_CHUNK, LANE), lambda b, c: (rb(b, c), COL_DT // LANE)),
                  _layer_spec((SSD_CONV, XBC_WIDTH), layer),
                  _layer_spec((1, XBC_WIDTH), layer),
                  _layer_spec((1, LANE), layer),
                  _layer_spec((1, LANE), layer),
                  _layer_spec((1, GROUP_WIDTH), layer),
                  _layer_spec((1, GROUP_WIDTH), layer),
                  _whole_spec(head_lanes.shape), _whole_spec(head_cols.shape)],
        out_specs=(pl.BlockSpec((SSD_CHUNK, GROUP_WIDTH), lambda b, c: (rb(b, c), 0)),
                   pl.BlockSpec((1, SSD_CONV - 1, XBC_WIDTH), lambda b, c: (_prompt_batch(b), 0, 0)),
                   pl.BlockSpec((1, SSD_HEADS, SSD_HEAD_DIM, SSD_STATE), lambda b, c: (_prompt_batch(b), 0, 0, 0))),
        out_shape=(jax.ShapeDtypeStruct((ALL_ROWS, GROUP_WIDTH), BF16),
                   jax.ShapeDtypeStruct((BATCH, SSD_CONV - 1, XBC_WIDTH), F32),
                   jax.ShapeDtypeStruct((BATCH, SSD_HEADS, SSD_HEAD_DIM, SSD_STATE), F32)),
        scratch_shapes=[pltpu.VMEM((SUBLANE + SSD_CHUNK, XBC_WIDTH), F32),
                        pltpu.VMEM((SSD_HEADS, SSD_HEAD_DIM, SSD_STATE), F32)],
        compiler_params=_cparams(("arbitrary", "arbitrary")),
        name="prompt_ssd",
    )(proj, proj, sp["ssd_conv_w"], sp["ssd_conv_b"], sp["ssd_dt_bias"], sp["ssd_a_log"],
      sp["ssd_d_exp"], sp["g_ssd"], head_lanes, head_cols)


def _pool_counts(pos, w):
    return jnp.minimum(w, pos + 1).astype(F32)


def _prompt_convpool_body(gb_ref, gc_ref, hv_ref, up_ref, scw_ref, pw_ref, ps_ref, gsc_ref, gpl_ref,
                          msc_ref, mpl_ref, scst_ref, plst_ref, vbuf_ref, pbuf_ref):
    rows = gb_ref.shape[0]
    vb = SUBLANE
    pb = 2 * SUBLANE
    vh = SC_CONV - 1
    t = pl.program_id(1)

    @pl.when(t == 0)
    def _():
        vbuf_ref[0:vb, :] = jnp.zeros((vb, GROUP_WIDTH), F32)
        pbuf_ref[0:pb, :] = jnp.zeros((pb, GROUP_WIDTH), F32)

    @pl.when(t > 0)
    def _():
        vbuf_ref[vb - vh:vb, :] = vbuf_ref[vb + rows - vh:vb + rows, :]
        pbuf_ref[0:pb, :] = pbuf_ref[rows:rows + pb, :]

    v = gc_ref[...] * hv_ref[...]
    vbuf_ref[vb:vb + rows, :] = v
    acc = scw_ref[vh:vh + 1, :] * v
    for k in range(vh):
        acc = acc + scw_ref[k:k + 1, :] * vbuf_ref[vb - vh + k:vb - vh + k + rows, :]
    msc_ref[...] = _rms(gb_ref[...] * acc, gsc_ref[...]).astype(msc_ref.dtype)
    scst_ref[0] = vbuf_ref[vb + rows - vh:vb + rows, :]

    u = up_ref[...]
    pbuf_ref[pb:pb + rows, :] = u
    pos = t * rows + lax.broadcasted_iota(jnp.int32, (rows, 1), 0)
    ys = []
    for gi, w in enumerate(POOL_WINDOWS):
        cs = slice(gi * POOL_GROUP, (gi + 1) * POOL_GROUP)
        s = u[:, cs]
        for j in range(1, w):
            s = s + pbuf_ref[pb - j:pb - j + rows, cs]
        pooled = s / _pool_counts(pos, w) - u[:, cs]
        ys.append(_dot(pooled.astype(BF16), pw_ref[gi]))
    y = jnp.concatenate(ys, axis=-1) * ps_ref[...]
    mpl_ref[...] = _rms(y, gpl_ref[...]).astype(mpl_ref.dtype)
    plst_ref[0] = pbuf_ref[pb + rows - POOL_HIST:pb + rows, :]


N_S5_IN, N_S5_OUT, N_S5_SCRATCH = 15, 3, 4
N_CP_IN, N_CP_OUT = 9, 4


def _prompt_b_mixers_kernel(*refs):
    s5_in, refs = refs[:N_S5_IN], refs[N_S5_IN:]
    cp_in, refs = refs[:N_CP_IN], refs[N_CP_IN:]
    s5_out, refs = refs[:N_S5_OUT], refs[N_S5_OUT:]
    cp_out, refs = refs[:N_CP_OUT], refs[N_CP_OUT:]
    s5_scratch, cp_scratch = refs[:N_S5_SCRATCH], refs[N_S5_SCRATCH:]

    def body():
        _prompt_s5_body(*s5_in, *s5_out, *s5_scratch)
        _prompt_convpool_body(*cp_in, *cp_out, *cp_scratch)

    _prompt_step(body, (s5_out[0], cp_out[0], cp_out[1]))


def prompt_b_mixers(proj, sp, layer):
    nt = SEQ // S5_CHUNK
    t_of_row = (np.arange(S5_CHUNK) % SUBLANE) * S5_SEG + np.arange(S5_CHUNK) // SUBLANE
    perm = jnp.asarray(np.eye(S5_CHUNK, dtype=np.float32)[t_of_row], BF16)
    rb = lambda b, t: _prompt_row_block(b, t, nt, S5_CHUNK)
    col = lambda c: pl.BlockSpec((S5_CHUNK, GROUP_WIDTH), lambda b, t: (rb(b, t), c // GROUP_WIDTH))
    mix_spec = pl.BlockSpec((S5_CHUNK, GROUP_WIDTH), lambda b, t: (rb(b, t), 0))
    mix_shape = jax.ShapeDtypeStruct((ALL_ROWS, GROUP_WIDTH), BF16)
    seq_spec = lambda *s: pl.BlockSpec((1,) + s, lambda b, t: (_prompt_batch(b),) + (0,) * len(s))
    seq_shape = lambda *s: jax.ShapeDtypeStruct((BATCH,) + s, F32)
    s5_in_specs = [col(COL_U5), _whole_spec((S5_CHUNK, S5_CHUNK)), _whole_spec((S5_CHUNK, S5_CHUNK)),
                   _layer_spec((S5_SEG, S5_LANES), layer), _layer_spec((S5_SEG, S5_LANES), layer),
                   _layer_spec((S5_SEG, SUBLANE, S5_LANES), layer),
                   _layer_spec((S5_SEG, SUBLANE, S5_LANES), layer)] + _s5_param_specs(layer)
    cp_in_specs = [col(COL_GB), col(COL_GC), col(COL_HV), col(COL_UP),
                   _layer_spec((SC_CONV, GROUP_WIDTH), layer),
                   _layer_spec((len(POOL_WINDOWS), POOL_GROUP, POOL_GROUP), layer),
                   _layer_spec((1, GROUP_WIDTH), layer), _layer_spec((1, GROUP_WIDTH), layer),
                   _layer_spec((1, GROUP_WIDTH), layer)]
    assert len(s5_in_specs) == N_S5_IN and len(cp_in_specs) == N_CP_IN
    return pl.pallas_call(
        _prompt_b_mixers_kernel,
        grid=(BATCH + 1, nt),
        in_specs=s5_in_specs + cp_in_specs,
        out_specs=(mix_spec, seq_spec(1, S5_LANES), seq_spec(1, S5_LANES),
                   mix_spec, mix_spec, seq_spec(SC_CONV - 1, GROUP_WIDTH), seq_spec(POOL_HIST, GROUP_WIDTH)),
        out_shape=(mix_shape, seq_shape(1, S5_LANES), seq_shape(1, S5_LANES),
                   mix_shape, mix_shape, seq_shape(SC_CONV - 1, GROUP_WIDTH), seq_shape(POOL_HIST, GROUP_WIDTH)),
        scratch_shapes=[pltpu.VMEM((S5_CHUNK, S5_LANES), F32), pltpu.VMEM((S5_CHUNK, S5_LANES), F32),
                        pltpu.VMEM((1, S5_LANES), F32), pltpu.VMEM((1, S5_LANES), F32),
                        pltpu.VMEM((SUBLANE + S5_CHUNK, GROUP_WIDTH), F32),
                        pltpu.VMEM((2 * SUBLANE + S5_CHUNK, GROUP_WIDTH), F32)],
        compiler_params=_cparams(("arbitrary", "arbitrary")),
        name="prompt_b_mixers",
    )(proj, perm, perm.T, sp["s5_pw_re"], sp["s5_pw_im"], sp["s5_pwt_re"], sp["s5_pwt_im"], *_s5_param_args(sp),
      proj, proj, proj, proj, sp["sc_conv_w"], sp["pool_w"], sp["pool_scale"], sp["g_sc"], sp["g_pool"])


def _sample_mix_kernel(proj_ref, pb_ref, cprev_ref, s5r_ref, s5i_ref, scprev_ref, plprev_ref,
                       cw_ref, cb_ref, dtb_ref, alog_ref,
                       pwr_ref, pwi_ref, bre_ref, bim_ref, cre_ref, cimn_ref, d5_ref, wglu_ref, bglu_ref, g5_ref,
                       scw_ref, pw_ref, ps_ref, gsc_ref, gpl_ref,
                       m5_ref, msc_ref, mpl_ref, cst_ref, s5ro_ref, s5io_ref, scst_ref, plst_ref,
                       xs_ref, xdt_ref, da_ref, b2_ref, c2_ref):
    nb = proj_ref.shape[0]
    W = GROUP_WIDTH
    xbc = proj_ref[:, COL_XBC:COL_XBC + XBC_WIDTH]
    hist = SSD_CONV - 1
    conv = cb_ref[...] + cw_ref[hist:hist + 1, :] * xbc
    for k in range(hist):
        conv = conv + cw_ref[k:k + 1, :] * cprev_ref[:, k * XBC_WIDTH:(k + 1) * XBC_WIDTH]
    cst_ref[:, 0:(hist - 1) * XBC_WIDTH] = cprev_ref[:, XBC_WIDTH:hist * XBC_WIDTH]
    cst_ref[:, (hist - 1) * XBC_WIDTH:hist * XBC_WIDTH] = xbc
    xc = _silu(conv)
    xs = xc[:, :W]
    bm = xc[:, W:W + LANE]
    cm = xc[:, W + LANE:W + 2 * LANE]
    dt = _softplus(proj_ref[:, COL_DT:COL_DT + LANE] + dtb_ref[...])
    da = jnp.exp(dt * (-jnp.exp(alog_ref[...])))
    lane_w = lax.broadcasted_iota(jnp.int32, (nb, W), 1)
    dt_exp = jnp.zeros((nb, W), F32)
    for h in range(SSD_HEADS):
        dt_exp = jnp.where(lane_w // SSD_HEAD_DIM == h, dt[:, h:h + 1], dt_exp)
        da_ref[:, h * LANE:(h + 1) * LANE] = jnp.broadcast_to(da[:, h:h + 1], (nb, LANE))
    xs_ref[...] = xs
    xdt_ref[...] = xs * dt_exp
    lane = lax.broadcasted_iota(jnp.int32, (nb, LANE), 1)
    low = lane < SSD_STATE
    for src, dst in ((bm, b2_ref), (cm, c2_ref)):
        swapped = pltpu.roll(src, SSD_STATE, 1)
        dst[:, 0:LANE] = jnp.where(low, src, swapped)
        dst[:, LANE:2 * LANE] = jnp.where(low, swapped, src)

    u5 = pb_ref[:, COL_U5:COL_U5 + W]
    bu_r, bu_i = _s5_bu(u5.astype(BF16), bre_ref, bim_ref)
    sblk = S5_LANES // len(bu_r)
    for i in range(len(bu_r)):
        ls = slice(i * sblk, (i + 1) * sblk)
        n_r, n_i = _cmul_add(bu_r[i], bu_i[i], pwr_ref[0:1, ls], pwi_ref[0:1, ls], s5r_ref[:, ls], s5i_ref[:, ls])
        s5ro_ref[:, ls] = n_r
        s5io_ref[:, ls] = n_i
    y5 = _s5_tail(u5, lambda i: s5ro_ref[:, i * sblk:(i + 1) * sblk], lambda i: s5io_ref[:, i * sblk:(i + 1) * sblk],
                  cre_ref, cimn_ref, d5_ref, wglu_ref, bglu_ref, g5_ref)
    m5_ref[...] = y5.astype(m5_ref.dtype)

    v = pb_ref[:, COL_GC:COL_GC + W] * pb_ref[:, COL_HV:COL_HV + W]
    vh = SC_CONV - 1
    acc = scw_ref[vh:vh + 1, :] * v
    for k in range(vh):
        acc = acc + scw_ref[k:k + 1, :] * scprev_ref[:, k * W:(k + 1) * W]
    msc_ref[...] = _rms(pb_ref[:, COL_GB:COL_GB + W] * acc, gsc_ref[...]).astype(msc_ref.dtype)
    scst_ref[:, 0:(vh - 1) * W] = scprev_ref[:, W:vh * W]
    scst_ref[:, (vh - 1) * W:vh * W] = v

    up = pb_ref[:, COL_UP:COL_UP + W]
    ys = []
    for gi, w in enumerate(POOL_WINDOWS):
        cs = slice(gi * POOL_GROUP, (gi + 1) * POOL_GROUP)
        s = up[:, cs]
        for j in range(1, w):
            k = POOL_HIST - j
            s = s + plprev_ref[:, k * W + gi * POOL_GROUP:k * W + (gi + 1) * POOL_GROUP]
        pooled = s / float(min(w, PAST_LEN + 1)) - up[:, cs]
        ys.append(_dot(pooled.astype(BF16), pw_ref[gi]))
    y = jnp.concatenate(ys, axis=-1) * ps_ref[...]
    mpl_ref[...] = _rms(y, gpl_ref[...]).astype(mpl_ref.dtype)
    plst_ref[:, 0:(POOL_HIST - 1) * W] = plprev_ref[:, W:POOL_HIST * W]
    plst_ref[:, (POOL_HIST - 1) * W:POOL_HIST * W] = up


def sample_mix(proj_a, proj_b, states, sp, layer):
    nb = DEC_BATCH
    W = GROUP_WIDTH
    f = lambda n: jax.ShapeDtypeStruct((nb, n), F32)
    b = lambda n: jax.ShapeDtypeStruct((nb, n), BF16)
    out_widths = ((SSD_CONV - 1) * XBC_WIDTH, S5_LANES, S5_LANES, (SC_CONV - 1) * W, POOL_HIST * W,
                  W, W, SSD_HEADS * LANE, SSD_GROUPS * LANE, SSD_GROUPS * LANE)
    out_shape = (b(W), b(W), b(W)) + tuple(f(n) for n in out_widths)
    nblk = GROUP_WIDTH // LANE
    in_specs = ([pl.BlockSpec((nb, A_WIDTH), lambda i: (SAMPLE_BLOCK, 0)),
                 pl.BlockSpec((nb, B_WIDTH), lambda i: (SAMPLE_BLOCK, 0))]
                + [_layer_spec((nb, s.shape[2]), layer) for s in states]
                + [_layer_spec((SSD_CONV, XBC_WIDTH), layer), _layer_spec((1, XBC_WIDTH), layer),
                   _layer_spec((1, LANE), layer), _layer_spec((1, LANE), layer),
                   _layer_spec((S5_SEG, S5_LANES), layer), _layer_spec((S5_SEG, S5_LANES), layer)]
                + _s5_param_specs(layer)
                + [_layer_spec((SC_CONV, W), layer), _layer_spec((len(POOL_WINDOWS), POOL_GROUP, POOL_GROUP), layer),
                   _layer_spec((1, W), layer), _layer_spec((1, W), layer), _layer_spec((1, W), layer)])
    return pl.pallas_call(
        _sample_mix_kernel,
        grid=(1,),
        in_specs=in_specs,
        out_specs=tuple(_whole_spec(s.shape) for s in out_shape),
        out_shape=out_shape,
        compiler_params=_cparams(("arbitrary",)),
        name="sample_mix",
    )(proj_a, proj_b, *states, sp["ssd_conv_w"], sp["ssd_conv_b"], sp["ssd_dt_bias"], sp["ssd_a_log"],
      sp["s5_pw_re"], sp["s5_pw_im"], *_s5_param_args(sp),
      sp["sc_conv_w"], sp["pool_w"], sp["pool_scale"], sp["g_sc"], sp["g_pool"])


def _sample_ssd_kernel(h0_ref, xdt_ref, da_ref, b2_ref, c2_ref, e_ref, r_ref, xs_ref, z_ref, dexp_ref, g_ref,
                       hn_ref, mix_ref, xrep_ref, prod_ref, y_ref):
    hp = pl.program_id(0)
    hpl = 2 * SSD_HEAD_DIM * SSD_STATE
    per_head = SSD_HEAD_DIM * SSD_STATE
    x_hi, x_lo = _split_bf16(xdt_ref[...])
    xrep_ref[...] = _dot(x_hi, e_ref[...]) + _dot(x_lo, e_ref[...])
    b2, c2 = b2_ref[...], c2_ref[...]
    for j in range(hpl // LANE):
        ls = slice(j * LANE, (j + 1) * LANE)
        hl = (j * LANE) // per_head
        hn = da_ref[:, hl * LANE:(hl + 1) * LANE] * h0_ref[:, ls] + xrep_ref[:, ls] * b2
        hn_ref[:, ls] = hn
        prod_ref[:, ls] = hn * c2
    p_hi, p_lo = _split_bf16(prod_ref[...])
    y_ref[hp] = _dot(p_hi, r_ref[...]) + _dot(p_lo, r_ref[...])

    @pl.when(hp == pl.num_programs(0) - 1)
    def _():
        y = jnp.concatenate([y_ref[i] for i in range(SSD_HEADS // 2)], axis=-1)
        y = (y + dexp_ref[...] * xs_ref[...]) * _silu(z_ref[...])
        mix_ref[...] = _rms(y, g_ref[...]).astype(mix_ref.dtype)


def sample_ssd(h0_all, xdt, da, b2, c2, xs, proj, sp, layer):
    nb = DEC_BATCH
    npairs = SSD_HEADS // 2
    hpl = 2 * SSD_HEAD_DIM * SSD_STATE
    expand = jnp.repeat(jnp.eye(LANE, dtype=BF16), SSD_STATE, axis=1)
    return pl.pallas_call(
        _sample_ssd_kernel,
        grid=(npairs,),
        in_specs=[pl.BlockSpec((None, nb, hpl), lambda i: (layer, 0, i)),
                  pl.BlockSpec((nb, LANE), lambda i: (0, i)),
                  pl.BlockSpec((nb, 2 * LANE), lambda i: (0, i)),
                  pl.BlockSpec((nb, LANE), lambda i: (0, i // (npairs // SSD_GROUPS))),
                  pl.BlockSpec((nb, LANE), lambda i: (0, i // (npairs // SSD_GROUPS))),
                  _whole_spec((LANE, hpl)),
                  _whole_spec((hpl, LANE)),
                  _whole_spec((nb, GROUP_WIDTH)),
                  pl.BlockSpec((nb, GROUP_WIDTH), lambda i: (SAMPLE_BLOCK, COL_Z // GROUP_WIDTH)),
                  _layer_spec((1, GROUP_WIDTH), layer),
                  _layer_spec((1, GROUP_WIDTH), layer)],
        out_specs=(pl.BlockSpec((nb, hpl), lambda i: (0, i)),
                   _whole_spec((nb, GROUP_WIDTH))),
        out_shape=(jax.ShapeDtypeStruct((nb, SSD_HEADS * SSD_HEAD_DIM * SSD_STATE), F32),
                   jax.ShapeDtypeStruct((nb, GROUP_WIDTH), BF16)),
        scratch_shapes=[pltpu.VMEM((nb, hpl), F32), pltpu.VMEM((nb, hpl), F32),
                        pltpu.VMEM((npairs, nb, LANE), F32)],
        compiler_params=_cparams(("arbitrary",)),
        name="sample_ssd",
    )(h0_all, xdt, da, b2, c2, expand, expand.T, xs, proj, sp["ssd_d_exp"], sp["g_ssd"])


def _split_w_in(w_in):
    head = COL_DT + SSD_HEADS
    part_a = jnp.pad(w_in[..., :head], ((0, 0), (0, 0), (0, A_WIDTH - head)))
    return part_a.astype(BF16), w_in[..., head:].astype(BF16)


def _stacked_params(w):
    row = lambda v: v.reshape(DEPTH, 1, -1)
    pad_heads = lambda v: row(jnp.pad(v, ((0, 0), (0, LANE - SSD_HEADS))))
    g_mix = w["mix_out_g"].reshape(DEPTH, 4, 1, GROUP_WIDTH)
    w_in_a, w_in_b = _split_w_in(w["w_in"])
    sp = dict(w_in_a=w_in_a, w_in_b=w_in_b)
    for k in ("s5_w_glu", "pool_w"):
        sp[k] = w[k].astype(BF16)
    for k in ("w_out", "w_q", "w_k", "w_v", "w_o", "w_up", "w_down"):
        sp[k] = w[k]
    for k in ("norm_mix_g", "norm_xa_g", "norm_mem_g", "norm_mlp_g", "ssd_conv_b", "s5_d", "s5_b_glu", "pool_scale"):
        sp[k] = row(w[k])
    sp.update(ssd_conv_w=w["ssd_conv_w"], sc_conv_w=w["sc_conv_w"],
              ssd_dt_bias=pad_heads(w["ssd_dt_bias"]), ssd_a_log=pad_heads(w["ssd_a_log"]),
              ssd_d_exp=row(jnp.repeat(w["ssd_d"], SSD_HEAD_DIM, axis=1)),
              g_ssd=g_mix[:, 0], g_s5=g_mix[:, 1], g_sc=g_mix[:, 2], g_pool=g_mix[:, 3])
    s5 = s5_prepare(w["s5_lam_re"], w["s5_lam_im"], w["s5_log_dt"], w["s5_b_re"], w["s5_b_im"],
                    w["s5_c_re"], w["s5_c_im"])
    sp.update({"s5_" + k: v for k, v in s5.items()})
    return sp


def _forward(x_prompt, x_sample, mem_prompt, state_ssd_conv, state_ssd, state_s5_re, state_s5_im,
             state_sconv, state_pool, cache_mem_k, cache_mem_v, final_norm_g, w):
    sp = _stacked_params(w)
    flat = lambda s: s.reshape(DEPTH, DEC_BATCH, -1)
    s_states = tuple(flat(s) for s in (state_ssd_conv, state_s5_re, state_s5_im, state_sconv, state_pool))
    s_ssd0 = flat(state_ssd)
    h = jnp.concatenate([x_prompt.reshape(P_ROWS, D_MODEL), x_sample.reshape(DEC_BATCH, D_MODEL)], axis=0)
    mem = mem_prompt.reshape(BATCH * N_MEM, D_MODEL)
    p_out = [[] for _ in range(8)]
    s_out = [[] for _ in range(6)]
    put_sample = lambda full, rows: lax.dynamic_update_slice(full, rows, (P_ROWS, 0))
    for l in range(DEPTH):
        mk = norm_matmul(mem, sp["norm_mem_g"], sp["w_k"], l, tm=BATCH * N_MEM, tn=MEM_COL_TILE)
        mv = norm_matmul(mem, sp["norm_mem_g"], sp["w_v"], l, tm=BATCH * N_MEM, tn=MEM_COL_TILE)

        proj_a, proj_b = in_proj(h, sp["norm_mix_g"], sp["w_in_a"], sp["w_in_b"], l, tm=RESIDENT_ROW_TILE)
        m_ssd, p_conv, p_ssd = prompt_ssd(proj_a, sp, l)
        m_s5, p_s5r, p_s5i, m_sc, m_pl, p_sc, p_pl = prompt_b_mixers(proj_b, sp, l)
        (s_m5, s_msc, s_mpl, s_conv, s_s5r, s_s5i, s_sc, s_pl, xs, xdt, da, b2, c2) = sample_mix(
            proj_a, proj_b, s_states, sp, l)
        s_ssd, s_mssd = sample_ssd(s_ssd0, xdt, da, b2, c2, xs, proj_a, sp, l)
        mixes = (put_sample(m_ssd, s_mssd), put_sample(m_s5, s_m5),
                 put_sample(m_sc, s_msc), put_sample(m_pl, s_mpl))

        h = out_proj(mixes, sp["w_out"], h, l, tm=RESIDENT_ROW_TILE)
        q = norm_matmul(h, sp["norm_xa_g"], sp["w_q"], l, out_dtype=BF16, tm=RESIDENT_ROW_TILE, tn=D_MODEL)
        o = prompt_attn(q, mk.reshape(BATCH, N_MEM, D_MODEL), mv.reshape(BATCH, N_MEM, D_MODEL), tq=512)
        q_s = q[P_ROWS:].astype(F32).reshape(DEC_BATCH, XA_HEADS, XA_HEAD_DIM)
        o_s = sample_attn(q_s, cache_mem_k, cache_mem_v, l)
        o = put_sample(o, o_s.reshape(DEC_BATCH, D_MODEL))
        h = res_matmul(o, sp["w_o"], h, l, tm=RESIDENT_ROW_TILE, tn=D_MODEL)
        h = mlp(h, sp["norm_mlp_g"], sp["w_up"], sp["w_down"], l, tm=MLP_ROW_TILE, tf=FF_TILE)

        for lst, val in zip(p_out, (p_conv, p_ssd, p_s5r, p_s5i, p_sc, p_pl, mk, mv)):
            lst.append(val)
        for lst, val in zip(s_out, (s_conv, s_ssd, s_s5r, s_s5i, s_sc, s_pl)):
            lst.append(val)

    y_prompt = final_norm(h, final_norm_g, rows=P_ROWS, first_block=0, tm=FINAL_ROW_TILE)
    y_sample = final_norm(h, final_norm_g, rows=DEC_BATCH, first_block=SAMPLE_BLOCK, tm=DEC_BATCH)
    p_shapes = ((BATCH, SSD_CONV - 1, XBC_WIDTH), (BATCH, SSD_HEADS, SSD_HEAD_DIM, SSD_STATE),
                (BATCH, S5_GROUPS, S5_STATE), (BATCH, S5_GROUPS, S5_STATE),
                (BATCH, SC_CONV - 1, GROUP_WIDTH), (BATCH, POOL_HIST, GROUP_WIDTH),
                (BATCH, N_MEM, XA_HEADS, XA_HEAD_DIM), (BATCH, N_MEM, XA_HEADS, XA_HEAD_DIM))
    s_shapes = tuple((DEC_BATCH,) + s[1:] for s in p_shapes[:6])
    stack = lambda vals, shape: jnp.stack(vals).reshape((DEPTH,) + shape)
    return ((y_prompt.reshape(BATCH, SEQ, D_MODEL), y_sample.reshape(DEC_BATCH, 1, D_MODEL))
            + tuple(stack(v, s) for v, s in zip(p_out, p_shapes))
            + tuple(stack(v, s) for v, s in zip(s_out, s_shapes)))


_forward_jit = jax.jit(_forward)


def kernel(x_prompt, x_sample, mem_prompt, state_ssd_conv, state_ssd, state_s5_re, state_s5_im, state_sconv, state_pool, cache_mem_k, cache_mem_v, norm_mix_g, w_in, ssd_conv_w, ssd_conv_b, ssd_dt_bias, ssd_a_log, ssd_d, s5_lam_re, s5_lam_im, s5_log_dt, s5_b_re, s5_b_im, s5_c_re, s5_c_im, s5_d, s5_w_glu, s5_b_glu, sc_conv_w, pool_w, pool_scale, mix_out_g, w_out, norm_xa_g, norm_mem_g, w_q, w_k, w_v, w_o, norm_mlp_g, w_up, w_down, final_norm_g):
    w = dict(norm_mix_g=norm_mix_g, w_in=w_in, ssd_conv_w=ssd_conv_w, ssd_conv_b=ssd_conv_b,
             ssd_dt_bias=ssd_dt_bias, ssd_a_log=ssd_a_log, ssd_d=ssd_d, s5_lam_re=s5_lam_re,
             s5_lam_im=s5_lam_im, s5_log_dt=s5_log_dt, s5_b_re=s5_b_re, s5_b_im=s5_b_im,
             s5_c_re=s5_c_re, s5_c_im=s5_c_im, s5_d=s5_d, s5_w_glu=s5_w_glu, s5_b_glu=s5_b_glu,
             sc_conv_w=sc_conv_w, pool_w=pool_w, pool_scale=pool_scale, mix_out_g=mix_out_g,
             w_out=w_out, norm_xa_g=norm_xa_g, norm_mem_g=norm_mem_g, w_q=w_q, w_k=w_k, w_v=w_v,
             w_o=w_o, norm_mlp_g=norm_mlp_g, w_up=w_up, w_down=w_down)
    return _forward_jit(x_prompt, x_sample, mem_prompt, state_ssd_conv, state_ssd, state_s5_re, state_s5_im,
                        state_sconv, state_pool, cache_mem_k, cache_mem_v, final_norm_g, w)
```
